```python
import jax, jax.numpy as jnp
from jax import lax
import numpy as np

D_MODEL = 1024
BATCH = 32
SEQ = 2048
DEPTH = 4

HEAD_DIM = 64
N_HEADS_FOX = 8
N_HEADS_DIL = 8
FOX_WIDTH = N_HEADS_FOX * HEAD_DIM
DIL_WIDTH = N_HEADS_DIL * HEAD_DIM
DIL_PATTERNS = ((128, 1), (512, 4), (2048, 16))
ATTN_BLOCK = 128
ROPE_THETA = 10000.0
N_EXPERTS = 64
TOP_K = 8
N_GROUPS = 8
TOPK_GROUPS = 4
EXPERT_FF = 256
SHARED_FF = 256
ROUTED_SCALE = 2.5
MOE_BLOCK = 128
DN_ALPHA = (2 * DEPTH) ** 0.25
DN_BETA = (8 * DEPTH) ** -0.25
LN_EPS = 1e-5
FORGET_BIAS_INIT = 3.0
ADA_INIT = 0.2
NEG_INF = -1e30
IN_SIZES = (FOX_WIDTH, FOX_WIDTH, FOX_WIDTH, N_HEADS_FOX, DIL_WIDTH, DIL_WIDTH, DIL_WIDTH, D_MODEL, D_MODEL)
IN_COLS = FOX_WIDTH * 3 + N_HEADS_FOX + DIL_WIDTH * 3 + 2 * D_MODEL

kernel_name = "fox_dilated_gated_moe_deepnorm_adaln"


def _split_points(sizes):
    pts, acc = [], 0
    for s in sizes[:-1]:
        acc += s
        pts.append(acc)
    return pts


def layer_norm(x, g, b):
    xf = x.astype(jnp.float32)
    mu = jnp.mean(xf, axis=-1, keepdims=True)
    var = jnp.mean(jnp.square(xf - mu), axis=-1, keepdims=True)
    return ((xf - mu) * lax.rsqrt(var + LN_EPS) * g.astype(jnp.float32) + b.astype(jnp.float32)).astype(x.dtype)


def rotary(x, pos):
    half = HEAD_DIM // 2
    inv_freq = ROPE_THETA ** (-jnp.arange(half, dtype=jnp.float32) * 2.0 / HEAD_DIM)
    ang = pos[:, None] * inv_freq[None, :]
    cos = jnp.cos(ang)[None, :, None, :]
    sin = jnp.sin(ang)[None, :, None, :]
    xf = x.astype(jnp.float32)
    x1, x2 = xf[..., :half], xf[..., half:]
    return jnp.concatenate([x1 * cos - x2 * sin, x2 * cos + x1 * sin], axis=-1).astype(x.dtype)


def forgetting_attention(q, k, v, log_f):
    B, S, H, Dh = q.shape
    q, k, v = (t.transpose(0, 2, 1, 3) for t in (q, k, v))
    cum = jnp.cumsum(log_f, axis=1).transpose(0, 2, 1)
    scale = Dh ** -0.5
    outs = []
    for blk in range(S // ATTN_BLOCK):
        lo, hi = blk * ATTN_BLOCK, (blk + 1) * ATTN_BLOCK
        s = jnp.einsum('bhqd,bhkd->bhqk', q[:, :, lo:hi], k[:, :, :hi]).astype(jnp.float32) * scale
        s = s + cum[:, :, lo:hi, None] - cum[:, :, None, :hi]
        causal = jnp.arange(lo, hi)[:, None] >= jnp.arange(hi)[None, :]
        p = jax.nn.softmax(jnp.where(causal, s, NEG_INF), axis=-1).astype(v.dtype)
        outs.append(jnp.einsum('bhqk,bhkd->bhqd', p, v[:, :, :hi]))
    o = jnp.concatenate(outs, axis=2)
    return o.transpose(0, 2, 1, 3).reshape(B, S, H * Dh)


def dilated_window_attention(q, k, v, dilation, n_back):
    B, S, H, Dh = q.shape
    L = S // dilation
    nb = -(-L // ATTN_BLOCK)
    Lp = nb * ATTN_BLOCK

    def to_sub(t):
        t = t.reshape(B, L, dilation, H, Dh).transpose(0, 2, 3, 1, 4)
        return jnp.pad(t, ((0, 0), (0, 0), (0, 0), (0, Lp - L), (0, 0)))

    qs, ks, vs = to_sub(q), to_sub(k), to_sub(v)
    qb = qs.reshape(B, dilation, H, nb, ATTN_BLOCK, Dh)

    def key_band(t):
        tp = jnp.pad(t, ((0, 0), (0, 0), (0, 0), (ATTN_BLOCK, 0), (0, 0)))
        tp = tp.reshape(B, dilation, H, nb + 1, ATTN_BLOCK, Dh)
        return jnp.concatenate([tp[:, :, :, :-1], tp[:, :, :, 1:]], axis=4)

    kb, vb = key_band(ks), key_band(vs)
    s = jnp.einsum('brhnqd,brhnkd->brhnqk', qb, kb).astype(jnp.float32) * (Dh ** -0.5)
    i = jnp.arange(ATTN_BLOCK)[None, :, None]
    j = jnp.arange(2 * ATTN_BLOCK)[None, None, :]
    n = jnp.arange(nb)[:, None, None]
    dist = i + ATTN_BLOCK - j
    key_idx = n * ATTN_BLOCK + j - ATTN_BLOCK
    valid = (dist >= 0) & (dist <= n_back) & (key_idx >= 0)
    s = jnp.where(valid, s, NEG_INF)
    m = jnp.max(s, axis=-1, keepdims=True)
    e = jnp.exp(s - m)
    l = jnp.sum(e, axis=-1, keepdims=True)
    o = jnp.einsum('brhnqk,brhnkd->brhnqd', (e / l).astype(v.dtype), vb)
    lse = (m + jnp.log(l))[..., 0]
    o = o.reshape(B, dilation, H, Lp, Dh)[:, :, :, :L].transpose(0, 3, 1, 2, 4).reshape(B, S, H, Dh)
    lse = lse.reshape(B, dilation, H, Lp)[..., :L].transpose(0, 3, 1, 2).reshape(B, S, H)
    return o, lse


def dilated_mixture(q, k, v):
    B, S, H, Dh = q.shape
    outs, lses = [], []
    for window, dilation in DIL_PATTERNS:
        o, lse = dilated_window_attention(q, k, v, dilation, window // dilation)
        outs.append(o)
        lses.append(lse)
    w = jax.nn.softmax(jnp.stack(lses, axis=0), axis=0)
    o = jnp.sum(w[..., None] * jnp.stack(outs, axis=0).astype(jnp.float32), axis=0)
    return o.reshape(B, S, H * Dh).astype(v.dtype)


def token_mixer(u, w_in, b_forget, b_gate, w_br_fox, w_br_dil, w_o):
    B, S, D = u.shape
    proj = u @ w_in
    qf, kf, vf, ff, qd, kd, vd, ga, gb = jnp.split(proj, _split_points(IN_SIZES), axis=-1)
    fox_heads = lambda t: t.reshape(B, S, N_HEADS_FOX, HEAD_DIM)
    dil_heads = lambda t: t.reshape(B, S, N_HEADS_DIL, HEAD_DIM)
    log_f = jax.nn.log_sigmoid((ff + b_forget).astype(jnp.float32))
    y_fox = forgetting_attention(fox_heads(qf), fox_heads(kf), fox_heads(vf), log_f)
    pos = jnp.arange(S, dtype=jnp.float32)
    y_dil = dilated_mixture(rotary(dil_heads(qd), pos), rotary(dil_heads(kd), pos), dil_heads(vd))
    g_fox = jax.nn.sigmoid(ga + b_gate[:D])
    g_dil = jax.nn.sigmoid(gb + b_gate[D:])
    merged = g_fox * (y_fox @ w_br_fox) + g_dil * (y_dil @ w_br_dil)
    return merged @ w_o


def swiglu(h, w_gate, w_up, w_down):
    return (jax.nn.silu(h @ w_gate) * (h @ w_up)) @ w_down


def routed_experts(h, eidx, gw, w_eg, w_eu, w_ed):
    N, D = h.shape
    NK = N * TOP_K
    flat_e = eidx.reshape(NK)
    flat_w = gw.reshape(NK)
    flat_tok = jnp.arange(NK, dtype=jnp.int32) // TOP_K
    order = jnp.argsort(flat_e)
    e_s, tok_s, w_s = flat_e[order], flat_tok[order], flat_w[order]
    counts = jnp.bincount(flat_e, length=N_EXPERTS)
    starts = jnp.cumsum(counts) - counts
    padded = (counts + MOE_BLOCK - 1) // MOE_BLOCK * MOE_BLOCK
    pends = jnp.cumsum(padded)
    pstarts = pends - padded
    slot = pstarts[e_s] + (jnp.arange(NK, dtype=jnp.int32) - starts[e_s])
    n_blocks = -(-NK // MOE_BLOCK) + N_EXPERTS
    P = n_blocks * MOE_BLOCK
    slot_tok = jnp.full((P,), N, dtype=jnp.int32).at[slot].set(tok_s)
    slot_w = jnp.zeros((P,), h.dtype).at[slot].set(w_s)
    blk_e = jnp.minimum(jnp.searchsorted(pends, jnp.arange(n_blocks) * MOE_BLOCK, side='right'), N_EXPERTS - 1)
    h_pad = jnp.concatenate([h, jnp.zeros((1, D), h.dtype)], axis=0)

    def expert_block(args):
        toks, e = args
        xb = jnp.take(h_pad, toks, axis=0)
        return swiglu(xb, w_eg[e], w_eu[e], w_ed[e])

    out = lax.map(expert_block, (slot_tok.reshape(n_blocks, MOE_BLOCK), blk_e))
    out = out.reshape(P, D) * slot_w[:, None]
    return jax.ops.segment_sum(out, slot_tok, num_segments=N + 1)[:N]


def moe(h, w_router, router_bias, w_eg, w_eu, w_ed, w_sg, w_su, w_sd):
    N, D = h.shape
    scores = jax.nn.sigmoid(h.astype(jnp.float32) @ w_router.astype(jnp.float32))
    sel = scores + router_bias.astype(jnp.float32)
    grp = sel.reshape(N, N_GROUPS, N_EXPERTS // N_GROUPS)
    grp_score = jnp.sum(lax.top_k(grp, 2)[0], axis=-1)
    _, gidx = lax.top_k(grp_score, TOPK_GROUPS)
    gmask = jnp.sum(jax.nn.one_hot(gidx, N_GROUPS, dtype=jnp.float32), axis=1)
    emask = jnp.repeat(gmask, N_EXPERTS // N_GROUPS, axis=1)
    _, eidx = lax.top_k(jnp.where(emask > 0, sel, NEG_INF), TOP_K)
    gw = jnp.take_along_axis(scores, eidx, axis=-1)
    gw = gw / jnp.sum(gw, axis=-1, keepdims=True) * ROUTED_SCALE
    routed = routed_experts(h, eidx, gw.astype(h.dtype), w_eg, w_eu, w_ed)
    return routed + swiglu(h, w_sg, w_su, w_sd)


def setup_inputs(seed: int = 0) -> dict:
    key = jax.random.key(seed)
    ks = jax.random.split(key, 22)
    nrm = lambda k, shape, scale: jax.random.normal(k, shape, jnp.float32) * scale
    D = D_MODEL
    return {
        "x": nrm(ks[0], (BATCH, SEQ, D), 1.0),
        "c": nrm(ks[1], (BATCH, D), 1.0),
        "w_ada": nrm(ks[2], (DEPTH, D, 6 * D), ADA_INIT * D ** -0.5),
        "b_ada": nrm(ks[3], (DEPTH, 6 * D), 0.02),
        "w_in": nrm(ks[4], (DEPTH, D, IN_COLS), D ** -0.5),
        "b_forget": FORGET_BIAS_INIT + nrm(ks[5], (DEPTH, N_HEADS_FOX), 0.5),
        "b_gate": nrm(ks[6], (DEPTH, 2 * D), 0.02),
        "w_br_fox": nrm(ks[7], (DEPTH, FOX_WIDTH, D), FOX_WIDTH ** -0.5),
        "w_br_dil": nrm(ks[8], (DEPTH, DIL_WIDTH, D), DIL_WIDTH ** -0.5),
        "w_o": nrm(ks[9], (DEPTH, D, D), DN_BETA * D ** -0.5),
        "ln_g": 1.0 + nrm(ks[10], (DEPTH, 2, D), 0.02),
        "ln_b": nrm(ks[11], (DEPTH, 2, D), 0.02),
        "w_router": nrm(ks[12], (DEPTH, D, N_EXPERTS), D ** -0.5),
        "router_bias": nrm(ks[13], (DEPTH, N_EXPERTS), 0.01),
        "w_exp_gate": nrm(ks[14], (DEPTH, N_EXPERTS, D, EXPERT_FF), D ** -0.5),
        "w_exp_up": nrm(ks[15], (DEPTH, N_EXPERTS, D, EXPERT_FF), D ** -0.5),
        "w_exp_down": nrm(ks[16], (DEPTH, N_EXPERTS, EXPERT_FF, D), DN_BETA * EXPERT_FF ** -0.5),
        "w_sh_gate": nrm(ks[17], (DEPTH, D, SHARED_FF), D ** -0.5),
        "w_sh_up": nrm(ks[18], (DEPTH, D, SHARED_FF), D ** -0.5),
        "w_sh_down": nrm(ks[19], (DEPTH, SHARED_FF, D), DN_BETA * SHARED_FF ** -0.5),
    }


def reference(x, c, w_ada, b_ada, w_in, b_forget, b_gate, w_br_fox, w_br_dil, w_o, ln_g, ln_b,
              w_router, router_bias, w_exp_gate, w_exp_up, w_exp_down, w_sh_gate, w_sh_up, w_sh_down):
    B, S, D = x.shape
    cs = jax.nn.silu(c)
    for l in range(DEPTH):
        mod = cs @ w_ada[l] + b_ada[l]
        sh1, sc1, g1, sh2, sc2, g2 = [m[:, None, :] for m in jnp.split(mod, 6, axis=-1)]
        u = x * (1.0 + sc1) + sh1
        y = token_mixer(u, w_in[l], b_forget[l], b_gate[l], w_br_fox[l], w_br_dil[l], w_o[l])
        x = layer_norm(DN_ALPHA * x + (1.0 + g1) * y, ln_g[l, 0], ln_b[l, 0])
        u = (x * (1.0 + sc2) + sh2).reshape(B * S, D)
        y = moe(u, w_router[l], router_bias[l], w_exp_gate[l], w_exp_up[l], w_exp_down[l],
                w_sh_gate[l], w_sh_up[l], w_sh_down[l]).reshape(B, S, D)
        x = layer_norm(DN_ALPHA * x + (1.0 + g2) * y, ln_g[l, 1], ln_b[l, 1])
    return x
```

```python
import functools

import jax
import jax.numpy as jnp
from jax import lax
from jax.experimental import pallas as pl
from jax.experimental.pallas import tpu as pltpu

D_MODEL = 1024
DEPTH = 4
HEAD_DIM = 64
N_HEADS = 8
HEADS_WIDTH = N_HEADS * HEAD_DIM
DIL_PATTERNS = ((128, 1), (512, 4), (2048, 16))
ATTN_BLOCK = 128
ROPE_THETA = 10000.0
N_EXPERTS = 64
TOP_K = 8
N_GROUPS = 8
GROUP_SIZE = N_EXPERTS // N_GROUPS
TOPK_GROUPS = 4
EXPERT_FF = 256
SHARED_FF = 256
ROUTED_SCALE = 2.5
DN_ALPHA = (2 * DEPTH) ** 0.25
LN_EPS = 1e-5
NEG_INF = -1e30
SCORE_SCALE = HEAD_DIM ** -0.5

LANES = 128
PAIR = 2 * HEAD_DIM
N_PAIRS = HEADS_WIDTH // PAIR

ROW_TILE = 512
FOX_BLOCK = 256
ROUTE_CHUNK = 512
MOE_CHUNK = 128

BF16 = jnp.bfloat16
F32 = jnp.float32


def _dot(a, b):
    return jnp.dot(a, b, preferred_element_type=F32)


def _dot_nt(a, b):
    return lax.dot_general(a, b, (((1,), (1,)), ((), ())), preferred_element_type=F32)


def _split2(x):
    hi = x.astype(BF16)
    lo = (x - hi.astype(F32)).astype(BF16)
    return hi, lo


def _split3(x):
    hi = x.astype(BF16)
    r = x - hi.astype(F32)
    mid = r.astype(BF16)
    lo = (r - mid.astype(F32)).astype(BF16)
    return hi, mid, lo


def _sigmoid(x):
    return 1.0 / (1.0 + jnp.exp(-x))


def _layer_norm(z, g, b):
    mu = jnp.mean(z, axis=-1, keepdims=True)
    zc = z - mu
    var = jnp.mean(zc * zc, axis=-1, keepdims=True)
    return zc * lax.rsqrt(var + LN_EPS) * g + b


def _params(sem, vmem_mb):
    return pltpu.CompilerParams(dimension_semantics=sem, vmem_limit_bytes=vmem_mb * 1024 * 1024)


def _ada_kernel(c_ref, w_ref, b_ref, o_ref):
    c = c_ref[...]
    cs = c * _sigmoid(c)
    c_hi, c_lo = _split2(cs)
    w_hi, w_lo = _split2(w_ref[0])
    o_ref[0] = _dot(c_hi, w_hi) + _dot(c_hi, w_lo) + _dot(c_lo, w_hi) + b_ref[0]


def _ada(c, w_ada, b_ada):
    depth, d, n = w_ada.shape
    bsz = c.shape[0]
    tn = 1536
    return pl.pallas_call(
        _ada_kernel,
        grid=(depth, n // tn),
        in_specs=[
            pl.BlockSpec((bsz, d), lambda l, j: (0, 0)),
            pl.BlockSpec((1, d, tn), lambda l, j: (l, 0, j)),
            pl.BlockSpec((1, 1, tn), lambda l, j: (l, 0, j)),
        ],
        out_specs=pl.BlockSpec((1, bsz, tn), lambda l, j: (l, 0, j)),
        out_shape=jax.ShapeDtypeStruct((depth, bsz, n), F32),
        compiler_params=_params(("arbitrary", "arbitrary"), 40),
        name="ada",
    )(c, w_ada, b_ada.reshape(depth, 1, n))


def _inproj_kernel(x_ref, sc_ref, sh_ref, wfox_ref, wdil_ref, wgate_ref, wfg_ref, bfg_ref, bgate_ref,
                   cos_ref, sina_ref, sinb_ref,
                   qf_ref, kf_ref, vf_ref, qd_ref, kd_ref, vd_ref, gf_ref, gd_ref, cum_ref, cumt_ref,
                   carry_ref):
    j = pl.program_id(1)
    tm = x_ref.shape[1]
    u = (x_ref[0] * (1.0 + sc_ref[0]) + sh_ref[0]).astype(BF16)

    pf = _dot(u, wfox_ref[...])
    qf_ref[0] = pf[:, :HEADS_WIDTH].astype(BF16)
    kf_ref[0] = pf[:, HEADS_WIDTH:2 * HEADS_WIDTH].astype(BF16)
    vf_ref[0] = pf[:, 2 * HEADS_WIDTH:].astype(BF16)

    pd = _dot(u, wdil_ref[...])
    rows = pl.ds(pl.multiple_of(j * tm, tm), tm)
    cos = cos_ref[rows, :]
    sina = sina_ref[rows, :]
    sinb = sinb_ref[rows, :]
    for dst, off in ((qd_ref, 0), (kd_ref, HEADS_WIDTH)):
        for g in range(N_PAIRS):
            xg = pd[:, off + g * LANES: off + (g + 1) * LANES]
            ahead = pltpu.roll(xg, LANES - HEAD_DIM // 2, 1)
            behind = pltpu.roll(xg, HEAD_DIM // 2, 1)
            dst[0, :, g * LANES:(g + 1) * LANES] = (xg * cos + ahead * sina + behind * sinb).astype(BF16)
    vd_ref[0] = pd[:, 2 * HEADS_WIDTH:].astype(BF16)

    pg = _sigmoid(_dot(u, wgate_ref[...]) + bgate_ref[...])
    gf_ref[0] = pg[:, :D_MODEL].astype(BF16)
    gd_ref[0] = pg[:, D_MODEL:].astype(BF16)

    ff = _dot(u, wfg_ref[...]) + bfg_ref[...]
    lf = jnp.minimum(ff, 0.0) - jnp.log(1.0 + jnp.exp(-jnp.abs(ff)))
    r = lax.broadcasted_iota(jnp.int32, (tm, tm), 0)
    c = lax.broadcasted_iota(jnp.int32, (tm, tm), 1)
    tri = jnp.where(r >= c, 1.0, 0.0).astype(BF16)
    hi, mid, lo = _split3(lf)
    csum = _dot(tri, hi) + _dot(tri, mid) + _dot(tri, lo)

    @pl.when(j == 0)
    def _():
        carry_ref[...] = jnp.zeros_like(carry_ref)

    cum = csum + carry_ref[...]
    carry_ref[...] = cum[tm - 1:tm, :]
    cum_ref[0] = cum[:, :N_HEADS]
    cumt_ref[0] = cum.T[:N_HEADS, :]


def _inproj(x, mod3, wfox, wdil, wgate, wfg, bfg, bgate, cos_t, sina_t, sinb_t):
    bsz, seq, d = x.shape
    tm = ROW_TILE
    const = lambda shape: pl.BlockSpec(shape, lambda b, j: (0,) * len(shape))
    tok = lambda w: pl.BlockSpec((1, tm, w), lambda b, j: (b, j, 0))
    hw = HEADS_WIDTH
    outs = [jax.ShapeDtypeStruct((bsz, seq, hw), BF16)] * 6 + [jax.ShapeDtypeStruct((bsz, seq, d), BF16)] * 2 + [
        jax.ShapeDtypeStruct((bsz, seq, N_HEADS), F32), jax.ShapeDtypeStruct((bsz, N_HEADS, seq), F32)]
    return pl.pallas_call(
        _inproj_kernel,
        grid=(bsz, seq // tm),
        in_specs=[
            tok(d),
            pl.BlockSpec((1, 1, d), lambda b, j: (b, 0, 1)),
            pl.BlockSpec((1, 1, d), lambda b, j: (b, 0, 0)),
            const(wfox.shape), const(wdil.shape), const(wgate.shape), const(wfg.shape),
            const(bfg.shape), const(bgate.shape),
            const(cos_t.shape), const(sina_t.shape), const(sinb_t.shape),
        ],
        out_specs=[tok(hw)] * 6 + [tok(d)] * 2 + [
            pl.BlockSpec((1, tm, N_HEADS), lambda b, j: (b, j, 0)),
            pl.BlockSpec((1, N_HEADS, tm), lambda b, j: (b, 0, j))],
        out_shape=outs,
        scratch_shapes=[pltpu.VMEM((1, LANES), F32)],
        compiler_params=_params(("arbitrary", "arbitrary"), 56),
        name="inproj",
    )(x, mod3, mod3, wfox, wdil, wgate, wfg, bfg, bgate, cos_t, sina_t, sinb_t)


def _fox_kernel(q_ref, k_ref, v_ref, cum_ref, cumt_ref, o_ref):
    seq = q_ref.shape[1]
    tq = FOX_BLOCK
    lane = lax.broadcasted_iota(jnp.int32, (1, PAIR), 1)
    first = lane < HEAD_DIM
    row = lax.broadcasted_iota(jnp.int32, (tq, tq), 0)
    col = lax.broadcasted_iota(jnp.int32, (tq, tq), 1)
    causal = row >= col

    for p in range(N_PAIRS):
        ls = slice(p * PAIR, (p + 1) * PAIR)

        def q_body(qi, _, p=p, ls=ls):
            q0 = pl.multiple_of(qi * tq, tq)
            qb = q_ref[0, pl.ds(q0, tq), ls]
            zero = jnp.zeros_like(qb)
            qs = (jnp.where(first, qb, zero), jnp.where(first, zero, qb))
            cqs = tuple(cum_ref[0, pl.ds(q0, tq), 2 * p + hh:2 * p + hh + 1] for hh in range(2))

            def kv_step(k0, carry, diag):
                kb = k_ref[0, pl.ds(k0, tq), ls]
                vb = v_ref[0, pl.ds(k0, tq), ls]
                out = []
                for hh in range(2):
                    m, l, acc = carry[hh]
                    ck = cumt_ref[0, 2 * p + hh:2 * p + hh + 1, pl.ds(k0, tq)]
                    s = _dot_nt(qs[hh], kb) * SCORE_SCALE + (cqs[hh] - ck)
                    if diag:
                        s = jnp.where(causal, s, NEG_INF)
                    m_new = jnp.maximum(m, jnp.max(s, axis=-1, keepdims=True))
                    a = jnp.exp(m - m_new)
                    e = jnp.exp(s - m_new)
                    l = a * l + jnp.sum(e, axis=-1, keepdims=True)
                    acc = a * acc + _dot(e.astype(BF16), vb)
                    out.append((m_new, l, acc))
                return tuple(out)

            init = tuple((jnp.full((tq, 1), NEG_INF, F32), jnp.zeros((tq, 1), F32), jnp.zeros((tq, PAIR), F32))
                         for _ in range(2))
            carry = lax.fori_loop(
                0, qi, lambda jb, cr: kv_step(pl.multiple_of(jb * tq, tq), cr, False), init)
            (_, l0, a0), (_, l1, a1) = kv_step(q0, carry, True)
            o_ref[0, pl.ds(q0, tq), ls] = jnp.where(first, a0 / l0, a1 / l1).astype(BF16)
            return 0

        lax.fori_loop(0, seq // tq, q_body, 0)


def _fox(qf, kf, vf, cum, cumt):
    bsz, seq, hw = qf.shape
    full = pl.BlockSpec((1, seq, hw), lambda b: (b, 0, 0))
    return pl.pallas_call(
        _fox_kernel,
        grid=(bsz,),
        in_specs=[full, full, full,
                  pl.BlockSpec((1, seq, N_HEADS), lambda b: (b, 0, 0)),
                  pl.BlockSpec((1, N_HEADS, seq), lambda b: (b, 0, 0))],
        out_specs=full,
        out_shape=jax.ShapeDtypeStruct((bsz, seq, hw), BF16),
        compiler_params=_params(("arbitrary",), 40),
        name="fox",
    )(qf, kf, vf, cum, cumt)


def _dil_kernel(q_ref, kc_ref, kp_ref, vc_ref, vp_ref, o_ref, lse_ref):
    n = pl.program_id(2)
    blk = ATTN_BLOCK
    lane = lax.broadcasted_iota(jnp.int32, (1, PAIR), 1)
    first = lane < HEAD_DIM
    i = lax.broadcasted_iota(jnp.int32, (blk, 2 * blk), 0)
    c = lax.broadcasted_iota(jnp.int32, (blk, 2 * blk), 1)
    valid = (c >= i) & (c <= i + blk) & ((c >= blk) | (n > 0))

    for p in range(N_PAIRS):
        ls = slice(p * PAIR, (p + 1) * PAIR)
        qb = q_ref[0, :, ls]
        zero = jnp.zeros_like(qb)
        kb = jnp.concatenate([kp_ref[0, :, ls], kc_ref[0, :, ls]], axis=0)
        vb = jnp.concatenate([vp_ref[0, :, ls], vc_ref[0, :, ls]], axis=0)
        res = []
        for hh in range(2):
            qh = jnp.where(first, qb, zero) if hh == 0 else jnp.where(first, zero, qb)
            s = jnp.where(valid, _dot_nt(qh, kb) * SCORE_SCALE, NEG_INF)
            m = jnp.max(s, axis=-1, keepdims=True)
            e = jnp.exp(s - m)
            l = jnp.sum(e, axis=-1, keepdims=True)
            o = _dot(e.astype(BF16), vb) / l
            res.append((o, m + jnp.log(l)))
        o_ref[0, :, ls] = jnp.where(first, res[0][0], res[1][0]).astype(BF16)
        lse_ref[0, :, ls] = jnp.where(first, res[0][1], res[1][1])


def _dil(qd, kd, vd, dilation):
    bsz, seq, hw = qd.shape
    length = seq // dilation
    nb = length // ATTN_BLOCK
    view = lambda t: t.reshape(bsz, length, dilation * hw)
    cur = pl.BlockSpec((1, ATTN_BLOCK, hw), lambda b, r, n: (b, n, r))
    prev = pl.BlockSpec((1, ATTN_BLOCK, hw), lambda b, r, n: (b, jnp.maximum(n - 1, 0), r))
    o, lse = pl.pallas_call(
        _dil_kernel,
        grid=(bsz, dilation, nb),
        in_specs=[cur, cur, prev, cur, prev],
        out_specs=[cur, cur],
        out_shape=[jax.ShapeDtypeStruct((bsz, length, dilation * hw), BF16),
                   jax.ShapeDtypeStruct((bsz, length, dilation * hw), F32)],
        compiler_params=_params(("arbitrary", "arbitrary", "arbitrary"), 32),
        name=f"dil{dilation}",
    )(view(qd), view(kd), view(kd), view(vd), view(vd))
    return o.reshape(bsz, seq, hw), lse.reshape(bsz, seq, hw)


def _mix_kernel(x_ref, yf_ref, o1_ref, o2_ref, o3_ref, l1_ref, l2_ref, l3_ref, gf_ref, gd_ref, g1_ref,
                wbf_ref, wbd_ref, wo_ref, lng_ref, lnb_ref, out_ref):
    l1, l2, l3 = l1_ref[0], l2_ref[0], l3_ref[0]
    mx = jnp.maximum(jnp.maximum(l1, l2), l3)
    e1, e2, e3 = jnp.exp(l1 - mx), jnp.exp(l2 - mx), jnp.exp(l3 - mx)
    yd = (e1 * o1_ref[0].astype(F32) + e2 * o2_ref[0].astype(F32) + e3 * o3_ref[0].astype(F32)) / (e1 + e2 + e3)
    merged = (gf_ref[0].astype(F32) * _dot(yf_ref[0], wbf_ref[...])
              + gd_ref[0].astype(F32) * _dot(yd.astype(BF16), wbd_ref[...]))
    y = _dot(merged.astype(BF16), wo_ref[...])
    z = DN_ALPHA * x_ref[0] + (1.0 + g1_ref[0]) * y
    out_ref[0] = _layer_norm(z, lng_ref[...], lnb_ref[...])


def _mix(x, yf, os_, lses, gf, gd, mod3, wbf, wbd, wo, lng, lnb):
    bsz, seq, d = x.shape
    tm = ROW_TILE
    const = lambda shape: pl.BlockSpec(shape, lambda b, j: (0,) * len(shape))
    tok = lambda w: pl.BlockSpec((1, tm, w), lambda b, j: (b, j, 0))
    hw = HEADS_WIDTH
    return pl.pallas_call(
        _mix_kernel,
        grid=(bsz, seq // tm),
        in_specs=[tok(d)] + [tok(hw)] * 7 + [tok(d), tok(d),
                  pl.BlockSpec((1, 1, d), lambda b, j: (b, 0, 2)),
                  const(wbf.shape), const(wbd.shape), const(wo.shape), const(lng.shape), const(lnb.shape)],
        out_specs=tok(d),
        out_shape=jax.ShapeDtypeStruct((bsz, seq, d), F32),
        compiler_params=_params(("arbitrary", "arbitrary"), 48),
        name="mix",
    )(x, yf, *os_, *lses, gf, gd, mod3, wbf, wbd, wo, lng, lnb)


def _router_kernel(x_ref, sc_ref, sh_ref, wrh_ref, wrl_ref, bias_ref, u_ref, rank_ref, gw_ref, cnt_ref):
    seq = x_ref.shape[1]
    tc = ROUTE_CHUNK
    e_iota = lax.broadcasted_iota(jnp.int32, (N_EXPERTS, tc), 0)
    g_iota = lax.broadcasted_iota(jnp.int32, (N_GROUPS, tc), 0)
    s_iota = lax.broadcasted_iota(jnp.int32, (GROUP_SIZE, tc), 0)
    r = lax.broadcasted_iota(jnp.int32, (tc, tc), 0)
    c = lax.broadcasted_iota(jnp.int32, (tc, tc), 1)
    upper = jnp.where(r <= c, 1.0, 0.0).astype(BF16)
    ones = jnp.ones((16, tc), BF16)
    carry = jnp.zeros((N_EXPERTS, 1), F32)
    cnt = jnp.zeros((16, N_EXPERTS), F32)

    for ch in range(seq // tc):
        rows = slice(ch * tc, (ch + 1) * tc)
        u = x_ref[0, rows, :] * (1.0 + sc_ref[0]) + sh_ref[0]
        u_ref[0, rows, :] = u.astype(BF16)
        u_hi, u_lo = _split2(u)
        logits = _dot_nt(wrh_ref[...], u_hi) + _dot_nt(wrh_ref[...], u_lo) + _dot_nt(wrl_ref[...], u_hi)
        scores = _sigmoid(logits)
        sel = scores + bias_ref[...]

        gs_rows = []
        for g in range(N_GROUPS):
            blk = sel[g * GROUP_SIZE:(g + 1) * GROUP_SIZE]
            m1 = jnp.max(blk, axis=0, keepdims=True)
            i1 = jnp.min(jnp.where(blk == m1, s_iota, GROUP_SIZE), axis=0, keepdims=True)
            m2 = jnp.max(jnp.where(s_iota == i1, -jnp.inf, blk), axis=0, keepdims=True)
            gs_rows.append(m1 + m2)
        gs = jnp.concatenate(gs_rows, axis=0)
        beaten = jnp.zeros((N_GROUPS, tc), F32)
        for g in range(N_GROUPS):
            other = gs[g:g + 1]
            wins = (other > gs) | ((other == gs) & (g_iota > g))
            beaten = beaten + jnp.where(wins, 1.0, 0.0)
        keep = beaten < TOPK_GROUPS
        cand = jnp.concatenate(
            [jnp.where(keep[g:g + 1], sel[g * GROUP_SIZE:(g + 1) * GROUP_SIZE], NEG_INF) for g in range(N_GROUPS)],
            axis=0)
        chosen = jnp.zeros((N_EXPERTS, tc), F32)
        for _ in range(TOP_K):
            m = jnp.max(cand, axis=0, keepdims=True)
            idx = jnp.min(jnp.where(cand == m, e_iota, N_EXPERTS), axis=0, keepdims=True)
            hit = e_iota == idx
            chosen = jnp.where(hit, 1.0, chosen)
            cand = jnp.where(hit, -jnp.inf, cand)
        picked = chosen > 0.0
        g_raw = jnp.where(picked, scores, 0.0)
        gw_ref[0, :, rows] = g_raw / jnp.sum(g_raw, axis=0, keepdims=True) * ROUTED_SCALE

        chosen_b = chosen.astype(BF16)
        incl = _dot(chosen_b, upper) + carry
        rank_ref[0, :, rows] = jnp.where(picked, incl - 1.0, -1.0)
        carry = incl[:, tc - 1:tc]
        cnt = cnt + _dot_nt(ones, chosen_b)

    cnt_ref[0] = cnt[0:1].astype(jnp.int32)


def _router(x1, mod3, wr_hi, wr_lo, bias):
    bsz, seq, d = x1.shape
    full = pl.BlockSpec((1, seq, d), lambda b: (b, 0, 0))
    es = pl.BlockSpec((1, N_EXPERTS, seq), lambda b: (b, 0, 0))
    return pl.pallas_call(
        _router_kernel,
        grid=(bsz,),
        in_specs=[full,
                  pl.BlockSpec((1, 1, d), lambda b: (b, 0, 4)),
                  pl.BlockSpec((1, 1, d), lambda b: (b, 0, 3)),
                  pl.BlockSpec(wr_hi.shape, lambda b: (0, 0)),
                  pl.BlockSpec(wr_lo.shape, lambda b: (0, 0)),
                  pl.BlockSpec(bias.shape, lambda b: (0, 0))],
        out_specs=[full, es, es, pl.BlockSpec((1, 1, N_EXPERTS), lambda b: (b, 0, 0))],
        out_shape=[jax.ShapeDtypeStruct((bsz, seq, d), BF16),
                   jax.ShapeDtypeStruct((bsz, N_EXPERTS, seq), F32),
                   jax.ShapeDtypeStruct((bsz, N_EXPERTS, seq), F32),
                   jax.ShapeDtypeStruct((bsz, 1, N_EXPERTS), jnp.int32)],
        compiler_params=_params(("arbitrary",), 48),
        name="router",
    )(x1, mod3, mod3, wr_hi, wr_lo, bias)


def _moe_kernel(cnt_ref, u_ref, rank_ref, gw_ref, wgu_ref, wdt_ref, out_ref, acc_ref):
    b = pl.program_id(0)
    e = pl.program_id(1)
    seq = u_ref.shape[1]
    rc = MOE_CHUNK

    @pl.when(e == 0)
    def _():
        acc_ref[...] = jnp.zeros_like(acc_ref)

    count = cnt_ref[b * N_EXPERTS + e]
    rank = rank_ref[0, 0]
    sub = lax.broadcasted_iota(jnp.int32, (16, seq), 0)
    g_hi, g_lo = _split2(jnp.where(sub == 0, gw_ref[0, 0], 0.0))
    slot = lax.broadcasted_iota(jnp.int32, (rc, 1), 0).astype(F32)

    def chunk(ci, _):
        base = (ci * rc).astype(F32)
        onehot = jnp.where(rank == slot + base, 1.0, 0.0).astype(BF16)
        xs = _dot(onehot, u_ref[0]).astype(BF16)
        hgu = _dot(xs, wgu_ref[0])
        hg = hgu[:, :EXPERT_FF]
        h = (hg * _sigmoid(hg) * hgu[:, EXPERT_FF:]).astype(BF16)
        yt = _dot_nt(wdt_ref[0], h)
        w = (_dot_nt(g_hi, onehot) + _dot_nt(g_lo, onehot))[0:1]
        acc_ref[...] += _dot((yt * w).astype(BF16), onehot)
        return 0

    lax.fori_loop(0, (count + rc - 1) // rc, chunk, 0)

    @pl.when(e == N_EXPERTS - 1)
    def _():
        step = 256
        for t in range(seq // step):
            out_ref[0, t * step:(t + 1) * step, :] = acc_ref[:, t * step:(t + 1) * step].T


def _moe(u2, rank, gw, cnt, wgu, wdt):
    bsz, seq, d = u2.shape
    rank4 = rank.reshape(bsz, N_EXPERTS, 1, seq)
    gw4 = gw.reshape(bsz, N_EXPERTS, 1, seq)
    row = pl.BlockSpec((1, 1, 1, seq), lambda b, e, c: (b, e, 0, 0))
    grid_spec = pltpu.PrefetchScalarGridSpec(
        num_scalar_prefetch=1,
        grid=(bsz, N_EXPERTS),
        in_specs=[pl.BlockSpec((1, seq, d), lambda b, e, c: (b, 0, 0)),
                  row, row,
                  pl.BlockSpec((1, d, 2 * EXPERT_FF), lambda b, e, c: (e, 0, 0)),
                  pl.BlockSpec((1, d, EXPERT_FF), lambda b, e, c: (e, 0, 0))],
        out_specs=pl.BlockSpec((1, seq, d), lambda b, e, c: (b, 0, 0)),
        scratch_shapes=[pltpu.VMEM((d, seq), F32)],
    )
    return pl.pallas_call(
        _moe_kernel,
        grid_spec=grid_spec,
        out_shape=jax.ShapeDtypeStruct((bsz, seq, d), F32),
        compiler_params=_params(("arbitrary", "arbitrary"), 56),
        name="moe",
    )(cnt.reshape(bsz * N_EXPERTS), u2, rank4, gw4, wgu, wdt)


def _ffn_out_kernel(x_ref, u_ref, yr_ref, g2_ref, wsgu_ref, wsd_ref, lng_ref, lnb_ref, out_ref):
    hgu = _dot(u_ref[0], wsgu_ref[...])
    hg = hgu[:, :SHARED_FF]
    h = (hg * _sigmoid(hg) * hgu[:, SHARED_FF:]).astype(BF16)
    y = yr_ref[0] + _dot(h, wsd_ref[...])
    z = DN_ALPHA * x_ref[0] + (1.0 + g2_ref[0]) * y
    out_ref[0] = _layer_norm(z, lng_ref[...], lnb_ref[...])


def _ffn_out(x1, u2, yr, mod3, wsgu, wsd, lng, lnb):
    bsz, seq, d = x1.shape
    tm = ROW_TILE
    const = lambda shape: pl.BlockSpec(shape, lambda b, j: (0,) * len(shape))
    tok = pl.BlockSpec((1, tm, d), lambda b, j: (b, j, 0))
    return pl.pallas_call(
        _ffn_out_kernel,
        grid=(bsz, seq // tm),
        in_specs=[tok, tok, tok,
                  pl.BlockSpec((1, 1, d), lambda b, j: (b, 0, 5)),
                  const(wsgu.shape), const(wsd.shape), const(lng.shape), const(lnb.shape)],
        out_specs=tok,
        out_shape=jax.ShapeDtypeStruct((bsz, seq, d), F32),
        compiler_params=_params(("arbitrary", "arbitrary"), 40),
        name="ffn_out",
    )(x1, u2, yr, mod3, wsgu, wsd, lng, lnb)


def _rope_tables(seq):
    half = HEAD_DIM // 2
    inv_freq = ROPE_THETA ** (-jnp.arange(half, dtype=F32) * 2.0 / HEAD_DIM)
    ang = jnp.arange(seq, dtype=F32)[:, None] * inv_freq[None, :]
    cos, sin = jnp.cos(ang), jnp.sin(ang)
    zero = jnp.zeros_like(sin)
    reps = LANES // HEAD_DIM
    cos_t = jnp.tile(jnp.concatenate([cos, cos], axis=1), (1, reps))
    sina_t = jnp.tile(jnp.concatenate([-sin, zero], axis=1), (1, reps))
    sinb_t = jnp.tile(jnp.concatenate([zero, sin], axis=1), (1, reps))
    return cos_t, sina_t, sinb_t


def kernel(x, c, w_ada, b_ada, w_in, b_forget, b_gate, w_br_fox, w_br_dil, w_o, ln_g, ln_b, w_router, router_bias,
           w_exp_gate, w_exp_up, w_exp_down, w_sh_gate, w_sh_up, w_sh_down):
    bsz, seq, d = x.shape
    depth = w_ada.shape[0]
    hw = HEADS_WIDTH

    mod = _ada(c, w_ada, b_ada)

    o_f, o_d, o_g = 3 * hw, 3 * hw + N_HEADS, 6 * hw + N_HEADS
    wfox = w_in[:, :, :o_f].astype(BF16)
    wfg = jnp.pad(w_in[:, :, o_f:o_d], ((0, 0), (0, 0), (0, LANES - N_HEADS))).astype(BF16)
    wdil = w_in[:, :, o_d:o_g].astype(BF16)
    wgate = w_in[:, :, o_g:].astype(BF16)
    bfg = jnp.pad(b_forget, ((0, 0), (0, LANES - N_HEADS))).reshape(depth, 1, LANES)
    bgate = b_gate.reshape(depth, 1, 2 * d)
    wr_t = jnp.swapaxes(w_router, 1, 2)
    wr_hi = wr_t.astype(BF16)
    wr_lo = (wr_t - wr_hi.astype(F32)).astype(BF16)
    rbias = router_bias.reshape(depth, N_EXPERTS, 1)
    wgu = jnp.concatenate([w_exp_gate, w_exp_up], axis=-1).astype(BF16)
    wdt = jnp.swapaxes(w_exp_down, 2, 3).astype(BF16)
    wsgu = jnp.concatenate([w_sh_gate, w_sh_up], axis=-1).astype(BF16)
    wsd = w_sh_down.astype(BF16)
    cos_t, sina_t, sinb_t = _rope_tables(seq)

    layer_params = (mod, wfox, wdil, wgate, wfg, bfg, bgate, w_br_fox.astype(BF16), w_br_dil.astype(BF16),
                    w_o.astype(BF16), ln_g, ln_b, wr_hi, wr_lo, rbias, wgu, wdt, wsgu, wsd)

    def layer(xc, prm):
        (mod_l, wfox_l, wdil_l, wgate_l, wfg_l, bfg_l, bgate_l, wbf_l, wbd_l, wo_l, lng_l, lnb_l,
         wrh_l, wrl_l, rb_l, wgu_l, wdt_l, wsgu_l, wsd_l) = prm
        mod3 = mod_l.reshape(bsz, 1, 6 * d)
        qf, kf, vf, qd, kd, vd, gf, gd, cum, cumt = _inproj(
            xc, mod3, wfox_l, wdil_l, wgate_l, wfg_l, bfg_l, bgate_l, cos_t, sina_t, sinb_t)
        yf = _fox(qf, kf, vf, cum, cumt)
        dil = [_dil(qd, kd, vd, dilation) for _, dilation in DIL_PATTERNS]
        x1 = _mix(xc, yf, [o for o, _ in dil], [l for _, l in dil], gf, gd, mod3, wbf_l, wbd_l, wo_l,
                  lng_l[0:1], lnb_l[0:1])
        u2, rank, gw, cnt = _router(x1, mod3, wrh_l, wrl_l, rb_l)
        yr = _moe(u2, rank, gw, cnt, wgu_l, wdt_l)
        x2 = _ffn_out(x1, u2, yr, mod3, wsgu_l, wsd_l, lng_l[1:2], lnb_l[1:2])
        return x2, None

    out, _ = lax.scan(layer, x, layer_params)
    return out
```

```python
import functools

import jax
import jax.numpy as jnp
from jax import lax
from jax.experimental import pallas as pl
from jax.experimental.pallas import tpu as pltpu
from jax.experimental.pallas import tpu_sc as plsc

D_MODEL = 1024
DEPTH = 4
HEAD_DIM = 64
N_HEADS = 8
HEADS_WIDTH = N_HEADS * HEAD_DIM
DIL_PATTERNS = ((128, 1), (512, 4), (2048, 16))
ATTN_BLOCK = 128
ROPE_THETA = 10000.0
N_EXPERTS = 64
TOP_K = 8
N_GROUPS = 8
GROUP_SIZE = N_EXPERTS // N_GROUPS
TOPK_GROUPS = 4
EXPERT_FF = 256
SHARED_FF = 256
ROUTED_SCALE = 2.5
DN_ALPHA = (2 * DEPTH) ** 0.25
LN_EPS = 1e-5
NEG_INF = -1e30
SCORE_SCALE = HEAD_DIM ** -0.5

LANES = 128
PAIR = 2 * HEAD_DIM
N_PAIRS = HEADS_WIDTH // PAIR

ROW_TILE = 512
FOX_BLOCK = 256
ROUTE_CHUNK = 512
EXPERT_BLOCK = 512
PACKED = D_MODEL // 2

SC_CORES = 2
SC_SUBCORES = 16
SC_WORKERS = SC_CORES * SC_SUBCORES
DISPATCH_CHUNK = 128
GATHER_CHUNK = 64

BF16 = jnp.bfloat16
F32 = jnp.float32


def _dot(a, b):
    return jnp.dot(a, b, preferred_element_type=F32)


def _dot_nt(a, b):
    return lax.dot_general(a, b, (((1,), (1,)), ((), ())), preferred_element_type=F32)


def _split2(x):
    hi = x.astype(BF16)
    lo = (x - hi.astype(F32)).astype(BF16)
    return hi, lo


def _split3(x):
    hi = x.astype(BF16)
    r = x - hi.astype(F32)
    mid = r.astype(BF16)
    lo = (r - mid.astype(F32)).astype(BF16)
    return hi, mid, lo


def _sigmoid(x):
    return 1.0 / (1.0 + jnp.exp(-x))


def _layer_norm(z, g, b):
    mu = jnp.mean(z, axis=-1, keepdims=True)
    zc = z - mu
    var = jnp.mean(zc * zc, axis=-1, keepdims=True)
    return zc * lax.rsqrt(var + LN_EPS) * g + b


def _pack_words(v):
    half = v.shape[1] // 2
    bits = pltpu.bitcast(v.astype(BF16).astype(F32), jnp.uint32)
    word = (bits[:, half:] & jnp.uint32(0xFFFF0000)) | (bits[:, :half] >> 16)
    return pltpu.bitcast(word, F32)


def _unpack_words(w):
    bits = pltpu.bitcast(w, jnp.uint32)
    return pltpu.bitcast(bits << 16, F32), pltpu.bitcast(bits & jnp.uint32(0xFFFF0000), F32)


def _params(sem, vmem_mb):
    return pltpu.CompilerParams(dimension_semantics=sem, vmem_limit_bytes=vmem_mb * 1024 * 1024)


def _ada_kernel(c_ref, w_ref, b_ref, o_ref):
    c = c_ref[...]
    cs = c * _sigmoid(c)
    c_hi, c_lo = _split2(cs)
    w_hi, w_lo = _split2(w_ref[0])
    o_ref[0] = _dot(c_hi, w_hi) + _dot(c_hi, w_lo) + _dot(c_lo, w_hi) + b_ref[0]


def _ada(c, w_ada, b_ada):
    depth, d, n = w_ada.shape
    bsz = c.shape[0]
    tn = 1536
    return pl.pallas_call(
        _ada_kernel,
        grid=(depth, n // tn),
        in_specs=[
            pl.BlockSpec((bsz, d), lambda l, j: (0, 0)),
            pl.BlockSpec((1, d, tn), lambda l, j: (l, 0, j)),
            pl.BlockSpec((1, 1, tn), lambda l, j: (l, 0, j)),
        ],
        out_specs=pl.BlockSpec((1, bsz, tn), lambda l, j: (l, 0, j)),
        out_shape=jax.ShapeDtypeStruct((depth, bsz, n), F32),
        compiler_params=_params(("arbitrary", "arbitrary"), 40),
        name="ada",
    )(c, w_ada, b_ada.reshape(depth, 1, n))


def _inproj_kernel(x_ref, sc_ref, sh_ref, wfox_ref, wdil_ref, wgate_ref, wfg_ref, bfg_ref, bgate_ref,
                   cos_ref, sina_ref, sinb_ref,
                   qf_ref, kf_ref, vf_ref, qd_ref, kd_ref, vd_ref, gf_ref, gd_ref, cum_ref, cumt_ref,
                   carry_ref):
    j = pl.program_id(1)
    tm = x_ref.shape[1]
    u = (x_ref[0] * (1.0 + sc_ref[0]) + sh_ref[0]).astype(BF16)

    pf = _dot(u, wfox_ref[...])
    qf_ref[0] = pf[:, :HEADS_WIDTH].astype(BF16)
    kf_ref[0] = pf[:, HEADS_WIDTH:2 * HEADS_WIDTH].astype(BF16)
    vf_ref[0] = pf[:, 2 * HEADS_WIDTH:].astype(BF16)

    pd = _dot(u, wdil_ref[...])
    rows = pl.ds(pl.multiple_of(j * tm, tm), tm)
    cos = cos_ref[rows, :]
    sina = sina_ref[rows, :]
    sinb = sinb_ref[rows, :]
    for dst, off in ((qd_ref, 0), (kd_ref, HEADS_WIDTH)):
        for g in range(N_PAIRS):
            xg = pd[:, off + g * LANES: off + (g + 1) * LANES]
            ahead = pltpu.roll(xg, LANES - HEAD_DIM // 2, 1)
            behind = pltpu.roll(xg, HEAD_DIM // 2, 1)
            dst[0, :, g * LANES:(g + 1) * LANES] = (xg * cos + ahead * sina + behind * sinb).astype(BF16)
    vd_ref[0] = pd[:, 2 * HEADS_WIDTH:].astype(BF16)

    pg = _sigmoid(_dot(u, wgate_ref[...]) + bgate_ref[...])
    gf_ref[0] = pg[:, :D_MODEL].astype(BF16)
    gd_ref[0] = pg[:, D_MODEL:].astype(BF16)

    ff = _dot(u, wfg_ref[...]) + bfg_ref[...]
    lf = jnp.minimum(ff, 0.0) - jnp.log(1.0 + jnp.exp(-jnp.abs(ff)))
    r = lax.broadcasted_iota(jnp.int32, (tm, tm), 0)
    c = lax.broadcasted_iota(jnp.int32, (tm, tm), 1)
    tri = jnp.where(r >= c, 1.0, 0.0).astype(BF16)
    hi, mid, lo = _split3(lf)
    csum = _dot(tri, hi) + _dot(tri, mid) + _dot(tri, lo)

    @pl.when(j == 0)
    def _():
        carry_ref[...] = jnp.zeros_like(carry_ref)

    cum = csum + carry_ref[...]
    carry_ref[...] = cum[tm - 1:tm, :]
    cum_ref[0] = cum[:, :N_HEADS]
    cumt_ref[0] = cum.T[:N_HEADS, :]


def _inproj(x, mod3, wfox, wdil, wgate, wfg, bfg, bgate, cos_t, sina_t, sinb_t):
    bsz, seq, d = x.shape
    tm = ROW_TILE
    const = lambda shape: pl.BlockSpec(shape, lambda b, j: (0,) * len(shape))
    tok = lambda w: pl.BlockSpec((1, tm, w), lambda b, j: (b, j, 0))
    hw = HEADS_WIDTH
    outs = [jax.ShapeDtypeStruct((bsz, seq, hw), BF16)] * 6 + [jax.ShapeDtypeStruct((bsz, seq, d), BF16)] * 2 + [
        jax.ShapeDtypeStruct((bsz, seq, N_HEADS), F32), jax.ShapeDtypeStruct((bsz, N_HEADS, seq), F32)]
    return pl.pallas_call(
        _inproj_kernel,
        grid=(bsz, seq // tm),
        in_specs=[
            tok(d),
            pl.BlockSpec((1, 1, d), lambda b, j: (b, 0, 1)),
            pl.BlockSpec((1, 1, d), lambda b, j: (b, 0, 0)),
            const(wfox.shape), const(wdil.shape), const(wgate.shape), const(wfg.shape),
            const(bfg.shape), const(bgate.shape),
            const(cos_t.shape), const(sina_t.shape), const(sinb_t.shape),
        ],
        out_specs=[tok(hw)] * 6 + [tok(d)] * 2 + [
            pl.BlockSpec((1, tm, N_HEADS), lambda b, j: (b, j, 0)),
            pl.BlockSpec((1, N_HEADS, tm), lambda b, j: (b, 0, j))],
        out_shape=outs,
        scratch_shapes=[pltpu.VMEM((1, LANES), F32)],
        compiler_params=_params(("arbitrary", "arbitrary"), 56),
        name="inproj",
    )(x, mod3, mod3, wfox, wdil, wgate, wfg, bfg, bgate, cos_t, sina_t, sinb_t)


def _fox_kernel(q_ref, k_ref, v_ref, cum_ref, cumt_ref, o_ref):
    seq = q_ref.shape[1]
    tq = FOX_BLOCK
    lane = lax.broadcasted_iota(jnp.int32, (1, PAIR), 1)
    first = lane < HEAD_DIM
    row = lax.broadcasted_iota(jnp.int32, (tq, tq), 0)
    col = lax.broadcasted_iota(jnp.int32, (tq, tq), 1)
    causal = row >= col

    for p in range(N_PAIRS):
        ls = slice(p * PAIR, (p + 1) * PAIR)

        def q_body(qi, _, p=p, ls=ls):
            q0 = pl.multiple_of(qi * tq, tq)
            qb = q_ref[0, pl.ds(q0, tq), ls]
            zero = jnp.zeros_like(qb)
            qs = (jnp.where(first, qb, zero), jnp.where(first, zero, qb))
            cqs = tuple(cum_ref[0, pl.ds(q0, tq), 2 * p + hh:2 * p + hh + 1] for hh in range(2))

            def kv_step(k0, carry, diag):
                kb = k_ref[0, pl.ds(k0, tq), ls]
                vb = v_ref[0, pl.ds(k0, tq), ls]
                out = []
                for hh in range(2):
                    m, l, acc = carry[hh]
                    ck = cumt_ref[0, 2 * p + hh:2 * p + hh + 1, pl.ds(k0, tq)]
                    s = _dot_nt(qs[hh], kb) * SCORE_SCALE + (cqs[hh] - ck)
                    if diag:
                        s = jnp.where(causal, s, NEG_INF)
                    m_new = jnp.maximum(m, jnp.max(s, axis=-1, keepdims=True))
                    a = jnp.exp(m - m_new)
                    e = jnp.exp(s - m_new)
                    l = a * l + jnp.sum(e, axis=-1, keepdims=True)
                    acc = a * acc + _dot(e.astype(BF16), vb)
                    out.append((m_new, l, acc))
                return tuple(out)

            init = tuple((jnp.full((tq, 1), NEG_INF, F32), jnp.zeros((tq, 1), F32), jnp.zeros((tq, PAIR), F32))
                         for _ in range(2))
            carry = lax.fori_loop(
                0, qi, lambda jb, cr: kv_step(pl.multiple_of(jb * tq, tq), cr, False), init)
            (_, l0, a0), (_, l1, a1) = kv_step(q0, carry, True)
            o_ref[0, pl.ds(q0, tq), ls] = jnp.where(first, a0 / l0, a1 / l1).astype(BF16)
            return 0

        lax.fori_loop(0, seq // tq, q_body, 0)


def _fox(qf, kf, vf, cum, cumt):
    bsz, seq, hw = qf.shape
    full = pl.BlockSpec((1, seq, hw), lambda b: (b, 0, 0))
    return pl.pallas_call(
        _fox_kernel,
        grid=(bsz,),
        in_specs=[full, full, full,
                  pl.BlockSpec((1, seq, N_HEADS), lambda b: (b, 0, 0)),
                  pl.BlockSpec((1, N_HEADS, seq), lambda b: (b, 0, 0))],
        out_specs=full,
        out_shape=jax.ShapeDtypeStruct((bsz, seq, hw), BF16),
        compiler_params=_params(("arbitrary",), 40),
        name="fox",
    )(qf, kf, vf, cum, cumt)


def _dil_kernel(q_ref, kc_ref, kp_ref, vc_ref, vp_ref, o_ref, lse_ref):
    n = pl.program_id(2)
    blk = ATTN_BLOCK
    lane = lax.broadcasted_iota(jnp.int32, (1, PAIR), 1)
    first = lane < HEAD_DIM
    i = lax.broadcasted_iota(jnp.int32, (blk, 2 * blk), 0)
    c = lax.broadcasted_iota(jnp.int32, (blk, 2 * blk), 1)
    valid = (c >= i) & (c <= i + blk) & ((c >= blk) | (n > 0))

    for p in range(N_PAIRS):
        ls = slice(p * PAIR, (p + 1) * PAIR)
        qb = q_ref[0, :, ls]
        zero = jnp.zeros_like(qb)
        kb = jnp.concatenate([kp_ref[0, :, ls], kc_ref[0, :, ls]], axis=0)
        vb = jnp.concatenate([vp_ref[0, :, ls], vc_ref[0, :, ls]], axis=0)
        res = []
        for hh in range(2):
            qh = jnp.where(first, qb, zero) if hh == 0 else jnp.where(first, zero, qb)
            s = jnp.where(valid, _dot_nt(qh, kb) * SCORE_SCALE, NEG_INF)
            m = jnp.max(s, axis=-1, keepdims=True)
            e = jnp.exp(s - m)
            l = jnp.sum(e, axis=-1, keepdims=True)
            o = _dot(e.astype(BF16), vb) / l
            res.append((o, m + jnp.log(l)))
        o_ref[0, :, ls] = jnp.where(first, res[0][0], res[1][0]).astype(BF16)
        lse_ref[0, :, ls] = jnp.where(first, res[0][1], res[1][1])


def _dil(qd, kd, vd, dilation):
    bsz, seq, hw = qd.shape
    length = seq // dilation
    nb = length // ATTN_BLOCK
    view = lambda t: t.reshape(bsz, length, dilation * hw)
    cur = pl.BlockSpec((1, ATTN_BLOCK, hw), lambda b, r, n: (b, n, r))
    prev = pl.BlockSpec((1, ATTN_BLOCK, hw), lambda b, r, n: (b, jnp.maximum(n - 1, 0), r))
    o, lse = pl.pallas_call(
        _dil_kernel,
        grid=(bsz, dilation, nb),
        in_specs=[cur, cur, prev, cur, prev],
        out_specs=[cur, cur],
        out_shape=[jax.ShapeDtypeStruct((bsz, length, dilation * hw), BF16),
                   jax.ShapeDtypeStruct((bsz, length, dilation * hw), F32)],
        compiler_params=_params(("arbitrary", "arbitrary", "arbitrary"), 32),
        name=f"dil{dilation}",
    )(view(qd), view(kd), view(kd), view(vd), view(vd))
    return o.reshape(bsz, seq, hw), lse.reshape(bsz, seq, hw)


def _mix_kernel(x_ref, yf_ref, o1_ref, o2_ref, o3_ref, l1_ref, l2_ref, l3_ref, gf_ref, gd_ref, g1_ref,
                wbf_ref, wbd_ref, wo_ref, lng_ref, lnb_ref, out_ref):
    l1, l2, l3 = l1_ref[0], l2_ref[0], l3_ref[0]
    mx = jnp.maximum(jnp.maximum(l1, l2), l3)
    e1, e2, e3 = jnp.exp(l1 - mx), jnp.exp(l2 - mx), jnp.exp(l3 - mx)
    yd = (e1 * o1_ref[0].astype(F32) + e2 * o2_ref[0].astype(F32) + e3 * o3_ref[0].astype(F32)) / (e1 + e2 + e3)
    merged = (gf_ref[0].astype(F32) * _dot(yf_ref[0], wbf_ref[...])
              + gd_ref[0].astype(F32) * _dot(yd.astype(BF16), wbd_ref[...]))
    y = _dot(merged.astype(BF16), wo_ref[...])
    z = DN_ALPHA * x_ref[0] + (1.0 + g1_ref[0]) * y
    out_ref[0] = _layer_norm(z, lng_ref[...], lnb_ref[...])


def _mix(x, yf, os_, lses, gf, gd, mod3, wbf, wbd, wo, lng, lnb):
    bsz, seq, d = x.shape
    tm = ROW_TILE
    const = lambda shape: pl.BlockSpec(shape, lambda b, j: (0,) * len(shape))
    tok = lambda w: pl.BlockSpec((1, tm, w), lambda b, j: (b, j, 0))
    hw = HEADS_WIDTH
    return pl.pallas_call(
        _mix_kernel,
        grid=(bsz, seq // tm),
        in_specs=[tok(d)] + [tok(hw)] * 7 + [tok(d), tok(d),
                  pl.BlockSpec((1, 1, d), lambda b, j: (b, 0, 2)),
                  const(wbf.shape), const(wbd.shape), const(wo.shape), const(lng.shape), const(lnb.shape)],
        out_specs=tok(d),
        out_shape=jax.ShapeDtypeStruct((bsz, seq, d), F32),
        compiler_params=_params(("arbitrary", "arbitrary"), 48),
        name="mix",
    )(x, yf, *os_, *lses, gf, gd, mod3, wbf, wbd, wo, lng, lnb)


def _router_kernel(x_ref, sc_ref, sh_ref, wrh_ref, wrl_ref, bias_ref,
                   u_ref, up_ref, eidx_ref, rank_ref, gw_ref, cnt_ref):
    seq = x_ref.shape[1]
    tc = ROUTE_CHUNK
    e_iota = lax.broadcasted_iota(jnp.int32, (N_EXPERTS, tc), 0)
    g_iota = lax.broadcasted_iota(jnp.int32, (N_GROUPS, tc), 0)
    s_iota = lax.broadcasted_iota(jnp.int32, (GROUP_SIZE, tc), 0)
    r = lax.broadcasted_iota(jnp.int32, (tc, tc), 0)
    c = lax.broadcasted_iota(jnp.int32, (tc, tc), 1)
    upper = jnp.where(r <= c, 1.0, 0.0).astype(BF16)
    ones = jnp.ones((16, tc), BF16)
    carry = jnp.zeros((N_EXPERTS, 1), F32)
    cnt = jnp.zeros((16, N_EXPERTS), F32)

    for ch in range(seq // tc):
        rows = slice(ch * tc, (ch + 1) * tc)
        u = x_ref[0, rows, :] * (1.0 + sc_ref[0]) + sh_ref[0]
        u_ref[0, rows, :] = u.astype(BF16)
        up_ref[0, rows, :] = _pack_words(u)
        u_hi, u_lo = _split2(u)
        logits = _dot_nt(wrh_ref[...], u_hi) + _dot_nt(wrh_ref[...], u_lo) + _dot_nt(wrl_ref[...], u_hi)
        scores = _sigmoid(logits)
        sel = scores + bias_ref[...]

        gs_rows = []
        for g in range(N_GROUPS):
            blk = sel[g * GROUP_SIZE:(g + 1) * GROUP_SIZE]
            m1 = jnp.max(blk, axis=0, keepdims=True)
            i1 = jnp.min(jnp.where(blk == m1, s_iota, GROUP_SIZE), axis=0, keepdims=True)
            m2 = jnp.max(jnp.where(s_iota == i1, -jnp.inf, blk), axis=0, keepdims=True)
            gs_rows.append(m1 + m2)
        gs = jnp.concatenate(gs_rows, axis=0)
        beaten = jnp.zeros((N_GROUPS, tc), F32)
        for g in range(N_GROUPS):
            other = gs[g:g + 1]
            wins = (other > gs) | ((other == gs) & (g_iota > g))
            beaten = beaten + jnp.where(wins, 1.0, 0.0)
        keep = beaten < TOPK_GROUPS
        cand = jnp.concatenate(
            [jnp.where(keep[g:g + 1], sel[g * GROUP_SIZE:(g + 1) * GROUP_SIZE], NEG_INF) for g in range(N_GROUPS)],
            axis=0)
        chosen = jnp.zeros((N_EXPERTS, tc), F32)
        picks = []
        for _ in range(TOP_K):
            m = jnp.max(cand, axis=0, keepdims=True)
            idx = jnp.min(jnp.where(cand == m, e_iota, N_EXPERTS), axis=0, keepdims=True)
            hit = e_iota == idx
            chosen = jnp.where(hit, 1.0, chosen)
            cand = jnp.where(hit, -jnp.inf, cand)
            picks.append(idx)
        g_raw = jnp.where(chosen > 0.0, scores, 0.0)
        gate = g_raw / jnp.sum(g_raw, axis=0, keepdims=True) * ROUTED_SCALE

        chosen_b = chosen.astype(BF16)
        incl = _dot(chosen_b, upper) + carry
        carry = incl[:, tc - 1:tc]
        cnt = cnt + _dot_nt(ones, chosen_b)

        rank_rows, gate_rows = [], []
        for idx in picks:
            hit = e_iota == idx
            rank_rows.append(jnp.sum(jnp.where(hit, incl - 1.0, 0.0), axis=0, keepdims=True))
            gate_rows.append(jnp.sum(jnp.where(hit, gate, 0.0), axis=0, keepdims=True))
        eidx_ref[0, :, rows] = jnp.concatenate(picks, axis=0)
        rank_ref[0, :, rows] = jnp.concatenate(rank_rows, axis=0).astype(jnp.int32)
        gate_t = jnp.concatenate(gate_rows + [jnp.zeros((LANES - TOP_K, tc), F32)], axis=0).T
        gw_ref[0, rows, :] = gate_t[:, :TOP_K]

    cnt_ref[0] = cnt[0:1].astype(jnp.int32)


def _router(x1, mod3, wr_hi, wr_lo, bias):
    bsz, seq, d = x1.shape
    full = pl.BlockSpec((1, seq, d), lambda b: (b, 0, 0))
    ks = pl.BlockSpec((1, TOP_K, seq), lambda b: (b, 0, 0))
    return pl.pallas_call(
        _router_kernel,
        grid=(bsz,),
        in_specs=[full,
                  pl.BlockSpec((1, 1, d), lambda b: (b, 0, 4)),
                  pl.BlockSpec((1, 1, d), lambda b: (b, 0, 3)),
                  pl.BlockSpec(wr_hi.shape, lambda b: (0, 0)),
                  pl.BlockSpec(wr_lo.shape, lambda b: (0, 0)),
                  pl.BlockSpec(bias.shape, lambda b: (0, 0))],
        out_specs=[full, pl.BlockSpec((1, seq, PACKED), lambda b: (b, 0, 0)), ks, ks,
                   pl.BlockSpec((1, seq, TOP_K), lambda b: (b, 0, 0)),
                   pl.BlockSpec((1, 1, N_EXPERTS), lambda b: (b, 0, 0))],
        out_shape=[jax.ShapeDtypeStruct((bsz, seq, d), BF16),
                   jax.ShapeDtypeStruct((bsz, seq, PACKED), F32),
                   jax.ShapeDtypeStruct((bsz, TOP_K, seq), jnp.int32),
                   jax.ShapeDtypeStruct((bsz, TOP_K, seq), jnp.int32),
                   jax.ShapeDtypeStruct((bsz, seq, TOP_K), F32),
                   jax.ShapeDtypeStruct((bsz, 1, N_EXPERTS), jnp.int32)],
        compiler_params=_params(("arbitrary",), 48),
        name="router",
    )(x1, mod3, mod3, wr_hi, wr_lo, bias)


def _slot_kernel(base_ref, eidx_ref, rank_ref, slot_ref):
    b = pl.program_id(0)
    eidx = eidx_ref[0]
    start = jnp.zeros(eidx.shape, jnp.int32)
    for e in range(N_EXPERTS):
        start = jnp.where(eidx == e, base_ref[b * N_EXPERTS + e], start)
    slot_ref[0] = start + rank_ref[0]


def _slots(base, eidx, rank):
    bsz, _, seq = eidx.shape
    ks = pl.BlockSpec((1, TOP_K, seq), lambda b, c: (b, 0, 0))
    return pl.pallas_call(
        _slot_kernel,
        grid_spec=pltpu.PrefetchScalarGridSpec(num_scalar_prefetch=1, grid=(bsz,), in_specs=[ks, ks], out_specs=ks),
        out_shape=jax.ShapeDtypeStruct((bsz, TOP_K, seq), jnp.int32),
        compiler_params=_params(("arbitrary",), 16),
        name="slots",
    )(base.reshape(bsz * N_EXPERTS), eidx, rank)


def _sc_mesh():
    return plsc.VectorSubcoreMesh(core_axis_name="core", subcore_axis_name="subcore")


def _sc_worker():
    return lax.axis_index("subcore") * SC_CORES + lax.axis_index("core")


def _sc_dispatch(rows, slot, n_slots):
    n_tok, width = rows.shape
    seq = slot.shape[1]
    chunk = DISPATCH_CHUNK
    per_worker = n_tok // SC_WORKERS
    assert per_worker % chunk == 0 and seq % chunk == 0

    @functools.partial(
        pl.kernel, mesh=_sc_mesh(),
        out_type=jax.ShapeDtypeStruct((n_slots, width), rows.dtype),
        scratch_types=[pltpu.VMEM((TOP_K, chunk), jnp.int32), pltpu.VMEM((chunk, width), rows.dtype),
                       pltpu.SemaphoreType.DMA],
    )
    def dispatch(rows_hbm, slot_hbm, out_hbm, idx_v, rows_v, sem):
        base = _sc_worker() * per_worker

        @pl.loop(0, per_worker // chunk)
        def _(i):
            off = base + i * chunk
            b = off // seq
            n0 = off - b * seq
            pltpu.sync_copy(slot_hbm.at[pl.ds(b * TOP_K, TOP_K), pl.ds(n0, chunk)], idx_v)
            pltpu.sync_copy(rows_hbm.at[pl.ds(off, chunk)], rows_v)
            copies = [pltpu.async_copy(rows_v, out_hbm.at[idx_v.at[k]], sem) for k in range(TOP_K)]
            for cp in copies:
                cp.wait()

    return dispatch(rows, slot)


def _sc_gather(table, idx):
    n_out = idx.shape[0]
    width = table.shape[1]
    chunk = GATHER_CHUNK
    per_worker = n_out // SC_WORKERS
    steps = per_worker // chunk
    assert per_worker % chunk == 0 and steps % 2 == 0

    @functools.partial(
        pl.kernel, mesh=_sc_mesh(),
        out_type=jax.ShapeDtypeStruct((n_out, width), table.dtype),
        scratch_types=[pltpu.VMEM((2, chunk), jnp.int32), pltpu.VMEM((2, chunk, width), table.dtype),
                       pltpu.SemaphoreType.DMA, pltpu.SemaphoreType.DMA((2,))],
    )
    def gather(table_hbm, idx_hbm, out_hbm, idx_v, rows_v, gather_sem, out_sems):
        base = _sc_worker() * per_worker

        def write_out(buf, off):
            return pltpu.make_async_copy(rows_v.at[buf], out_hbm.at[pl.ds(off, chunk)], out_sems.at[buf])

        @pl.loop(0, steps, step=2)
        def _(i):
            for buf in range(2):
                off = base + (i + buf) * chunk

                @pl.when(i >= 2)
                def _():
                    write_out(buf, off - 2 * chunk).wait()

                pltpu.sync_copy(idx_hbm.at[pl.ds(off, chunk)], idx_v.at[buf])
                pltpu.async_copy(table_hbm.at[idx_v.at[buf]], rows_v.at[buf], gather_sem).wait()
                write_out(buf, off).start()

        for buf in range(2):
            write_out(buf, base + (steps - 2 + buf) * chunk).wait()

    return gather(table, idx)


def _gmm_kernel(blk_e_ref, used_ref, xs_ref, wgu_ref, wd_ref, out_ref):
    half = PACKED

    @pl.when(pl.program_id(0) < used_ref[0])
    def _():
        lo, hi = _unpack_words(xs_ref[...])
        hgu = _dot(lo.astype(BF16), wgu_ref[0, :half, :]) + _dot(hi.astype(BF16), wgu_ref[0, half:, :])
        hg = hgu[:, :EXPERT_FF]
        h = (hg * _sigmoid(hg) * hgu[:, EXPERT_FF:]).astype(BF16)
        out_ref[...] = _pack_words(_dot(h, wd_ref[0]))


def _gmm(xs, blk_e, used, wgu, wd):
    n_slots, width = xs.shape
    bm = EXPERT_BLOCK
    d = wgu.shape[1]
    live = lambda i, blk_e, used: (jnp.minimum(i, used[0] - 1), 0)
    return pl.pallas_call(
        _gmm_kernel,
        grid_spec=pltpu.PrefetchScalarGridSpec(
            num_scalar_prefetch=2,
            grid=(n_slots // bm,),
            in_specs=[pl.BlockSpec((bm, width), live),
                      pl.BlockSpec((1, d, 2 * EXPERT_FF), lambda i, blk_e, used: (blk_e[i], 0, 0)),
                      pl.BlockSpec((1, EXPERT_FF, d), lambda i, blk_e, used: (blk_e[i], 0, 0))],
            out_specs=pl.BlockSpec((bm, width), live)),
        out_shape=jax.ShapeDtypeStruct((n_slots, width), F32),
        compiler_params=_params(("arbitrary",), 32),
        name="gmm",
    )(blk_e, used, xs, wgu, wd)


def _ffn_out_kernel(x_ref, u_ref, yg_ref, gw_ref, g2_ref, wsgu_ref, wsd_ref, lng_ref, lnb_ref, out_ref):
    hgu = _dot(u_ref[0], wsgu_ref[...])
    hg = hgu[:, :SHARED_FF]
    h = (hg * _sigmoid(hg) * hgu[:, SHARED_FF:]).astype(BF16)
    shared = _dot(h, wsd_ref[...])
    gw = gw_ref[0]
    acc_lo = shared[:, :PACKED]
    acc_hi = shared[:, PACKED:]
    for k in range(TOP_K):
        lo, hi = _unpack_words(yg_ref[0, k])
        w = gw[:, k:k + 1]
        acc_lo = acc_lo + w * lo
        acc_hi = acc_hi + w * hi
    y = jnp.concatenate([acc_lo, acc_hi], axis=1)
    z = DN_ALPHA * x_ref[0] + (1.0 + g2_ref[0]) * y
    out_ref[0] = _layer_norm(z, lng_ref[...], lnb_ref[...])


def _ffn_out(x1, u2, yg, gw, mod3, wsgu, wsd, lng, lnb):
    bsz, seq, d = x1.shape
    tm = ROW_TILE
    const = lambda shape: pl.BlockSpec(shape, lambda b, j: (0,) * len(shape))
    tok = pl.BlockSpec((1, tm, d), lambda b, j: (b, j, 0))
    return pl.pallas_call(
        _ffn_out_kernel,
        grid=(bsz, seq // tm),
        in_specs=[tok, tok,
                  pl.BlockSpec((1, TOP_K, tm, PACKED), lambda b, j: (b, 0, j, 0)),
                  pl.BlockSpec((1, tm, TOP_K), lambda b, j: (b, j, 0)),
                  pl.BlockSpec((1, 1, d), lambda b, j: (b, 0, 5)),
                  const(wsgu.shape), const(wsd.shape), const(lng.shape), const(lnb.shape)],
        out_specs=tok,
        out_shape=jax.ShapeDtypeStruct((bsz, seq, d), F32),
        compiler_params=_params(("arbitrary", "arbitrary"), 48),
        name="ffn_out",
    )(x1, u2, yg, gw, mod3, wsgu, wsd, lng, lnb)


def _moe_layout(cnt, n_blocks):
    bm = EXPERT_BLOCK
    total = jnp.sum(cnt, axis=0)
    padded = (total + bm - 1) // bm * bm
    ends = jnp.cumsum(padded)
    base = (ends - padded)[None, :] + jnp.cumsum(cnt, axis=0) - cnt
    blk_e = jnp.minimum(jnp.searchsorted(ends, jnp.arange(n_blocks) * bm, side="right"), N_EXPERTS - 1)
    used = (ends[-1:] // bm)
    return base.astype(jnp.int32), blk_e.astype(jnp.int32), used.astype(jnp.int32)


def _rope_tables(seq):
    half = HEAD_DIM // 2
    inv_freq = ROPE_THETA ** (-jnp.arange(half, dtype=F32) * 2.0 / HEAD_DIM)
    ang = jnp.arange(seq, dtype=F32)[:, None] * inv_freq[None, :]
    cos, sin = jnp.cos(ang), jnp.sin(ang)
    zero = jnp.zeros_like(sin)
    reps = LANES // HEAD_DIM
    cos_t = jnp.tile(jnp.concatenate([cos, cos], axis=1), (1, reps))
    sina_t = jnp.tile(jnp.concatenate([-sin, zero], axis=1), (1, reps))
    sinb_t = jnp.tile(jnp.concatenate([zero, sin], axis=1), (1, reps))
    return cos_t, sina_t, sinb_t


def kernel(x, c, w_ada, b_ada, w_in, b_forget, b_gate, w_br_fox, w_br_dil, w_o, ln_g, ln_b, w_router, router_bias,
           w_exp_gate, w_exp_up, w_exp_down, w_sh_gate, w_sh_up, w_sh_down):
    bsz, seq, d = x.shape
    depth = w_ada.shape[0]
    hw = HEADS_WIDTH
    n_tok = bsz * seq
    n_blocks = n_tok * TOP_K // EXPERT_BLOCK + N_EXPERTS

    mod = _ada(c, w_ada, b_ada)

    o_f, o_d, o_g = 3 * hw, 3 * hw + N_HEADS, 6 * hw + N_HEADS
    wfox = w_in[:, :, :o_f].astype(BF16)
    wfg = jnp.pad(w_in[:, :, o_f:o_d], ((0, 0), (0, 0), (0, LANES - N_HEADS))).astype(BF16)
    wdil = w_in[:, :, o_d:o_g].astype(BF16)
    wgate = w_in[:, :, o_g:].astype(BF16)
    bfg = jnp.pad(b_forget, ((0, 0), (0, LANES - N_HEADS))).reshape(depth, 1, LANES)
    bgate = b_gate.reshape(depth, 1, 2 * d)
    wr_t = jnp.swapaxes(w_router, 1, 2)
    wr_hi = wr_t.astype(BF16)
    wr_lo = (wr_t - wr_hi.astype(F32)).astype(BF16)
    rbias = router_bias.reshape(depth, N_EXPERTS, 1)
    wgu = jnp.concatenate([w_exp_gate, w_exp_up], axis=-1).astype(BF16)
    wd = w_exp_down.astype(BF16)
    wsgu = jnp.concatenate([w_sh_gate, w_sh_up], axis=-1).astype(BF16)
    wsd = w_sh_down.astype(BF16)
    cos_t, sina_t, sinb_t = _rope_tables(seq)

    layer_params = (mod, wfox, wdil, wgate, wfg, bfg, bgate, w_br_fox.astype(BF16), w_br_dil.astype(BF16),
                    w_o.astype(BF16), ln_g, ln_b, wr_hi, wr_lo, rbias, wgu, wd, wsgu, wsd)

    def layer(xc, prm):
        (mod_l, wfox_l, wdil_l, wgate_l, wfg_l, bfg_l, bgate_l, wbf_l, wbd_l, wo_l, lng_l, lnb_l,
         wrh_l, wrl_l, rb_l, wgu_l, wd_l, wsgu_l, wsd_l) = prm
        mod3 = mod_l.reshape(bsz, 1, 6 * d)
        qf, kf, vf, qd, kd, vd, gf, gd, cum, cumt = _inproj(
            xc, mod3, wfox_l, wdil_l, wgate_l, wfg_l, bfg_l, bgate_l, cos_t, sina_t, sinb_t)
        yf = _fox(qf, kf, vf, cum, cumt)
        dil = [_dil(qd, kd, vd, dilation) for _, dilation in DIL_PATTERNS]
        x1 = _mix(xc, yf, [o for o, _ in dil], [l for _, l in dil], gf, gd, mod3, wbf_l, wbd_l, wo_l,
                  lng_l[0:1], lnb_l[0:1])
        u2, u2p, eidx, rank, gw, cnt = _router(x1, mod3, wrh_l, wrl_l, rb_l)
        base, blk_e, used = _moe_layout(cnt.reshape(bsz, N_EXPERTS), n_blocks)
        slot = _slots(base, eidx, rank)
        xs = _sc_dispatch(u2p.reshape(n_tok, PACKED), slot.reshape(bsz * TOP_K, seq), n_blocks * EXPERT_BLOCK)
        ys = _gmm(xs, blk_e, used, wgu_l, wd_l)
        yg = _sc_gather(ys, slot.reshape(n_tok * TOP_K)).reshape(bsz, TOP_K, seq, PACKED)
        x2 = _ffn_out(x1, u2, yg, gw, mod3, wsgu_l, wsd_l, lng_l[1:2], lnb_l[1:2])
        return x2, None

    out, _ = lax.scan(layer, x, layer_params)
    return out
```

```python
import functools

import jax
import jax.numpy as jnp
from jax import lax
from jax.experimental import pallas as pl
from jax.experimental.pallas import tpu as pltpu
from jax.experimental.pallas import tpu_sc as plsc

D_MODEL = 1024
DEPTH = 4
HEAD_DIM = 64
N_HEADS = 8
HEADS_WIDTH = N_HEADS * HEAD_DIM
DIL_PATTERNS = ((128, 1), (512, 4), (2048, 16))
ATTN_BLOCK = 128
ROPE_THETA = 10000.0
N_EXPERTS = 64
TOP_K = 8
N_GROUPS = 8
GROUP_SIZE = N_EXPERTS // N_GROUPS
TOPK_GROUPS = 4
EXPERT_FF = 256
SHARED_FF = 256
ROUTED_SCALE = 2.5
DN_ALPHA = (2 * DEPTH) ** 0.25
LN_EPS = 1e-5
NEG_INF = -1e30
SCORE_SCALE = HEAD_DIM ** -0.5

LANES = 128
PAIR = 2 * HEAD_DIM
N_PAIRS = HEADS_WIDTH // PAIR

ROW_TILE = 512
FOX_Q = 256
FOX_K = 128
FOX_BIAS_LANE = HEAD_DIM
ROUTE_CHUNK = 512
EXPERT_BLOCK = 512
PACKED = D_MODEL // 2

SC_CORES = 2
SC_SUBCORES = 16
SC_WORKERS = SC_CORES * SC_SUBCORES
DISPATCH_CHUNK = 128
GATHER_CHUNK = 64

BF16 = jnp.bfloat16
F32 = jnp.float32


def _dot(a, b):
    return jnp.dot(a, b, preferred_element_type=F32)


def _dot_nt(a, b):
    return lax.dot_general(a, b, (((1,), (1,)), ((), ())), preferred_element_type=F32)


def _split2(x):
    hi = x.astype(BF16)
    lo = (x - hi.astype(F32)).astype(BF16)
    return hi, lo


def _split3(x):
    hi = x.astype(BF16)
    r = x - hi.astype(F32)
    mid = r.astype(BF16)
    lo = (r - mid.astype(F32)).astype(BF16)
    return hi, mid, lo


def _sigmoid(x):
    return 1.0 / (1.0 + jnp.exp(-x))


def _layer_norm(z, g, b):
    mu = jnp.mean(z, axis=-1, keepdims=True)
    zc = z - mu
    var = jnp.mean(zc * zc, axis=-1, keepdims=True)
    return zc * lax.rsqrt(var + LN_EPS) * g + b


def _pack_words(v):
    half = v.shape[1] // 2
    bits = pltpu.bitcast(v.astype(BF16).astype(F32), jnp.uint32)
    word = (bits[:, half:] & jnp.uint32(0xFFFF0000)) | (bits[:, :half] >> 16)
    return pltpu.bitcast(word, F32)


def _unpack_words(w):
    bits = pltpu.bitcast(w, jnp.uint32)
    return pltpu.bitcast(bits << 16, F32), pltpu.bitcast(bits & jnp.uint32(0xFFFF0000), F32)


def _params(sem, vmem_mb):
    return pltpu.CompilerParams(dimension_semantics=sem, vmem_limit_bytes=vmem_mb * 1024 * 1024)


def _ada_kernel(c_ref, w_ref, b_ref, o_ref):
    c = c_ref[...]
    cs = c * _sigmoid(c)
    c_hi, c_lo = _split2(cs)
    w_hi, w_lo = _split2(w_ref[0])
    o_ref[0] = _dot(c_hi, w_hi) + _dot(c_hi, w_lo) + _dot(c_lo, w_hi) + b_ref[0]


def _ada(c, w_ada, b_ada):
    depth, d, n = w_ada.shape
    bsz = c.shape[0]
    tn = 1536
    return pl.pallas_call(
        _ada_kernel,
        grid=(depth, n // tn),
        in_specs=[
            pl.BlockSpec((bsz, d), lambda l, j: (0, 0)),
            pl.BlockSpec((1, d, tn), lambda l, j: (l, 0, j)),
            pl.BlockSpec((1, 1, tn), lambda l, j: (l, 0, j)),
        ],
        out_specs=pl.BlockSpec((1, bsz, tn), lambda l, j: (l, 0, j)),
        out_shape=jax.ShapeDtypeStruct((depth, bsz, n), F32),
        compiler_params=_params(("arbitrary", "arbitrary"), 40),
        name="ada",
    )(c, w_ada, b_ada.reshape(depth, 1, n))


def _inproj_kernel(x_ref, sc_ref, sh_ref, wfox_ref, wdil_ref, wgate_ref, wfg_ref, bfg_ref, bgate_ref,
                   cos_ref, sina_ref, sinb_ref, selq_ref, selk_ref, oneq_ref, onek_ref,
                   qa_ref, ka_ref, vt_ref, qd_ref, kd_ref, vd_ref, gf_ref, gd_ref,
                   carry_ref):
    j = pl.program_id(1)
    tm = x_ref.shape[1]
    u = (x_ref[0] * (1.0 + sc_ref[0]) + sh_ref[0]).astype(BF16)

    ff = _dot(u, wfg_ref[...]) + bfg_ref[...]
    lf = jnp.minimum(ff, 0.0) - jnp.log(1.0 + jnp.exp(-jnp.abs(ff)))
    r = lax.broadcasted_iota(jnp.int32, (tm, tm), 0)
    c = lax.broadcasted_iota(jnp.int32, (tm, tm), 1)
    tri = jnp.where(r >= c, 1.0, 0.0).astype(BF16)
    hi, mid, lo = _split3(lf)
    csum = _dot(tri, hi) + _dot(tri, mid) + _dot(tri, lo)

    @pl.when(j == 0)
    def _():
        carry_ref[...] = jnp.zeros_like(carry_ref)

    cum = csum + carry_ref[...]
    carry_ref[...] = cum[tm - 1:tm, :]

    c_hi, c_mid, c_lo = _split3(cum)
    bias_q = (_dot(c_hi, selq_ref[0]) + _dot(c_mid, selq_ref[1]) + _dot(c_lo, selq_ref[2])) + oneq_ref[...]
    bias_k = (_dot(c_hi, selk_ref[0]) + _dot(c_mid, selk_ref[1]) + _dot(c_lo, selk_ref[2])) + onek_ref[...]
    pf = _dot(u, wfox_ref[...])
    lane = lax.broadcasted_iota(jnp.int32, (1, LANES), 1)
    data = lane < HEAD_DIM
    for dst, off, bias, scale in ((qa_ref, 0, bias_q, SCORE_SCALE), (ka_ref, HEADS_WIDTH, bias_k, 1.0)):
        for p in range(N_PAIRS):
            pair = pf[:, off + p * PAIR: off + (p + 1) * PAIR] * scale
            for hh, src in ((0, pair), (1, pltpu.roll(pair, HEAD_DIM, 1))):
                t = (2 * p + hh) * LANES
                dst[0, :, t:t + LANES] = jnp.where(data, src, bias[:, t:t + LANES]).astype(BF16)
    vt = pf[:, 2 * HEADS_WIDTH:].T
    for cb in range(tm // FOX_K):
        vt_ref[0, cb] = vt[:, cb * FOX_K:(cb + 1) * FOX_K].astype(BF16)

    pd = _dot(u, wdil_ref[...])
    rows = pl.ds(pl.multiple_of(j * tm, tm), tm)
    cos = cos_ref[rows, :]
    sina = sina_ref[rows, :]
    sinb = sinb_ref[rows, :]
    for dst, off in ((qd_ref, 0), (kd_ref, HEADS_WIDTH)):
        for g in range(N_PAIRS):
            xg = pd[:, off + g * LANES: off + (g + 1) * LANES]
            ahead = pltpu.roll(xg, LANES - HEAD_DIM // 2, 1)
            behind = pltpu.roll(xg, HEAD_DIM // 2, 1)
            dst[0, :, g * LANES:(g + 1) * LANES] = (xg * cos + ahead * sina + behind * sinb).astype(BF16)
    vd_ref[0] = pd[:, 2 * HEADS_WIDTH:].astype(BF16)

    pg = _sigmoid(_dot(u, wgate_ref[...]) + bgate_ref[...])
    gf_ref[0] = pg[:, :D_MODEL].astype(BF16)
    gd_ref[0] = pg[:, D_MODEL:].astype(BF16)


def _inproj(x, mod3, wfox, wdil, wgate, wfg, bfg, bgate, cos_t, sina_t, sinb_t, selq, selk, oneq, onek):
    bsz, seq, d = x.shape
    tm = ROW_TILE
    const = lambda shape: pl.BlockSpec(shape, lambda b, j: (0,) * len(shape))
    tok = lambda w: pl.BlockSpec((1, tm, w), lambda b, j: (b, j, 0))
    hw = HEADS_WIDTH
    aw = N_HEADS * LANES
    outs = ([jax.ShapeDtypeStruct((bsz, seq, aw), BF16)] * 2
            + [jax.ShapeDtypeStruct((bsz, seq // FOX_K, hw, FOX_K), BF16)]
            + [jax.ShapeDtypeStruct((bsz, seq, hw), BF16)] * 3 + [jax.ShapeDtypeStruct((bsz, seq, d), BF16)] * 2)
    return pl.pallas_call(
        _inproj_kernel,
        grid=(bsz, seq // tm),
        in_specs=[
            tok(d),
            pl.BlockSpec((1, 1, d), lambda b, j: (b, 0, 1)),
            pl.BlockSpec((1, 1, d), lambda b, j: (b, 0, 0)),
            const(wfox.shape), const(wdil.shape), const(wgate.shape), const(wfg.shape),
            const(bfg.shape), const(bgate.shape),
            const(cos_t.shape), const(sina_t.shape), const(sinb_t.shape),
            const(selq.shape), const(selk.shape), const(oneq.shape), const(onek.shape),
        ],
        out_specs=[tok(aw)] * 2 + [pl.BlockSpec((1, tm // FOX_K, hw, FOX_K), lambda b, j: (b, j, 0, 0))]
                  + [tok(hw)] * 3 + [tok(d)] * 2,
        out_shape=outs,
        scratch_shapes=[pltpu.VMEM((1, LANES), F32)],
        compiler_params=_params(("arbitrary", "arbitrary"), 56),
        name="inproj",
    )(x, mod3, mod3, wfox, wdil, wgate, wfg, bfg, bgate, cos_t, sina_t, sinb_t, selq, selk, oneq, onek)


def _fox_kernel(qa_ref, ka_ref, vt_ref, o_ref, m_ref, l_ref, acc_ref):
    seq = qa_ref.shape[1]
    tq, tk = FOX_Q, FOX_K
    k_pos = lax.broadcasted_iota(jnp.int32, (tk, tq), 0)
    q_pos = lax.broadcasted_iota(jnp.int32, (tk, tq), 1)

    def q_body(qi, _):
        q0 = pl.multiple_of(qi * tq, tq)
        m_ref[...] = jnp.full(m_ref.shape, NEG_INF, F32)
        l_ref[...] = jnp.zeros(l_ref.shape, F32)
        acc_ref[...] = jnp.zeros(acc_ref.shape, F32)

        def kv_step(kb, k0, visible):
            scores = []
            for h in range(N_HEADS):
                ls = slice(h * LANES, (h + 1) * LANES)
                scores.append(_dot_nt(ka_ref[0, pl.ds(k0, tk), ls], qa_ref[0, pl.ds(q0, tq), ls]))
            m_all = m_ref[...]
            l_all = l_ref[...]
            probs, decay, m_rows, l_rows = [], [], [], []
            for h in range(N_HEADS):
                s = scores[h] if visible is None else jnp.where(visible, scores[h], NEG_INF)
                m_old = m_all[h:h + 1, :]
                m_new = jnp.maximum(m_old, jnp.max(s, axis=0, keepdims=True))
                a = jnp.exp(m_old - m_new)
                e = jnp.exp(s - m_new)
                l_rows.append(a * l_all[h:h + 1, :] + jnp.sum(e, axis=0, keepdims=True))
                m_rows.append(m_new)
                probs.append(e.astype(BF16))
                decay.append(a)
            m_ref[...] = jnp.concatenate(m_rows, axis=0)
            l_ref[...] = jnp.concatenate(l_rows, axis=0)
            for h in range(N_HEADS):
                vh = vt_ref[0, kb, h * HEAD_DIM:(h + 1) * HEAD_DIM, :]
                acc_ref[h] = decay[h] * acc_ref[h] + _dot(vh, probs[h])

        def full_block(jb, carry):
            kv_step(jb, pl.multiple_of(jb * tk, tk), None)
            return carry

        lax.fori_loop(0, qi * (tq // tk), full_block, 0)
        for dblk in range(tq // tk):
            kb = qi * (tq // tk) + dblk
            kv_step(kb, pl.multiple_of(kb * tk, tk), k_pos + dblk * tk <= q_pos)

        for p in range(N_PAIRS):
            pair = jnp.concatenate([acc_ref[2 * p + hh] / l_ref[2 * p + hh:2 * p + hh + 1, :] for hh in range(2)],
                                   axis=0)
            o_ref[0, pl.ds(q0, tq), p * PAIR:(p + 1) * PAIR] = pair.T.astype(BF16)
        return 0

    lax.fori_loop(0, seq // tq, q_body, 0)


def _fox(qa, ka, vt):
    bsz, seq, aw = qa.shape
    hw = HEADS_WIDTH
    full = pl.BlockSpec((1, seq, aw), lambda b: (b, 0, 0))
    return pl.pallas_call(
        _fox_kernel,
        grid=(bsz,),
        in_specs=[full, full, pl.BlockSpec((1, seq // FOX_K, hw, FOX_K), lambda b: (b, 0, 0, 0))],
        out_specs=pl.BlockSpec((1, seq, hw), lambda b: (b, 0, 0)),
        out_shape=jax.ShapeDtypeStruct((bsz, seq, hw), BF16),
        scratch_shapes=[pltpu.VMEM((N_HEADS, FOX_Q), F32), pltpu.VMEM((N_HEADS, FOX_Q), F32),
                        pltpu.VMEM((N_HEADS, HEAD_DIM, FOX_Q), F32)],
        compiler_params=_params(("arbitrary",), 48),
        name="fox",
    )(qa, ka, vt)


def _dil_kernel(q_ref, kc_ref, kp_ref, vc_ref, vp_ref, o_ref, lse_ref):
    n = pl.program_id(2)
    blk = ATTN_BLOCK
    lane = lax.broadcasted_iota(jnp.int32, (1, PAIR), 1)
    first = lane < HEAD_DIM
    i = lax.broadcasted_iota(jnp.int32, (blk, 2 * blk), 0)
    c = lax.broadcasted_iota(jnp.int32, (blk, 2 * blk), 1)
    valid = (c >= i) & (c <= i + blk) & ((c >= blk) | (n > 0))

    for p in range(N_PAIRS):
        ls = slice(p * PAIR, (p + 1) * PAIR)
        qb = q_ref[0, :, ls]
        zero = jnp.zeros_like(qb)
        kb = jnp.concatenate([kp_ref[0, :, ls], kc_ref[0, :, ls]], axis=0)
        vb = jnp.concatenate([vp_ref[0, :, ls], vc_ref[0, :, ls]], axis=0)
        res = []
        for hh in range(2):
            qh = jnp.where(first, qb, zero) if hh == 0 else jnp.where(first, zero, qb)
            s = jnp.where(valid, _dot_nt(qh, kb) * SCORE_SCALE, NEG_INF)
            m = jnp.max(s, axis=-1, keepdims=True)
            e = jnp.exp(s - m)
            l = jnp.sum(e, axis=-1, keepdims=True)
            o = _dot(e.astype(BF16), vb) / l
            res.append((o, m + jnp.log(l)))
        o_ref[0, :, ls] = jnp.where(first, res[0][0], res[1][0]).astype(BF16)
        lse_ref[0, :, ls] = jnp.where(first, res[0][1], res[1][1])


def _dil(qd, kd, vd, dilation):
    bsz, seq, hw = qd.shape
    length = seq // dilation
    nb = length // ATTN_BLOCK
    view = lambda t: t.reshape(bsz, length, dilation * hw)
    cur = pl.BlockSpec((1, ATTN_BLOCK, hw), lambda b, r, n: (b, n, r))
    prev = pl.BlockSpec((1, ATTN_BLOCK, hw), lambda b, r, n: (b, jnp.maximum(n - 1, 0), r))
    o, lse = pl.pallas_call(
        _dil_kernel,
        grid=(bsz, dilation, nb),
        in_specs=[cur, cur, prev, cur, prev],
        out_specs=[cur, cur],
        out_shape=[jax.ShapeDtypeStruct((bsz, length, dilation * hw), BF16),
                   jax.ShapeDtypeStruct((bsz, length, dilation * hw), F32)],
        compiler_params=_params(("arbitrary", "arbitrary", "arbitrary"), 32),
        name=f"dil{dilation}",
    )(view(qd), view(kd), view(kd), view(vd), view(vd))
    return o.reshape(bsz, seq, hw), lse.reshape(bsz, seq, hw)


def _mix_kernel(x_ref, yf_ref, o1_ref, o2_ref, o3_ref, l1_ref, l2_ref, l3_ref, gf_ref, gd_ref, g1_ref,
                wbf_ref, wbd_ref, wo_ref, lng_ref, lnb_ref, out_ref):
    l1, l2, l3 = l1_ref[0], l2_ref[0], l3_ref[0]
    mx = jnp.maximum(jnp.maximum(l1, l2), l3)
    e1, e2, e3 = jnp.exp(l1 - mx), jnp.exp(l2 - mx), jnp.exp(l3 - mx)
    yd = (e1 * o1_ref[0].astype(F32) + e2 * o2_ref[0].astype(F32) + e3 * o3_ref[0].astype(F32)) / (e1 + e2 + e3)
    merged = (gf_ref[0].astype(F32) * _dot(yf_ref[0], wbf_ref[...])
              + gd_ref[0].astype(F32) * _dot(yd.astype(BF16), wbd_ref[...]))
    y = _dot(merged.astype(BF16), wo_ref[...])
    z = DN_ALPHA * x_ref[0] + (1.0 + g1_ref[0]) * y
    out_ref[0] = _layer_norm(z, lng_ref[...], lnb_ref[...])


def _mix(x, yf, os_, lses, gf, gd, mod3, wbf, wbd, wo, lng, lnb):
    bsz, seq, d = x.shape
    tm = ROW_TILE
    const = lambda shape: pl.BlockSpec(shape, lambda b, j: (0,) * len(shape))
    tok = lambda w: pl.BlockSpec((1, tm, w), lambda b, j: (b, j, 0))
    hw = HEADS_WIDTH
    return pl.pallas_call(
        _mix_kernel,
        grid=(bsz, seq // tm),
        in_specs=[tok(d)] + [tok(hw)] * 7 + [tok(d), tok(d),
                  pl.BlockSpec((1, 1, d), lambda b, j: (b, 0, 2)),
                  const(wbf.shape), const(wbd.shape), const(wo.shape), const(lng.shape), const(lnb.shape)],
        out_specs=tok(d),
        out_shape=jax.ShapeDtypeStruct((bsz, seq, d), F32),
        compiler_params=_params(("arbitrary", "arbitrary"), 48),
        name="mix",
    )(x, yf, *os_, *lses, gf, gd, mod3, wbf, wbd, wo, lng, lnb)


def _router_kernel(x_ref, sc_ref, sh_ref, wrh_ref, wrl_ref, bias_ref,
                   u_ref, up_ref, eidx_ref, rank_ref, gw_ref, cnt_ref):
    seq = x_ref.shape[1]
    tc = ROUTE_CHUNK
    e_iota = lax.broadcasted_iota(jnp.int32, (N_EXPERTS, tc), 0)
    g_iota = lax.broadcasted_iota(jnp.int32, (N_GROUPS, tc), 0)
    s_iota = lax.broadcasted_iota(jnp.int32, (GROUP_SIZE, tc), 0)
    r = lax.broadcasted_iota(jnp.int32, (tc, tc), 0)
    c = lax.broadcasted_iota(jnp.int32, (tc, tc), 1)
    upper = jnp.where(r <= c, 1.0, 0.0).astype(BF16)
    ones = jnp.ones((16, tc), BF16)
    carry = jnp.zeros((N_EXPERTS, 1), F32)
    cnt = jnp.zeros((16, N_EXPERTS), F32)

    for ch in range(seq // tc):
        rows = slice(ch * tc, (ch + 1) * tc)
        u = x_ref[0, rows, :] * (1.0 + sc_ref[0]) + sh_ref[0]
        u_ref[0, rows, :] = u.astype(BF16)
        up_ref[0, rows, :] = _pack_words(u)
        u_hi, u_lo = _split2(u)
        logits = _dot_nt(wrh_ref[...], u_hi) + _dot_nt(wrh_ref[...], u_lo) + _dot_nt(wrl_ref[...], u_hi)
        scores = _sigmoid(logits)
        sel = scores + bias_ref[...]

        gs_rows = []
        for g in range(N_GROUPS):
            blk = sel[g * GROUP_SIZE:(g + 1) * GROUP_SIZE]
            m1 = jnp.max(blk, axis=0, keepdims=True)
            i1 = jnp.min(jnp.where(blk == m1, s_iota, GROUP_SIZE), axis=0, keepdims=True)
            m2 = jnp.max(jnp.where(s_iota == i1, -jnp.inf, blk), axis=0, keepdims=True)
            gs_rows.append(m1 + m2)
        gs = jnp.concatenate(gs_rows, axis=0)
        beaten = jnp.zeros((N_GROUPS, tc), F32)
        for g in range(N_GROUPS):
            other = gs[g:g + 1]
            wins = (other > gs) | ((other == gs) & (g_iota > g))
            beaten = beaten + jnp.where(wins, 1.0, 0.0)
        keep = beaten < TOPK_GROUPS
        cand = jnp.concatenate(
            [jnp.where(keep[g:g + 1], sel[g * GROUP_SIZE:(g + 1) * GROUP_SIZE], NEG_INF) for g in range(N_GROUPS)],
            axis=0)
        chosen = jnp.zeros((N_EXPERTS, tc), F32)
        picks = []
        for _ in range(TOP_K):
            m = jnp.max(cand, axis=0, keepdims=True)
            idx = jnp.min(jnp.where(cand == m, e_iota, N_EXPERTS), axis=0, keepdims=True)
            hit = e_iota == idx
            chosen = jnp.where(hit, 1.0, chosen)
            cand = jnp.where(hit, -jnp.inf, cand)
            picks.append(idx)
        g_raw = jnp.where(chosen > 0.0, scores, 0.0)
        gate = g_raw / jnp.sum(g_raw, axis=0, keepdims=True) * ROUTED_SCALE

        chosen_b = chosen.astype(BF16)
        incl = _dot(chosen_b, upper) + carry
        carry = incl[:, tc - 1:tc]
        cnt = cnt + _dot_nt(ones, chosen_b)

        rank_rows, gate_rows = [], []
        for idx in picks:
            hit = e_iota == idx
            rank_rows.append(jnp.sum(jnp.where(hit, incl - 1.0, 0.0), axis=0, keepdims=True))
            gate_rows.append(jnp.sum(jnp.where(hit, gate, 0.0), axis=0, keepdims=True))
        eidx_ref[0, :, rows] = jnp.concatenate(picks, axis=0)
        rank_ref[0, :, rows] = jnp.concatenate(rank_rows, axis=0).astype(jnp.int32)
        gate_t = jnp.concatenate(gate_rows + [jnp.zeros((LANES - TOP_K, tc), F32)], axis=0).T
        gw_ref[0, rows, :] = gate_t[:, :TOP_K]

    cnt_ref[0] = cnt[0:1].astype(jnp.int32)


def _router(x1, mod3, wr_hi, wr_lo, bias):
    bsz, seq, d = x1.shape
    full = pl.BlockSpec((1, seq, d), lambda b: (b, 0, 0))
    ks = pl.BlockSpec((1, TOP_K, seq), lambda b: (b, 0, 0))
    return pl.pallas_call(
        _router_kernel,
        grid=(bsz,),
        in_specs=[full,
                  pl.BlockSpec((1, 1, d), lambda b: (b, 0, 4)),
                  pl.BlockSpec((1, 1, d), lambda b: (b, 0, 3)),
                  pl.BlockSpec(wr_hi.shape, lambda b: (0, 0)),
                  pl.BlockSpec(wr_lo.shape, lambda b: (0, 0)),
                  pl.BlockSpec(bias.shape, lambda b: (0, 0))],
        out_specs=[full, pl.BlockSpec((1, seq, PACKED), lambda b: (b, 0, 0)), ks, ks,
                   pl.BlockSpec((1, seq, TOP_K), lambda b: (b, 0, 0)),
                   pl.BlockSpec((1, 1, N_EXPERTS), lambda b: (b, 0, 0))],
        out_shape=[jax.ShapeDtypeStruct((bsz, seq, d), BF16),
                   jax.ShapeDtypeStruct((bsz, seq, PACKED), F32),
                   jax.ShapeDtypeStruct((bsz, TOP_K, seq), jnp.int32),
                   jax.ShapeDtypeStruct((bsz, TOP_K, seq), jnp.int32),
                   jax.ShapeDtypeStruct((bsz, seq, TOP_K), F32),
                   jax.ShapeDtypeStruct((bsz, 1, N_EXPERTS), jnp.int32)],
        compiler_params=_params(("arbitrary",), 48),
        name="router",
    )(x1, mod3, mod3, wr_hi, wr_lo, bias)


def _slot_kernel(base_ref, eidx_ref, rank_ref, slot_ref):
    b = pl.program_id(0)
    eidx = eidx_ref[0]
    start = jnp.zeros(eidx.shape, jnp.int32)
    for e in range(N_EXPERTS):
        start = jnp.where(eidx == e, base_ref[b * N_EXPERTS + e], start)
    slot_ref[0] = start + rank_ref[0]


def _slots(base, eidx, rank):
    bsz, _, seq = eidx.shape
    ks = pl.BlockSpec((1, TOP_K, seq), lambda b, c: (b, 0, 0))
    return pl.pallas_call(
        _slot_kernel,
        grid_spec=pltpu.PrefetchScalarGridSpec(num_scalar_prefetch=1, grid=(bsz,), in_specs=[ks, ks], out_specs=ks),
        out_shape=jax.ShapeDtypeStruct((bsz, TOP_K, seq), jnp.int32),
        compiler_params=_params(("arbitrary",), 16),
        name="slots",
    )(base.reshape(bsz * N_EXPERTS), eidx, rank)


def _sc_mesh():
    return plsc.VectorSubcoreMesh(core_axis_name="core", subcore_axis_name="subcore")


def _sc_worker():
    return lax.axis_index("subcore") * SC_CORES + lax.axis_index("core")


def _sc_dispatch(rows, slot, n_slots):
    n_tok, width = rows.shape
    seq = slot.shape[1]
    chunk = DISPATCH_CHUNK
    per_worker = n_tok // SC_WORKERS
    assert per_worker % chunk == 0 and seq % chunk == 0

    @functools.partial(
        pl.kernel, mesh=_sc_mesh(),
        out_type=jax.ShapeDtypeStruct((n_slots, width), rows.dtype),
        scratch_types=[pltpu.VMEM((TOP_K, chunk), jnp.int32), pltpu.VMEM((chunk, width), rows.dtype),
                       pltpu.SemaphoreType.DMA],
    )
    def dispatch(rows_hbm, slot_hbm, out_hbm, idx_v, rows_v, sem):
        base = _sc_worker() * per_worker

        @pl.loop(0, per_worker // chunk)
        def _(i):
            off = base + i * chunk
            b = off // seq
            n0 = off - b * seq
            pltpu.sync_copy(slot_hbm.at[pl.ds(b * TOP_K, TOP_K), pl.ds(n0, chunk)], idx_v)
            pltpu.sync_copy(rows_hbm.at[pl.ds(off, chunk)], rows_v)
            copies = [pltpu.async_copy(rows_v, out_hbm.at[idx_v.at[k]], sem) for k in range(TOP_K)]
            for cp in copies:
                cp.wait()

    return dispatch(rows, slot)


def _sc_gather(table, idx):
    n_out = idx.shape[0]
    width = table.shape[1]
    chunk = GATHER_CHUNK
    per_worker = n_out // SC_WORKERS
    steps = per_worker // chunk
    assert per_worker % chunk == 0 and steps % 2 == 0

    @functools.partial(
        pl.kernel, mesh=_sc_mesh(),
        out_type=jax.ShapeDtypeStruct((n_out, width), table.dtype),
        scratch_types=[pltpu.VMEM((2, chunk), jnp.int32), pltpu.VMEM((2, chunk, width), table.dtype),
                       pltpu.SemaphoreType.DMA, pltpu.SemaphoreType.DMA((2,))],
    )
    def gather(table_hbm, idx_hbm, out_hbm, idx_v, rows_v, gather_sem, out_sems):
        base = _sc_worker() * per_worker

        def write_out(buf, off):
            return pltpu.make_async_copy(rows_v.at[buf], out_hbm.at[pl.ds(off, chunk)], out_sems.at[buf])

        @pl.loop(0, steps, step=2)
        def _(i):
            for buf in range(2):
                off = base + (i + buf) * chunk

                @pl.when(i >= 2)
                def _():
                    write_out(buf, off - 2 * chunk).wait()

                pltpu.sync_copy(idx_hbm.at[pl.ds(off, chunk)], idx_v.at[buf])
                pltpu.async_copy(table_hbm.at[idx_v.at[buf]], rows_v.at[buf], gather_sem).wait()
                write_out(buf, off).start()

        for buf in range(2):
            write_out(buf, base + (steps - 2 + buf) * chunk).wait()

    return gather(table, idx)


def _gmm_kernel(blk_e_ref, used_ref, xs_ref, wgu_ref, wd_ref, out_ref):
    half = PACKED

    @pl.when(pl.program_id(0) < used_ref[0])
    def _():
        lo, hi = _unpack_words(xs_ref[...])
        hgu = _dot(lo.astype(BF16), wgu_ref[0, :half, :]) + _dot(hi.astype(BF16), wgu_ref[0, half:, :])
        hg = hgu[:, :EXPERT_FF]
        h = (hg * _sigmoid(hg) * hgu[:, EXPERT_FF:]).astype(BF16)
        out_ref[...] = _pack_words(_dot(h, wd_ref[0]))


def _gmm(xs, blk_e, used, wgu, wd):
    n_slots, width = xs.shape
    bm = EXPERT_BLOCK
    d = wgu.shape[1]
    live = lambda i, blk_e, used: (jnp.minimum(i, used[0] - 1), 0)
    return pl.pallas_call(
        _gmm_kernel,
        grid_spec=pltpu.PrefetchScalarGridSpec(
            num_scalar_prefetch=2,
            grid=(n_slots // bm,),
            in_specs=[pl.BlockSpec((bm, width), live),
                      pl.BlockSpec((1, d, 2 * EXPERT_FF), lambda i, blk_e, used: (blk_e[i], 0, 0)),
                      pl.BlockSpec((1, EXPERT_FF, d), lambda i, blk_e, used: (blk_e[i], 0, 0))],
            out_specs=pl.BlockSpec((bm, width), live)),
        out_shape=jax.ShapeDtypeStruct((n_slots, width), F32),
        compiler_params=_params(("arbitrary",), 32),
        name="gmm",
    )(blk_e, used, xs, wgu, wd)


def _ffn_out_kernel(x_ref, u_ref, yg_ref, gw_ref, g2_ref, wsgu_ref, wsd_ref, lng_ref, lnb_ref, out_ref):
    hgu = _dot(u_ref[0], wsgu_ref[...])
    hg = hgu[:, :SHARED_FF]
    h = (hg * _sigmoid(hg) * hgu[:, SHARED_FF:]).astype(BF16)
    shared = _dot(h, wsd_ref[...])
    gw = gw_ref[0]
    acc_lo = shared[:, :PACKED]
    acc_hi = shared[:, PACKED:]
    for k in range(TOP_K):
        lo, hi = _unpack_words(yg_ref[0, k])
        w = gw[:, k:k + 1]
        acc_lo = acc_lo + w * lo
        acc_hi = acc_hi + w * hi
    y = jnp.concatenate([acc_lo, acc_hi], axis=1)
    z = DN_ALPHA * x_ref[0] + (1.0 + g2_ref[0]) * y
    out_ref[0] = _layer_norm(z, lng_ref[...], lnb_ref[...])


def _ffn_out(x1, u2, yg, gw, mod3, wsgu, wsd, lng, lnb):
    bsz, seq, d = x1.shape
    tm = ROW_TILE
    const = lambda shape: pl.BlockSpec(shape, lambda b, j: (0,) * len(shape))
    tok = pl.BlockSpec((1, tm, d), lambda b, j: (b, j, 0))
    return pl.pallas_call(
        _ffn_out_kernel,
        grid=(bsz, seq // tm),
        in_specs=[tok, tok,
                  pl.BlockSpec((1, TOP_K, tm, PACKED), lambda b, j: (b, 0, j, 0)),
                  pl.BlockSpec((1, tm, TOP_K), lambda b, j: (b, j, 0)),
                  pl.BlockSpec((1, 1, d), lambda b, j: (b, 0, 5)),
                  const(wsgu.shape), const(wsd.shape), const(lng.shape), const(lnb.shape)],
        out_specs=tok,
        out_shape=jax.ShapeDtypeStruct((bsz, seq, d), F32),
        compiler_params=_params(("arbitrary", "arbitrary"), 48),
        name="ffn_out",
    )(x1, u2, yg, gw, mod3, wsgu, wsd, lng, lnb)


def _moe_layout(cnt, n_blocks):
    bm = EXPERT_BLOCK
    total = jnp.sum(cnt, axis=0)
    padded = (total + bm - 1) // bm * bm
    ends = jnp.cumsum(padded)
    base = (ends - padded)[None, :] + jnp.cumsum(cnt, axis=0) - cnt
    first_row = jnp.arange(n_blocks, dtype=ends.dtype) * bm
    blk_e = jnp.minimum(jnp.sum(ends[None, :] <= first_row[:, None], axis=1), N_EXPERTS - 1)
    used = (ends[-1:] // bm)
    return base.astype(jnp.int32), blk_e.astype(jnp.int32), used.astype(jnp.int32)


def _rope_tables(seq):
    half = HEAD_DIM // 2
    inv_freq = ROPE_THETA ** (-jnp.arange(half, dtype=F32) * 2.0 / HEAD_DIM)
    ang = jnp.arange(seq, dtype=F32)[:, None] * inv_freq[None, :]
    cos, sin = jnp.cos(ang), jnp.sin(ang)
    zero = jnp.zeros_like(sin)
    reps = LANES // HEAD_DIM
    cos_t = jnp.tile(jnp.concatenate([cos, cos], axis=1), (1, reps))
    sina_t = jnp.tile(jnp.concatenate([-sin, zero], axis=1), (1, reps))
    sinb_t = jnp.tile(jnp.concatenate([zero, sin], axis=1), (1, reps))
    return cos_t, sina_t, sinb_t


def _fox_bias_tables():
    n_split = 3
    aw = N_HEADS * LANES
    heads = jnp.arange(N_HEADS)
    selq = jnp.zeros((n_split, LANES, aw), F32)
    selk = jnp.zeros((n_split, LANES, aw), F32)
    oneq = jnp.zeros((1, aw), F32)
    onek = jnp.zeros((1, aw), F32)
    for j in range(n_split):
        selk = selk.at[j, heads, heads * LANES + FOX_BIAS_LANE + j].set(-1.0)
        selq = selq.at[j, heads, heads * LANES + FOX_BIAS_LANE + n_split + j].set(1.0)
        oneq = oneq.at[0, heads * LANES + FOX_BIAS_LANE + j].set(1.0)
        onek = onek.at[0, heads * LANES + FOX_BIAS_LANE + n_split + j].set(1.0)
    return selq.astype(BF16), selk.astype(BF16), oneq, onek


def kernel(x, c, w_ada, b_ada, w_in, b_forget, b_gate, w_br_fox, w_br_dil, w_o, ln_g, ln_b, w_router, router_bias,
           w_exp_gate, w_exp_up, w_exp_down, w_sh_gate, w_sh_up, w_sh_down):
    bsz, seq, d = x.shape
    depth = w_ada.shape[0]
    hw = HEADS_WIDTH
    n_tok = bsz * seq
    n_blocks = n_tok * TOP_K // EXPERT_BLOCK + N_EXPERTS

    mod = _ada(c, w_ada, b_ada)

    o_f, o_d, o_g = 3 * hw, 3 * hw + N_HEADS, 6 * hw + N_HEADS
    wfox = w_in[:, :, :o_f].astype(BF16)
    wfg = jnp.pad(w_in[:, :, o_f:o_d], ((0, 0), (0, 0), (0, LANES - N_HEADS))).astype(BF16)
    wdil = w_in[:, :, o_d:o_g].astype(BF16)
    wgate = w_in[:, :, o_g:].astype(BF16)
    bfg = jnp.pad(b_forget, ((0, 0), (0, LANES - N_HEADS))).reshape(depth, 1, LANES)
    bgate = b_gate.reshape(depth, 1, 2 * d)
    wr_t = jnp.swapaxes(w_router, 1, 2)
    wr_hi = wr_t.astype(BF16)
    wr_lo = (wr_t - wr_hi.astype(F32)).astype(BF16)
    rbias = router_bias.reshape(depth, N_EXPERTS, 1)
    wgu = jnp.concatenate([w_exp_gate, w_exp_up], axis=-1).astype(BF16)
    wd = w_exp_down.astype(BF16)
    wsgu = jnp.concatenate([w_sh_gate, w_sh_up], axis=-1).astype(BF16)
    wsd = w_sh_down.astype(BF16)
    cos_t, sina_t, sinb_t = _rope_tables(seq)
    selq, selk, oneq, onek = _fox_bias_tables()

    layer_params = (mod, wfox, wdil, wgate, wfg, bfg, bgate, w_br_fox.astype(BF16), w_br_dil.astype(BF16),
                    w_o.astype(BF16), ln_g, ln_b, wr_hi, wr_lo, rbias, wgu, wd, wsgu, wsd)

    def layer(xc, prm):
        (mod_l, wfox_l, wdil_l, wgate_l, wfg_l, bfg_l, bgate_l, wbf_l, wbd_l, wo_l, lng_l, lnb_l,
         wrh_l, wrl_l, rb_l, wgu_l, wd_l, wsgu_l, wsd_l) = prm
        mod3 = mod_l.reshape(bsz, 1, 6 * d)
        qa, ka, vt, qd, kd, vd, gf, gd = _inproj(
            xc, mod3, wfox_l, wdil_l, wgate_l, wfg_l, bfg_l, bgate_l, cos_t, sina_t, sinb_t, selq, selk, oneq, onek)
        yf = _fox(qa, ka, vt)
        dil = [_dil(qd, kd, vd, dilation) for _, dilation in DIL_PATTERNS]
        x1 = _mix(xc, yf, [o for o, _ in dil], [l for _, l in dil], gf, gd, mod3, wbf_l, wbd_l, wo_l,
                  lng_l[0:1], lnb_l[0:1])
        u2, u2p, eidx, rank, gw, cnt = _router(x1, mod3, wrh_l, wrl_l, rb_l)
        base, blk_e, used = _moe_layout(cnt.reshape(bsz, N_EXPERTS), n_blocks)
        slot = _slots(base, eidx, rank)
        xs = _sc_dispatch(u2p.reshape(n_tok, PACKED), slot.reshape(bsz * TOP_K, seq), n_blocks * EXPERT_BLOCK)
        ys = _gmm(xs, blk_e, used, wgu_l, wd_l)
        yg = _sc_gather(ys, slot.reshape(n_tok * TOP_K)).reshape(bsz, TOP_K, seq, PACKED)
        x2 = _ffn_out(x1, u2, yg, gw, mod3, wsgu_l, wsd_l, lng_l[1:2], lnb_l[1:2])
        return x2, None

    out, _ = lax.scan(layer, x, layer_params)
    return out
```

```python
import functools

import jax
import jax.numpy as jnp
from jax import lax
from jax.experimental import pallas as pl
from jax.experimental.pallas import tpu as pltpu
from jax.experimental.pallas import tpu_sc as plsc

D_MODEL = 1024
DEPTH = 4
HEAD_DIM = 64
N_HEADS = 8
HEADS_WIDTH = N_HEADS * HEAD_DIM
DIL_PATTERNS = ((128, 1), (512, 4), (2048, 16))
ATTN_BLOCK = 128
ROPE_THETA = 10000.0
N_EXPERTS = 64
TOP_K = 8
N_GROUPS = 8
GROUP_SIZE = N_EXPERTS // N_GROUPS
TOPK_GROUPS = 4
EXPERT_FF = 256
SHARED_FF = 256
ROUTED_SCALE = 2.5
DN_ALPHA = (2 * DEPTH) ** 0.25
LN_EPS = 1e-5
NEG_INF = -1e30
SCORE_SCALE = HEAD_DIM ** -0.5

LANES = 128
PAIR = 2 * HEAD_DIM
N_PAIRS = HEADS_WIDTH // PAIR

ROW_TILE = 512
FOX_Q = 256
FOX_K = 128
FOX_BIAS_LANE = HEAD_DIM
ROUTE_CHUNK = 512
EXPERT_BLOCK = 512
PACKED = D_MODEL // 2

SC_CORES = 2
SC_SUBCORES = 16
SC_WORKERS = SC_CORES * SC_SUBCORES
DISPATCH_CHUNK = 128
GATHER_CHUNK = 64

BF16 = jnp.bfloat16
F32 = jnp.float32


def _dot(a, b):
    return jnp.dot(a, b, preferred_element_type=F32)


def _dot_nt(a, b):
    return lax.dot_general(a, b, (((1,), (1,)), ((), ())), preferred_element_type=F32)


def _split2(x):
    hi = x.astype(BF16)
    lo = (x - hi.astype(F32)).astype(BF16)
    return hi, lo


def _split3(x):
    hi = x.astype(BF16)
    r = x - hi.astype(F32)
    mid = r.astype(BF16)
    lo = (r - mid.astype(F32)).astype(BF16)
    return hi, mid, lo


def _sigmoid(x):
    return 1.0 / (1.0 + jnp.exp(-x))


def _layer_norm(z, g, b):
    mu = jnp.mean(z, axis=-1, keepdims=True)
    zc = z - mu
    var = jnp.mean(zc * zc, axis=-1, keepdims=True)
    return zc * lax.rsqrt(var + LN_EPS) * g + b


def _pack_words(v):
    half = v.shape[1] // 2
    bits = pltpu.bitcast(v.astype(BF16).astype(F32), jnp.uint32)
    word = (bits[:, half:] & jnp.uint32(0xFFFF0000)) | (bits[:, :half] >> 16)
    return pltpu.bitcast(word, F32)


def _unpack_words(w):
    bits = pltpu.bitcast(w, jnp.uint32)
    return pltpu.bitcast(bits << 16, F32), pltpu.bitcast(bits & jnp.uint32(0xFFFF0000), F32)


def _params(sem, vmem_mb):
    return pltpu.CompilerParams(dimension_semantics=sem, vmem_limit_bytes=vmem_mb * 1024 * 1024)


def _ada_kernel(c_ref, w_ref, b_ref, o_ref):
    c = c_ref[...]
    cs = c * _sigmoid(c)
    c_hi, c_lo = _split2(cs)
    w_hi, w_lo = _split2(w_ref[0])
    o_ref[0] = _dot(c_hi, w_hi) + _dot(c_hi, w_lo) + _dot(c_lo, w_hi) + b_ref[0]


def _ada(c, w_ada, b_ada):
    depth, d, n = w_ada.shape
    bsz = c.shape[0]
    tn = 1536
    return pl.pallas_call(
        _ada_kernel,
        grid=(depth, n // tn),
        in_specs=[
            pl.BlockSpec((bsz, d), lambda l, j: (0, 0)),
            pl.BlockSpec((1, d, tn), lambda l, j: (l, 0, j)),
            pl.BlockSpec((1, 1, tn), lambda l, j: (l, 0, j)),
        ],
        out_specs=pl.BlockSpec((1, bsz, tn), lambda l, j: (l, 0, j)),
        out_shape=jax.ShapeDtypeStruct((depth, bsz, n), F32),
        compiler_params=_params(("arbitrary", "arbitrary"), 40),
        name="ada",
    )(c, w_ada, b_ada.reshape(depth, 1, n))


def _inproj_kernel(x_ref, sc_ref, sh_ref, wfox_ref, wdil_ref, wgate_ref, wfg_ref, bfg_ref, bgate_ref,
                   cos_ref, sina_ref, sinb_ref, selq_ref, selk_ref, oneq_ref, onek_ref,
                   qa_ref, ka_ref, vt_ref, *rest):
    dil_refs = rest[:9]
    gf_ref, gd_ref, carry_ref, perm_ref = rest[9:]
    j = pl.program_id(1)
    tm = x_ref.shape[1]
    u = (x_ref[0] * (1.0 + sc_ref[0]) + sh_ref[0]).astype(BF16)

    ff = _dot(u, wfg_ref[...]) + bfg_ref[...]
    lf = jnp.minimum(ff, 0.0) - jnp.log(1.0 + jnp.exp(-jnp.abs(ff)))
    r = lax.broadcasted_iota(jnp.int32, (tm, tm), 0)
    c = lax.broadcasted_iota(jnp.int32, (tm, tm), 1)
    tri = jnp.where(r >= c, 1.0, 0.0).astype(BF16)
    hi, mid, lo = _split3(lf)
    csum = _dot(tri, hi) + _dot(tri, mid) + _dot(tri, lo)

    @pl.when(j == 0)
    def _():
        carry_ref[...] = jnp.zeros_like(carry_ref)

    cum = csum + carry_ref[...]
    carry_ref[...] = cum[tm - 1:tm, :]

    c_hi, c_mid, c_lo = _split3(cum)
    bias_q = (_dot(c_hi, selq_ref[0]) + _dot(c_mid, selq_ref[1]) + _dot(c_lo, selq_ref[2])) + oneq_ref[...]
    bias_k = (_dot(c_hi, selk_ref[0]) + _dot(c_mid, selk_ref[1]) + _dot(c_lo, selk_ref[2])) + onek_ref[...]
    pf = _dot(u, wfox_ref[...])
    lane = lax.broadcasted_iota(jnp.int32, (1, LANES), 1)
    data = lane < HEAD_DIM
    for dst, off, bias, scale in ((qa_ref, 0, bias_q, SCORE_SCALE), (ka_ref, HEADS_WIDTH, bias_k, 1.0)):
        for p in range(N_PAIRS):
            pair = pf[:, off + p * PAIR: off + (p + 1) * PAIR] * scale
            for hh, src in ((0, pair), (1, pltpu.roll(pair, HEAD_DIM, 1))):
                t = (2 * p + hh) * LANES
                dst[0, :, t:t + LANES] = jnp.where(data, src, bias[:, t:t + LANES]).astype(BF16)
    vt = pf[:, 2 * HEADS_WIDTH:].T
    for cb in range(tm // FOX_K):
        vt_ref[0, cb] = vt[:, cb * FOX_K:(cb + 1) * FOX_K].astype(BF16)

    pd = _dot(u, wdil_ref[...])
    rows = pl.ds(pl.multiple_of(j * tm, tm), tm)
    cos = cos_ref[rows, :]
    sina = sina_ref[rows, :]
    sinb = sinb_ref[rows, :]
    for i, off in ((0, 0), (1, HEADS_WIDTH)):
        for g in range(N_PAIRS):
            xg = pd[:, off + g * LANES: off + (g + 1) * LANES]
            ahead = pltpu.roll(xg, LANES - HEAD_DIM // 2, 1)
            behind = pltpu.roll(xg, HEAD_DIM // 2, 1)
            perm_ref[i, g] = xg * cos + ahead * sina + behind * sinb
    for g in range(N_PAIRS):
        perm_ref[2, g] = pd[:, 2 * HEADS_WIDTH + g * LANES:2 * HEADS_WIDTH + (g + 1) * LANES]
    for i in range(3):
        for (_, dilation), dst in zip(DIL_PATTERNS, dil_refs[i::3]):
            for res in range(dilation):
                for g in range(N_PAIRS):
                    dst[0, res, :, g * LANES:(g + 1) * LANES] = perm_ref[
                        i, g, pl.ds(res, tm // dilation, stride=dilation), :].astype(BF16)

    pg = _sigmoid(_dot(u, wgate_ref[...]) + bgate_ref[...])
    gf_ref[0] = pg[:, :D_MODEL].astype(BF16)
    gd_ref[0] = pg[:, D_MODEL:].astype(BF16)


def _inproj(x, mod3, wfox, wdil, wgate, wfg, bfg, bgate, cos_t, sina_t, sinb_t, selq, selk, oneq, onek):
    bsz, seq, d = x.shape
    tm = ROW_TILE
    const = lambda shape: pl.BlockSpec(shape, lambda b, j: (0,) * len(shape))
    tok = lambda w: pl.BlockSpec((1, tm, w), lambda b, j: (b, j, 0))
    hw = HEADS_WIDTH
    aw = N_HEADS * LANES
    dil_shapes, dil_specs = [], []
    for _, dil in DIL_PATTERNS:
        dil_shapes += [jax.ShapeDtypeStruct((bsz, dil, seq // dil, hw), BF16)] * 3
        dil_specs += [pl.BlockSpec((1, dil, tm // dil, hw), lambda b, j: (b, 0, j, 0))] * 3
    outs = ([jax.ShapeDtypeStruct((bsz, seq, aw), BF16)] * 2
            + [jax.ShapeDtypeStruct((bsz, seq // FOX_K, hw, FOX_K), BF16)]
            + dil_shapes + [jax.ShapeDtypeStruct((bsz, seq, d), BF16)] * 2)
    return pl.pallas_call(
        _inproj_kernel,
        grid=(bsz, seq // tm),
        in_specs=[
            tok(d),
            pl.BlockSpec((1, 1, d), lambda b, j: (b, 0, 1)),
            pl.BlockSpec((1, 1, d), lambda b, j: (b, 0, 0)),
            const(wfox.shape), const(wdil.shape), const(wgate.shape), const(wfg.shape),
            const(bfg.shape), const(bgate.shape),
            const(cos_t.shape), const(sina_t.shape), const(sinb_t.shape),
            const(selq.shape), const(selk.shape), const(oneq.shape), const(onek.shape),
        ],
        out_specs=[tok(aw)] * 2 + [pl.BlockSpec((1, tm // FOX_K, hw, FOX_K), lambda b, j: (b, j, 0, 0))]
                  + dil_specs + [tok(d)] * 2,
        out_shape=outs,
        scratch_shapes=[pltpu.VMEM((1, LANES), F32), pltpu.VMEM((3, N_PAIRS, tm, LANES), F32)],
        compiler_params=_params(("arbitrary", "arbitrary"), 56),
        name="inproj",
    )(x, mod3, mod3, wfox, wdil, wgate, wfg, bfg, bgate, cos_t, sina_t, sinb_t, selq, selk, oneq, onek)


def _fox_kernel(qa_ref, ka_ref, vt_ref, o_ref, m_ref, l_ref, acc_ref):
    seq = qa_ref.shape[1]
    tq, tk = FOX_Q, FOX_K
    k_pos = lax.broadcasted_iota(jnp.int32, (tk, tq), 0)
    q_pos = lax.broadcasted_iota(jnp.int32, (tk, tq), 1)

    def q_body(qi, _):
        q0 = pl.multiple_of(qi * tq, tq)
        m_ref[...] = jnp.full(m_ref.shape, NEG_INF, F32)
        l_ref[...] = jnp.zeros(l_ref.shape, F32)
        acc_ref[...] = jnp.zeros(acc_ref.shape, F32)

        def kv_step(kb, k0, visible):
            scores = []
            for h in range(N_HEADS):
                ls = slice(h * LANES, (h + 1) * LANES)
                scores.append(_dot_nt(ka_ref[0, pl.ds(k0, tk), ls], qa_ref[0, pl.ds(q0, tq), ls]))
            m_all = m_ref[...]
            l_all = l_ref[...]
            probs, decay, m_rows, l_rows = [], [], [], []
            for h in range(N_HEADS):
                s = scores[h] if visible is None else jnp.where(visible, scores[h], NEG_INF)
                m_old = m_all[h:h + 1, :]
                m_new = jnp.maximum(m_old, jnp.max(s, axis=0, keepdims=True))
                a = jnp.exp(m_old - m_new)
                e = jnp.exp(s - m_new)
                l_rows.append(a * l_all[h:h + 1, :] + jnp.sum(e, axis=0, keepdims=True))
                m_rows.append(m_new)
                probs.append(e.astype(BF16))
                decay.append(a)
            m_ref[...] = jnp.concatenate(m_rows, axis=0)
            l_ref[...] = jnp.concatenate(l_rows, axis=0)
            for h in range(N_HEADS):
                vh = vt_ref[0, kb, h * HEAD_DIM:(h + 1) * HEAD_DIM, :]
                acc_ref[h] = decay[h] * acc_ref[h] + _dot(vh, probs[h])

        def full_block(jb, carry):
            kv_step(jb, pl.multiple_of(jb * tk, tk), None)
            return carry

        lax.fori_loop(0, qi * (tq // tk), full_block, 0)
        for dblk in range(tq // tk):
            kb = qi * (tq // tk) + dblk
            kv_step(kb, pl.multiple_of(kb * tk, tk), k_pos + dblk * tk <= q_pos)

        for p in range(N_PAIRS):
            pair = jnp.concatenate([acc_ref[2 * p + hh] / l_ref[2 * p + hh:2 * p + hh + 1, :] for hh in range(2)],
                                   axis=0)
            o_ref[0, pl.ds(q0, tq), p * PAIR:(p + 1) * PAIR] = pair.T.astype(BF16)
        return 0

    lax.fori_loop(0, seq // tq, q_body, 0)


def _fox(qa, ka, vt):
    bsz, seq, aw = qa.shape
    hw = HEADS_WIDTH
    full = pl.BlockSpec((1, seq, aw), lambda b: (b, 0, 0))
    return pl.pallas_call(
        _fox_kernel,
        grid=(bsz,),
        in_specs=[full, full, pl.BlockSpec((1, seq // FOX_K, hw, FOX_K), lambda b: (b, 0, 0, 0))],
        out_specs=pl.BlockSpec((1, seq, hw), lambda b: (b, 0, 0)),
        out_shape=jax.ShapeDtypeStruct((bsz, seq, hw), BF16),
        scratch_shapes=[pltpu.VMEM((N_HEADS, FOX_Q), F32), pltpu.VMEM((N_HEADS, FOX_Q), F32),
                        pltpu.VMEM((N_HEADS, HEAD_DIM, FOX_Q), F32)],
        compiler_params=_params(("arbitrary",), 48),
        name="fox",
    )(qa, ka, vt)


def _dil_kernel(q_ref, kc_ref, kp_ref, vc_ref, vp_ref, o_ref, lse_ref):
    n = pl.program_id(2)
    blk = ATTN_BLOCK
    lane = lax.broadcasted_iota(jnp.int32, (1, PAIR), 1)
    first = lane < HEAD_DIM
    i = lax.broadcasted_iota(jnp.int32, (blk, 2 * blk), 0)
    c = lax.broadcasted_iota(jnp.int32, (blk, 2 * blk), 1)
    valid = (c >= i) & (c <= i + blk) & ((c >= blk) | (n > 0))

    for p in range(N_PAIRS):
        ls = slice(p * PAIR, (p + 1) * PAIR)
        qb = q_ref[:, ls]
        zero = jnp.zeros_like(qb)
        kb = jnp.concatenate([kp_ref[:, ls], kc_ref[:, ls]], axis=0)
        vb = jnp.concatenate([vp_ref[:, ls], vc_ref[:, ls]], axis=0)
        res = []
        for hh in range(2):
            qh = jnp.where(first, qb, zero) if hh == 0 else jnp.where(first, zero, qb)
            s = jnp.where(valid, _dot_nt(qh, kb) * SCORE_SCALE, NEG_INF)
            m = jnp.max(s, axis=-1, keepdims=True)
            e = jnp.exp(s - m)
            l = jnp.sum(e, axis=-1, keepdims=True)
            o = _dot(e.astype(BF16), vb) / l
            res.append((o, m + jnp.log(l)))
        o_ref[:, ls] = jnp.where(first, res[0][0], res[1][0]).astype(BF16)
        lse_ref[:, ls] = jnp.where(first, res[0][1], res[1][1])


def _dil(qd, kd, vd):
    bsz, dilation, length, hw = qd.shape
    nb = length // ATTN_BLOCK
    cur = pl.BlockSpec((None, None, ATTN_BLOCK, hw), lambda b, r, n: (b, r, n, 0))
    prev = pl.BlockSpec((None, None, ATTN_BLOCK, hw), lambda b, r, n: (b, r, jnp.maximum(n - 1, 0), 0))
    return pl.pallas_call(
        _dil_kernel,
        grid=(bsz, dilation, nb),
        in_specs=[cur, cur, prev, cur, prev],
        out_specs=[cur, cur],
        out_shape=[jax.ShapeDtypeStruct(qd.shape, BF16), jax.ShapeDtypeStruct(qd.shape, F32)],
        compiler_params=_params(("arbitrary", "arbitrary", "arbitrary"), 32),
        name=f"dil{dilation}",
    )(qd, kd, kd, vd, vd)


def _mix_kernel(x_ref, yf_ref, o1_ref, o2_ref, o3_ref, l1_ref, l2_ref, l3_ref, gf_ref, gd_ref, g1_ref,
                wbf_ref, wbd_ref, wo_ref, lng_ref, lnb_ref, out_ref, nat_ref):
    tm = x_ref.shape[1]

    def natural(i, src_ref):
        dilation = src_ref.shape[1]
        for res in range(dilation):
            for g in range(N_PAIRS):
                nat_ref[i, g, pl.ds(res, tm // dilation, stride=dilation), :] = src_ref[
                    0, res, :, g * LANES:(g + 1) * LANES].astype(F32)
        return jnp.concatenate([nat_ref[i, g] for g in range(N_PAIRS)], axis=1)

    os_ = [natural(i, ref) for i, ref in enumerate((o1_ref, o2_ref, o3_ref))]
    l1, l2, l3 = [natural(3 + i, ref) for i, ref in enumerate((l1_ref, l2_ref, l3_ref))]
    mx = jnp.maximum(jnp.maximum(l1, l2), l3)
    e1, e2, e3 = jnp.exp(l1 - mx), jnp.exp(l2 - mx), jnp.exp(l3 - mx)
    yd = (e1 * os_[0] + e2 * os_[1] + e3 * os_[2]) / (e1 + e2 + e3)
    merged = (gf_ref[0].astype(F32) * _dot(yf_ref[0], wbf_ref[...])
              + gd_ref[0].astype(F32) * _dot(yd.astype(BF16), wbd_ref[...]))
    y = _dot(merged.astype(BF16), wo_ref[...])
    z = DN_ALPHA * x_ref[0] + (1.0 + g1_ref[0]) * y
    out_ref[0] = _layer_norm(z, lng_ref[...], lnb_ref[...])


def _mix(x, yf, os_, lses, gf, gd, mod3, wbf, wbd, wo, lng, lnb):
    bsz, seq, d = x.shape
    tm = ROW_TILE
    const = lambda shape: pl.BlockSpec(shape, lambda b, j: (0,) * len(shape))
    tok = lambda w: pl.BlockSpec((1, tm, w), lambda b, j: (b, j, 0))
    hw = HEADS_WIDTH
    res_major = [pl.BlockSpec((1, dil, tm // dil, hw), lambda b, j: (b, 0, j, 0)) for _, dil in DIL_PATTERNS]
    return pl.pallas_call(
        _mix_kernel,
        grid=(bsz, seq // tm),
        in_specs=[tok(d), tok(hw)] + res_major * 2 + [tok(d), tok(d),
                  pl.BlockSpec((1, 1, d), lambda b, j: (b, 0, 2)),
                  const(wbf.shape), const(wbd.shape), const(wo.shape), const(lng.shape), const(lnb.shape)],
        out_specs=tok(d),
        out_shape=jax.ShapeDtypeStruct((bsz, seq, d), F32),
        scratch_shapes=[pltpu.VMEM((6, N_PAIRS, tm, LANES), F32)],
        compiler_params=_params(("arbitrary", "arbitrary"), 48),
        name="mix",
    )(x, yf, *os_, *lses, gf, gd, mod3, wbf, wbd, wo, lng, lnb)


def _router_kernel(x_ref, sc_ref, sh_ref, wrh_ref, wrl_ref, bias_ref,
                   u_ref, up_ref, eidx_ref, rank_ref, gw_ref, cnt_ref):
    seq = x_ref.shape[1]
    tc = ROUTE_CHUNK
    e_iota = lax.broadcasted_iota(jnp.int32, (N_EXPERTS, tc), 0)
    g_iota = lax.broadcasted_iota(jnp.int32, (N_GROUPS, tc), 0)
    s_iota = lax.broadcasted_iota(jnp.int32, (GROUP_SIZE, tc), 0)
    r = lax.broadcasted_iota(jnp.int32, (tc, tc), 0)
    c = lax.broadcasted_iota(jnp.int32, (tc, tc), 1)
    upper = jnp.where(r <= c, 1.0, 0.0).astype(BF16)
    ones = jnp.ones((16, tc), BF16)
    carry = jnp.zeros((N_EXPERTS, 1), F32)
    cnt = jnp.zeros((16, N_EXPERTS), F32)

    for ch in range(seq // tc):
        rows = slice(ch * tc, (ch + 1) * tc)
        u = x_ref[0, rows, :] * (1.0 + sc_ref[0]) + sh_ref[0]
        u_ref[0, rows, :] = u.astype(BF16)
        up_ref[0, rows, :] = _pack_words(u)
        u_hi, u_lo = _split2(u)
        logits = _dot_nt(wrh_ref[...], u_hi) + _dot_nt(wrh_ref[...], u_lo) + _dot_nt(wrl_ref[...], u_hi)
        scores = _sigmoid(logits)
        sel = scores + bias_ref[...]

        gs_rows = []
        for g in range(N_GROUPS):
            blk = sel[g * GROUP_SIZE:(g + 1) * GROUP_SIZE]
            m1 = jnp.max(blk, axis=0, keepdims=True)
            i1 = jnp.min(jnp.where(blk == m1, s_iota, GROUP_SIZE), axis=0, keepdims=True)
            m2 = jnp.max(jnp.where(s_iota == i1, -jnp.inf, blk), axis=0, keepdims=True)
            gs_rows.append(m1 + m2)
        gs = jnp.concatenate(gs_rows, axis=0)
        beaten = jnp.zeros((N_GROUPS, tc), F32)
        for g in range(N_GROUPS):
            other = gs[g:g + 1]
            wins = (other > gs) | ((other == gs) & (g_iota > g))
            beaten = beaten + jnp.where(wins, 1.0, 0.0)
        keep = beaten < TOPK_GROUPS
        cand = jnp.concatenate(
            [jnp.where(keep[g:g + 1], sel[g * GROUP_SIZE:(g + 1) * GROUP_SIZE], NEG_INF) for g in range(N_GROUPS)],
            axis=0)
        chosen = jnp.zeros((N_EXPERTS, tc), F32)
        picks = []
        for _ in range(TOP_K):
            m = jnp.max(cand, axis=0, keepdims=True)
            idx = jnp.min(jnp.where(cand == m, e_iota, N_EXPERTS), axis=0, keepdims=True)
            hit = e_iota == idx
            chosen = jnp.where(hit, 1.0, chosen)
            cand = jnp.where(hit, -jnp.inf, cand)
            picks.append(idx)
        g_raw = jnp.where(chosen > 0.0, scores, 0.0)
        gate = g_raw / jnp.sum(g_raw, axis=0, keepdims=True) * ROUTED_SCALE

        chosen_b = chosen.astype(BF16)
        incl = _dot(chosen_b, upper) + carry
        carry = incl[:, tc - 1:tc]
        cnt = cnt + _dot_nt(ones, chosen_b)

        rank_rows, gate_rows = [], []
        for idx in picks:
            hit = e_iota == idx
            rank_rows.append(jnp.sum(jnp.where(hit, incl - 1.0, 0.0), axis=0, keepdims=True))
            gate_rows.append(jnp.sum(jnp.where(hit, gate, 0.0), axis=0, keepdims=True))
        eidx_ref[0, :, rows] = jnp.concatenate(picks, axis=0)
        rank_ref[0, :, rows] = jnp.concatenate(rank_rows, axis=0).astype(jnp.int32)
        gate_t = jnp.concatenate(gate_rows + [jnp.zeros((LANES - TOP_K, tc), F32)], axis=0).T
        gw_ref[0, rows, :] = gate_t[:, :TOP_K]

    cnt_ref[0] = cnt[0:1].astype(jnp.int32)


def _router(x1, mod3, wr_hi, wr_lo, bias):
    bsz, seq, d = x1.shape
    full = pl.BlockSpec((1, seq, d), lambda b: (b, 0, 0))
    ks = pl.BlockSpec((1, TOP_K, seq), lambda b: (b, 0, 0))
    return pl.pallas_call(
        _router_kernel,
        grid=(bsz,),
        in_specs=[full,
                  pl.BlockSpec((1, 1, d), lambda b: (b, 0, 4)),
                  pl.BlockSpec((1, 1, d), lambda b: (b, 0, 3)),
                  pl.BlockSpec(wr_hi.shape, lambda b: (0, 0)),
                  pl.BlockSpec(wr_lo.shape, lambda b: (0, 0)),
                  pl.BlockSpec(bias.shape, lambda b: (0, 0))],
        out_specs=[full, pl.BlockSpec((1, seq, PACKED), lambda b: (b, 0, 0)), ks, ks,
                   pl.BlockSpec((1, seq, TOP_K), lambda b: (b, 0, 0)),
                   pl.BlockSpec((1, 1, N_EXPERTS), lambda b: (b, 0, 0))],
        out_shape=[jax.ShapeDtypeStruct((bsz, seq, d), BF16),
                   jax.ShapeDtypeStruct((bsz, seq, PACKED), F32),
                   jax.ShapeDtypeStruct((bsz, TOP_K, seq), jnp.int32),
                   jax.ShapeDtypeStruct((bsz, TOP_K, seq), jnp.int32),
                   jax.ShapeDtypeStruct((bsz, seq, TOP_K), F32),
                   jax.ShapeDtypeStruct((bsz, 1, N_EXPERTS), jnp.int32)],
        compiler_params=_params(("arbitrary",), 48),
        name="router",
    )(x1, mod3, mod3, wr_hi, wr_lo, bias)


def _slot_kernel(base_ref, eidx_ref, rank_ref, slot_ref):
    b = pl.program_id(0)
    eidx = eidx_ref[0]
    start = jnp.zeros(eidx.shape, jnp.int32)
    for e in range(N_EXPERTS):
        start = jnp.where(eidx == e, base_ref[b * N_EXPERTS + e], start)
    slot_ref[0] = start + rank_ref[0]


def _slots(base, eidx, rank):
    bsz, _, seq = eidx.shape
    ks = pl.BlockSpec((1, TOP_K, seq), lambda b, c: (b, 0, 0))
    return pl.pallas_call(
        _slot_kernel,
        grid_spec=pltpu.PrefetchScalarGridSpec(num_scalar_prefetch=1, grid=(bsz,), in_specs=[ks, ks], out_specs=ks),
        out_shape=jax.ShapeDtypeStruct((bsz, TOP_K, seq), jnp.int32),
        compiler_params=_params(("arbitrary",), 16),
        name="slots",
    )(base.reshape(bsz * N_EXPERTS), eidx, rank)


def _sc_mesh():
    return plsc.VectorSubcoreMesh(core_axis_name="core", subcore_axis_name="subcore")


def _sc_worker():
    return lax.axis_index("subcore") * SC_CORES + lax.axis_index("core")


def _sc_dispatch(rows, slot, n_slots):
    n_tok, width = rows.shape
    seq = slot.shape[1]
    chunk = DISPATCH_CHUNK
    per_worker = n_tok // SC_WORKERS
    assert per_worker % chunk == 0 and seq % chunk == 0

    @functools.partial(
        pl.kernel, mesh=_sc_mesh(),
        out_type=jax.ShapeDtypeStruct((n_slots, width), rows.dtype),
        scratch_types=[pltpu.VMEM((TOP_K, chunk), jnp.int32), pltpu.VMEM((chunk, width), rows.dtype),
                       pltpu.SemaphoreType.DMA],
    )
    def dispatch(rows_hbm, slot_hbm, out_hbm, idx_v, rows_v, sem):
        base = _sc_worker() * per_worker

        @pl.loop(0, per_worker // chunk)
        def _(i):
            off = base + i * chunk
            b = off // seq
            n0 = off - b * seq
            pltpu.sync_copy(slot_hbm.at[pl.ds(b * TOP_K, TOP_K), pl.ds(n0, chunk)], idx_v)
            pltpu.sync_copy(rows_hbm.at[pl.ds(off, chunk)], rows_v)
            copies = [pltpu.async_copy(rows_v, out_hbm.at[idx_v.at[k]], sem) for k in range(TOP_K)]
            for cp in copies:
                cp.wait()

    return dispatch(rows, slot)


def _sc_gather(table, idx):
    n_out = idx.shape[0]
    width = table.shape[1]
    chunk = GATHER_CHUNK
    per_worker = n_out // SC_WORKERS
    steps = per_worker // chunk
    assert per_worker % chunk == 0 and steps % 2 == 0

    @functools.partial(
        pl.kernel, mesh=_sc_mesh(),
        out_type=jax.ShapeDtypeStruct((n_out, width), table.dtype),
        scratch_types=[pltpu.VMEM((2, chunk), jnp.int32), pltpu.VMEM((2, chunk, width), table.dtype),
                       pltpu.SemaphoreType.DMA, pltpu.SemaphoreType.DMA((2,))],
    )
    def gather(table_hbm, idx_hbm, out_hbm, idx_v, rows_v, gather_sem, out_sems):
        base = _sc_worker() * per_worker

        def write_out(buf, off):
            return pltpu.make_async_copy(rows_v.at[buf], out_hbm.at[pl.ds(off, chunk)], out_sems.at[buf])

        @pl.loop(0, steps, step=2)
        def _(i):
            for buf in range(2):
                off = base + (i + buf) * chunk

                @pl.when(i >= 2)
                def _():
                    write_out(buf, off - 2 * chunk).wait()

                pltpu.sync_copy(idx_hbm.at[pl.ds(off, chunk)], idx_v.at[buf])
                pltpu.async_copy(table_hbm.at[idx_v.at[buf]], rows_v.at[buf], gather_sem).wait()
                write_out(buf, off).start()

        for buf in range(2):
            write_out(buf, base + (steps - 2 + buf) * chunk).wait()

    return gather(table, idx)


def _gmm_kernel(blk_e_ref, used_ref, xs_ref, wgu_ref, wd_ref, out_ref):
    half = PACKED

    @pl.when(pl.program_id(0) < used_ref[0])
    def _():
        lo, hi = _unpack_words(xs_ref[...])
        hgu = _dot(lo.astype(BF16), wgu_ref[0, :half, :]) + _dot(hi.astype(BF16), wgu_ref[0, half:, :])
        hg = hgu[:, :EXPERT_FF]
        h = (hg * _sigmoid(hg) * hgu[:, EXPERT_FF:]).astype(BF16)
        out_ref[...] = _pack_words(_dot(h, wd_ref[0]))


def _gmm(xs, blk_e, used, wgu, wd):
    n_slots, width = xs.shape
    bm = EXPERT_BLOCK
    d = wgu.shape[1]
    live = lambda i, blk_e, used: (jnp.minimum(i, used[0] - 1), 0)
    return pl.pallas_call(
        _gmm_kernel,
        grid_spec=pltpu.PrefetchScalarGridSpec(
            num_scalar_prefetch=2,
            grid=(n_slots // bm,),
            in_specs=[pl.BlockSpec((bm, width), live),
                      pl.BlockSpec((1, d, 2 * EXPERT_FF), lambda i, blk_e, used: (blk_e[i], 0, 0)),
                      pl.BlockSpec((1, EXPERT_FF, d), lambda i, blk_e, used: (blk_e[i], 0, 0))],
            out_specs=pl.BlockSpec((bm, width), live)),
        out_shape=jax.ShapeDtypeStruct((n_slots, width), F32),
        compiler_params=_params(("arbitrary",), 32),
        name="gmm",
    )(blk_e, used, xs, wgu, wd)


def _ffn_out_kernel(x_ref, u_ref, yg_ref, gw_ref, g2_ref, wsgu_ref, wsd_ref, lng_ref, lnb_ref, out_ref):
    hgu = _dot(u_ref[0], wsgu_ref[...])
    hg = hgu[:, :SHARED_FF]
    h = (hg * _sigmoid(hg) * hgu[:, SHARED_FF:]).astype(BF16)
    shared = _dot(h, wsd_ref[...])
    gw = gw_ref[0]
    acc_lo = shared[:, :PACKED]
    acc_hi = shared[:, PACKED:]
    for k in range(TOP_K):
        lo, hi = _unpack_words(yg_ref[0, k])
        w = gw[:, k:k + 1]
        acc_lo = acc_lo + w * lo
        acc_hi = acc_hi + w * hi
    y = jnp.concatenate([acc_lo, acc_hi], axis=1)
    z = DN_ALPHA * x_ref[0] + (1.0 + g2_ref[0]) * y
    out_ref[0] = _layer_norm(z, lng_ref[...], lnb_ref[...])


def _ffn_out(x1, u2, yg, gw, mod3, wsgu, wsd, lng, lnb):
    bsz, seq, d = x1.shape
    tm = ROW_TILE
    const = lambda shape: pl.BlockSpec(shape, lambda b, j: (0,) * len(shape))
    tok = pl.BlockSpec((1, tm, d), lambda b, j: (b, j, 0))
    return pl.pallas_call(
        _ffn_out_kernel,
        grid=(bsz, seq // tm),
        in_specs=[tok, tok,
                  pl.BlockSpec((1, TOP_K, tm, PACKED), lambda b, j: (b, 0, j, 0)),
                  pl.BlockSpec((1, tm, TOP_K), lambda b, j: (b, j, 0)),
                  pl.BlockSpec((1, 1, d), lambda b, j: (b, 0, 5)),
                  const(wsgu.shape), const(wsd.shape), const(lng.shape), const(lnb.shape)],
        out_specs=tok,
        out_shape=jax.ShapeDtypeStruct((bsz, seq, d), F32),
        compiler_params=_params(("arbitrary", "arbitrary"), 48),
        name="ffn_out",
    )(x1, u2, yg, gw, mod3, wsgu, wsd, lng, lnb)


def _moe_layout(cnt, n_blocks):
    bm = EXPERT_BLOCK
    total = jnp.sum(cnt, axis=0)
    padded = (total + bm - 1) // bm * bm
    ends = jnp.cumsum(padded)
    base = (ends - padded)[None, :] + jnp.cumsum(cnt, axis=0) - cnt
    first_row = jnp.arange(n_blocks, dtype=ends.dtype) * bm
    blk_e = jnp.minimum(jnp.sum(ends[None, :] <= first_row[:, None], axis=1), N_EXPERTS - 1)
    used = (ends[-1:] // bm)
    return base.astype(jnp.int32), blk_e.astype(jnp.int32), used.astype(jnp.int32)


def _rope_tables(seq):
    half = HEAD_DIM // 2
    inv_freq = ROPE_THETA ** (-jnp.arange(half, dtype=F32) * 2.0 / HEAD_DIM)
    ang = jnp.arange(seq, dtype=F32)[:, None] * inv_freq[None, :]
    cos, sin = jnp.cos(ang), jnp.sin(ang)
    zero = jnp.zeros_like(sin)
    reps = LANES // HEAD_DIM
    cos_t = jnp.tile(jnp.concatenate([cos, cos], axis=1), (1, reps))
    sina_t = jnp.tile(jnp.concatenate([-sin, zero], axis=1), (1, reps))
    sinb_t = jnp.tile(jnp.concatenate([zero, sin], axis=1), (1, reps))
    return cos_t, sina_t, sinb_t


def _fox_bias_tables():
    n_split = 3
    aw = N_HEADS * LANES
    heads = jnp.arange(N_HEADS)
    selq = jnp.zeros((n_split, LANES, aw), F32)
    selk = jnp.zeros((n_split, LANES, aw), F32)
    oneq = jnp.zeros((1, aw), F32)
    onek = jnp.zeros((1, aw), F32)
    for j in range(n_split):
        selk = selk.at[j, heads, heads * LANES + FOX_BIAS_LANE + j].set(-1.0)
        selq = selq.at[j, heads, heads * LANES + FOX_BIAS_LANE + n_split + j].set(1.0)
        oneq = oneq.at[0, heads * LANES + FOX_BIAS_LANE + j].set(1.0)
        onek = onek.at[0, heads * LANES + FOX_BIAS_LANE + n_split + j].set(1.0)
    return selq.astype(BF16), selk.astype(BF16), oneq, onek


def kernel(x, c, w_ada, b_ada, w_in, b_forget, b_gate, w_br_fox, w_br_dil, w_o, ln_g, ln_b, w_router, router_bias,
           w_exp_gate, w_exp_up, w_exp_down, w_sh_gate, w_sh_up, w_sh_down):
    bsz, seq, d = x.shape
    depth = w_ada.shape[0]
    hw = HEADS_WIDTH
    n_tok = bsz * seq
    n_blocks = n_tok * TOP_K // EXPERT_BLOCK + N_EXPERTS

    mod = _ada(c, w_ada, b_ada)

    o_f, o_d, o_g = 3 * hw, 3 * hw + N_HEADS, 6 * hw + N_HEADS
    wfox = w_in[:, :, :o_f].astype(BF16)
    wfg = jnp.pad(w_in[:, :, o_f:o_d], ((0, 0), (0, 0), (0, LANES - N_HEADS))).astype(BF16)
    wdil = w_in[:, :, o_d:o_g].astype(BF16)
    wgate = w_in[:, :, o_g:].astype(BF16)
    bfg = jnp.pad(b_forget, ((0, 0), (0, LANES - N_HEADS))).reshape(depth, 1, LANES)
    bgate = b_gate.reshape(depth, 1, 2 * d)
    wr_t = jnp.swapaxes(w_router, 1, 2)
    wr_hi = wr_t.astype(BF16)
    wr_lo = (wr_t - wr_hi.astype(F32)).astype(BF16)
    rbias = router_bias.reshape(depth, N_EXPERTS, 1)
    wgu = jnp.concatenate([w_exp_gate, w_exp_up], axis=-1).astype(BF16)
    wd = w_exp_down.astype(BF16)
    wsgu = jnp.concatenate([w_sh_gate, w_sh_up], axis=-1).astype(BF16)
    wsd = w_sh_down.astype(BF16)
    cos_t, sina_t, sinb_t = _rope_tables(seq)
    selq, selk, oneq, onek = _fox_bias_tables()

    layer_params = (mod, wfox, wdil, wgate, wfg, bfg, bgate, w_br_fox.astype(BF16), w_br_dil.astype(BF16),
                    w_o.astype(BF16), ln_g, ln_b, wr_hi, wr_lo, rbias, wgu, wd, wsgu, wsd)

    def layer(xc, prm):
        (mod_l, wfox_l, wdil_l, wgate_l, wfg_l, bfg_l, bgate_l, wbf_l, wbd_l, wo_l, lng_l, lnb_l,
         wrh_l, wrl_l, rb_l, wgu_l, wd_l, wsgu_l, wsd_l) = prm
        mod3 = mod_l.reshape(bsz, 1, 6 * d)
        qa, ka, vt, *rest = _inproj(
            xc, mod3, wfox_l, wdil_l, wgate_l, wfg_l, bfg_l, bgate_l, cos_t, sina_t, sinb_t, selq, selk, oneq, onek)
        gf, gd = rest[9:]
        yf = _fox(qa, ka, vt)
        dil = [_dil(*rest[3 * i:3 * i + 3]) for i in range(len(DIL_PATTERNS))]
        x1 = _mix(xc, yf, [o for o, _ in dil], [l for _, l in dil], gf, gd, mod3, wbf_l, wbd_l, wo_l,
                  lng_l[0:1], lnb_l[0:1])
        u2, u2p, eidx, rank, gw, cnt = _router(x1, mod3, wrh_l, wrl_l, rb_l)
        base, blk_e, used = _moe_layout(cnt.reshape(bsz, N_EXPERTS), n_blocks)
        slot = _slots(base, eidx, rank)
        xs = _sc_dispatch(u2p.reshape(n_tok, PACKED), slot.reshape(bsz * TOP_K, seq), n_blocks * EXPERT_BLOCK)
        ys = _gmm(xs, blk_e, used, wgu_l, wd_l)
        yg = _sc_gather(ys, slot.reshape(n_tok * TOP_K)).reshape(bsz, TOP_K, seq, PACKED)
        x2 = _ffn_out(x1, u2, yg, gw, mod3, wsgu_l, wsd_l, lng_l[1:2], lnb_l[1:2])
        return x2, None

    out, _ = lax.scan(layer, x, layer_params)
    return out
```

```python
import functools

import jax
import jax.numpy as jnp
from jax import lax
from jax.experimental import pallas as pl
from jax.experimental.pallas import tpu as pltpu
from jax.experimental.pallas import tpu_sc as plsc

D_MODEL = 1024
DEPTH = 4
HEAD_DIM = 64
N_HEADS = 8
HEADS_WIDTH = N_HEADS * HEAD_DIM
DIL_PATTERNS = ((128, 1), (512, 4), (2048, 16))
ATTN_BLOCK = 128
ROPE_THETA = 10000.0
N_EXPERTS = 64
TOP_K = 8
N_GROUPS = 8
GROUP_SIZE = N_EXPERTS // N_GROUPS
TOPK_GROUPS = 4
EXPERT_FF = 256
SHARED_FF = 256
ROUTED_SCALE = 2.5
DN_ALPHA = (2 * DEPTH) ** 0.25
LN_EPS = 1e-5
NEG_INF = -1e30
SCORE_SCALE = HEAD_DIM ** -0.5

LANES = 128
PAIR = 2 * HEAD_DIM
N_PAIRS = HEADS_WIDTH // PAIR

ROW_TILE = 512
FOX_Q = 256
FOX_K = 128
FOX_BIAS_LANE = HEAD_DIM
ROUTE_CHUNK = 512
EXPERT_BLOCK = 512
PACKED = D_MODEL // 2

SC_CORES = 2
SC_SUBCORES = 16
SC_WORKERS = SC_CORES * SC_SUBCORES
DISPATCH_CHUNK = 128
GATHER_CHUNK = 64

BF16 = jnp.bfloat16
F32 = jnp.float32


def _dot(a, b):
    return jnp.dot(a, b, preferred_element_type=F32)


def _dot_nt(a, b):
    return lax.dot_general(a, b, (((1,), (1,)), ((), ())), preferred_element_type=F32)


def _split2(x):
    hi = x.astype(BF16)
    lo = (x - hi.astype(F32)).astype(BF16)
    return hi, lo


def _split3(x):
    hi = x.astype(BF16)
    r = x - hi.astype(F32)
    mid = r.astype(BF16)
    lo = (r - mid.astype(F32)).astype(BF16)
    return hi, mid, lo


def _sigmoid(x):
    return 1.0 / (1.0 + jnp.exp(-x))


def _layer_norm(z, g, b):
    mu = jnp.mean(z, axis=-1, keepdims=True)
    zc = z - mu
    var = jnp.mean(zc * zc, axis=-1, keepdims=True)
    return zc * lax.rsqrt(var + LN_EPS) * g + b


def _pack_words(v):
    half = v.shape[1] // 2
    bits = pltpu.bitcast(v.astype(BF16).astype(F32), jnp.uint32)
    word = (bits[:, half:] & jnp.uint32(0xFFFF0000)) | (bits[:, :half] >> 16)
    return pltpu.bitcast(word, F32)


def _unpack_words(w):
    bits = pltpu.bitcast(w, jnp.uint32)
    return pltpu.bitcast(bits << 16, F32), pltpu.bitcast(bits & jnp.uint32(0xFFFF0000), F32)


def _params(sem, vmem_mb):
    return pltpu.CompilerParams(dimension_semantics=sem, vmem_limit_bytes=vmem_mb * 1024 * 1024)


def _ada_kernel(c_ref, w_ref, b_ref, o_ref):
    c = c_ref[...]
    cs = c * _sigmoid(c)
    c_hi, c_lo = _split2(cs)
    w_hi, w_lo = _split2(w_ref[0])
    o_ref[0] = _dot(c_hi, w_hi) + _dot(c_hi, w_lo) + _dot(c_lo, w_hi) + b_ref[0]


def _ada(c, w_ada, b_ada):
    depth, d, n = w_ada.shape
    bsz = c.shape[0]
    tn = 1536
    return pl.pallas_call(
        _ada_kernel,
        grid=(depth, n // tn),
        in_specs=[
            pl.BlockSpec((bsz, d), lambda l, j: (0, 0)),
            pl.BlockSpec((1, d, tn), lambda l, j: (l, 0, j)),
            pl.BlockSpec((1, 1, tn), lambda l, j: (l, 0, j)),
        ],
        out_specs=pl.BlockSpec((1, bsz, tn), lambda l, j: (l, 0, j)),
        out_shape=jax.ShapeDtypeStruct((depth, bsz, n), F32),
        compiler_params=_params(("arbitrary", "arbitrary"), 40),
        name="ada",
    )(c, w_ada, b_ada.reshape(depth, 1, n))


def _inproj_kernel(x_ref, sc_ref, sh_ref, wfox_ref, wdil_ref, wgate_ref, wfg_ref, bfg_ref, bgate_ref,
                   cos_ref, sina_ref, sinb_ref, selq_ref, selk_ref, oneq_ref, onek_ref,
                   qa_ref, ka_ref, vt_ref, *rest):
    dil_refs = rest[:9]
    gf_ref, gd_ref, carry_ref, perm_ref = rest[9:]
    j = pl.program_id(1)
    tm = x_ref.shape[1]
    u = (x_ref[0] * (1.0 + sc_ref[0]) + sh_ref[0]).astype(BF16)

    ff = _dot(u, wfg_ref[...]) + bfg_ref[...]
    lf = jnp.minimum(ff, 0.0) - jnp.log(1.0 + jnp.exp(-jnp.abs(ff)))
    r = lax.broadcasted_iota(jnp.int32, (tm, tm), 0)
    c = lax.broadcasted_iota(jnp.int32, (tm, tm), 1)
    tri = jnp.where(r >= c, 1.0, 0.0).astype(BF16)
    hi, mid, lo = _split3(lf)
    csum = _dot(tri, hi) + _dot(tri, mid) + _dot(tri, lo)

    @pl.when(j == 0)
    def _():
        carry_ref[...] = jnp.zeros_like(carry_ref)

    cum = csum + carry_ref[...]
    carry_ref[...] = cum[tm - 1:tm, :]

    c_hi, c_mid, c_lo = _split3(cum)
    bias_q = (_dot(c_hi, selq_ref[0]) + _dot(c_mid, selq_ref[1]) + _dot(c_lo, selq_ref[2])) + oneq_ref[...]
    bias_k = (_dot(c_hi, selk_ref[0]) + _dot(c_mid, selk_ref[1]) + _dot(c_lo, selk_ref[2])) + onek_ref[...]
    pf = _dot(u, wfox_ref[...])
    lane = lax.broadcasted_iota(jnp.int32, (1, LANES), 1)
    data = lane < HEAD_DIM
    for dst, off, bias, scale in ((qa_ref, 0, bias_q, SCORE_SCALE), (ka_ref, HEADS_WIDTH, bias_k, 1.0)):
        for p in range(N_PAIRS):
            pair = pf[:, off + p * PAIR: off + (p + 1) * PAIR] * scale
            for hh, src in ((0, pair), (1, pltpu.roll(pair, HEAD_DIM, 1))):
                t = (2 * p + hh) * LANES
                dst[0, :, t:t + LANES] = jnp.where(data, src, bias[:, t:t + LANES]).astype(BF16)
    vt = pf[:, 2 * HEADS_WIDTH:].T
    for cb in range(tm // FOX_K):
        vt_ref[0, cb] = vt[:, cb * FOX_K:(cb + 1) * FOX_K].astype(BF16)

    pd = _dot(u, wdil_ref[...])
    rows = pl.ds(pl.multiple_of(j * tm, tm), tm)
    cos = cos_ref[rows, :]
    sina = sina_ref[rows, :]
    sinb = sinb_ref[rows, :]
    for i, off in ((0, 0), (1, HEADS_WIDTH)):
        for g in range(N_PAIRS):
            xg = pd[:, off + g * LANES: off + (g + 1) * LANES]
            ahead = pltpu.roll(xg, LANES - HEAD_DIM // 2, 1)
            behind = pltpu.roll(xg, HEAD_DIM // 2, 1)
            perm_ref[i, g] = xg * cos + ahead * sina + behind * sinb
    for g in range(N_PAIRS):
        perm_ref[2, g] = pd[:, 2 * HEADS_WIDTH + g * LANES:2 * HEADS_WIDTH + (g + 1) * LANES]
    for i in range(3):
        for (_, dilation), dst in zip(DIL_PATTERNS, dil_refs[i::3]):
            for res in range(dilation):
                for g in range(N_PAIRS):
                    dst[0, res, :, g * LANES:(g + 1) * LANES] = perm_ref[
                        i, g, pl.ds(res, tm // dilation, stride=dilation), :].astype(BF16)

    pg = _sigmoid(_dot(u, wgate_ref[...]) + bgate_ref[...])
    gf_ref[0] = pg[:, :D_MODEL].astype(BF16)
    gd_ref[0] = pg[:, D_MODEL:].astype(BF16)


def _inproj(x, mod3, wfox, wdil, wgate, wfg, bfg, bgate, cos_t, sina_t, sinb_t, selq, selk, oneq, onek):
    bsz, seq, d = x.shape
    tm = ROW_TILE
    const = lambda shape: pl.BlockSpec(shape, lambda b, j: (0,) * len(shape))
    tok = lambda w: pl.BlockSpec((1, tm, w), lambda b, j: (b, j, 0))
    hw = HEADS_WIDTH
    aw = N_HEADS * LANES
    dil_shapes, dil_specs = [], []
    for _, dil in DIL_PATTERNS:
        dil_shapes += [jax.ShapeDtypeStruct((bsz, dil, seq // dil, hw), BF16)] * 3
        dil_specs += [pl.BlockSpec((1, dil, tm // dil, hw), lambda b, j: (b, 0, j, 0))] * 3
    outs = ([jax.ShapeDtypeStruct((bsz, seq, aw), BF16)] * 2
            + [jax.ShapeDtypeStruct((bsz, seq // FOX_K, hw, FOX_K), BF16)]
            + dil_shapes + [jax.ShapeDtypeStruct((bsz, seq, d), BF16)] * 2)
    return pl.pallas_call(
        _inproj_kernel,
        grid=(bsz, seq // tm),
        in_specs=[
            tok(d),
            pl.BlockSpec((1, 1, d), lambda b, j: (b, 0, 1)),
            pl.BlockSpec((1, 1, d), lambda b, j: (b, 0, 0)),
            const(wfox.shape), const(wdil.shape), const(wgate.shape), const(wfg.shape),
            const(bfg.shape), const(bgate.shape),
            const(cos_t.shape), const(sina_t.shape), const(sinb_t.shape),
            const(selq.shape), const(selk.shape), const(oneq.shape), const(onek.shape),
        ],
        out_specs=[tok(aw)] * 2 + [pl.BlockSpec((1, tm // FOX_K, hw, FOX_K), lambda b, j: (b, j, 0, 0))]
                  + dil_specs + [tok(d)] * 2,
        out_shape=outs,
        scratch_shapes=[pltpu.VMEM((1, LANES), F32), pltpu.VMEM((3, N_PAIRS, tm, LANES), F32)],
        compiler_params=_params(("arbitrary", "arbitrary"), 56),
        name="inproj",
    )(x, mod3, mod3, wfox, wdil, wgate, wfg, bfg, bgate, cos_t, sina_t, sinb_t, selq, selk, oneq, onek)


def _fox_kernel(qa_ref, ka_ref, vt_ref, o_ref, m_ref, l_ref, acc_ref):
    seq = qa_ref.shape[1]
    tq, tk = FOX_Q, FOX_K
    k_pos = lax.broadcasted_iota(jnp.int32, (tk, tq), 0)
    q_pos = lax.broadcasted_iota(jnp.int32, (tk, tq), 1)

    def q_body(qi, _):
        q0 = pl.multiple_of(qi * tq, tq)
        m_ref[...] = jnp.full(m_ref.shape, NEG_INF, F32)
        l_ref[...] = jnp.zeros(l_ref.shape, F32)
        acc_ref[...] = jnp.zeros(acc_ref.shape, F32)

        def kv_step(kb, k0, visible):
            scores = []
            for h in range(N_HEADS):
                ls = slice(h * LANES, (h + 1) * LANES)
                scores.append(_dot_nt(ka_ref[0, pl.ds(k0, tk), ls], qa_ref[0, pl.ds(q0, tq), ls]))
            m_all = m_ref[...]
            l_all = l_ref[...]
            probs, decay, m_rows, l_rows = [], [], [], []
            for h in range(N_HEADS):
                s = scores[h] if visible is None else jnp.where(visible, scores[h], NEG_INF)
                m_old = m_all[h:h + 1, :]
                m_new = jnp.maximum(m_old, jnp.max(s, axis=0, keepdims=True))
                a = jnp.exp(m_old - m_new)
                e = jnp.exp(s - m_new)
                l_rows.append(a * l_all[h:h + 1, :] + jnp.sum(e, axis=0, keepdims=True))
                m_rows.append(m_new)
                probs.append(e.astype(BF16))
                decay.append(a)
            m_ref[...] = jnp.concatenate(m_rows, axis=0)
            l_ref[...] = jnp.concatenate(l_rows, axis=0)
            for h in range(N_HEADS):
                vh = vt_ref[0, kb, h * HEAD_DIM:(h + 1) * HEAD_DIM, :]
                acc_ref[h] = decay[h] * acc_ref[h] + _dot(vh, probs[h])

        def full_block(jb, carry):
            kv_step(jb, pl.multiple_of(jb * tk, tk), None)
            return carry

        lax.fori_loop(0, qi * (tq // tk), full_block, 0)
        for dblk in range(tq // tk):
            kb = qi * (tq // tk) + dblk
            kv_step(kb, pl.multiple_of(kb * tk, tk), k_pos + dblk * tk <= q_pos)

        for p in range(N_PAIRS):
            pair = jnp.concatenate([acc_ref[2 * p + hh] / l_ref[2 * p + hh:2 * p + hh + 1, :] for hh in range(2)],
                                   axis=0)
            o_ref[0, pl.ds(q0, tq), p * PAIR:(p + 1) * PAIR] = pair.T.astype(BF16)
        return 0

    lax.fori_loop(0, seq // tq, q_body, 0)


def _fox(qa, ka, vt):
    bsz, seq, aw = qa.shape
    hw = HEADS_WIDTH
    full = pl.BlockSpec((1, seq, aw), lambda b: (b, 0, 0))
    return pl.pallas_call(
        _fox_kernel,
        grid=(bsz,),
        in_specs=[full, full, pl.BlockSpec((1, seq // FOX_K, hw, FOX_K), lambda b: (b, 0, 0, 0))],
        out_specs=pl.BlockSpec((1, seq, hw), lambda b: (b, 0, 0)),
        out_shape=jax.ShapeDtypeStruct((bsz, seq, hw), BF16),
        scratch_shapes=[pltpu.VMEM((N_HEADS, FOX_Q), F32), pltpu.VMEM((N_HEADS, FOX_Q), F32),
                        pltpu.VMEM((N_HEADS, HEAD_DIM, FOX_Q), F32)],
        compiler_params=_params(("arbitrary",), 48),
        name="fox",
    )(qa, ka, vt)


def _dil_kernel(q_ref, kc_ref, kp_ref, vc_ref, vp_ref, o_ref, lse_ref):
    n = pl.program_id(2)
    blk = ATTN_BLOCK
    lane = lax.broadcasted_iota(jnp.int32, (1, PAIR), 1)
    first = lane < HEAD_DIM
    i = lax.broadcasted_iota(jnp.int32, (blk, 2 * blk), 0)
    c = lax.broadcasted_iota(jnp.int32, (blk, 2 * blk), 1)
    valid = (c >= i) & (c <= i + blk) & ((c >= blk) | (n > 0))

    for p in range(N_PAIRS):
        ls = slice(p * PAIR, (p + 1) * PAIR)
        qb = q_ref[:, ls]
        zero = jnp.zeros_like(qb)
        kb = jnp.concatenate([kp_ref[:, ls], kc_ref[:, ls]], axis=0)
        vb = jnp.concatenate([vp_ref[:, ls], vc_ref[:, ls]], axis=0)
        res = []
        for hh in range(2):
            qh = jnp.where(first, qb, zero) if hh == 0 else jnp.where(first, zero, qb)
            s = jnp.where(valid, _dot_nt(qh, kb) * SCORE_SCALE, NEG_INF)
            m = jnp.max(s, axis=-1, keepdims=True)
            e = jnp.exp(s - m)
            l = jnp.sum(e, axis=-1, keepdims=True)
            o = _dot(e.astype(BF16), vb) / l
            res.append((o, m + jnp.log(l)))
        o_ref[:, ls] = jnp.where(first, res[0][0], res[1][0]).astype(BF16)
        lse_ref[:, ls] = jnp.where(first, res[0][1], res[1][1])


def _dil(qd, kd, vd):
    bsz, dilation, length, hw = qd.shape
    nb = length // ATTN_BLOCK
    cur = pl.BlockSpec((None, None, ATTN_BLOCK, hw), lambda b, r, n: (b, r, n, 0))
    prev = pl.BlockSpec((None, None, ATTN_BLOCK, hw), lambda b, r, n: (b, r, jnp.maximum(n - 1, 0), 0))
    return pl.pallas_call(
        _dil_kernel,
        grid=(bsz, dilation, nb),
        in_specs=[cur, cur, prev, cur, prev],
        out_specs=[cur, cur],
        out_shape=[jax.ShapeDtypeStruct(qd.shape, BF16), jax.ShapeDtypeStruct(qd.shape, F32)],
        compiler_params=_params(("arbitrary", "arbitrary", "arbitrary"), 32),
        name=f"dil{dilation}",
    )(qd, kd, kd, vd, vd)


def _mix_kernel(x_ref, yf_ref, o1_ref, o2_ref, o3_ref, l1_ref, l2_ref, l3_ref, gf_ref, gd_ref, g1_ref,
                wbf_ref, wbd_ref, wo_ref, lng_ref, lnb_ref, out_ref, nat_ref):
    tm = x_ref.shape[1]

    def natural(i, src_ref):
        dilation = src_ref.shape[1]
        for res in range(dilation):
            for g in range(N_PAIRS):
                nat_ref[i, g, pl.ds(res, tm // dilation, stride=dilation), :] = src_ref[
                    0, res, :, g * LANES:(g + 1) * LANES].astype(F32)
        return jnp.concatenate([nat_ref[i, g] for g in range(N_PAIRS)], axis=1)

    os_ = [natural(i, ref) for i, ref in enumerate((o1_ref, o2_ref, o3_ref))]
    l1, l2, l3 = [natural(3 + i, ref) for i, ref in enumerate((l1_ref, l2_ref, l3_ref))]
    mx = jnp.maximum(jnp.maximum(l1, l2), l3)
    e1, e2, e3 = jnp.exp(l1 - mx), jnp.exp(l2 - mx), jnp.exp(l3 - mx)
    yd = (e1 * os_[0] + e2 * os_[1] + e3 * os_[2]) / (e1 + e2 + e3)
    merged = (gf_ref[0].astype(F32) * _dot(yf_ref[0], wbf_ref[...])
              + gd_ref[0].astype(F32) * _dot(yd.astype(BF16), wbd_ref[...]))
    y = _dot(merged.astype(BF16), wo_ref[...])
    z = DN_ALPHA * x_ref[0] + (1.0 + g1_ref[0]) * y
    out_ref[0] = _layer_norm(z, lng_ref[...], lnb_ref[...])


def _mix(x, yf, os_, lses, gf, gd, mod3, wbf, wbd, wo, lng, lnb):
    bsz, seq, d = x.shape
    tm = ROW_TILE
    const = lambda shape: pl.BlockSpec(shape, lambda b, j: (0,) * len(shape))
    tok = lambda w: pl.BlockSpec((1, tm, w), lambda b, j: (b, j, 0))
    hw = HEADS_WIDTH
    res_major = [pl.BlockSpec((1, dil, tm // dil, hw), lambda b, j: (b, 0, j, 0)) for _, dil in DIL_PATTERNS]
    return pl.pallas_call(
        _mix_kernel,
        grid=(bsz, seq // tm),
        in_specs=[tok(d), tok(hw)] + res_major * 2 + [tok(d), tok(d),
                  pl.BlockSpec((1, 1, d), lambda b, j: (b, 0, 2)),
                  const(wbf.shape), const(wbd.shape), const(wo.shape), const(lng.shape), const(lnb.shape)],
        out_specs=tok(d),
        out_shape=jax.ShapeDtypeStruct((bsz, seq, d), F32),
        scratch_shapes=[pltpu.VMEM((6, N_PAIRS, tm, LANES), F32)],
        compiler_params=_params(("arbitrary", "arbitrary"), 48),
        name="mix",
    )(x, yf, *os_, *lses, gf, gd, mod3, wbf, wbd, wo, lng, lnb)


def _router_kernel(x_ref, sc_ref, sh_ref, wrh_ref, wrl_ref, bias_ref,
                   u_ref, up_ref, eidx_ref, rank_ref, gw_ref, cnt_ref):
    seq = x_ref.shape[1]
    tc = ROUTE_CHUNK
    e_iota = lax.broadcasted_iota(jnp.int32, (N_EXPERTS, tc), 0)
    g_iota = lax.broadcasted_iota(jnp.int32, (N_GROUPS, tc), 0)
    s_iota = lax.broadcasted_iota(jnp.int32, (GROUP_SIZE, tc), 0)
    r = lax.broadcasted_iota(jnp.int32, (tc, tc), 0)
    c = lax.broadcasted_iota(jnp.int32, (tc, tc), 1)
    upper = jnp.where(r <= c, 1.0, 0.0).astype(BF16)
    ones = jnp.ones((16, tc), BF16)
    carry = jnp.zeros((N_EXPERTS, 1), F32)
    cnt = jnp.zeros((16, N_EXPERTS), F32)

    for ch in range(seq // tc):
        rows = slice(ch * tc, (ch + 1) * tc)
        u = x_ref[0, rows, :] * (1.0 + sc_ref[0]) + sh_ref[0]
        u_ref[0, rows, :] = u.astype(BF16)
        up_ref[0, rows, :] = _pack_words(u)
        u_hi, u_lo = _split2(u)
        logits = _dot_nt(wrh_ref[...], u_hi) + _dot_nt(wrh_ref[...], u_lo) + _dot_nt(wrl_ref[...], u_hi)
        scores = _sigmoid(logits)
        sel = scores + bias_ref[...]

        gs_rows = []
        for g in range(N_GROUPS):
            blk = sel[g * GROUP_SIZE:(g + 1) * GROUP_SIZE]
            m1 = jnp.max(blk, axis=0, keepdims=True)
            i1 = jnp.min(jnp.where(blk == m1, s_iota, GROUP_SIZE), axis=0, keepdims=True)
            m2 = jnp.max(jnp.where(s_iota == i1, -jnp.inf, blk), axis=0, keepdims=True)
            gs_rows.append(m1 + m2)
        gs = jnp.concatenate(gs_rows, axis=0)
        beaten = jnp.zeros((N_GROUPS, tc), F32)
        for g in range(N_GROUPS):
            other = gs[g:g + 1]
            wins = (other > gs) | ((other == gs) & (g_iota > g))
            beaten = beaten + jnp.where(wins, 1.0, 0.0)
        keep = beaten < TOPK_GROUPS
        cand = jnp.concatenate(
            [jnp.where(keep[g:g + 1], sel[g * GROUP_SIZE:(g + 1) * GROUP_SIZE], NEG_INF) for g in range(N_GROUPS)],
            axis=0)
        chosen = jnp.zeros((N_EXPERTS, tc), F32)
        picks = []
        for _ in range(TOP_K):
            m = jnp.max(cand, axis=0, keepdims=True)
            idx = jnp.min(jnp.where(cand == m, e_iota, N_EXPERTS), axis=0, keepdims=True)
            hit = e_iota == idx
            chosen = jnp.where(hit, 1.0, chosen)
            cand = jnp.where(hit, -jnp.inf, cand)
            picks.append(idx)
        g_raw = jnp.where(chosen > 0.0, scores, 0.0)
        gate = g_raw / jnp.sum(g_raw, axis=0, keepdims=True) * ROUTED_SCALE

        chosen_b = chosen.astype(BF16)
        incl = _dot(chosen_b, upper) + carry
        carry = incl[:, tc - 1:tc]
        cnt = cnt + _dot_nt(ones, chosen_b)

        rank_rows, gate_rows = [], []
        for idx in picks:
            hit = e_iota == idx
            rank_rows.append(jnp.sum(jnp.where(hit, incl - 1.0, 0.0), axis=0, keepdims=True))
            gate_rows.append(jnp.sum(jnp.where(hit, gate, 0.0), axis=0, keepdims=True))
        eidx_ref[0, :, rows] = jnp.concatenate(picks, axis=0)
        rank_ref[0, :, rows] = jnp.concatenate(rank_rows, axis=0).astype(jnp.int32)
        gate_t = jnp.concatenate(gate_rows + [jnp.zeros((LANES - TOP_K, tc), F32)], axis=0).T
        gw_ref[0, rows, :] = gate_t[:, :TOP_K]

    cnt_ref[0] = cnt[0:1].astype(jnp.int32)


def _router(x1, mod3, wr_hi, wr_lo, bias):
    bsz, seq, d = x1.shape
    full = pl.BlockSpec((1, seq, d), lambda b: (b, 0, 0))
    ks = pl.BlockSpec((1, TOP_K, seq), lambda b: (b, 0, 0))
    return pl.pallas_call(
        _router_kernel,
        grid=(bsz,),
        in_specs=[full,
                  pl.BlockSpec((1, 1, d), lambda b: (b, 0, 4)),
                  pl.BlockSpec((1, 1, d), lambda b: (b, 0, 3)),
                  pl.BlockSpec(wr_hi.shape, lambda b: (0, 0)),
                  pl.BlockSpec(wr_lo.shape, lambda b: (0, 0)),
                  pl.BlockSpec(bias.shape, lambda b: (0, 0))],
        out_specs=[full, pl.BlockSpec((1, seq, PACKED), lambda b: (b, 0, 0)), ks, ks,
                   pl.BlockSpec((1, seq, TOP_K), lambda b: (b, 0, 0)),
                   pl.BlockSpec((1, 1, N_EXPERTS), lambda b: (b, 0, 0))],
        out_shape=[jax.ShapeDtypeStruct((bsz, seq, d), BF16),
                   jax.ShapeDtypeStruct((bsz, seq, PACKED), F32),
                   jax.ShapeDtypeStruct((bsz, TOP_K, seq), jnp.int32),
                   jax.ShapeDtypeStruct((bsz, TOP_K, seq), jnp.int32),
                   jax.ShapeDtypeStruct((bsz, seq, TOP_K), F32),
                   jax.ShapeDtypeStruct((bsz, 1, N_EXPERTS), jnp.int32)],
        compiler_params=_params(("arbitrary",), 48),
        name="router",
    )(x1, mod3, mod3, wr_hi, wr_lo, bias)


def _slot_kernel(base_ref, eidx_ref, rank_ref, slot_ref):
    b = pl.program_id(0)
    eidx = eidx_ref[0]
    start = jnp.zeros(eidx.shape, jnp.int32)
    for e in range(N_EXPERTS):
        start = jnp.where(eidx == e, base_ref[b * N_EXPERTS + e], start)
    slot_ref[0] = start + rank_ref[0]


def _slots(base, eidx, rank):
    bsz, _, seq = eidx.shape
    ks = pl.BlockSpec((1, TOP_K, seq), lambda b, c: (b, 0, 0))
    return pl.pallas_call(
        _slot_kernel,
        grid_spec=pltpu.PrefetchScalarGridSpec(num_scalar_prefetch=1, grid=(bsz,), in_specs=[ks, ks], out_specs=ks),
        out_shape=jax.ShapeDtypeStruct((bsz, TOP_K, seq), jnp.int32),
        compiler_params=_params(("arbitrary",), 16),
        name="slots",
    )(base.reshape(bsz * N_EXPERTS), eidx, rank)


def _sc_mesh():
    return plsc.VectorSubcoreMesh(core_axis_name="core", subcore_axis_name="subcore")


def _sc_worker():
    return lax.axis_index("subcore") * SC_CORES + lax.axis_index("core")


def _sc_dispatch(rows, slot, n_slots):
    n_tok, width = rows.shape
    seq = slot.shape[1]
    chunk = DISPATCH_CHUNK
    per_worker = n_tok // SC_WORKERS
    assert per_worker % chunk == 0 and seq % chunk == 0

    @functools.partial(
        pl.kernel, mesh=_sc_mesh(),
        out_type=jax.ShapeDtypeStruct((n_slots, width), rows.dtype),
        scratch_types=[pltpu.VMEM((TOP_K, chunk), jnp.int32), pltpu.VMEM((chunk, width), rows.dtype),
                       pltpu.SemaphoreType.DMA],
    )
    def dispatch(rows_hbm, slot_hbm, out_hbm, idx_v, rows_v, sem):
        base = _sc_worker() * per_worker

        @pl.loop(0, per_worker // chunk)
        def _(i):
            off = base + i * chunk
            b = off // seq
            n0 = off - b * seq
            pltpu.sync_copy(slot_hbm.at[pl.ds(b * TOP_K, TOP_K), pl.ds(n0, chunk)], idx_v)
            pltpu.sync_copy(rows_hbm.at[pl.ds(off, chunk)], rows_v)
            copies = [pltpu.async_copy(rows_v, out_hbm.at[idx_v.at[k]], sem) for k in range(TOP_K)]
            for cp in copies:
                cp.wait()

    return dispatch(rows, slot)


def _sc_gather(table, idx):
    n_out = idx.shape[0]
    width = table.shape[1]
    chunk = GATHER_CHUNK
    per_worker = n_out // SC_WORKERS
    steps = per_worker // chunk
    assert per_worker % chunk == 0 and steps % 2 == 0

    @functools.partial(
        pl.kernel, mesh=_sc_mesh(),
        out_type=jax.ShapeDtypeStruct((n_out, width), table.dtype),
        scratch_types=[pltpu.VMEM((2, chunk), jnp.int32), pltpu.VMEM((2, chunk, width), table.dtype),
                       pltpu.SemaphoreType.DMA, pltpu.SemaphoreType.DMA((2,))],
    )
    def gather(table_hbm, idx_hbm, out_hbm, idx_v, rows_v, gather_sem, out_sems):
        base = _sc_worker() * per_worker

        def write_out(buf, off):
            return pltpu.make_async_copy(rows_v.at[buf], out_hbm.at[pl.ds(off, chunk)], out_sems.at[buf])

        @pl.loop(0, steps, step=2)
        def _(i):
            for buf in range(2):
                off = base + (i + buf) * chunk

                @pl.when(i >= 2)
                def _():
                    write_out(buf, off - 2 * chunk).wait()

                pltpu.sync_copy(idx_hbm.at[pl.ds(off, chunk)], idx_v.at[buf])
                pltpu.async_copy(table_hbm.at[idx_v.at[buf]], rows_v.at[buf], gather_sem).wait()
                write_out(buf, off).start()

        for buf in range(2):
            write_out(buf, base + (steps - 2 + buf) * chunk).wait()

    return gather(table, idx)


def _gmm_kernel(blk_e_ref, used_ref, xs_ref, wgu_ref, wd_ref, out_ref):
    half = PACKED

    @pl.when(pl.program_id(0) < used_ref[0])
    def _():
        lo, hi = _unpack_words(xs_ref[...])
        hgu = _dot(lo.astype(BF16), wgu_ref[0, :half, :]) + _dot(hi.astype(BF16), wgu_ref[0, half:, :])
        hg = hgu[:, :EXPERT_FF]
        h = (hg * _sigmoid(hg) * hgu[:, EXPERT_FF:]).astype(BF16)
        out_ref[...] = _pack_words(_dot(h, wd_ref[0]))


def _gmm(xs, blk_e, used, wgu, wd):
    n_slots, width = xs.shape
    bm = EXPERT_BLOCK
    d = wgu.shape[1]
    live = lambda i, blk_e, used: (jnp.minimum(i, used[0] - 1), 0)
    return pl.pallas_call(
        _gmm_kernel,
        grid_spec=pltpu.PrefetchScalarGridSpec(
            num_scalar_prefetch=2,
            grid=(n_slots // bm,),
            in_specs=[pl.BlockSpec((bm, width), live),
                      pl.BlockSpec((1, d, 2 * EXPERT_FF), lambda i, blk_e, used: (blk_e[i], 0, 0)),
                      pl.BlockSpec((1, EXPERT_FF, d), lambda i, blk_e, used: (blk_e[i], 0, 0))],
            out_specs=pl.BlockSpec((bm, width), live)),
        out_shape=jax.ShapeDtypeStruct((n_slots, width), F32),
        compiler_params=_params(("arbitrary",), 32),
        name="gmm",
    )(blk_e, used, xs, wgu, wd)


def _ffn_out_kernel(x_ref, u_ref, yg_ref, gw_ref, g2_ref, wsgu_ref, wsd_ref, lng_ref, lnb_ref, out_ref):
    hgu = _dot(u_ref[0], wsgu_ref[...])
    hg = hgu[:, :SHARED_FF]
    h = (hg * _sigmoid(hg) * hgu[:, SHARED_FF:]).astype(BF16)
    shared = _dot(h, wsd_ref[...])
    gw = gw_ref[0]
    acc_lo = shared[:, :PACKED]
    acc_hi = shared[:, PACKED:]
    for k in range(TOP_K):
        lo, hi = _unpack_words(yg_ref[0, k])
        w = gw[:, k:k + 1]
        acc_lo = acc_lo + w * lo
        acc_hi = acc_hi + w * hi
    y = jnp.concatenate([acc_lo, acc_hi], axis=1)
    z = DN_ALPHA * x_ref[0] + (1.0 + g2_ref[0]) * y
    out_ref[0] = _layer_norm(z, lng_ref[...], lnb_ref[...])


def _ffn_out(x1, u2, yg, gw, mod3, wsgu, wsd, lng, lnb):
    bsz, seq, d = x1.shape
    tm = ROW_TILE
    const = lambda shape: pl.BlockSpec(shape, lambda b, j: (0,) * len(shape))
    tok = pl.BlockSpec((1, tm, d), lambda b, j: (b, j, 0))
    return pl.pallas_call(
        _ffn_out_kernel,
        grid=(bsz, seq // tm),
        in_specs=[tok, tok,
                  pl.BlockSpec((1, TOP_K, tm, PACKED), lambda b, j: (b, 0, j, 0)),
                  pl.BlockSpec((1, tm, TOP_K), lambda b, j: (b, j, 0)),
                  pl.BlockSpec((1, 1, d), lambda b, j: (b, 0, 5)),
                  const(wsgu.shape), const(wsd.shape), const(lng.shape), const(lnb.shape)],
        out_specs=tok,
        out_shape=jax.ShapeDtypeStruct((bsz, seq, d), F32),
        compiler_params=_params(("arbitrary", "arbitrary"), 48),
        name="ffn_out",
    )(x1, u2, yg, gw, mod3, wsgu, wsd, lng, lnb)


def _moe_layout(cnt, n_blocks):
    bm = EXPERT_BLOCK
    total = jnp.sum(cnt, axis=0)
    padded = (total + bm - 1) // bm * bm
    ends = jnp.cumsum(padded)
    base = (ends - padded)[None, :] + jnp.cumsum(cnt, axis=0) - cnt
    first_row = jnp.arange(n_blocks, dtype=ends.dtype) * bm
    blk_e = jnp.minimum(jnp.sum(ends[None, :] <= first_row[:, None], axis=1), N_EXPERTS - 1)
    used = (ends[-1:] // bm)
    return base.astype(jnp.int32), blk_e.astype(jnp.int32), used.astype(jnp.int32)


def _rope_tables(seq):
    half = HEAD_DIM // 2
    inv_freq = ROPE_THETA ** (-jnp.arange(half, dtype=F32) * 2.0 / HEAD_DIM)
    ang = jnp.arange(seq, dtype=F32)[:, None] * inv_freq[None, :]
    cos, sin = jnp.cos(ang), jnp.sin(ang)
    zero = jnp.zeros_like(sin)
    reps = LANES // HEAD_DIM
    cos_t = jnp.tile(jnp.concatenate([cos, cos], axis=1), (1, reps))
    sina_t = jnp.tile(jnp.concatenate([-sin, zero], axis=1), (1, reps))
    sinb_t = jnp.tile(jnp.concatenate([zero, sin], axis=1), (1, reps))
    return cos_t, sina_t, sinb_t


def _fox_bias_tables():
    n_split = 3
    aw = N_HEADS * LANES
    heads = jnp.arange(N_HEADS)
    selq = jnp.zeros((n_split, LANES, aw), F32)
    selk = jnp.zeros((n_split, LANES, aw), F32)
    oneq = jnp.zeros((1, aw), F32)
    onek = jnp.zeros((1, aw), F32)
    for j in range(n_split):
        selk = selk.at[j, heads, heads * LANES + FOX_BIAS_LANE + j].set(-1.0)
        selq = selq.at[j, heads, heads * LANES + FOX_BIAS_LANE + n_split + j].set(1.0)
        oneq = oneq.at[0, heads * LANES + FOX_BIAS_LANE + j].set(1.0)
        onek = onek.at[0, heads * LANES + FOX_BIAS_LANE + n_split + j].set(1.0)
    return selq.astype(BF16), selk.astype(BF16), oneq, onek


def kernel(x, c, w_ada, b_ada, w_in, b_forget, b_gate, w_br_fox, w_br_dil, w_o, ln_g, ln_b, w_router, router_bias,
           w_exp_gate, w_exp_up, w_exp_down, w_sh_gate, w_sh_up, w_sh_down):
    bsz, seq, d = x.shape
    depth = w_ada.shape[0]
    hw = HEADS_WIDTH
    n_tok = bsz * seq

    mod = _ada(c, w_ada, b_ada)

    o_f, o_d, o_g = 3 * hw, 3 * hw + N_HEADS, 6 * hw + N_HEADS
    wfox = w_in[:, :, :o_f].astype(BF16)
    wfg = jnp.pad(w_in[:, :, o_f:o_d], ((0, 0), (0, 0), (0, LANES - N_HEADS))).astype(BF16)
    wdil = w_in[:, :, o_d:o_g].astype(BF16)
    wgate = w_in[:, :, o_g:].astype(BF16)
    bfg = jnp.pad(b_forget, ((0, 0), (0, LANES - N_HEADS))).reshape(depth, 1, LANES)
    bgate = b_gate.reshape(depth, 1, 2 * d)
    wr_t = jnp.swapaxes(w_router, 1, 2)
    wr_hi = wr_t.astype(BF16)
    wr_lo = (wr_t - wr_hi.astype(F32)).astype(BF16)
    rbias = router_bias.reshape(depth, N_EXPERTS, 1)
    wgu = jnp.concatenate([w_exp_gate, w_exp_up], axis=-1).astype(BF16)
    wd = w_exp_down.astype(BF16)
    wsgu = jnp.concatenate([w_sh_gate, w_sh_up], axis=-1).astype(BF16)
    wsd = w_sh_down.astype(BF16)
    cos_t, sina_t, sinb_t = _rope_tables(seq)
    selq, selk, oneq, onek = _fox_bias_tables()

    layer_params = (mod, wfox, wdil, wgate, wfg, bfg, bgate, w_br_fox.astype(BF16), w_br_dil.astype(BF16),
                    w_o.astype(BF16), ln_g, ln_b, wr_hi, wr_lo, rbias, wgu, wd, wsgu, wsd)

    n_streams = 2 if n_tok % (2 * SC_WORKERS * DISPATCH_CHUNK) == 0 and bsz % 2 == 0 else 1
    sb = bsz // n_streams
    s_tok = sb * seq
    n_blocks = s_tok * TOP_K // EXPERT_BLOCK + N_EXPERTS

    def stream_layer(xc, mod3, prm):
        (_, wfox_l, wdil_l, wgate_l, wfg_l, bfg_l, bgate_l, wbf_l, wbd_l, wo_l, lng_l, lnb_l,
         wrh_l, wrl_l, rb_l, wgu_l, wd_l, wsgu_l, wsd_l) = prm
        qa, ka, vt, *rest = _inproj(
            xc, mod3, wfox_l, wdil_l, wgate_l, wfg_l, bfg_l, bgate_l, cos_t, sina_t, sinb_t, selq, selk, oneq, onek)
        gf, gd = rest[9:]
        yf = _fox(qa, ka, vt)
        dil = [_dil(*rest[3 * i:3 * i + 3]) for i in range(len(DIL_PATTERNS))]
        x1 = _mix(xc, yf, [o for o, _ in dil], [l for _, l in dil], gf, gd, mod3, wbf_l, wbd_l, wo_l,
                  lng_l[0:1], lnb_l[0:1])
        u2, u2p, eidx, rank, gw, cnt = _router(x1, mod3, wrh_l, wrl_l, rb_l)
        base, blk_e, used = _moe_layout(cnt.reshape(sb, N_EXPERTS), n_blocks)
        slot = _slots(base, eidx, rank)
        xs = _sc_dispatch(u2p.reshape(s_tok, PACKED), slot.reshape(sb * TOP_K, seq), n_blocks * EXPERT_BLOCK)
        ys = _gmm(xs, blk_e, used, wgu_l, wd_l)
        yg = _sc_gather(ys, slot.reshape(s_tok * TOP_K)).reshape(sb, TOP_K, seq, PACKED)
        return _ffn_out(x1, u2, yg, gw, mod3, wsgu_l, wsd_l, lng_l[1:2], lnb_l[1:2])

    def layer(xcs, prm):
        mod3 = prm[0].reshape(bsz, 1, 6 * d)
        return tuple(stream_layer(xc, mod3[i * sb:(i + 1) * sb], prm) for i, xc in enumerate(xcs)), None

    outs, _ = lax.scan(layer, tuple(x[i * sb:(i + 1) * sb] for i in range(n_streams)), layer_params)
    return outs[0] if n_streams == 1 else jnp.concatenate(outs, axis=0)
```

```python
import functools

import jax
import jax.numpy as jnp
from jax import lax
from jax.experimental import pallas as pl
from jax.experimental.pallas import tpu as pltpu
from jax.experimental.pallas import tpu_sc as plsc

D_MODEL = 1024
DEPTH = 4
HEAD_DIM = 64
N_HEADS = 8
HEADS_WIDTH = N_HEADS * HEAD_DIM
DIL_PATTERNS = ((128, 1), (512, 4), (2048, 16))
ATTN_BLOCK = 128
ROPE_THETA = 10000.0
N_EXPERTS = 64
TOP_K = 8
N_GROUPS = 8
GROUP_SIZE = N_EXPERTS // N_GROUPS
TOPK_GROUPS = 4
EXPERT_FF = 256
SHARED_FF = 256
ROUTED_SCALE = 2.5
DN_ALPHA = (2 * DEPTH) ** 0.25
LN_EPS = 1e-5
NEG_INF = -1e30
SCORE_SCALE = HEAD_DIM ** -0.5
LOG2_E = 1.4426950408889634

LANES = 128
PAIR = 2 * HEAD_DIM
N_PAIRS = HEADS_WIDTH // PAIR

ROW_TILE = 512
FOX_Q = 256
FOX_K = 128
FOX_BIAS_LANE = HEAD_DIM
DIL_STEP_BLOCKS = 4
ROUTE_CHUNK = 512
EXPERT_BLOCK = 512
PACKED = D_MODEL // 2

SC_CORES = 2
SC_SUBCORES = 16
SC_WORKERS = SC_CORES * SC_SUBCORES
DISPATCH_CHUNK = 128
GATHER_CHUNK = 64

BF16 = jnp.bfloat16
F32 = jnp.float32


def _dot(a, b):
    return jnp.dot(a, b, preferred_element_type=F32)


def _dot_nt(a, b):
    return lax.dot_general(a, b, (((1,), (1,)), ((), ())), preferred_element_type=F32)


def _split2(x):
    hi = x.astype(BF16)
    lo = (x - hi.astype(F32)).astype(BF16)
    return hi, lo


def _split3(x):
    hi = x.astype(BF16)
    r = x - hi.astype(F32)
    mid = r.astype(BF16)
    lo = (r - mid.astype(F32)).astype(BF16)
    return hi, mid, lo


def _sigmoid(x):
    return 1.0 / (1.0 + jnp.exp(-x))


def _layer_norm(z, g, b):
    mu = jnp.mean(z, axis=-1, keepdims=True)
    zc = z - mu
    var = jnp.mean(zc * zc, axis=-1, keepdims=True)
    return zc * lax.rsqrt(var + LN_EPS) * g + b


def _pack_words(v):
    half = v.shape[1] // 2
    bits = pltpu.bitcast(v.astype(BF16).astype(F32), jnp.uint32)
    word = (bits[:, half:] & jnp.uint32(0xFFFF0000)) | (bits[:, :half] >> 16)
    return pltpu.bitcast(word, F32)


def _unpack_words(w):
    bits = pltpu.bitcast(w, jnp.uint32)
    return pltpu.bitcast(bits << 16, F32), pltpu.bitcast(bits & jnp.uint32(0xFFFF0000), F32)


def _params(sem, vmem_mb):
    return pltpu.CompilerParams(dimension_semantics=sem, vmem_limit_bytes=vmem_mb * 1024 * 1024)


def _ada_kernel(c_ref, w_ref, b_ref, o_ref):
    c = c_ref[...]
    cs = c * _sigmoid(c)
    c_hi, c_lo = _split2(cs)
    w_hi, w_lo = _split2(w_ref[0])
    o_ref[0] = _dot(c_hi, w_hi) + _dot(c_hi, w_lo) + _dot(c_lo, w_hi) + b_ref[0]


def _ada(c, w_ada, b_ada):
    depth, d, n = w_ada.shape
    bsz = c.shape[0]
    tn = 1536
    return pl.pallas_call(
        _ada_kernel,
        grid=(depth, n // tn),
        in_specs=[
            pl.BlockSpec((bsz, d), lambda l, j: (0, 0)),
            pl.BlockSpec((1, d, tn), lambda l, j: (l, 0, j)),
            pl.BlockSpec((1, 1, tn), lambda l, j: (l, 0, j)),
        ],
        out_specs=pl.BlockSpec((1, bsz, tn), lambda l, j: (l, 0, j)),
        out_shape=jax.ShapeDtypeStruct((depth, bsz, n), F32),
        compiler_params=_params(("arbitrary", "arbitrary"), 40),
        name="ada",
    )(c, w_ada, b_ada.reshape(depth, 1, n))


def _inproj_kernel(x_ref, sc_ref, sh_ref, wfox_ref, wdil_ref, wgate_ref, wfg_ref, bfg_ref, bgate_ref,
                   cos_ref, sina_ref, sinb_ref, selq_ref, selk_ref, oneq_ref, onek_ref,
                   qa_ref, ka_ref, vt_ref, *rest):
    dil_refs = rest[:9]
    gf_ref, gd_ref, carry_ref, perm_ref = rest[9:]
    j = pl.program_id(1)
    tm = x_ref.shape[1]
    u = (x_ref[0] * (1.0 + sc_ref[0]) + sh_ref[0]).astype(BF16)

    ff = _dot(u, wfg_ref[...]) + bfg_ref[...]
    lf = jnp.minimum(ff, 0.0) - jnp.log(1.0 + jnp.exp(-jnp.abs(ff)))
    r = lax.broadcasted_iota(jnp.int32, (tm, tm), 0)
    c = lax.broadcasted_iota(jnp.int32, (tm, tm), 1)
    tri = jnp.where(r >= c, 1.0, 0.0).astype(BF16)
    hi, mid, lo = _split3(lf)
    csum = _dot(tri, hi) + _dot(tri, mid) + _dot(tri, lo)

    @pl.when(j == 0)
    def _():
        carry_ref[...] = jnp.zeros_like(carry_ref)

    cum = csum + carry_ref[...]
    carry_ref[...] = cum[tm - 1:tm, :]

    c_hi, c_mid, c_lo = _split3(cum * LOG2_E)
    bias_q = (_dot(c_hi, selq_ref[0]) + _dot(c_mid, selq_ref[1]) + _dot(c_lo, selq_ref[2])) + oneq_ref[...]
    bias_k = (_dot(c_hi, selk_ref[0]) + _dot(c_mid, selk_ref[1]) + _dot(c_lo, selk_ref[2])) + onek_ref[...]
    pf = _dot(u, wfox_ref[...])
    lane = lax.broadcasted_iota(jnp.int32, (1, LANES), 1)
    data = lane < HEAD_DIM
    for dst, off, bias, scale in ((qa_ref, 0, bias_q, SCORE_SCALE * LOG2_E), (ka_ref, HEADS_WIDTH, bias_k, 1.0)):
        for p in range(N_PAIRS):
            pair = pf[:, off + p * PAIR: off + (p + 1) * PAIR] * scale
            for hh, src in ((0, pair), (1, pltpu.roll(pair, HEAD_DIM, 1))):
                t = (2 * p + hh) * LANES
                dst[0, :, t:t + LANES] = jnp.where(data, src, bias[:, t:t + LANES]).astype(BF16)
    vt = pf[:, 2 * HEADS_WIDTH:].T
    for cb in range(tm // FOX_K):
        vt_ref[0, cb] = vt[:, cb * FOX_K:(cb + 1) * FOX_K].astype(BF16)

    pd = _dot(u, wdil_ref[...])
    rows = pl.ds(pl.multiple_of(j * tm, tm), tm)
    cos = cos_ref[rows, :]
    sina = sina_ref[rows, :]
    sinb = sinb_ref[rows, :]
    for i, off in ((0, 0), (1, HEADS_WIDTH)):
        for g in range(N_PAIRS):
            xg = pd[:, off + g * LANES: off + (g + 1) * LANES]
            ahead = pltpu.roll(xg, LANES - HEAD_DIM // 2, 1)
            behind = pltpu.roll(xg, HEAD_DIM // 2, 1)
            rot = xg * cos + ahead * sina + behind * sinb
            perm_ref[i, g] = rot * SCORE_SCALE if i == 0 else rot
    for g in range(N_PAIRS):
        perm_ref[2, g] = pd[:, 2 * HEADS_WIDTH + g * LANES:2 * HEADS_WIDTH + (g + 1) * LANES]
    for i in range(3):
        for (_, dilation), dst in zip(DIL_PATTERNS, dil_refs[i::3]):
            for res in range(dilation):
                for g in range(N_PAIRS):
                    dst[0, res, :, g * LANES:(g + 1) * LANES] = perm_ref[
                        i, g, pl.ds(res, tm // dilation, stride=dilation), :].astype(BF16)

    pg = _sigmoid(_dot(u, wgate_ref[...]) + bgate_ref[...])
    gf_ref[0] = pg[:, :D_MODEL].astype(BF16)
    gd_ref[0] = pg[:, D_MODEL:].astype(BF16)


def _inproj(x, mod3, wfox, wdil, wgate, wfg, bfg, bgate, cos_t, sina_t, sinb_t, selq, selk, oneq, onek):
    bsz, seq, d = x.shape
    tm = ROW_TILE
    const = lambda shape: pl.BlockSpec(shape, lambda b, j: (0,) * len(shape))
    tok = lambda w: pl.BlockSpec((1, tm, w), lambda b, j: (b, j, 0))
    hw = HEADS_WIDTH
    aw = N_HEADS * LANES
    dil_shapes, dil_specs = [], []
    for _, dil in DIL_PATTERNS:
        dil_shapes += [jax.ShapeDtypeStruct((bsz, dil, seq // dil, hw), BF16)] * 3
        dil_specs += [pl.BlockSpec((1, dil, tm // dil, hw), lambda b, j: (b, 0, j, 0))] * 3
    outs = ([jax.ShapeDtypeStruct((bsz, seq, aw), BF16)] * 2
            + [jax.ShapeDtypeStruct((bsz, seq // FOX_K, hw, FOX_K), BF16)]
            + dil_shapes + [jax.ShapeDtypeStruct((bsz, seq, d), BF16)] * 2)
    return pl.pallas_call(
        _inproj_kernel,
        grid=(bsz, seq // tm),
        in_specs=[
            tok(d),
            pl.BlockSpec((1, 1, d), lambda b, j: (b, 0, 1)),
            pl.BlockSpec((1, 1, d), lambda b, j: (b, 0, 0)),
            const(wfox.shape), const(wdil.shape), const(wgate.shape), const(wfg.shape),
            const(bfg.shape), const(bgate.shape),
            const(cos_t.shape), const(sina_t.shape), const(sinb_t.shape),
            const(selq.shape), const(selk.shape), const(oneq.shape), const(onek.shape),
        ],
        out_specs=[tok(aw)] * 2 + [pl.BlockSpec((1, tm // FOX_K, hw, FOX_K), lambda b, j: (b, j, 0, 0))]
                  + dil_specs + [tok(d)] * 2,
        out_shape=outs,
        scratch_shapes=[pltpu.VMEM((1, LANES), F32), pltpu.VMEM((3, N_PAIRS, tm, LANES), F32)],
        compiler_params=_params(("arbitrary", "arbitrary"), 56),
        name="inproj",
    )(x, mod3, mod3, wfox, wdil, wgate, wfg, bfg, bgate, cos_t, sina_t, sinb_t, selq, selk, oneq, onek)


def _fox_kernel(qa_ref, ka_ref, vt_ref, o_ref, m_ref, l_ref, acc_ref):
    seq = qa_ref.shape[1]
    tq, tk = FOX_Q, FOX_K
    k_pos = lax.broadcasted_iota(jnp.int32, (tk, tq), 0)
    q_pos = lax.broadcasted_iota(jnp.int32, (tk, tq), 1)

    def q_body(qi, _):
        q0 = pl.multiple_of(qi * tq, tq)
        m_ref[...] = jnp.full(m_ref.shape, NEG_INF, F32)
        l_ref[...] = jnp.zeros(l_ref.shape, F32)
        acc_ref[...] = jnp.zeros(acc_ref.shape, F32)

        def kv_step(kb, k0, visible):
            scores = []
            for h in range(N_HEADS):
                ls = slice(h * LANES, (h + 1) * LANES)
                scores.append(_dot_nt(ka_ref[0, pl.ds(k0, tk), ls], qa_ref[0, pl.ds(q0, tq), ls]))
            m_all = m_ref[...]
            l_all = l_ref[...]
            probs, decay, m_rows, l_rows = [], [], [], []
            for h in range(N_HEADS):
                s = scores[h] if visible is None else jnp.where(visible, scores[h], NEG_INF)
                m_old = m_all[h:h + 1, :]
                m_new = jnp.maximum(m_old, jnp.max(s, axis=0, keepdims=True))
                a = jnp.exp2(m_old - m_new)
                e = jnp.exp2(s - m_new)
                l_rows.append(a * l_all[h:h + 1, :] + jnp.sum(e, axis=0, keepdims=True))
                m_rows.append(m_new)
                probs.append(e.astype(BF16))
                decay.append(a)
            m_ref[...] = jnp.concatenate(m_rows, axis=0)
            l_ref[...] = jnp.concatenate(l_rows, axis=0)
            for h in range(N_HEADS):
                vh = vt_ref[0, kb, h * HEAD_DIM:(h + 1) * HEAD_DIM, :]
                acc_ref[h] = decay[h] * acc_ref[h] + _dot(vh, probs[h])

        def full_block(jb, carry):
            kv_step(jb, pl.multiple_of(jb * tk, tk), None)
            return carry

        lax.fori_loop(0, qi * (tq // tk), full_block, 0)
        for dblk in range(tq // tk):
            kb = qi * (tq // tk) + dblk
            kv_step(kb, pl.multiple_of(kb * tk, tk), k_pos + dblk * tk <= q_pos)

        for p in range(N_PAIRS):
            pair = jnp.concatenate([acc_ref[2 * p + hh] / l_ref[2 * p + hh:2 * p + hh + 1, :] for hh in range(2)],
                                   axis=0)
            o_ref[0, pl.ds(q0, tq), p * PAIR:(p + 1) * PAIR] = pair.T.astype(BF16)
        return 0

    lax.fori_loop(0, seq // tq, q_body, 0)


def _fox(qa, ka, vt):
    bsz, seq, aw = qa.shape
    hw = HEADS_WIDTH
    full = pl.BlockSpec((1, seq, aw), lambda b: (b, 0, 0))
    return pl.pallas_call(
        _fox_kernel,
        grid=(bsz,),
        in_specs=[full, full, pl.BlockSpec((1, seq // FOX_K, hw, FOX_K), lambda b: (b, 0, 0, 0))],
        out_specs=pl.BlockSpec((1, seq, hw), lambda b: (b, 0, 0)),
        out_shape=jax.ShapeDtypeStruct((bsz, seq, hw), BF16),
        scratch_shapes=[pltpu.VMEM((N_HEADS, FOX_Q), F32), pltpu.VMEM((N_HEADS, FOX_Q), F32),
                        pltpu.VMEM((N_HEADS, HEAD_DIM, FOX_Q), F32)],
        compiler_params=_params(("arbitrary",), 48),
        name="fox",
    )(qa, ka, vt)


def _dil_kernel(q_ref, kc_ref, kp_ref, vc_ref, vp_ref, o_ref, lse_ref):
    n = pl.program_id(2)
    blk = ATTN_BLOCK
    n_res = q_ref.shape[0]
    n_sub = q_ref.shape[1] // blk
    lane = lax.broadcasted_iota(jnp.int32, (1, PAIR), 1)
    first = lane < HEAD_DIM
    i = lax.broadcasted_iota(jnp.int32, (blk, 2 * blk), 0)
    c = lax.broadcasted_iota(jnp.int32, (blk, 2 * blk), 1)
    in_window = (c >= i) & (c <= i + blk)
    has_prev = (c >= blk) | (n > 0)

    for res in range(n_res):
        for sub in range(n_sub):
            rows = slice(sub * blk, (sub + 1) * blk)
            before = slice((sub - 1) * blk, sub * blk)
            valid = in_window & has_prev if sub == 0 else in_window
            scores, values = [], []
            for p in range(N_PAIRS):
                ls = slice(p * PAIR, (p + 1) * PAIR)
                qb = q_ref[res, rows, ls]
                zero = jnp.zeros_like(qb)
                k_prev = kp_ref[res, :, ls] if sub == 0 else kc_ref[res, before, ls]
                v_prev = vp_ref[res, :, ls] if sub == 0 else vc_ref[res, before, ls]
                kb = jnp.concatenate([k_prev, kc_ref[res, rows, ls]], axis=0)
                values.append(jnp.concatenate([v_prev, vc_ref[res, rows, ls]], axis=0))
                scores.append(_dot_nt(jnp.where(first, qb, zero), kb))
                scores.append(_dot_nt(jnp.where(first, zero, qb), kb))
            probs, denom, lses = [], [], []
            for s in scores:
                s = jnp.where(valid, s, NEG_INF)
                m = jnp.max(s, axis=-1, keepdims=True)
                e = jnp.exp(s - m)
                l = jnp.sum(e, axis=-1, keepdims=True)
                probs.append(e.astype(BF16))
                denom.append(l)
                lses.append(m + jnp.log(l))
            for p in range(N_PAIRS):
                ls = slice(p * PAIR, (p + 1) * PAIR)
                o0 = _dot(probs[2 * p], values[p]) / denom[2 * p]
                o1 = _dot(probs[2 * p + 1], values[p]) / denom[2 * p + 1]
                o_ref[res, rows, ls] = jnp.where(first, o0, o1).astype(BF16)
                lse_ref[res, rows, ls] = jnp.where(first, lses[2 * p], lses[2 * p + 1])


def _dil(qd, kd, vd):
    bsz, dilation, length, hw = qd.shape
    nb = length // ATTN_BLOCK
    n_sub = min(nb, DIL_STEP_BLOCKS)
    n_res = min(dilation, DIL_STEP_BLOCKS // n_sub)
    cur = pl.BlockSpec((None, n_res, n_sub * ATTN_BLOCK, hw), lambda b, r, n: (b, r, n, 0))
    prev = pl.BlockSpec((None, n_res, ATTN_BLOCK, hw), lambda b, r, n: (b, r, jnp.maximum(n * n_sub - 1, 0), 0))
    return pl.pallas_call(
        _dil_kernel,
        grid=(bsz, dilation // n_res, nb // n_sub),
        in_specs=[cur, cur, prev, cur, prev],
        out_specs=[cur, cur],
        out_shape=[jax.ShapeDtypeStruct(qd.shape, BF16), jax.ShapeDtypeStruct(qd.shape, F32)],
        compiler_params=_params(("arbitrary", "arbitrary", "arbitrary"), 32),
        name=f"dil{dilation}",
    )(qd, kd, kd, vd, vd)


def _mix_kernel(x_ref, yf_ref, o1_ref, o2_ref, o3_ref, l1_ref, l2_ref, l3_ref, gf_ref, gd_ref, g1_ref,
                wbf_ref, wbd_ref, wo_ref, lng_ref, lnb_ref, out_ref, nat_ref):
    tm = x_ref.shape[1]

    def natural(i, src_ref):
        dilation = src_ref.shape[1]
        for res in range(dilation):
            for g in range(N_PAIRS):
                nat_ref[i, g, pl.ds(res, tm // dilation, stride=dilation), :] = src_ref[
                    0, res, :, g * LANES:(g + 1) * LANES].astype(F32)
        return jnp.concatenate([nat_ref[i, g] for g in range(N_PAIRS)], axis=1)

    os_ = [natural(i, ref) for i, ref in enumerate((o1_ref, o2_ref, o3_ref))]
    l1, l2, l3 = [natural(3 + i, ref) for i, ref in enumerate((l1_ref, l2_ref, l3_ref))]
    mx = jnp.maximum(jnp.maximum(l1, l2), l3)
    e1, e2, e3 = jnp.exp(l1 - mx), jnp.exp(l2 - mx), jnp.exp(l3 - mx)
    yd = (e1 * os_[0] + e2 * os_[1] + e3 * os_[2]) / (e1 + e2 + e3)
    merged = (gf_ref[0].astype(F32) * _dot(yf_ref[0], wbf_ref[...])
              + gd_ref[0].astype(F32) * _dot(yd.astype(BF16), wbd_ref[...]))
    y = _dot(merged.astype(BF16), wo_ref[...])
    z = DN_ALPHA * x_ref[0] + (1.0 + g1_ref[0]) * y
    out_ref[0] = _layer_norm(z, lng_ref[...], lnb_ref[...])


def _mix(x, yf, os_, lses, gf, gd, mod3, wbf, wbd, wo, lng, lnb):
    bsz, seq, d = x.shape
    tm = ROW_TILE
    const = lambda shape: pl.BlockSpec(shape, lambda b, j: (0,) * len(shape))
    tok = lambda w: pl.BlockSpec((1, tm, w), lambda b, j: (b, j, 0))
    hw = HEADS_WIDTH
    res_major = [pl.BlockSpec((1, dil, tm // dil, hw), lambda b, j: (b, 0, j, 0)) for _, dil in DIL_PATTERNS]
    return pl.pallas_call(
        _mix_kernel,
        grid=(bsz, seq // tm),
        in_specs=[tok(d), tok(hw)] + res_major * 2 + [tok(d), tok(d),
                  pl.BlockSpec((1, 1, d), lambda b, j: (b, 0, 2)),
                  const(wbf.shape), const(wbd.shape), const(wo.shape), const(lng.shape), const(lnb.shape)],
        out_specs=tok(d),
        out_shape=jax.ShapeDtypeStruct((bsz, seq, d), F32),
        scratch_shapes=[pltpu.VMEM((6, N_PAIRS, tm, LANES), F32)],
        compiler_params=_params(("arbitrary", "arbitrary"), 48),
        name="mix",
    )(x, yf, *os_, *lses, gf, gd, mod3, wbf, wbd, wo, lng, lnb)


def _router_kernel(x_ref, sc_ref, sh_ref, wrh_ref, wrl_ref, bias_ref,
                   u_ref, up_ref, eidx_ref, rank_ref, gw_ref, cnt_ref):
    seq = x_ref.shape[1]
    tc = ROUTE_CHUNK
    e_iota = lax.broadcasted_iota(jnp.int32, (N_EXPERTS, tc), 0)
    g_iota = lax.broadcasted_iota(jnp.int32, (N_GROUPS, tc), 0)
    s_iota = lax.broadcasted_iota(jnp.int32, (GROUP_SIZE, tc), 0)
    r = lax.broadcasted_iota(jnp.int32, (tc, tc), 0)
    c = lax.broadcasted_iota(jnp.int32, (tc, tc), 1)
    upper = jnp.where(r <= c, 1.0, 0.0).astype(BF16)
    ones = jnp.ones((16, tc), BF16)
    carry = jnp.zeros((N_EXPERTS, 1), F32)
    cnt = jnp.zeros((16, N_EXPERTS), F32)

    for ch in range(seq // tc):
        rows = slice(ch * tc, (ch + 1) * tc)
        u = x_ref[0, rows, :] * (1.0 + sc_ref[0]) + sh_ref[0]
        u_ref[0, rows, :] = u.astype(BF16)
        up_ref[0, rows, :] = _pack_words(u)
        u_hi, u_lo = _split2(u)
        logits = _dot_nt(wrh_ref[...], u_hi) + _dot_nt(wrh_ref[...], u_lo) + _dot_nt(wrl_ref[...], u_hi)
        scores = _sigmoid(logits)
        sel = scores + bias_ref[...]

        gs_rows = []
        for g in range(N_GROUPS):
            blk = sel[g * GROUP_SIZE:(g + 1) * GROUP_SIZE]
            m1 = jnp.max(blk, axis=0, keepdims=True)
            i1 = jnp.min(jnp.where(blk == m1, s_iota, GROUP_SIZE), axis=0, keepdims=True)
            m2 = jnp.max(jnp.where(s_iota == i1, -jnp.inf, blk), axis=0, keepdims=True)
            gs_rows.append(m1 + m2)
        gs = jnp.concatenate(gs_rows, axis=0)
        beaten = jnp.zeros((N_GROUPS, tc), F32)
        for g in range(N_GROUPS):
            other = gs[g:g + 1]
            wins = (other > gs) | ((other == gs) & (g_iota > g))
            beaten = beaten + jnp.where(wins, 1.0, 0.0)
        keep = beaten < TOPK_GROUPS
        cand = jnp.concatenate(
            [jnp.where(keep[g:g + 1], sel[g * GROUP_SIZE:(g + 1) * GROUP_SIZE], NEG_INF) for g in range(N_GROUPS)],
            axis=0)
        chosen = jnp.zeros((N_EXPERTS, tc), F32)
        picks = []
        for _ in range(TOP_K):
            m = jnp.max(cand, axis=0, keepdims=True)
            idx = jnp.min(jnp.where(cand == m, e_iota, N_EXPERTS), axis=0, keepdims=True)
            hit = e_iota == idx
            chosen = jnp.where(hit, 1.0, chosen)
            cand = jnp.where(hit, -jnp.inf, cand)
            picks.append(idx)
        g_raw = jnp.where(chosen > 0.0, scores, 0.0)
        gate = g_raw / jnp.sum(g_raw, axis=0, keepdims=True) * ROUTED_SCALE

        chosen_b = chosen.astype(BF16)
        incl = _dot(chosen_b, upper) + carry
        carry = incl[:, tc - 1:tc]
        cnt = cnt + _dot_nt(ones, chosen_b)

        rank_rows, gate_rows = [], []
        for idx in picks:
            hit = e_iota == idx
            rank_rows.append(jnp.sum(jnp.where(hit, incl - 1.0, 0.0), axis=0, keepdims=True))
            gate_rows.append(jnp.sum(jnp.where(hit, gate, 0.0), axis=0, keepdims=True))
        eidx_ref[0, :, rows] = jnp.concatenate(picks, axis=0)
        rank_ref[0, :, rows] = jnp.concatenate(rank_rows, axis=0).astype(jnp.int32)
        gate_t = jnp.concatenate(gate_rows + [jnp.zeros((LANES - TOP_K, tc), F32)], axis=0).T
        gw_ref[0, rows, :] = gate_t[:, :TOP_K]

    cnt_ref[0] = cnt[0:1].astype(jnp.int32)


def _router(x1, mod3, wr_hi, wr_lo, bias):
    bsz, seq, d = x1.shape
    full = pl.BlockSpec((1, seq, d), lambda b: (b, 0, 0))
    ks = pl.BlockSpec((1, TOP_K, seq), lambda b: (b, 0, 0))
    return pl.pallas_call(
        _router_kernel,
        grid=(bsz,),
        in_specs=[full,
                  pl.BlockSpec((1, 1, d), lambda b: (b, 0, 4)),
                  pl.BlockSpec((1, 1, d), lambda b: (b, 0, 3)),
                  pl.BlockSpec(wr_hi.shape, lambda b: (0, 0)),
                  pl.BlockSpec(wr_lo.shape, lambda b: (0, 0)),
                  pl.BlockSpec(bias.shape, lambda b: (0, 0))],
        out_specs=[full, pl.BlockSpec((1, seq, PACKED), lambda b: (b, 0, 0)), ks, ks,
                   pl.BlockSpec((1, seq, TOP_K), lambda b: (b, 0, 0)),
                   pl.BlockSpec((1, 1, N_EXPERTS), lambda b: (b, 0, 0))],
        out_shape=[jax.ShapeDtypeStruct((bsz, seq, d), BF16),
                   jax.ShapeDtypeStruct((bsz, seq, PACKED), F32),
                   jax.ShapeDtypeStruct((bsz, TOP_K, seq), jnp.int32),
                   jax.ShapeDtypeStruct((bsz, TOP_K, seq), jnp.int32),
                   jax.ShapeDtypeStruct((bsz, seq, TOP_K), F32),
                   jax.ShapeDtypeStruct((bsz, 1, N_EXPERTS), jnp.int32)],
        compiler_params=_params(("arbitrary",), 48),
        name="router",
    )(x1, mod3, mod3, wr_hi, wr_lo, bias)


def _slot_kernel(base_ref, eidx_ref, rank_ref, slot_ref):
    b = pl.program_id(0)
    eidx = eidx_ref[0]
    start = jnp.zeros(eidx.shape, jnp.int32)
    for e in range(N_EXPERTS):
        start = jnp.where(eidx == e, base_ref[b * N_EXPERTS + e], start)
    slot_ref[0] = start + rank_ref[0]


def _slots(base, eidx, rank):
    bsz, _, seq = eidx.shape
    ks = pl.BlockSpec((1, TOP_K, seq), lambda b, c: (b, 0, 0))
    return pl.pallas_call(
        _slot_kernel,
        grid_spec=pltpu.PrefetchScalarGridSpec(num_scalar_prefetch=1, grid=(bsz,), in_specs=[ks, ks], out_specs=ks),
        out_shape=jax.ShapeDtypeStruct((bsz, TOP_K, seq), jnp.int32),
        compiler_params=_params(("arbitrary",), 16),
        name="slots",
    )(base.reshape(bsz * N_EXPERTS), eidx, rank)


def _sc_mesh():
    return plsc.VectorSubcoreMesh(core_axis_name="core", subcore_axis_name="subcore")


def _sc_worker():
    return lax.axis_index("subcore") * SC_CORES + lax.axis_index("core")


def _sc_dispatch(rows, slot, n_slots):
    n_tok, width = rows.shape
    seq = slot.shape[1]
    chunk = DISPATCH_CHUNK
    per_worker = n_tok // SC_WORKERS
    assert per_worker % chunk == 0 and seq % chunk == 0

    @functools.partial(
        pl.kernel, mesh=_sc_mesh(),
        out_type=jax.ShapeDtypeStruct((n_slots, width), rows.dtype),
        scratch_types=[pltpu.VMEM((TOP_K, chunk), jnp.int32), pltpu.VMEM((chunk, width), rows.dtype),
                       pltpu.SemaphoreType.DMA],
    )
    def dispatch(rows_hbm, slot_hbm, out_hbm, idx_v, rows_v, sem):
        base = _sc_worker() * per_worker

        @pl.loop(0, per_worker // chunk)
        def _(i):
            off = base + i * chunk
            b = off // seq
            n0 = off - b * seq
            pltpu.sync_copy(slot_hbm.at[pl.ds(b * TOP_K, TOP_K), pl.ds(n0, chunk)], idx_v)
            pltpu.sync_copy(rows_hbm.at[pl.ds(off, chunk)], rows_v)
            copies = [pltpu.async_copy(rows_v, out_hbm.at[idx_v.at[k]], sem) for k in range(TOP_K)]
            for cp in copies:
                cp.wait()

    return dispatch(rows, slot)


def _sc_gather(table, idx):
    n_out = idx.shape[0]
    width = table.shape[1]
    chunk = GATHER_CHUNK
    per_worker = n_out // SC_WORKERS
    steps = per_worker // chunk
    assert per_worker % chunk == 0 and steps % 2 == 0

    @functools.partial(
        pl.kernel, mesh=_sc_mesh(),
        out_type=jax.ShapeDtypeStruct((n_out, width), table.dtype),
        scratch_types=[pltpu.VMEM((2, chunk), jnp.int32), pltpu.VMEM((2, chunk, width), table.dtype),
                       pltpu.SemaphoreType.DMA, pltpu.SemaphoreType.DMA((2,))],
    )
    def gather(table_hbm, idx_hbm, out_hbm, idx_v, rows_v, gather_sem, out_sems):
        base = _sc_worker() * per_worker

        def write_out(buf, off):
            return pltpu.make_async_copy(rows_v.at[buf], out_hbm.at[pl.ds(off, chunk)], out_sems.at[buf])

        @pl.loop(0, steps, step=2)
        def _(i):
            for buf in range(2):
                off = base + (i + buf) * chunk

                @pl.when(i >= 2)
                def _():
                    write_out(buf, off - 2 * chunk).wait()

                pltpu.sync_copy(idx_hbm.at[pl.ds(off, chunk)], idx_v.at[buf])
                pltpu.async_copy(table_hbm.at[idx_v.at[buf]], rows_v.at[buf], gather_sem).wait()
                write_out(buf, off).start()

        for buf in range(2):
            write_out(buf, base + (steps - 2 + buf) * chunk).wait()

    return gather(table, idx)


def _gmm_kernel(blk_e_ref, used_ref, xs_ref, wgu_ref, wd_ref, out_ref):
    half = PACKED

    @pl.when(pl.program_id(0) < used_ref[0])
    def _():
        lo, hi = _unpack_words(xs_ref[...])
        hgu = _dot(lo.astype(BF16), wgu_ref[0, :half, :]) + _dot(hi.astype(BF16), wgu_ref[0, half:, :])
        hg = hgu[:, :EXPERT_FF]
        h = (hg * _sigmoid(hg) * hgu[:, EXPERT_FF:]).astype(BF16)
        out_ref[...] = _pack_words(_dot(h, wd_ref[0]))


def _gmm(xs, blk_e, used, wgu, wd):
    n_slots, width = xs.shape
    bm = EXPERT_BLOCK
    d = wgu.shape[1]
    live = lambda i, blk_e, used: (jnp.minimum(i, used[0] - 1), 0)
    return pl.pallas_call(
        _gmm_kernel,
        grid_spec=pltpu.PrefetchScalarGridSpec(
            num_scalar_prefetch=2,
            grid=(n_slots // bm,),
            in_specs=[pl.BlockSpec((bm, width), live),
                      pl.BlockSpec((1, d, 2 * EXPERT_FF), lambda i, blk_e, used: (blk_e[i], 0, 0)),
                      pl.BlockSpec((1, EXPERT_FF, d), lambda i, blk_e, used: (blk_e[i], 0, 0))],
            out_specs=pl.BlockSpec((bm, width), live)),
        out_shape=jax.ShapeDtypeStruct((n_slots, width), F32),
        compiler_params=_params(("arbitrary",), 32),
        name="gmm",
    )(blk_e, used, xs, wgu, wd)


def _ffn_out_kernel(x_ref, u_ref, yg_ref, gw_ref, g2_ref, wsgu_ref, wsd_ref, lng_ref, lnb_ref, out_ref):
    hgu = _dot(u_ref[0], wsgu_ref[...])
    hg = hgu[:, :SHARED_FF]
    h = (hg * _sigmoid(hg) * hgu[:, SHARED_FF:]).astype(BF16)
    shared = _dot(h, wsd_ref[...])
    gw = gw_ref[0]
    acc_lo = shared[:, :PACKED]
    acc_hi = shared[:, PACKED:]
    for k in range(TOP_K):
        lo, hi = _unpack_words(yg_ref[0, k])
        w = gw[:, k:k + 1]
        acc_lo = acc_lo + w * lo
        acc_hi = acc_hi + w * hi
    y = jnp.concatenate([acc_lo, acc_hi], axis=1)
    z = DN_ALPHA * x_ref[0] + (1.0 + g2_ref[0]) * y
    out_ref[0] = _layer_norm(z, lng_ref[...], lnb_ref[...])


def _ffn_out(x1, u2, yg, gw, mod3, wsgu, wsd, lng, lnb):
    bsz, seq, d = x1.shape
    tm = ROW_TILE
    const = lambda shape: pl.BlockSpec(shape, lambda b, j: (0,) * len(shape))
    tok = pl.BlockSpec((1, tm, d), lambda b, j: (b, j, 0))
    return pl.pallas_call(
        _ffn_out_kernel,
        grid=(bsz, seq // tm),
        in_specs=[tok, tok,
                  pl.BlockSpec((1, TOP_K, tm, PACKED), lambda b, j: (b, 0, j, 0)),
                  pl.BlockSpec((1, tm, TOP_K), lambda b, j: (b, j, 0)),
                  pl.BlockSpec((1, 1, d), lambda b, j: (b, 0, 5)),
                  const(wsgu.shape), const(wsd.shape), const(lng.shape), const(lnb.shape)],
        out_specs=tok,
        out_shape=jax.ShapeDtypeStruct((bsz, seq, d), F32),
        compiler_params=_params(("arbitrary", "arbitrary"), 48),
        name="ffn_out",
    )(x1, u2, yg, gw, mod3, wsgu, wsd, lng, lnb)


def _moe_layout(cnt, n_blocks):
    bm = EXPERT_BLOCK
    total = jnp.sum(cnt, axis=0)
    padded = (total + bm - 1) // bm * bm
    ends = jnp.cumsum(padded)
    base = (ends - padded)[None, :] + jnp.cumsum(cnt, axis=0) - cnt
    first_row = jnp.arange(n_blocks, dtype=ends.dtype) * bm
    blk_e = jnp.minimum(jnp.sum(ends[None, :] <= first_row[:, None], axis=1), N_EXPERTS - 1)
    used = (ends[-1:] // bm)
    return base.astype(jnp.int32), blk_e.astype(jnp.int32), used.astype(jnp.int32)


def _rope_tables(seq):
    half = HEAD_DIM // 2
    inv_freq = ROPE_THETA ** (-jnp.arange(half, dtype=F32) * 2.0 / HEAD_DIM)
    ang = jnp.arange(seq, dtype=F32)[:, None] * inv_freq[None, :]
    cos, sin = jnp.cos(ang), jnp.sin(ang)
    zero = jnp.zeros_like(sin)
    reps = LANES // HEAD_DIM
    cos_t = jnp.tile(jnp.concatenate([cos, cos], axis=1), (1, reps))
    sina_t = jnp.tile(jnp.concatenate([-sin, zero], axis=1), (1, reps))
    sinb_t = jnp.tile(jnp.concatenate([zero, sin], axis=1), (1, reps))
    return cos_t, sina_t, sinb_t


def _fox_bias_tables():
    n_split = 3
    aw = N_HEADS * LANES
    heads = jnp.arange(N_HEADS)
    selq = jnp.zeros((n_split, LANES, aw), F32)
    selk = jnp.zeros((n_split, LANES, aw), F32)
    oneq = jnp.zeros((1, aw), F32)
    onek = jnp.zeros((1, aw), F32)
    for j in range(n_split):
        selk = selk.at[j, heads, heads * LANES + FOX_BIAS_LANE + j].set(-1.0)
        selq = selq.at[j, heads, heads * LANES + FOX_BIAS_LANE + n_split + j].set(1.0)
        oneq = oneq.at[0, heads * LANES + FOX_BIAS_LANE + j].set(1.0)
        onek = onek.at[0, heads * LANES + FOX_BIAS_LANE + n_split + j].set(1.0)
    return selq.astype(BF16), selk.astype(BF16), oneq, onek


def kernel(x, c, w_ada, b_ada, w_in, b_forget, b_gate, w_br_fox, w_br_dil, w_o, ln_g, ln_b, w_router, router_bias,
           w_exp_gate, w_exp_up, w_exp_down, w_sh_gate, w_sh_up, w_sh_down):
    bsz, seq, d = x.shape
    depth = w_ada.shape[0]
    hw = HEADS_WIDTH
    n_tok = bsz * seq

    mod = _ada(c, w_ada, b_ada)

    o_f, o_d, o_g = 3 * hw, 3 * hw + N_HEADS, 6 * hw + N_HEADS
    wfox = w_in[:, :, :o_f].astype(BF16)
    wfg = jnp.pad(w_in[:, :, o_f:o_d], ((0, 0), (0, 0), (0, LANES - N_HEADS))).astype(BF16)
    wdil = w_in[:, :, o_d:o_g].astype(BF16)
    wgate = w_in[:, :, o_g:].astype(BF16)
    bfg = jnp.pad(b_forget, ((0, 0), (0, LANES - N_HEADS))).reshape(depth, 1, LANES)
    bgate = b_gate.reshape(depth, 1, 2 * d)
    wr_t = jnp.swapaxes(w_router, 1, 2)
    wr_hi = wr_t.astype(BF16)
    wr_lo = (wr_t - wr_hi.astype(F32)).astype(BF16)
    rbias = router_bias.reshape(depth, N_EXPERTS, 1)
    wgu = jnp.concatenate([w_exp_gate, w_exp_up], axis=-1).astype(BF16)
    wd = w_exp_down.astype(BF16)
    wsgu = jnp.concatenate([w_sh_gate, w_sh_up], axis=-1).astype(BF16)
    wsd = w_sh_down.astype(BF16)
    cos_t, sina_t, sinb_t = _rope_tables(seq)
    selq, selk, oneq, onek = _fox_bias_tables()

    layer_params = (mod, wfox, wdil, wgate, wfg, bfg, bgate, w_br_fox.astype(BF16), w_br_dil.astype(BF16),
                    w_o.astype(BF16), ln_g, ln_b, wr_hi, wr_lo, rbias, wgu, wd, wsgu, wsd)

    n_streams = 2 if n_tok % (2 * SC_WORKERS * DISPATCH_CHUNK) == 0 and bsz % 2 == 0 else 1
    sb = bsz // n_streams
    s_tok = sb * seq
    n_blocks = s_tok * TOP_K // EXPERT_BLOCK + N_EXPERTS

    def stream_layer(xc, mod3, prm):
        (_, wfox_l, wdil_l, wgate_l, wfg_l, bfg_l, bgate_l, wbf_l, wbd_l, wo_l, lng_l, lnb_l,
         wrh_l, wrl_l, rb_l, wgu_l, wd_l, wsgu_l, wsd_l) = prm
        qa, ka, vt, *rest = _inproj(
            xc, mod3, wfox_l, wdil_l, wgate_l, wfg_l, bfg_l, bgate_l, cos_t, sina_t, sinb_t, selq, selk, oneq, onek)
        gf, gd = rest[9:]
        yf = _fox(qa, ka, vt)
        dil = [_dil(*rest[3 * i:3 * i + 3]) for i in range(len(DIL_PATTERNS))]
        x1 = _mix(xc, yf, [o for o, _ in dil], [l for _, l in dil], gf, gd, mod3, wbf_l, wbd_l, wo_l,
                  lng_l[0:1], lnb_l[0:1])
        u2, u2p, eidx, rank, gw, cnt = _router(x1, mod3, wrh_l, wrl_l, rb_l)
        base, blk_e, used = _moe_layout(cnt.reshape(sb, N_EXPERTS), n_blocks)
        slot = _slots(base, eidx, rank)
        xs = _sc_dispatch(u2p.reshape(s_tok, PACKED), slot.reshape(sb * TOP_K, seq), n_blocks * EXPERT_BLOCK)
        ys = _gmm(xs, blk_e, used, wgu_l, wd_l)
        yg = _sc_gather(ys, slot.reshape(s_tok * TOP_K)).reshape(sb, TOP_K, seq, PACKED)
        return _ffn_out(x1, u2, yg, gw, mod3, wsgu_l, wsd_l, lng_l[1:2], lnb_l[1:2])

    def layer(xcs, prm):
        mod3 = prm[0].reshape(bsz, 1, 6 * d)
        return tuple(stream_layer(xc, mod3[i * sb:(i + 1) * sb], prm) for i, xc in enumerate(xcs)), None

    outs, _ = lax.scan(layer, tuple(x[i * sb:(i + 1) * sb] for i in range(n_streams)), layer_params)
    return outs[0] if n_streams == 1 else jnp.concatenate(outs, axis=0)
```

```python
import functools

import jax
import jax.numpy as jnp
from jax import lax
from jax.experimental import pallas as pl
from jax.experimental.pallas import tpu as pltpu
from jax.experimental.pallas import tpu_sc as plsc

D_MODEL = 1024
DEPTH = 4
HEAD_DIM = 64
N_HEADS = 8
HEADS_WIDTH = N_HEADS * HEAD_DIM
DIL_PATTERNS = ((128, 1), (512, 4), (2048, 16))
ATTN_BLOCK = 128
ROPE_THETA = 10000.0
N_EXPERTS = 64
TOP_K = 8
N_GROUPS = 8
GROUP_SIZE = N_EXPERTS // N_GROUPS
TOPK_GROUPS = 4
EXPERT_FF = 256
SHARED_FF = 256
ROUTED_SCALE = 2.5
DN_ALPHA = (2 * DEPTH) ** 0.25
LN_EPS = 1e-5
NEG_INF = -1e30
SCORE_SCALE = HEAD_DIM ** -0.5
LOG2_E = 1.4426950408889634

LANES = 128
PAIR = 2 * HEAD_DIM
N_PAIRS = HEADS_WIDTH // PAIR

ROW_TILE = 512
FOX_Q = 256
FOX_K = 256
FOX_BIAS_LANE = HEAD_DIM
DIL_STEP_BLOCKS = 4
ROUTE_CHUNK = 512
EXPERT_BLOCK = 512
PACKED = D_MODEL // 2

SC_CORES = 2
SC_SUBCORES = 16
SC_WORKERS = SC_CORES * SC_SUBCORES
DISPATCH_CHUNK = 128
GATHER_CHUNK = 64

BF16 = jnp.bfloat16
F32 = jnp.float32


def _dot(a, b):
    return jnp.dot(a, b, preferred_element_type=F32)


def _dot_nt(a, b):
    return lax.dot_general(a, b, (((1,), (1,)), ((), ())), preferred_element_type=F32)


def _split2(x):
    hi = x.astype(BF16)
    lo = (x - hi.astype(F32)).astype(BF16)
    return hi, lo


def _split3(x):
    hi = x.astype(BF16)
    r = x - hi.astype(F32)
    mid = r.astype(BF16)
    lo = (r - mid.astype(F32)).astype(BF16)
    return hi, mid, lo


def _sigmoid(x):
    return 1.0 / (1.0 + jnp.exp(-x))


def _layer_norm(z, g, b):
    mu = jnp.mean(z, axis=-1, keepdims=True)
    zc = z - mu
    var = jnp.mean(zc * zc, axis=-1, keepdims=True)
    return zc * lax.rsqrt(var + LN_EPS) * g + b


def _pack_words(v):
    half = v.shape[1] // 2
    bits = pltpu.bitcast(v.astype(BF16).astype(F32), jnp.uint32)
    word = (bits[:, half:] & jnp.uint32(0xFFFF0000)) | (bits[:, :half] >> 16)
    return pltpu.bitcast(word, F32)


def _unpack_words(w):
    bits = pltpu.bitcast(w, jnp.uint32)
    return pltpu.bitcast(bits << 16, F32), pltpu.bitcast(bits & jnp.uint32(0xFFFF0000), F32)


def _params(sem, vmem_mb):
    return pltpu.CompilerParams(dimension_semantics=sem, vmem_limit_bytes=vmem_mb * 1024 * 1024)


def _ada_kernel(c_ref, w_ref, b_ref, o_ref):
    c = c_ref[...]
    cs = c * _sigmoid(c)
    c_hi, c_lo = _split2(cs)
    w_hi, w_lo = _split2(w_ref[0])
    o_ref[0] = _dot(c_hi, w_hi) + _dot(c_hi, w_lo) + _dot(c_lo, w_hi) + b_ref[0]


def _ada(c, w_ada, b_ada):
    depth, d, n = w_ada.shape
    bsz = c.shape[0]
    tn = 1536
    return pl.pallas_call(
        _ada_kernel,
        grid=(depth, n // tn),
        in_specs=[
            pl.BlockSpec((bsz, d), lambda l, j: (0, 0)),
            pl.BlockSpec((1, d, tn), lambda l, j: (l, 0, j)),
            pl.BlockSpec((1, 1, tn), lambda l, j: (l, 0, j)),
        ],
        out_specs=pl.BlockSpec((1, bsz, tn), lambda l, j: (l, 0, j)),
        out_shape=jax.ShapeDtypeStruct((depth, bsz, n), F32),
        compiler_params=_params(("arbitrary", "arbitrary"), 40),
        name="ada",
    )(c, w_ada, b_ada.reshape(depth, 1, n))


def _inproj_kernel(x_ref, sc_ref, sh_ref, wfox_ref, wdil_ref, wgate_ref, wfg_ref, bfg_ref, bgate_ref,
                   cos_ref, sina_ref, sinb_ref, selq_ref, selk_ref, oneq_ref, onek_ref,
                   qa_ref, ka_ref, vt_ref, *rest):
    dil_refs = rest[:9]
    gf_ref, gd_ref, carry_ref, perm_ref = rest[9:]
    j = pl.program_id(1)
    tm = x_ref.shape[1]
    u = (x_ref[0] * (1.0 + sc_ref[0]) + sh_ref[0]).astype(BF16)

    ff = _dot(u, wfg_ref[...]) + bfg_ref[...]
    lf = jnp.minimum(ff, 0.0) - jnp.log(1.0 + jnp.exp(-jnp.abs(ff)))
    r = lax.broadcasted_iota(jnp.int32, (tm, tm), 0)
    c = lax.broadcasted_iota(jnp.int32, (tm, tm), 1)
    tri = jnp.where(r >= c, 1.0, 0.0).astype(BF16)
    hi, mid, lo = _split3(lf)
    csum = _dot(tri, hi) + _dot(tri, mid) + _dot(tri, lo)

    @pl.when(j == 0)
    def _():
        carry_ref[...] = jnp.zeros_like(carry_ref)

    cum = csum + carry_ref[...]
    carry_ref[...] = cum[tm - 1:tm, :]

    lane = lax.broadcasted_iota(jnp.int32, (1, LANES), 1)
    c_hi, c_mid, c_lo = (t.astype(F32) for t in _split3(cum * LOG2_E))
    parts = jnp.where(lane < N_HEADS, c_hi,
                      jnp.where(lane < 2 * N_HEADS, pltpu.roll(c_mid, N_HEADS, 1),
                                jnp.where(lane < 3 * N_HEADS, pltpu.roll(c_lo, 2 * N_HEADS, 1), 0.0))).astype(BF16)
    bias_q = _dot(parts, selq_ref[...]) + oneq_ref[...]
    bias_k = _dot(parts, selk_ref[...]) + onek_ref[...]
    pf = _dot(u, wfox_ref[...])
    data = lane < HEAD_DIM
    for dst, off, bias, scale in ((qa_ref, 0, bias_q, SCORE_SCALE * LOG2_E), (ka_ref, HEADS_WIDTH, bias_k, 1.0)):
        for p in range(N_PAIRS):
            pair = pf[:, off + p * PAIR: off + (p + 1) * PAIR] * scale
            for hh, src in ((0, pair), (1, pltpu.roll(pair, HEAD_DIM, 1))):
                t = (2 * p + hh) * LANES
                dst[0, :, t:t + LANES] = jnp.where(data, src, bias[:, t:t + LANES]).astype(BF16)
    vt = pf[:, 2 * HEADS_WIDTH:].T
    for cb in range(tm // FOX_K):
        vt_ref[0, cb] = vt[:, cb * FOX_K:(cb + 1) * FOX_K].astype(BF16)

    pd = _dot(u, wdil_ref[...])
    rows = pl.ds(pl.multiple_of(j * tm, tm), tm)
    cos = cos_ref[rows, :]
    sina = sina_ref[rows, :]
    sinb = sinb_ref[rows, :]
    for i, off in ((0, 0), (1, HEADS_WIDTH)):
        for g in range(N_PAIRS):
            xg = pd[:, off + g * LANES: off + (g + 1) * LANES]
            ahead = pltpu.roll(xg, LANES - HEAD_DIM // 2, 1)
            behind = pltpu.roll(xg, HEAD_DIM // 2, 1)
            rot = xg * cos + ahead * sina + behind * sinb
            perm_ref[i, g] = rot * SCORE_SCALE if i == 0 else rot
    for g in range(N_PAIRS):
        perm_ref[2, g] = pd[:, 2 * HEADS_WIDTH + g * LANES:2 * HEADS_WIDTH + (g + 1) * LANES]
    for i in range(3):
        for (_, dilation), dst in zip(DIL_PATTERNS, dil_refs[i::3]):
            for res in range(dilation):
                for g in range(N_PAIRS):
                    dst[0, res, :, g * LANES:(g + 1) * LANES] = perm_ref[
                        i, g, pl.ds(res, tm // dilation, stride=dilation), :].astype(BF16)

    pg = _sigmoid(_dot(u, wgate_ref[...]) + bgate_ref[...])
    gf_ref[0] = pg[:, :D_MODEL].astype(BF16)
    gd_ref[0] = pg[:, D_MODEL:].astype(BF16)


def _inproj(x, mod3, wfox, wdil, wgate, wfg, bfg, bgate, cos_t, sina_t, sinb_t, selq, selk, oneq, onek):
    bsz, seq, d = x.shape
    tm = ROW_TILE
    const = lambda shape: pl.BlockSpec(shape, lambda b, j: (0,) * len(shape))
    tok = lambda w: pl.BlockSpec((1, tm, w), lambda b, j: (b, j, 0))
    hw = HEADS_WIDTH
    aw = N_HEADS * LANES
    dil_shapes, dil_specs = [], []
    for _, dil in DIL_PATTERNS:
        dil_shapes += [jax.ShapeDtypeStruct((bsz, dil, seq // dil, hw), BF16)] * 3
        dil_specs += [pl.BlockSpec((1, dil, tm // dil, hw), lambda b, j: (b, 0, j, 0))] * 3
    outs = ([jax.ShapeDtypeStruct((bsz, seq, aw), BF16)] * 2
            + [jax.ShapeDtypeStruct((bsz, seq // FOX_K, hw, FOX_K), BF16)]
            + dil_shapes + [jax.ShapeDtypeStruct((bsz, seq, d), BF16)] * 2)
    return pl.pallas_call(
        _inproj_kernel,
        grid=(bsz, seq // tm),
        in_specs=[
            tok(d),
            pl.BlockSpec((1, 1, d), lambda b, j: (b, 0, 1)),
            pl.BlockSpec((1, 1, d), lambda b, j: (b, 0, 0)),
            const(wfox.shape), const(wdil.shape), const(wgate.shape), const(wfg.shape),
            const(bfg.shape), const(bgate.shape),
            const(cos_t.shape), const(sina_t.shape), const(sinb_t.shape),
            const(selq.shape), const(selk.shape), const(oneq.shape), const(onek.shape),
        ],
        out_specs=[tok(aw)] * 2 + [pl.BlockSpec((1, tm // FOX_K, hw, FOX_K), lambda b, j: (b, j, 0, 0))]
                  + dil_specs + [tok(d)] * 2,
        out_shape=outs,
        scratch_shapes=[pltpu.VMEM((1, LANES), F32), pltpu.VMEM((3, N_PAIRS, tm, LANES), F32)],
        compiler_params=_params(("arbitrary", "arbitrary"), 56),
        name="inproj",
    )(x, mod3, mod3, wfox, wdil, wgate, wfg, bfg, bgate, cos_t, sina_t, sinb_t, selq, selk, oneq, onek)


def _fox_kernel(qa_ref, ka_ref, vt_ref, o_ref, m_ref, l_ref, acc_ref):
    seq = qa_ref.shape[1]
    tq, tk = FOX_Q, FOX_K
    k_pos = lax.broadcasted_iota(jnp.int32, (tk, tq), 0)
    q_pos = lax.broadcasted_iota(jnp.int32, (tk, tq), 1)
    ones = jnp.ones((16, tk), BF16)

    def q_body(qi, _):
        q0 = pl.multiple_of(qi * tq, tq)
        m_ref[...] = jnp.full(m_ref.shape, NEG_INF, F32)
        l_ref[...] = jnp.zeros(l_ref.shape, F32)
        acc_ref[...] = jnp.zeros(acc_ref.shape, F32)

        def kv_step(kb, k0, visible):
            scores = []
            for h in range(N_HEADS):
                ls = slice(h * LANES, (h + 1) * LANES)
                scores.append(_dot_nt(ka_ref[0, pl.ds(k0, tk), ls], qa_ref[0, pl.ds(q0, tq), ls]))
            m_all = m_ref[...]
            l_all = l_ref[...]
            probs, decay, m_rows, l_rows = [], [], [], []
            for h in range(N_HEADS):
                s = scores[h] if visible is None else jnp.where(visible, scores[h], NEG_INF)
                m_old = m_all[h:h + 1, :]
                m_new = jnp.maximum(m_old, jnp.max(s, axis=0, keepdims=True))
                m_rows.append(m_new)
                decay.append(jnp.exp2(m_old - m_new))
                probs.append(jnp.exp2((s - m_new).astype(BF16)))
            m_ref[...] = jnp.concatenate(m_rows, axis=0)
            for h in range(N_HEADS):
                vh = vt_ref[0, kb, h * HEAD_DIM:(h + 1) * HEAD_DIM, :]
                acc_ref[h] = decay[h] * acc_ref[h] + _dot(vh, probs[h])
                l_rows.append(decay[h] * l_all[h:h + 1, :] + _dot(ones, probs[h])[0:1])
            l_ref[...] = jnp.concatenate(l_rows, axis=0)

        def full_block(jb, carry):
            kv_step(jb, pl.multiple_of(jb * tk, tk), None)
            return carry

        lax.fori_loop(0, qi * (tq // tk), full_block, 0)
        for dblk in range(tq // tk):
            kb = qi * (tq // tk) + dblk
            kv_step(kb, pl.multiple_of(kb * tk, tk), k_pos + dblk * tk <= q_pos)

        for p in range(N_PAIRS):
            pair = jnp.concatenate([acc_ref[2 * p + hh] / l_ref[2 * p + hh:2 * p + hh + 1, :] for hh in range(2)],
                                   axis=0)
            o_ref[0, pl.ds(q0, tq), p * PAIR:(p + 1) * PAIR] = pair.T.astype(BF16)
        return 0

    lax.fori_loop(0, seq // tq, q_body, 0)


def _fox(qa, ka, vt):
    bsz, seq, aw = qa.shape
    hw = HEADS_WIDTH
    full = pl.BlockSpec((1, seq, aw), lambda b: (b, 0, 0))
    return pl.pallas_call(
        _fox_kernel,
        grid=(bsz,),
        in_specs=[full, full, pl.BlockSpec((1, seq // FOX_K, hw, FOX_K), lambda b: (b, 0, 0, 0))],
        out_specs=pl.BlockSpec((1, seq, hw), lambda b: (b, 0, 0)),
        out_shape=jax.ShapeDtypeStruct((bsz, seq, hw), BF16),
        scratch_shapes=[pltpu.VMEM((N_HEADS, FOX_Q), F32), pltpu.VMEM((N_HEADS, FOX_Q), F32),
                        pltpu.VMEM((N_HEADS, HEAD_DIM, FOX_Q), F32)],
        compiler_params=_params(("arbitrary",), 48),
        name="fox",
    )(qa, ka, vt)


def _dil_kernel(q_ref, kc_ref, kp_ref, vc_ref, vp_ref, o_ref, lse_ref):
    n = pl.program_id(2)
    blk = ATTN_BLOCK
    n_res = q_ref.shape[0]
    n_sub = q_ref.shape[1] // blk
    lane = lax.broadcasted_iota(jnp.int32, (1, PAIR), 1)
    first = lane < HEAD_DIM
    i = lax.broadcasted_iota(jnp.int32, (blk, 2 * blk), 0)
    c = lax.broadcasted_iota(jnp.int32, (blk, 2 * blk), 1)
    in_window = (c >= i) & (c <= i + blk)
    has_prev = (c >= blk) | (n > 0)

    for res in range(n_res):
        for sub in range(n_sub):
            rows = slice(sub * blk, (sub + 1) * blk)
            before = slice((sub - 1) * blk, sub * blk)
            valid = in_window & has_prev if sub == 0 else in_window
            scores, values = [], []
            for p in range(N_PAIRS):
                ls = slice(p * PAIR, (p + 1) * PAIR)
                qb = q_ref[res, rows, ls]
                zero = jnp.zeros_like(qb)
                k_prev = kp_ref[res, :, ls] if sub == 0 else kc_ref[res, before, ls]
                v_prev = vp_ref[res, :, ls] if sub == 0 else vc_ref[res, before, ls]
                kb = jnp.concatenate([k_prev, kc_ref[res, rows, ls]], axis=0)
                values.append(jnp.concatenate([v_prev, vc_ref[res, rows, ls]], axis=0))
                scores.append(_dot_nt(jnp.where(first, qb, zero), kb))
                scores.append(_dot_nt(jnp.where(first, zero, qb), kb))
            probs, denom, lses = [], [], []
            for s in scores:
                s = jnp.where(valid, s, NEG_INF)
                m = jnp.max(s, axis=-1, keepdims=True)
                e = jnp.exp(s - m)
                l = jnp.sum(e, axis=-1, keepdims=True)
                probs.append(e.astype(BF16))
                denom.append(l)
                lses.append(m + jnp.log(l))
            for p in range(N_PAIRS):
                ls = slice(p * PAIR, (p + 1) * PAIR)
                o0 = _dot(probs[2 * p], values[p]) / denom[2 * p]
                o1 = _dot(probs[2 * p + 1], values[p]) / denom[2 * p + 1]
                o_ref[res, rows, ls] = jnp.where(first, o0, o1).astype(BF16)
                lse_ref[res, rows, ls] = jnp.where(first, lses[2 * p], lses[2 * p + 1])


def _dil(qd, kd, vd):
    bsz, dilation, length, hw = qd.shape
    nb = length // ATTN_BLOCK
    n_sub = min(nb, DIL_STEP_BLOCKS)
    n_res = min(dilation, DIL_STEP_BLOCKS // n_sub)
    cur = pl.BlockSpec((None, n_res, n_sub * ATTN_BLOCK, hw), lambda b, r, n: (b, r, n, 0))
    prev = pl.BlockSpec((None, n_res, ATTN_BLOCK, hw), lambda b, r, n: (b, r, jnp.maximum(n * n_sub - 1, 0), 0))
    return pl.pallas_call(
        _dil_kernel,
        grid=(bsz, dilation // n_res, nb // n_sub),
        in_specs=[cur, cur, prev, cur, prev],
        out_specs=[cur, cur],
        out_shape=[jax.ShapeDtypeStruct(qd.shape, BF16), jax.ShapeDtypeStruct(qd.shape, F32)],
        compiler_params=_params(("arbitrary", "arbitrary", "arbitrary"), 32),
        name=f"dil{dilation}",
    )(qd, kd, kd, vd, vd)


def _mix_kernel(x_ref, yf_ref, o1_ref, o2_ref, o3_ref, l1_ref, l2_ref, l3_ref, gf_ref, gd_ref, g1_ref,
                wbf_ref, wbd_ref, wo_ref, lng_ref, lnb_ref, out_ref, nat_ref):
    tm = x_ref.shape[1]

    def natural(i, src_ref):
        dilation = src_ref.shape[1]
        for res in range(dilation):
            for g in range(N_PAIRS):
                nat_ref[i, g, pl.ds(res, tm // dilation, stride=dilation), :] = src_ref[
                    0, res, :, g * LANES:(g + 1) * LANES].astype(F32)
        return jnp.concatenate([nat_ref[i, g] for g in range(N_PAIRS)], axis=1)

    os_ = [natural(i, ref) for i, ref in enumerate((o1_ref, o2_ref, o3_ref))]
    l1, l2, l3 = [natural(3 + i, ref) for i, ref in enumerate((l1_ref, l2_ref, l3_ref))]
    mx = jnp.maximum(jnp.maximum(l1, l2), l3)
    e1, e2, e3 = jnp.exp(l1 - mx), jnp.exp(l2 - mx), jnp.exp(l3 - mx)
    yd = (e1 * os_[0] + e2 * os_[1] + e3 * os_[2]) / (e1 + e2 + e3)
    merged = (gf_ref[0].astype(F32) * _dot(yf_ref[0], wbf_ref[...])
              + gd_ref[0].astype(F32) * _dot(yd.astype(BF16), wbd_ref[...]))
    y = _dot(merged.astype(BF16), wo_ref[...])
    z = DN_ALPHA * x_ref[0] + (1.0 + g1_ref[0]) * y
    out_ref[0] = _layer_norm(z, lng_ref[...], lnb_ref[...])


def _mix(x, yf, os_, lses, gf, gd, mod3, wbf, wbd, wo, lng, lnb):
    bsz, seq, d = x.shape
    tm = ROW_TILE
    const = lambda shape: pl.BlockSpec(shape, lambda b, j: (0,) * len(shape))
    tok = lambda w: pl.BlockSpec((1, tm, w), lambda b, j: (b, j, 0))
    hw = HEADS_WIDTH
    res_major = [pl.BlockSpec((1, dil, tm // dil, hw), lambda b, j: (b, 0, j, 0)) for _, dil in DIL_PATTERNS]
    return pl.pallas_call(
        _mix_kernel,
        grid=(bsz, seq // tm),
        in_specs=[tok(d), tok(hw)] + res_major * 2 + [tok(d), tok(d),
                  pl.BlockSpec((1, 1, d), lambda b, j: (b, 0, 2)),
                  const(wbf.shape), const(wbd.shape), const(wo.shape), const(lng.shape), const(lnb.shape)],
        out_specs=tok(d),
        out_shape=jax.ShapeDtypeStruct((bsz, seq, d), F32),
        scratch_shapes=[pltpu.VMEM((6, N_PAIRS, tm, LANES), F32)],
        compiler_params=_params(("arbitrary", "arbitrary"), 48),
        name="mix",
    )(x, yf, *os_, *lses, gf, gd, mod3, wbf, wbd, wo, lng, lnb)


def _router_kernel(x_ref, sc_ref, sh_ref, wrh_ref, wrl_ref, bias_ref,
                   u_ref, up_ref, eidx_ref, rank_ref, gw_ref, cnt_ref):
    seq = x_ref.shape[1]
    tc = ROUTE_CHUNK
    e_iota = lax.broadcasted_iota(jnp.int32, (N_EXPERTS, tc), 0)
    g_iota = lax.broadcasted_iota(jnp.int32, (N_GROUPS, tc), 0)
    s_iota = lax.broadcasted_iota(jnp.int32, (GROUP_SIZE, tc), 0)
    r = lax.broadcasted_iota(jnp.int32, (tc, tc), 0)
    c = lax.broadcasted_iota(jnp.int32, (tc, tc), 1)
    upper = jnp.where(r <= c, 1.0, 0.0).astype(BF16)
    ones = jnp.ones((16, tc), BF16)
    carry = jnp.zeros((N_EXPERTS, 1), F32)
    cnt = jnp.zeros((16, N_EXPERTS), F32)

    for ch in range(seq // tc):
        rows = slice(ch * tc, (ch + 1) * tc)
        u = x_ref[0, rows, :] * (1.0 + sc_ref[0]) + sh_ref[0]
        u_ref[0, rows, :] = u.astype(BF16)
        up_ref[0, rows, :] = _pack_words(u)
        u_hi, u_lo = _split2(u)
        logits = _dot_nt(wrh_ref[...], u_hi) + _dot_nt(wrh_ref[...], u_lo) + _dot_nt(wrl_ref[...], u_hi)
        scores = _sigmoid(logits)
        sel = scores + bias_ref[...]

        gs_rows = []
        for g in range(N_GROUPS):
            blk = sel[g * GROUP_SIZE:(g + 1) * GROUP_SIZE]
            m1 = jnp.max(blk, axis=0, keepdims=True)
            i1 = jnp.min(jnp.where(blk == m1, s_iota, GROUP_SIZE), axis=0, keepdims=True)
            m2 = jnp.max(jnp.where(s_iota == i1, -jnp.inf, blk), axis=0, keepdims=True)
            gs_rows.append(m1 + m2)
        gs = jnp.concatenate(gs_rows, axis=0)
        beaten = jnp.zeros((N_GROUPS, tc), F32)
        for g in range(N_GROUPS):
            other = gs[g:g + 1]
            wins = (other > gs) | ((other == gs) & (g_iota > g))
            beaten = beaten + jnp.where(wins, 1.0, 0.0)
        keep = beaten < TOPK_GROUPS
        cand = jnp.concatenate(
            [jnp.where(keep[g:g + 1], sel[g * GROUP_SIZE:(g + 1) * GROUP_SIZE], NEG_INF) for g in range(N_GROUPS)],
            axis=0)
        chosen = jnp.zeros((N_EXPERTS, tc), F32)
        picks = []
        for _ in range(TOP_K):
            m = jnp.max(cand, axis=0, keepdims=True)
            idx = jnp.min(jnp.where(cand == m, e_iota, N_EXPERTS), axis=0, keepdims=True)
            hit = e_iota == idx
            chosen = jnp.where(hit, 1.0, chosen)
            cand = jnp.where(hit, -jnp.inf, cand)
            picks.append(idx)
        g_raw = jnp.where(chosen > 0.0, scores, 0.0)
        gate = g_raw / jnp.sum(g_raw, axis=0, keepdims=True) * ROUTED_SCALE

        chosen_b = chosen.astype(BF16)
        incl = _dot(chosen_b, upper) + carry
        carry = incl[:, tc - 1:tc]
        cnt = cnt + _dot_nt(ones, chosen_b)

        rank_rows, gate_rows = [], []
        for idx in picks:
            hit = e_iota == idx
            rank_rows.append(jnp.sum(jnp.where(hit, incl - 1.0, 0.0), axis=0, keepdims=True))
            gate_rows.append(jnp.sum(jnp.where(hit, gate, 0.0), axis=0, keepdims=True))
        eidx_ref[0, :, rows] = jnp.concatenate(picks, axis=0)
        rank_ref[0, :, rows] = jnp.concatenate(rank_rows, axis=0).astype(jnp.int32)
        gate_t = jnp.concatenate(gate_rows + [jnp.zeros((LANES - TOP_K, tc), F32)], axis=0).T
        gw_ref[0, rows, :] = gate_t[:, :TOP_K]

    cnt_ref[0] = cnt[0:1].astype(jnp.int32)


def _router(x1, mod3, wr_hi, wr_lo, bias):
    bsz, seq, d = x1.shape
    full = pl.BlockSpec((1, seq, d), lambda b: (b, 0, 0))
    ks = pl.BlockSpec((1, TOP_K, seq), lambda b: (b, 0, 0))
    return pl.pallas_call(
        _router_kernel,
        grid=(bsz,),
        in_specs=[full,
                  pl.BlockSpec((1, 1, d), lambda b: (b, 0, 4)),
                  pl.BlockSpec((1, 1, d), lambda b: (b, 0, 3)),
                  pl.BlockSpec(wr_hi.shape, lambda b: (0, 0)),
                  pl.BlockSpec(wr_lo.shape, lambda b: (0, 0)),
                  pl.BlockSpec(bias.shape, lambda b: (0, 0))],
        out_specs=[full, pl.BlockSpec((1, seq, PACKED), lambda b: (b, 0, 0)), ks, ks,
                   pl.BlockSpec((1, seq, TOP_K), lambda b: (b, 0, 0)),
                   pl.BlockSpec((1, 1, N_EXPERTS), lambda b: (b, 0, 0))],
        out_shape=[jax.ShapeDtypeStruct((bsz, seq, d), BF16),
                   jax.ShapeDtypeStruct((bsz, seq, PACKED), F32),
                   jax.ShapeDtypeStruct((bsz, TOP_K, seq), jnp.int32),
                   jax.ShapeDtypeStruct((bsz, TOP_K, seq), jnp.int32),
                   jax.ShapeDtypeStruct((bsz, seq, TOP_K), F32),
                   jax.ShapeDtypeStruct((bsz, 1, N_EXPERTS), jnp.int32)],
        compiler_params=_params(("arbitrary",), 48),
        name="router",
    )(x1, mod3, mod3, wr_hi, wr_lo, bias)


def _slot_kernel(base_ref, eidx_ref, rank_ref, slot_ref):
    b = pl.program_id(0)
    eidx = eidx_ref[0]
    start = jnp.zeros(eidx.shape, jnp.int32)
    for e in range(N_EXPERTS):
        start = jnp.where(eidx == e, base_ref[b * N_EXPERTS + e], start)
    slot_ref[0] = start + rank_ref[0]


def _slots(base, eidx, rank):
    bsz, _, seq = eidx.shape
    ks = pl.BlockSpec((1, TOP_K, seq), lambda b, c: (b, 0, 0))
    return pl.pallas_call(
        _slot_kernel,
        grid_spec=pltpu.PrefetchScalarGridSpec(num_scalar_prefetch=1, grid=(bsz,), in_specs=[ks, ks], out_specs=ks),
        out_shape=jax.ShapeDtypeStruct((bsz, TOP_K, seq), jnp.int32),
        compiler_params=_params(("arbitrary",), 16),
        name="slots",
    )(base.reshape(bsz * N_EXPERTS), eidx, rank)


def _sc_mesh():
    return plsc.VectorSubcoreMesh(core_axis_name="core", subcore_axis_name="subcore")


def _sc_worker():
    return lax.axis_index("subcore") * SC_CORES + lax.axis_index("core")


def _sc_dispatch(rows, slot, n_slots):
    n_tok, width = rows.shape
    seq = slot.shape[1]
    chunk = DISPATCH_CHUNK
    per_worker = n_tok // SC_WORKERS
    assert per_worker % chunk == 0 and seq % chunk == 0

    @functools.partial(
        pl.kernel, mesh=_sc_mesh(),
        out_type=jax.ShapeDtypeStruct((n_slots, width), rows.dtype),
        scratch_types=[pltpu.VMEM((TOP_K, chunk), jnp.int32), pltpu.VMEM((chunk, width), rows.dtype),
                       pltpu.SemaphoreType.DMA],
    )
    def dispatch(rows_hbm, slot_hbm, out_hbm, idx_v, rows_v, sem):
        base = _sc_worker() * per_worker

        @pl.loop(0, per_worker // chunk)
        def _(i):
            off = base + i * chunk
            b = off // seq
            n0 = off - b * seq
            pltpu.sync_copy(slot_hbm.at[pl.ds(b * TOP_K, TOP_K), pl.ds(n0, chunk)], idx_v)
            pltpu.sync_copy(rows_hbm.at[pl.ds(off, chunk)], rows_v)
            copies = [pltpu.async_copy(rows_v, out_hbm.at[idx_v.at[k]], sem) for k in range(TOP_K)]
            for cp in copies:
                cp.wait()

    return dispatch(rows, slot)


def _sc_gather(table, idx):
    n_out = idx.shape[0]
    width = table.shape[1]
    chunk = GATHER_CHUNK
    per_worker = n_out // SC_WORKERS
    steps = per_worker // chunk
    assert per_worker % chunk == 0 and steps % 2 == 0

    @functools.partial(
        pl.kernel, mesh=_sc_mesh(),
        out_type=jax.ShapeDtypeStruct((n_out, width), table.dtype),
        scratch_types=[pltpu.VMEM((2, chunk), jnp.int32), pltpu.VMEM((2, chunk, width), table.dtype),
                       pltpu.SemaphoreType.DMA, pltpu.SemaphoreType.DMA((2,))],
    )
    def gather(table_hbm, idx_hbm, out_hbm, idx_v, rows_v, gather_sem, out_sems):
        base = _sc_worker() * per_worker

        def write_out(buf, off):
            return pltpu.make_async_copy(rows_v.at[buf], out_hbm.at[pl.ds(off, chunk)], out_sems.at[buf])

        @pl.loop(0, steps, step=2)
        def _(i):
            for buf in range(2):
                off = base + (i + buf) * chunk

                @pl.when(i >= 2)
                def _():
                    write_out(buf, off - 2 * chunk).wait()

                pltpu.sync_copy(idx_hbm.at[pl.ds(off, chunk)], idx_v.at[buf])
                pltpu.async_copy(table_hbm.at[idx_v.at[buf]], rows_v.at[buf], gather_sem).wait()
                write_out(buf, off).start()

        for buf in range(2):
            write_out(buf, base + (steps - 2 + buf) * chunk).wait()

    return gather(table, idx)


def _gmm_kernel(blk_e_ref, used_ref, xs_ref, wgu_ref, wd_ref, out_ref):
    half = PACKED

    @pl.when(pl.program_id(0) < used_ref[0])
    def _():
        lo, hi = _unpack_words(xs_ref[...])
        hgu = _dot(lo.astype(BF16), wgu_ref[0, :half, :]) + _dot(hi.astype(BF16), wgu_ref[0, half:, :])
        hg = hgu[:, :EXPERT_FF]
        h = (hg * _sigmoid(hg) * hgu[:, EXPERT_FF:]).astype(BF16)
        out_ref[...] = _pack_words(_dot(h, wd_ref[0]))


def _gmm(xs, blk_e, used, wgu, wd):
    n_slots, width = xs.shape
    bm = EXPERT_BLOCK
    d = wgu.shape[1]
    live = lambda i, blk_e, used: (jnp.minimum(i, used[0] - 1), 0)
    return pl.pallas_call(
        _gmm_kernel,
        grid_spec=pltpu.PrefetchScalarGridSpec(
            num_scalar_prefetch=2,
            grid=(n_slots // bm,),
            in_specs=[pl.BlockSpec((bm, width), live),
                      pl.BlockSpec((1, d, 2 * EXPERT_FF), lambda i, blk_e, used: (blk_e[i], 0, 0)),
                      pl.BlockSpec((1, EXPERT_FF, d), lambda i, blk_e, used: (blk_e[i], 0, 0))],
            out_specs=pl.BlockSpec((bm, width), live)),
        out_shape=jax.ShapeDtypeStruct((n_slots, width), F32),
        compiler_params=_params(("arbitrary",), 32),
        name="gmm",
    )(blk_e, used, xs, wgu, wd)


def _ffn_out_kernel(x_ref, u_ref, yg_ref, gw_ref, g2_ref, wsgu_ref, wsd_ref, lng_ref, lnb_ref, out_ref):
    hgu = _dot(u_ref[0], wsgu_ref[...])
    hg = hgu[:, :SHARED_FF]
    h = (hg * _sigmoid(hg) * hgu[:, SHARED_FF:]).astype(BF16)
    shared = _dot(h, wsd_ref[...])
    gw = gw_ref[0]
    acc_lo = shared[:, :PACKED]
    acc_hi = shared[:, PACKED:]
    for k in range(TOP_K):
        lo, hi = _unpack_words(yg_ref[0, k])
        w = gw[:, k:k + 1]
        acc_lo = acc_lo + w * lo
        acc_hi = acc_hi + w * hi
    y = jnp.concatenate([acc_lo, acc_hi], axis=1)
    z = DN_ALPHA * x_ref[0] + (1.0 + g2_ref[0]) * y
    out_ref[0] = _layer_norm(z, lng_ref[...], lnb_ref[...])


def _ffn_out(x1, u2, yg, gw, mod3, wsgu, wsd, lng, lnb):
    bsz, seq, d = x1.shape
    tm = ROW_TILE
    const = lambda shape: pl.BlockSpec(shape, lambda b, j: (0,) * len(shape))
    tok = pl.BlockSpec((1, tm, d), lambda b, j: (b, j, 0))
    return pl.pallas_call(
        _ffn_out_kernel,
        grid=(bsz, seq // tm),
        in_specs=[tok, tok,
                  pl.BlockSpec((1, TOP_K, tm, PACKED), lambda b, j: (b, 0, j, 0)),
                  pl.BlockSpec((1, tm, TOP_K), lambda b, j: (b, j, 0)),
                  pl.BlockSpec((1, 1, d), lambda b, j: (b, 0, 5)),
                  const(wsgu.shape), const(wsd.shape), const(lng.shape), const(lnb.shape)],
        out_specs=tok,
        out_shape=jax.ShapeDtypeStruct((bsz, seq, d), F32),
        compiler_params=_params(("arbitrary", "arbitrary"), 48),
        name="ffn_out",
    )(x1, u2, yg, gw, mod3, wsgu, wsd, lng, lnb)


def _moe_layout(cnt, n_blocks):
    bm = EXPERT_BLOCK
    total = jnp.sum(cnt, axis=0)
    padded = (total + bm - 1) // bm * bm
    ends = jnp.cumsum(padded)
    base = (ends - padded)[None, :] + jnp.cumsum(cnt, axis=0) - cnt
    first_row = jnp.arange(n_blocks, dtype=ends.dtype) * bm
    blk_e = jnp.minimum(jnp.sum(ends[None, :] <= first_row[:, None], axis=1), N_EXPERTS - 1)
    used = (ends[-1:] // bm)
    return base.astype(jnp.int32), blk_e.astype(jnp.int32), used.astype(jnp.int32)


def _rope_tables(seq):
    half = HEAD_DIM // 2
    inv_freq = ROPE_THETA ** (-jnp.arange(half, dtype=F32) * 2.0 / HEAD_DIM)
    ang = jnp.arange(seq, dtype=F32)[:, None] * inv_freq[None, :]
    cos, sin = jnp.cos(ang), jnp.sin(ang)
    zero = jnp.zeros_like(sin)
    reps = LANES // HEAD_DIM
    cos_t = jnp.tile(jnp.concatenate([cos, cos], axis=1), (1, reps))
    sina_t = jnp.tile(jnp.concatenate([-sin, zero], axis=1), (1, reps))
    sinb_t = jnp.tile(jnp.concatenate([zero, sin], axis=1), (1, reps))
    return cos_t, sina_t, sinb_t


def _fox_bias_tables():
    n_split = 3
    aw = N_HEADS * LANES
    heads = jnp.arange(N_HEADS)
    selq = jnp.zeros((LANES, aw), F32)
    selk = jnp.zeros((LANES, aw), F32)
    oneq = jnp.zeros((1, aw), F32)
    onek = jnp.zeros((1, aw), F32)
    for j in range(n_split):
        selk = selk.at[j * N_HEADS + heads, heads * LANES + FOX_BIAS_LANE + j].set(-1.0)
        selq = selq.at[j * N_HEADS + heads, heads * LANES + FOX_BIAS_LANE + n_split + j].set(1.0)
        oneq = oneq.at[0, heads * LANES + FOX_BIAS_LANE + j].set(1.0)
        onek = onek.at[0, heads * LANES + FOX_BIAS_LANE + n_split + j].set(1.0)
    return selq.astype(BF16), selk.astype(BF16), oneq, onek


def kernel(x, c, w_ada, b_ada, w_in, b_forget, b_gate, w_br_fox, w_br_dil, w_o, ln_g, ln_b, w_router, router_bias,
           w_exp_gate, w_exp_up, w_exp_down, w_sh_gate, w_sh_up, w_sh_down):
    bsz, seq, d = x.shape
    depth = w_ada.shape[0]
    hw = HEADS_WIDTH
    n_tok = bsz * seq

    mod = _ada(c, w_ada, b_ada)

    o_f, o_d, o_g = 3 * hw, 3 * hw + N_HEADS, 6 * hw + N_HEADS
    wfox = w_in[:, :, :o_f].astype(BF16)
    wfg = jnp.pad(w_in[:, :, o_f:o_d], ((0, 0), (0, 0), (0, LANES - N_HEADS))).astype(BF16)
    wdil = w_in[:, :, o_d:o_g].astype(BF16)
    wgate = w_in[:, :, o_g:].astype(BF16)
    bfg = jnp.pad(b_forget, ((0, 0), (0, LANES - N_HEADS))).reshape(depth, 1, LANES)
    bgate = b_gate.reshape(depth, 1, 2 * d)
    wr_t = jnp.swapaxes(w_router, 1, 2)
    wr_hi = wr_t.astype(BF16)
    wr_lo = (wr_t - wr_hi.astype(F32)).astype(BF16)
    rbias = router_bias.reshape(depth, N_EXPERTS, 1)
    wgu = jnp.concatenate([w_exp_gate, w_exp_up], axis=-1).astype(BF16)
    wd = w_exp_down.astype(BF16)
    wsgu = jnp.concatenate([w_sh_gate, w_sh_up], axis=-1).astype(BF16)
    wsd = w_sh_down.astype(BF16)
    cos_t, sina_t, sinb_t = _rope_tables(seq)
    selq, selk, oneq, onek = _fox_bias_tables()

    layer_params = (mod, wfox, wdil, wgate, wfg, bfg, bgate, w_br_fox.astype(BF16), w_br_dil.astype(BF16),
                    w_o.astype(BF16), ln_g, ln_b, wr_hi, wr_lo, rbias, wgu, wd, wsgu, wsd)

    n_streams = 2 if n_tok % (2 * SC_WORKERS * DISPATCH_CHUNK) == 0 and bsz % 2 == 0 else 1
    sb = bsz // n_streams
    s_tok = sb * seq
    n_blocks = s_tok * TOP_K // EXPERT_BLOCK + N_EXPERTS

    def stream_layer(xc, mod3, prm):
        (_, wfox_l, wdil_l, wgate_l, wfg_l, bfg_l, bgate_l, wbf_l, wbd_l, wo_l, lng_l, lnb_l,
         wrh_l, wrl_l, rb_l, wgu_l, wd_l, wsgu_l, wsd_l) = prm
        qa, ka, vt, *rest = _inproj(
            xc, mod3, wfox_l, wdil_l, wgate_l, wfg_l, bfg_l, bgate_l, cos_t, sina_t, sinb_t, selq, selk, oneq, onek)
        gf, gd = rest[9:]
        yf = _fox(qa, ka, vt)
        dil = [_dil(*rest[3 * i:3 * i + 3]) for i in range(len(DIL_PATTERNS))]
        x1 = _mix(xc, yf, [o for o, _ in dil], [l for _, l in dil], gf, gd, mod3, wbf_l, wbd_l, wo_l,
                  lng_l[0:1], lnb_l[0:1])
        u2, u2p, eidx, rank, gw, cnt = _router(x1, mod3, wrh_l, wrl_l, rb_l)
        base, blk_e, used = _moe_layout(cnt.reshape(sb, N_EXPERTS), n_blocks)
        slot = _slots(base, eidx, rank)
        xs = _sc_dispatch(u2p.reshape(s_tok, PACKED), slot.reshape(sb * TOP_K, seq), n_blocks * EXPERT_BLOCK)
        ys = _gmm(xs, blk_e, used, wgu_l, wd_l)
        yg = _sc_gather(ys, slot.reshape(s_tok * TOP_K)).reshape(sb, TOP_K, seq, PACKED)
        return _ffn_out(x1, u2, yg, gw, mod3, wsgu_l, wsd_l, lng_l[1:2], lnb_l[1:2])

    def layer(xcs, prm):
        mod3 = prm[0].reshape(bsz, 1, 6 * d)
        return tuple(stream_layer(xc, mod3[i * sb:(i + 1) * sb], prm) for i, xc in enumerate(xcs)), None

    outs, _ = lax.scan(layer, tuple(x[i * sb:(i + 1) * sb] for i in range(n_streams)), layer_params)
    return outs[0] if n_streams == 1 else jnp.concatenate(outs, axis=0)
```

```python
import functools

import jax
import jax.numpy as jnp
from jax import lax
from jax.experimental import pallas as pl
from jax.experimental.pallas import tpu as pltpu
from jax.experimental.pallas import tpu_sc as plsc

D_MODEL = 1024
DEPTH = 4
HEAD_DIM = 64
N_HEADS = 8
HEADS_WIDTH = N_HEADS * HEAD_DIM
DIL_PATTERNS = ((128, 1), (512, 4), (2048, 16))
ATTN_BLOCK = 128
ROPE_THETA = 10000.0
N_EXPERTS = 64
TOP_K = 8
N_GROUPS = 8
GROUP_SIZE = N_EXPERTS // N_GROUPS
TOPK_GROUPS = 4
EXPERT_FF = 256
SHARED_FF = 256
ROUTED_SCALE = 2.5
DN_ALPHA = (2 * DEPTH) ** 0.25
LN_EPS = 1e-5
NEG_INF = -1e30
SCORE_SCALE = HEAD_DIM ** -0.5
LOG2_E = 1.4426950408889634

LANES = 128
PAIR = 2 * HEAD_DIM
N_PAIRS = HEADS_WIDTH // PAIR

ROW_TILE = 512
FOX_Q = 256
FOX_K = 256
FOX_BIAS_LANE = HEAD_DIM
DIL_STEP_BLOCKS = 4
ROUTE_CHUNK = 512
EXPERT_BLOCK = 512
PACKED = D_MODEL // 2

SC_CORES = 2
SC_SUBCORES = 16
SC_WORKERS = SC_CORES * SC_SUBCORES
DISPATCH_CHUNK = 128
GATHER_CHUNK = 64

BF16 = jnp.bfloat16
F32 = jnp.float32


def _dot(a, b):
    return jnp.dot(a, b, preferred_element_type=F32)


def _dot_nt(a, b):
    return lax.dot_general(a, b, (((1,), (1,)), ((), ())), preferred_element_type=F32)


def _split2(x):
    hi = x.astype(BF16)
    lo = (x - hi.astype(F32)).astype(BF16)
    return hi, lo


def _split3(x):
    hi = x.astype(BF16)
    r = x - hi.astype(F32)
    mid = r.astype(BF16)
    lo = (r - mid.astype(F32)).astype(BF16)
    return hi, mid, lo


def _sigmoid(x):
    return 1.0 / (1.0 + jnp.exp(-x))


def _layer_norm(z, g, b):
    mu = jnp.mean(z, axis=-1, keepdims=True)
    zc = z - mu
    var = jnp.mean(zc * zc, axis=-1, keepdims=True)
    return zc * lax.rsqrt(var + LN_EPS) * g + b


def _pack_words(v):
    half = v.shape[1] // 2
    bits = pltpu.bitcast(v.astype(BF16).astype(F32), jnp.uint32)
    word = (bits[:, half:] & jnp.uint32(0xFFFF0000)) | (bits[:, :half] >> 16)
    return pltpu.bitcast(word, F32)


def _unpack_words(w):
    bits = pltpu.bitcast(w, jnp.uint32)
    return pltpu.bitcast(bits << 16, F32), pltpu.bitcast(bits & jnp.uint32(0xFFFF0000), F32)


def _params(sem, vmem_mb):
    return pltpu.CompilerParams(dimension_semantics=sem, vmem_limit_bytes=vmem_mb * 1024 * 1024)


def _ada_kernel(c_ref, w_ref, b_ref, o_ref):
    c = c_ref[...]
    cs = c * _sigmoid(c)
    c_hi, c_lo = _split2(cs)
    w_hi, w_lo = _split2(w_ref[0])
    o_ref[0] = _dot(c_hi, w_hi) + _dot(c_hi, w_lo) + _dot(c_lo, w_hi) + b_ref[0]


def _ada(c, w_ada, b_ada):
    depth, d, n = w_ada.shape
    bsz = c.shape[0]
    tn = 1536
    return pl.pallas_call(
        _ada_kernel,
        grid=(depth, n // tn),
        in_specs=[
            pl.BlockSpec((bsz, d), lambda l, j: (0, 0)),
            pl.BlockSpec((1, d, tn), lambda l, j: (l, 0, j)),
            pl.BlockSpec((1, 1, tn), lambda l, j: (l, 0, j)),
        ],
        out_specs=pl.BlockSpec((1, bsz, tn), lambda l, j: (l, 0, j)),
        out_shape=jax.ShapeDtypeStruct((depth, bsz, n), F32),
        compiler_params=_params(("arbitrary", "arbitrary"), 40),
        name="ada",
    )(c, w_ada, b_ada.reshape(depth, 1, n))


def _inproj_kernel(x_ref, sc_ref, sh_ref, wfox_ref, wdil_ref, wgate_ref, wfg_ref, bfg_ref, bgate_ref,
                   cos_ref, sina_ref, sinb_ref, selq_ref, selk_ref, oneq_ref, onek_ref,
                   qa_ref, ka_ref, vt_ref, *rest):
    dil_refs = rest[:9]
    gf_ref, gd_ref, carry_ref, perm_ref = rest[9:]
    j = pl.program_id(1)
    tm = x_ref.shape[1]
    u = (x_ref[0] * (1.0 + sc_ref[0]) + sh_ref[0]).astype(BF16)

    ff = _dot(u, wfg_ref[...]) + bfg_ref[...]
    lf = jnp.minimum(ff, 0.0) - jnp.log(1.0 + jnp.exp(-jnp.abs(ff)))
    r = lax.broadcasted_iota(jnp.int32, (tm, tm), 0)
    c = lax.broadcasted_iota(jnp.int32, (tm, tm), 1)
    tri = jnp.where(r >= c, 1.0, 0.0).astype(BF16)
    hi, mid, lo = _split3(lf)
    csum = _dot(tri, hi) + _dot(tri, mid) + _dot(tri, lo)

    @pl.when(j == 0)
    def _():
        carry_ref[...] = jnp.zeros_like(carry_ref)

    cum = csum + carry_ref[...]
    carry_ref[...] = cum[tm - 1:tm, :]

    lane = lax.broadcasted_iota(jnp.int32, (1, LANES), 1)
    c_hi, c_mid, c_lo = (t.astype(F32) for t in _split3(cum * LOG2_E))
    parts = jnp.where(lane < N_HEADS, c_hi,
                      jnp.where(lane < 2 * N_HEADS, pltpu.roll(c_mid, N_HEADS, 1),
                                jnp.where(lane < 3 * N_HEADS, pltpu.roll(c_lo, 2 * N_HEADS, 1), 0.0))).astype(BF16)
    bias_q = _dot(parts, selq_ref[...]) + oneq_ref[...]
    bias_k = _dot(parts, selk_ref[...]) + onek_ref[...]
    pf = _dot(u, wfox_ref[...])
    data = lane < HEAD_DIM
    for dst, off, bias, scale in ((qa_ref, 0, bias_q, SCORE_SCALE * LOG2_E), (ka_ref, HEADS_WIDTH, bias_k, 1.0)):
        for p in range(N_PAIRS):
            pair = pf[:, off + p * PAIR: off + (p + 1) * PAIR] * scale
            for hh, src in ((0, pair), (1, pltpu.roll(pair, HEAD_DIM, 1))):
                t = (2 * p + hh) * LANES
                dst[0, :, t:t + LANES] = jnp.where(data, src, bias[:, t:t + LANES]).astype(BF16)
    vt = pf[:, 2 * HEADS_WIDTH:].T
    for cb in range(tm // FOX_K):
        vt_ref[0, cb] = vt[:, cb * FOX_K:(cb + 1) * FOX_K].astype(BF16)

    pd = _dot(u, wdil_ref[...])
    rows = pl.ds(pl.multiple_of(j * tm, tm), tm)
    cos = cos_ref[rows, :]
    sina = sina_ref[rows, :]
    sinb = sinb_ref[rows, :]
    for i, off in ((0, 0), (1, HEADS_WIDTH)):
        for g in range(N_PAIRS):
            xg = pd[:, off + g * LANES: off + (g + 1) * LANES]
            ahead = pltpu.roll(xg, LANES - HEAD_DIM // 2, 1)
            behind = pltpu.roll(xg, HEAD_DIM // 2, 1)
            rot = xg * cos + ahead * sina + behind * sinb
            perm_ref[i, g] = rot * SCORE_SCALE if i == 0 else rot
    for g in range(N_PAIRS):
        perm_ref[2, g] = pd[:, 2 * HEADS_WIDTH + g * LANES:2 * HEADS_WIDTH + (g + 1) * LANES]
    for i in range(3):
        for (_, dilation), dst in zip(DIL_PATTERNS, dil_refs[i::3]):
            for res in range(dilation):
                for g in range(N_PAIRS):
                    dst[0, res, :, g * LANES:(g + 1) * LANES] = perm_ref[
                        i, g, pl.ds(res, tm // dilation, stride=dilation), :].astype(BF16)

    pg = _sigmoid(_dot(u, wgate_ref[...]) + bgate_ref[...])
    gf_ref[0] = pg[:, :D_MODEL].astype(BF16)
    gd_ref[0] = pg[:, D_MODEL:].astype(BF16)


def _inproj(x, mod3, wfox, wdil, wgate, wfg, bfg, bgate, cos_t, sina_t, sinb_t, selq, selk, oneq, onek):
    bsz, seq, d = x.shape
    tm = ROW_TILE
    const = lambda shape: pl.BlockSpec(shape, lambda b, j: (0,) * len(shape))
    tok = lambda w: pl.BlockSpec((1, tm, w), lambda b, j: (b, j, 0))
    hw = HEADS_WIDTH
    aw = N_HEADS * LANES
    dil_shapes, dil_specs = [], []
    for _, dil in DIL_PATTERNS:
        dil_shapes += [jax.ShapeDtypeStruct((bsz, dil, seq // dil, hw), BF16)] * 3
        dil_specs += [pl.BlockSpec((1, dil, tm // dil, hw), lambda b, j: (b, 0, j, 0))] * 3
    outs = ([jax.ShapeDtypeStruct((bsz, seq, aw), BF16)] * 2
            + [jax.ShapeDtypeStruct((bsz, seq // FOX_K, hw, FOX_K), BF16)]
            + dil_shapes + [jax.ShapeDtypeStruct((bsz, seq, d), BF16)] * 2)
    return pl.pallas_call(
        _inproj_kernel,
        grid=(bsz, seq // tm),
        in_specs=[
            tok(d),
            pl.BlockSpec((1, 1, d), lambda b, j: (b, 0, 1)),
            pl.BlockSpec((1, 1, d), lambda b, j: (b, 0, 0)),
            const(wfox.shape), const(wdil.shape), const(wgate.shape), const(wfg.shape),
            const(bfg.shape), const(bgate.shape),
            const(cos_t.shape), const(sina_t.shape), const(sinb_t.shape),
            const(selq.shape), const(selk.shape), const(oneq.shape), const(onek.shape),
        ],
        out_specs=[tok(aw)] * 2 + [pl.BlockSpec((1, tm // FOX_K, hw, FOX_K), lambda b, j: (b, j, 0, 0))]
                  + dil_specs + [tok(d)] * 2,
        out_shape=outs,
        scratch_shapes=[pltpu.VMEM((1, LANES), F32), pltpu.VMEM((3, N_PAIRS, tm, LANES), F32)],
        compiler_params=_params(("arbitrary", "arbitrary"), 56),
        name="inproj",
    )(x, mod3, mod3, wfox, wdil, wgate, wfg, bfg, bgate, cos_t, sina_t, sinb_t, selq, selk, oneq, onek)


def _fox_kernel(qa_ref, ka_ref, vt_ref, o_ref, m_ref, l_ref, acc_ref):
    seq = qa_ref.shape[1]
    tq, tk = FOX_Q, FOX_K
    k_pos = lax.broadcasted_iota(jnp.int32, (tk, tq), 0)
    q_pos = lax.broadcasted_iota(jnp.int32, (tk, tq), 1)
    ones = jnp.ones((16, tk), BF16)

    def q_body(qi, _):
        q0 = pl.multiple_of(qi * tq, tq)
        m_ref[...] = jnp.full(m_ref.shape, NEG_INF, F32)
        l_ref[...] = jnp.zeros(l_ref.shape, F32)
        acc_ref[...] = jnp.zeros(acc_ref.shape, F32)

        def kv_step(kb, k0, visible):
            scores = []
            for h in range(N_HEADS):
                ls = slice(h * LANES, (h + 1) * LANES)
                scores.append(_dot_nt(ka_ref[0, pl.ds(k0, tk), ls], qa_ref[0, pl.ds(q0, tq), ls]))
            m_all = m_ref[...]
            l_all = l_ref[...]
            probs, decay, m_rows, l_rows = [], [], [], []
            for h in range(N_HEADS):
                s = scores[h] if visible is None else jnp.where(visible, scores[h], NEG_INF)
                m_old = m_all[h:h + 1, :]
                m_new = jnp.maximum(m_old, jnp.max(s, axis=0, keepdims=True))
                m_rows.append(m_new)
                decay.append(jnp.exp2(m_old - m_new))
                probs.append(jnp.exp2((s - m_new).astype(BF16)))
            m_ref[...] = jnp.concatenate(m_rows, axis=0)
            for h in range(N_HEADS):
                vh = vt_ref[0, kb, h * HEAD_DIM:(h + 1) * HEAD_DIM, :]
                acc_ref[h] = decay[h] * acc_ref[h] + _dot(vh, probs[h])
                l_rows.append(decay[h] * l_all[h:h + 1, :] + _dot(ones, probs[h])[0:1])
            l_ref[...] = jnp.concatenate(l_rows, axis=0)

        def full_block(jb, carry):
            kv_step(jb, pl.multiple_of(jb * tk, tk), None)
            return carry

        lax.fori_loop(0, qi * (tq // tk), full_block, 0)
        for dblk in range(tq // tk):
            kb = qi * (tq // tk) + dblk
            kv_step(kb, pl.multiple_of(kb * tk, tk), k_pos + dblk * tk <= q_pos)

        for p in range(N_PAIRS):
            pair = jnp.concatenate([acc_ref[2 * p + hh] / l_ref[2 * p + hh:2 * p + hh + 1, :] for hh in range(2)],
                                   axis=0)
            o_ref[0, pl.ds(q0, tq), p * PAIR:(p + 1) * PAIR] = pair.T.astype(BF16)
        return 0

    lax.fori_loop(0, seq // tq, q_body, 0)


def _fox(qa, ka, vt):
    bsz, seq, aw = qa.shape
    hw = HEADS_WIDTH
    full = pl.BlockSpec((1, seq, aw), lambda b: (b, 0, 0))
    return pl.pallas_call(
        _fox_kernel,
        grid=(bsz,),
        in_specs=[full, full, pl.BlockSpec((1, seq // FOX_K, hw, FOX_K), lambda b: (b, 0, 0, 0))],
        out_specs=pl.BlockSpec((1, seq, hw), lambda b: (b, 0, 0)),
        out_shape=jax.ShapeDtypeStruct((bsz, seq, hw), BF16),
        scratch_shapes=[pltpu.VMEM((N_HEADS, FOX_Q), F32), pltpu.VMEM((N_HEADS, FOX_Q), F32),
                        pltpu.VMEM((N_HEADS, HEAD_DIM, FOX_Q), F32)],
        compiler_params=_params(("arbitrary",), 48),
        name="fox",
    )(qa, ka, vt)


def _dil_kernel(q_ref, kc_ref, kp_ref, vc_ref, vp_ref, o_ref, lse_ref):
    n = pl.program_id(2)
    blk = ATTN_BLOCK
    n_res = q_ref.shape[0]
    n_sub = q_ref.shape[1] // blk
    lane = lax.broadcasted_iota(jnp.int32, (1, PAIR), 1)
    first = lane < HEAD_DIM
    i = lax.broadcasted_iota(jnp.int32, (blk, 2 * blk), 0)
    c = lax.broadcasted_iota(jnp.int32, (blk, 2 * blk), 1)
    in_window = (c >= i) & (c <= i + blk)
    has_prev = (c >= blk) | (n > 0)

    for res in range(n_res):
        for sub in range(n_sub):
            rows = slice(sub * blk, (sub + 1) * blk)
            before = slice((sub - 1) * blk, sub * blk)
            valid = in_window & has_prev if sub == 0 else in_window
            scores, values = [], []
            for p in range(N_PAIRS):
                ls = slice(p * PAIR, (p + 1) * PAIR)
                qb = q_ref[res, rows, ls]
                zero = jnp.zeros_like(qb)
                k_prev = kp_ref[res, :, ls] if sub == 0 else kc_ref[res, before, ls]
                v_prev = vp_ref[res, :, ls] if sub == 0 else vc_ref[res, before, ls]
                kb = jnp.concatenate([k_prev, kc_ref[res, rows, ls]], axis=0)
                values.append(jnp.concatenate([v_prev, vc_ref[res, rows, ls]], axis=0))
                scores.append(_dot_nt(jnp.where(first, qb, zero), kb))
                scores.append(_dot_nt(jnp.where(first, zero, qb), kb))
            probs, denom, lses = [], [], []
            for s in scores:
                s = jnp.where(valid, s, NEG_INF)
                m = jnp.max(s, axis=-1, keepdims=True)
                e = jnp.exp(s - m)
                l = jnp.sum(e, axis=-1, keepdims=True)
                probs.append(e.astype(BF16))
                denom.append(l)
                lses.append(m + jnp.log(l))
            for p in range(N_PAIRS):
                ls = slice(p * PAIR, (p + 1) * PAIR)
                o0 = _dot(probs[2 * p], values[p]) / denom[2 * p]
                o1 = _dot(probs[2 * p + 1], values[p]) / denom[2 * p + 1]
                o_ref[res, rows, ls] = jnp.where(first, o0, o1).astype(BF16)
                lse_ref[res, rows, ls] = jnp.where(first, lses[2 * p], lses[2 * p + 1])


def _dil(qd, kd, vd):
    bsz, dilation, length, hw = qd.shape
    nb = length // ATTN_BLOCK
    n_sub = min(nb, DIL_STEP_BLOCKS)
    n_res = min(dilation, DIL_STEP_BLOCKS // n_sub)
    cur = pl.BlockSpec((None, n_res, n_sub * ATTN_BLOCK, hw), lambda b, r, n: (b, r, n, 0))
    prev = pl.BlockSpec((None, n_res, ATTN_BLOCK, hw), lambda b, r, n: (b, r, jnp.maximum(n * n_sub - 1, 0), 0))
    return pl.pallas_call(
        _dil_kernel,
        grid=(bsz, dilation // n_res, nb // n_sub),
        in_specs=[cur, cur, prev, cur, prev],
        out_specs=[cur, cur],
        out_shape=[jax.ShapeDtypeStruct(qd.shape, BF16), jax.ShapeDtypeStruct(qd.shape, F32)],
        compiler_params=_params(("arbitrary", "arbitrary", "arbitrary"), 32),
        name=f"dil{dilation}",
    )(qd, kd, kd, vd, vd)


def _mix_kernel(x_ref, yf_ref, o1_ref, o2_ref, o3_ref, l1_ref, l2_ref, l3_ref, gf_ref, gd_ref, g1_ref,
                wbf_ref, wbd_ref, wo_ref, lng_ref, lnb_ref, out_ref, nat_ref):
    tm = x_ref.shape[1]

    def natural(i, src_ref):
        dilation = src_ref.shape[1]
        for res in range(dilation):
            for g in range(N_PAIRS):
                nat_ref[i, g, pl.ds(res, tm // dilation, stride=dilation), :] = src_ref[
                    0, res, :, g * LANES:(g + 1) * LANES].astype(F32)
        return jnp.concatenate([nat_ref[i, g] for g in range(N_PAIRS)], axis=1)

    os_ = [natural(i, ref) for i, ref in enumerate((o1_ref, o2_ref, o3_ref))]
    l1, l2, l3 = [natural(3 + i, ref) for i, ref in enumerate((l1_ref, l2_ref, l3_ref))]
    mx = jnp.maximum(jnp.maximum(l1, l2), l3)
    e1, e2, e3 = jnp.exp(l1 - mx), jnp.exp(l2 - mx), jnp.exp(l3 - mx)
    yd = (e1 * os_[0] + e2 * os_[1] + e3 * os_[2]) / (e1 + e2 + e3)
    merged = (gf_ref[0].astype(F32) * _dot(yf_ref[0], wbf_ref[...])
              + gd_ref[0].astype(F32) * _dot(yd.astype(BF16), wbd_ref[...]))
    y = _dot(merged.astype(BF16), wo_ref[...])
    z = DN_ALPHA * x_ref[0] + (1.0 + g1_ref[0]) * y
    out_ref[0] = _layer_norm(z, lng_ref[...], lnb_ref[...])


def _mix(x, yf, os_, lses, gf, gd, mod3, wbf, wbd, wo, lng, lnb):
    bsz, seq, d = x.shape
    tm = ROW_TILE
    const = lambda shape: pl.BlockSpec(shape, lambda b, j: (0,) * len(shape))
    tok = lambda w: pl.BlockSpec((1, tm, w), lambda b, j: (b, j, 0))
    hw = HEADS_WIDTH
    res_major = [pl.BlockSpec((1, dil, tm // dil, hw), lambda b, j: (b, 0, j, 0)) for _, dil in DIL_PATTERNS]
    return pl.pallas_call(
        _mix_kernel,
        grid=(bsz, seq // tm),
        in_specs=[tok(d), tok(hw)] + res_major * 2 + [tok(d), tok(d),
                  pl.BlockSpec((1, 1, d), lambda b, j: (b, 0, 2)),
                  const(wbf.shape), const(wbd.shape), const(wo.shape), const(lng.shape), const(lnb.shape)],
        out_specs=tok(d),
        out_shape=jax.ShapeDtypeStruct((bsz, seq, d), F32),
        scratch_shapes=[pltpu.VMEM((6, N_PAIRS, tm, LANES), F32)],
        compiler_params=_params(("arbitrary", "arbitrary"), 48),
        name="mix",
    )(x, yf, *os_, *lses, gf, gd, mod3, wbf, wbd, wo, lng, lnb)


def _router_kernel(x_ref, sc_ref, sh_ref, wrh_ref, wrl_ref, bias_ref,
                   u_ref, up_ref, eidx_ref, rank_ref, gw_ref, cnt_ref):
    seq = x_ref.shape[1]
    tc = ROUTE_CHUNK
    e_iota = lax.broadcasted_iota(jnp.int32, (N_EXPERTS, tc), 0)
    g_iota = lax.broadcasted_iota(jnp.int32, (N_GROUPS, tc), 0)
    s_iota = lax.broadcasted_iota(jnp.int32, (GROUP_SIZE, tc), 0)
    r = lax.broadcasted_iota(jnp.int32, (tc, tc), 0)
    c = lax.broadcasted_iota(jnp.int32, (tc, tc), 1)
    upper = jnp.where(r <= c, 1.0, 0.0).astype(BF16)
    ones = jnp.ones((16, tc), BF16)
    carry = jnp.zeros((N_EXPERTS, 1), F32)
    cnt = jnp.zeros((16, N_EXPERTS), F32)

    for ch in range(seq // tc):
        rows = slice(ch * tc, (ch + 1) * tc)
        u = x_ref[0, rows, :] * (1.0 + sc_ref[0]) + sh_ref[0]
        u_ref[0, rows, :] = u.astype(BF16)
        up_ref[0, rows, :] = _pack_words(u)
        u_hi, u_lo = _split2(u)
        logits = _dot_nt(wrh_ref[...], u_hi) + _dot_nt(wrh_ref[...], u_lo) + _dot_nt(wrl_ref[...], u_hi)
        scores = _sigmoid(logits)
        sel = scores + bias_ref[...]

        gs_rows = []
        for g in range(N_GROUPS):
            blk = sel[g * GROUP_SIZE:(g + 1) * GROUP_SIZE]
            m1 = jnp.max(blk, axis=0, keepdims=True)
            i1 = jnp.min(jnp.where(blk == m1, s_iota, GROUP_SIZE), axis=0, keepdims=True)
            m2 = jnp.max(jnp.where(s_iota == i1, -jnp.inf, blk), axis=0, keepdims=True)
            gs_rows.append(m1 + m2)
        gs = jnp.concatenate(gs_rows, axis=0)
        beaten = jnp.zeros((N_GROUPS, tc), F32)
        for g in range(N_GROUPS):
            other = gs[g:g + 1]
            wins = (other > gs) | ((other == gs) & (g_iota > g))
            beaten = beaten + jnp.where(wins, 1.0, 0.0)
        keep = beaten < TOPK_GROUPS
        cand = jnp.concatenate(
            [jnp.where(keep[g:g + 1], sel[g * GROUP_SIZE:(g + 1) * GROUP_SIZE], NEG_INF) for g in range(N_GROUPS)],
            axis=0)
        chosen = jnp.zeros((N_EXPERTS, tc), F32)
        picks = []
        for _ in range(TOP_K):
            m = jnp.max(cand, axis=0, keepdims=True)
            idx = jnp.min(jnp.where(cand == m, e_iota, N_EXPERTS), axis=0, keepdims=True)
            hit = e_iota == idx
            chosen = jnp.where(hit, 1.0, chosen)
            cand = jnp.where(hit, -jnp.inf, cand)
            picks.append(idx)
        g_raw = jnp.where(chosen > 0.0, scores, 0.0)
        gate = g_raw / jnp.sum(g_raw, axis=0, keepdims=True) * ROUTED_SCALE

        chosen_b = chosen.astype(BF16)
        incl = _dot(chosen_b, upper) + carry
        carry = incl[:, tc - 1:tc]
        cnt = cnt + _dot_nt(ones, chosen_b)

        rank_rows, gate_rows = [], []
        for idx in picks:
            hit = e_iota == idx
            rank_rows.append(jnp.sum(jnp.where(hit, incl - 1.0, 0.0), axis=0, keepdims=True))
            gate_rows.append(jnp.sum(jnp.where(hit, gate, 0.0), axis=0, keepdims=True))
        eidx_ref[0, :, rows] = jnp.concatenate(picks, axis=0)
        rank_ref[0, :, rows] = jnp.concatenate(rank_rows, axis=0).astype(jnp.int32)
        gate_t = jnp.concatenate(gate_rows + [jnp.zeros((LANES - TOP_K, tc), F32)], axis=0).T
        gw_ref[0, rows, :] = gate_t[:, :TOP_K]

    cnt_ref[0] = cnt[0:1].astype(jnp.int32)


def _router(x1, mod3, wr_hi, wr_lo, bias):
    bsz, seq, d = x1.shape
    full = pl.BlockSpec((1, seq, d), lambda b: (b, 0, 0))
    ks = pl.BlockSpec((1, TOP_K, seq), lambda b: (b, 0, 0))
    return pl.pallas_call(
        _router_kernel,
        grid=(bsz,),
        in_specs=[full,
                  pl.BlockSpec((1, 1, d), lambda b: (b, 0, 4)),
                  pl.BlockSpec((1, 1, d), lambda b: (b, 0, 3)),
                  pl.BlockSpec(wr_hi.shape, lambda b: (0, 0)),
                  pl.BlockSpec(wr_lo.shape, lambda b: (0, 0)),
                  pl.BlockSpec(bias.shape, lambda b: (0, 0))],
        out_specs=[full, pl.BlockSpec((1, seq, PACKED), lambda b: (b, 0, 0)), ks, ks,
                   pl.BlockSpec((1, seq, TOP_K), lambda b: (b, 0, 0)),
                   pl.BlockSpec((1, 1, N_EXPERTS), lambda b: (b, 0, 0))],
        out_shape=[jax.ShapeDtypeStruct((bsz, seq, d), BF16),
                   jax.ShapeDtypeStruct((bsz, seq, PACKED), F32),
                   jax.ShapeDtypeStruct((bsz, TOP_K, seq), jnp.int32),
                   jax.ShapeDtypeStruct((bsz, TOP_K, seq), jnp.int32),
                   jax.ShapeDtypeStruct((bsz, seq, TOP_K), F32),
                   jax.ShapeDtypeStruct((bsz, 1, N_EXPERTS), jnp.int32)],
        compiler_params=_params(("arbitrary",), 48),
        name="router",
    )(x1, mod3, mod3, wr_hi, wr_lo, bias)


def _slot_kernel(base_ref, eidx_ref, rank_ref, slot_ref):
    b = pl.program_id(0)
    eidx = eidx_ref[0]
    start = jnp.zeros(eidx.shape, jnp.int32)
    for e in range(N_EXPERTS):
        start = jnp.where(eidx == e, base_ref[b * N_EXPERTS + e], start)
    slot_ref[0] = start + rank_ref[0]


def _slots(base, eidx, rank):
    bsz, _, seq = eidx.shape
    ks = pl.BlockSpec((1, TOP_K, seq), lambda b, c: (b, 0, 0))
    return pl.pallas_call(
        _slot_kernel,
        grid_spec=pltpu.PrefetchScalarGridSpec(num_scalar_prefetch=1, grid=(bsz,), in_specs=[ks, ks], out_specs=ks),
        out_shape=jax.ShapeDtypeStruct((bsz, TOP_K, seq), jnp.int32),
        compiler_params=_params(("arbitrary",), 16),
        name="slots",
    )(base.reshape(bsz * N_EXPERTS), eidx, rank)


def _sc_mesh():
    return plsc.VectorSubcoreMesh(core_axis_name="core", subcore_axis_name="subcore")


def _sc_worker():
    return lax.axis_index("subcore") * SC_CORES + lax.axis_index("core")


def _sc_dispatch(rows, slot, n_slots):
    n_tok, width = rows.shape
    seq = slot.shape[1]
    chunk = DISPATCH_CHUNK
    per_worker = n_tok // SC_WORKERS
    assert per_worker % chunk == 0 and seq % chunk == 0

    @functools.partial(
        pl.kernel, mesh=_sc_mesh(),
        out_type=jax.ShapeDtypeStruct((n_slots, width), rows.dtype),
        scratch_types=[pltpu.VMEM((TOP_K, chunk), jnp.int32), pltpu.VMEM((chunk, width), rows.dtype),
                       pltpu.SemaphoreType.DMA],
    )
    def dispatch(rows_hbm, slot_hbm, out_hbm, idx_v, rows_v, sem):
        base = _sc_worker() * per_worker

        @pl.loop(0, per_worker // chunk)
        def _(i):
            off = base + i * chunk
            b = off // seq
            n0 = off - b * seq
            pltpu.sync_copy(slot_hbm.at[pl.ds(b * TOP_K, TOP_K), pl.ds(n0, chunk)], idx_v)
            pltpu.sync_copy(rows_hbm.at[pl.ds(off, chunk)], rows_v)
            copies = [pltpu.async_copy(rows_v, out_hbm.at[idx_v.at[k]], sem) for k in range(TOP_K)]
            for cp in copies:
                cp.wait()

    return dispatch(rows, slot)


def _sc_gather(table, idx):
    n_out = idx.shape[0]
    width = table.shape[1]
    chunk = GATHER_CHUNK
    per_worker = n_out // SC_WORKERS
    steps = per_worker // chunk
    assert per_worker % chunk == 0 and steps % 2 == 0

    @functools.partial(
        pl.kernel, mesh=_sc_mesh(),
        out_type=jax.ShapeDtypeStruct((n_out, width), table.dtype),
        scratch_types=[pltpu.VMEM((2, chunk), jnp.int32), pltpu.VMEM((2, chunk, width), table.dtype),
                       pltpu.SemaphoreType.DMA, pltpu.SemaphoreType.DMA((2,))],
    )
    def gather(table_hbm, idx_hbm, out_hbm, idx_v, rows_v, gather_sem, out_sems):
        base = _sc_worker() * per_worker

        def write_out(buf, off):
            return pltpu.make_async_copy(rows_v.at[buf], out_hbm.at[pl.ds(off, chunk)], out_sems.at[buf])

        @pl.loop(0, steps, step=2)
        def _(i):
            for buf in range(2):
                off = base + (i + buf) * chunk

                @pl.when(i >= 2)
                def _():
                    write_out(buf, off - 2 * chunk).wait()

                pltpu.sync_copy(idx_hbm.at[pl.ds(off, chunk)], idx_v.at[buf])
                pltpu.async_copy(table_hbm.at[idx_v.at[buf]], rows_v.at[buf], gather_sem).wait()
                write_out(buf, off).start()

        for buf in range(2):
            write_out(buf, base + (steps - 2 + buf) * chunk).wait()

    return gather(table, idx)


def _gmm_kernel(blk_e_ref, used_ref, xs_ref, wgu_ref, wd_ref, out_ref):
    half = PACKED

    @pl.when(pl.program_id(0) < used_ref[0])
    def _():
        lo, hi = _unpack_words(xs_ref[...])
        hgu = _dot(lo.astype(BF16), wgu_ref[0, :half, :]) + _dot(hi.astype(BF16), wgu_ref[0, half:, :])
        hg = hgu[:, :EXPERT_FF]
        h = (hg * _sigmoid(hg) * hgu[:, EXPERT_FF:]).astype(BF16)
        out_ref[...] = _pack_words(_dot(h, wd_ref[0]))


def _gmm(xs, blk_e, used, wgu, wd):
    n_slots, width = xs.shape
    bm = EXPERT_BLOCK
    d = wgu.shape[1]
    live = lambda i, blk_e, used: (jnp.minimum(i, used[0] - 1), 0)
    return pl.pallas_call(
        _gmm_kernel,
        grid_spec=pltpu.PrefetchScalarGridSpec(
            num_scalar_prefetch=2,
            grid=(n_slots // bm,),
            in_specs=[pl.BlockSpec((bm, width), live),
                      pl.BlockSpec((1, d, 2 * EXPERT_FF), lambda i, blk_e, used: (blk_e[i], 0, 0)),
                      pl.BlockSpec((1, EXPERT_FF, d), lambda i, blk_e, used: (blk_e[i], 0, 0))],
            out_specs=pl.BlockSpec((bm, width), live)),
        out_shape=jax.ShapeDtypeStruct((n_slots, width), F32),
        compiler_params=_params(("arbitrary",), 32),
        name="gmm",
    )(blk_e, used, xs, wgu, wd)


def _ffn_out_kernel(x_ref, u_ref, yg_ref, gw_ref, g2_ref, wsgu_ref, wsd_ref, lng_ref, lnb_ref, out_ref):
    hgu = _dot(u_ref[0], wsgu_ref[...])
    hg = hgu[:, :SHARED_FF]
    h = (hg * _sigmoid(hg) * hgu[:, SHARED_FF:]).astype(BF16)
    shared = _dot(h, wsd_ref[...])
    gw = gw_ref[0]
    acc_lo = shared[:, :PACKED]
    acc_hi = shared[:, PACKED:]
    for k in range(TOP_K):
        lo, hi = _unpack_words(yg_ref[0, k])
        w = gw[:, k:k + 1]
        acc_lo = acc_lo + w * lo
        acc_hi = acc_hi + w * hi
    y = jnp.concatenate([acc_lo, acc_hi], axis=1)
    z = DN_ALPHA * x_ref[0] + (1.0 + g2_ref[0]) * y
    out_ref[0] = _layer_norm(z, lng_ref[...], lnb_ref[...])


def _ffn_out(x1, u2, yg, gw, mod3, wsgu, wsd, lng, lnb):
    bsz, seq, d = x1.shape
    tm = ROW_TILE
    const = lambda shape: pl.BlockSpec(shape, lambda b, j: (0,) * len(shape))
    tok = pl.BlockSpec((1, tm, d), lambda b, j: (b, j, 0))
    return pl.pallas_call(
        _ffn_out_kernel,
        grid=(bsz, seq // tm),
        in_specs=[tok, tok,
                  pl.BlockSpec((1, TOP_K, tm, PACKED), lambda b, j: (b, 0, j, 0)),
                  pl.BlockSpec((1, tm, TOP_K), lambda b, j: (b, j, 0)),
                  pl.BlockSpec((1, 1, d), lambda b, j: (b, 0, 5)),
                  const(wsgu.shape), const(wsd.shape), const(lng.shape), const(lnb.shape)],
        out_specs=tok,
        out_shape=jax.ShapeDtypeStruct((bsz, seq, d), F32),
        compiler_params=_params(("arbitrary", "arbitrary"), 48),
        name="ffn_out",
    )(x1, u2, yg, gw, mod3, wsgu, wsd, lng, lnb)


def _moe_layout(cnt, n_blocks):
    bm = EXPERT_BLOCK
    total = jnp.sum(cnt, axis=0)
    padded = (total + bm - 1) // bm * bm
    ends = jnp.cumsum(padded)
    base = (ends - padded)[None, :] + jnp.cumsum(cnt, axis=0) - cnt
    first_row = jnp.arange(n_blocks, dtype=ends.dtype) * bm
    blk_e = jnp.minimum(jnp.sum(ends[None, :] <= first_row[:, None], axis=1), N_EXPERTS - 1)
    used = (ends[-1:] // bm)
    return base.astype(jnp.int32), blk_e.astype(jnp.int32), used.astype(jnp.int32)


def _rope_tables(seq):
    half = HEAD_DIM // 2
    inv_freq = ROPE_THETA ** (-jnp.arange(half, dtype=F32) * 2.0 / HEAD_DIM)
    ang = jnp.arange(seq, dtype=F32)[:, None] * inv_freq[None, :]
    cos, sin = jnp.cos(ang), jnp.sin(ang)
    zero = jnp.zeros_like(sin)
    reps = LANES // HEAD_DIM
    cos_t = jnp.tile(jnp.concatenate([cos, cos], axis=1), (1, reps))
    sina_t = jnp.tile(jnp.concatenate([-sin, zero], axis=1), (1, reps))
    sinb_t = jnp.tile(jnp.concatenate([zero, sin], axis=1), (1, reps))
    return cos_t, sina_t, sinb_t


def _fox_bias_tables():
    n_split = 3
    aw = N_HEADS * LANES
    heads = jnp.arange(N_HEADS)
    selq = jnp.zeros((LANES, aw), F32)
    selk = jnp.zeros((LANES, aw), F32)
    oneq = jnp.zeros((1, aw), F32)
    onek = jnp.zeros((1, aw), F32)
    for j in range(n_split):
        selk = selk.at[j * N_HEADS + heads, heads * LANES + FOX_BIAS_LANE + j].set(-1.0)
        selq = selq.at[j * N_HEADS + heads, heads * LANES + FOX_BIAS_LANE + n_split + j].set(1.0)
        oneq = oneq.at[0, heads * LANES + FOX_BIAS_LANE + j].set(1.0)
        onek = onek.at[0, heads * LANES + FOX_BIAS_LANE + n_split + j].set(1.0)
    return selq.astype(BF16), selk.astype(BF16), oneq, onek


def kernel(x, c, w_ada, b_ada, w_in, b_forget, b_gate, w_br_fox, w_br_dil, w_o, ln_g, ln_b, w_router, router_bias,
           w_exp_gate, w_exp_up, w_exp_down, w_sh_gate, w_sh_up, w_sh_down):
    bsz, seq, d = x.shape
    depth = w_ada.shape[0]
    hw = HEADS_WIDTH
    n_tok = bsz * seq

    mod = _ada(c, w_ada, b_ada)

    o_f, o_d, o_g = 3 * hw, 3 * hw + N_HEADS, 6 * hw + N_HEADS
    wfox = w_in[:, :, :o_f].astype(BF16)
    wfg = jnp.pad(w_in[:, :, o_f:o_d], ((0, 0), (0, 0), (0, LANES - N_HEADS))).astype(BF16)
    wdil = w_in[:, :, o_d:o_g].astype(BF16)
    wgate = w_in[:, :, o_g:].astype(BF16)
    bfg = jnp.pad(b_forget, ((0, 0), (0, LANES - N_HEADS))).reshape(depth, 1, LANES)
    bgate = b_gate.reshape(depth, 1, 2 * d)
    wr_t = jnp.swapaxes(w_router, 1, 2)
    wr_hi = wr_t.astype(BF16)
    wr_lo = (wr_t - wr_hi.astype(F32)).astype(BF16)
    rbias = router_bias.reshape(depth, N_EXPERTS, 1)
    wgu = jnp.concatenate([w_exp_gate, w_exp_up], axis=-1).astype(BF16)
    wd = w_exp_down.astype(BF16)
    wsgu = jnp.concatenate([w_sh_gate, w_sh_up], axis=-1).astype(BF16)
    wsd = w_sh_down.astype(BF16)
    cos_t, sina_t, sinb_t = _rope_tables(seq)
    selq, selk, oneq, onek = _fox_bias_tables()

    layer_params = (mod, wfox, wdil, wgate, wfg, bfg, bgate, w_br_fox.astype(BF16), w_br_dil.astype(BF16),
                    w_o.astype(BF16), ln_g, ln_b, wr_hi, wr_lo, rbias, wgu, wd, wsgu, wsd)

    n_streams = 2 if n_tok % (2 * SC_WORKERS * DISPATCH_CHUNK) == 0 and bsz % 2 == 0 else 1
    sb = bsz // n_streams
    s_tok = sb * seq
    n_blocks = s_tok * TOP_K // EXPERT_BLOCK + N_EXPERTS

    def stream_layer(xc, mod3, prm):
        (_, wfox_l, wdil_l, wgate_l, wfg_l, bfg_l, bgate_l, wbf_l, wbd_l, wo_l, lng_l, lnb_l,
         wrh_l, wrl_l, rb_l, wgu_l, wd_l, wsgu_l, wsd_l) = prm
        qa, ka, vt, *rest = _inproj(
            xc, mod3, wfox_l, wdil_l, wgate_l, wfg_l, bfg_l, bgate_l, cos_t, sina_t, sinb_t, selq, selk, oneq, onek)
        gf, gd = rest[9:]
        yf = _fox(qa, ka, vt)
        dil = [_dil(*rest[3 * i:3 * i + 3]) for i in range(len(DIL_PATTERNS))]
        x1 = _mix(xc, yf, [o for o, _ in dil], [l for _, l in dil], gf, gd, mod3, wbf_l, wbd_l, wo_l,
                  lng_l[0:1], lnb_l[0:1])
        u2, u2p, eidx, rank, gw, cnt = _router(x1, mod3, wrh_l, wrl_l, rb_l)
        base, blk_e, used = _moe_layout(cnt.reshape(sb, N_EXPERTS), n_blocks)
        slot = _slots(base, eidx, rank)
        xs = _sc_dispatch(u2p.reshape(s_tok, PACKED), slot.reshape(sb * TOP_K, seq), n_blocks * EXPERT_BLOCK)
        ys = _gmm(xs, blk_e, used, wgu_l, wd_l)
        yg = _sc_gather(ys, slot.reshape(s_tok * TOP_K)).reshape(sb, TOP_K, seq, PACKED)
        return _ffn_out(x1, u2, yg, gw, mod3, wsgu_l, wsd_l, lng_l[1:2], lnb_l[1:2])

    outs = [x[i * sb:(i + 1) * sb] for i in range(n_streams)]
    for l in range(depth):
        prm = tuple(p[l] for p in layer_params)
        mod3 = prm[0].reshape(bsz, 1, 6 * d)
        outs = [stream_layer(xc, mod3[i * sb:(i + 1) * sb], prm) for i, xc in enumerate(outs)]
    return outs[0] if n_streams == 1 else jnp.concatenate(outs, axis=0)
```

```python
import functools

import jax
import jax.numpy as jnp
from jax import lax
from jax.experimental import pallas as pl
from jax.experimental.pallas import tpu as pltpu
from jax.experimental.pallas import tpu_sc as plsc

D_MODEL = 1024
DEPTH = 4
HEAD_DIM = 64
N_HEADS = 8
HEADS_WIDTH = N_HEADS * HEAD_DIM
DIL_PATTERNS = ((128, 1), (512, 4), (2048, 16))
ATTN_BLOCK = 128
ROPE_THETA = 10000.0
N_EXPERTS = 64
TOP_K = 8
N_GROUPS = 8
GROUP_SIZE = N_EXPERTS // N_GROUPS
TOPK_GROUPS = 4
EXPERT_FF = 256
SHARED_FF = 256
ROUTED_SCALE = 2.5
DN_ALPHA = (2 * DEPTH) ** 0.25
LN_EPS = 1e-5
NEG_INF = -1e30
SCORE_SCALE = HEAD_DIM ** -0.5
LOG2_E = 1.4426950408889634
LN_2 = 0.6931471805599453

LANES = 128
PAIR = 2 * HEAD_DIM
N_PAIRS = HEADS_WIDTH // PAIR
LSE_LANES = LANES // N_HEADS

ROW_TILE = 512
FOX_Q = 256
FOX_K = 256
FOX_BIAS_LANE = HEAD_DIM
DIL_STEP_BLOCKS = 4
ROUTE_CHUNK = 512
EXPERT_BLOCK = 512
PACKED = D_MODEL // 2

SC_CORES = 2
SC_SUBCORES = 16
SC_WORKERS = SC_CORES * SC_SUBCORES
DISPATCH_CHUNK = 128
GATHER_CHUNK = 64

BF16 = jnp.bfloat16
F32 = jnp.float32


def _dot(a, b):
    return jnp.dot(a, b, preferred_element_type=F32)


def _dot_nt(a, b):
    return lax.dot_general(a, b, (((1,), (1,)), ((), ())), preferred_element_type=F32)


def _split2(x):
    hi = x.astype(BF16)
    lo = (x - hi.astype(F32)).astype(BF16)
    return hi, lo


def _split3(x):
    hi = x.astype(BF16)
    r = x - hi.astype(F32)
    mid = r.astype(BF16)
    lo = (r - mid.astype(F32)).astype(BF16)
    return hi, mid, lo


def _sigmoid(x):
    return 1.0 / (1.0 + jnp.exp(-x))


def _layer_norm(z, g, b):
    mu = jnp.mean(z, axis=-1, keepdims=True)
    zc = z - mu
    var = jnp.mean(zc * zc, axis=-1, keepdims=True)
    return zc * lax.rsqrt(var + LN_EPS) * g + b


def _pack_words(v):
    half = v.shape[1] // 2
    bits = pltpu.bitcast(v.astype(BF16).astype(F32), jnp.uint32)
    word = (bits[:, half:] & jnp.uint32(0xFFFF0000)) | (bits[:, :half] >> 16)
    return pltpu.bitcast(word, F32)


def _unpack_words(w):
    bits = pltpu.bitcast(w, jnp.uint32)
    return pltpu.bitcast(bits << 16, F32), pltpu.bitcast(bits & jnp.uint32(0xFFFF0000), F32)


def _params(sem, vmem_mb):
    return pltpu.CompilerParams(dimension_semantics=sem, vmem_limit_bytes=vmem_mb * 1024 * 1024)


def _ada_kernel(c_ref, w_ref, b_ref, o_ref):
    c = c_ref[...]
    cs = c * _sigmoid(c)
    c_hi, c_lo = _split2(cs)
    w_hi, w_lo = _split2(w_ref[0])
    o_ref[0] = _dot(c_hi, w_hi) + _dot(c_hi, w_lo) + _dot(c_lo, w_hi) + b_ref[0]


def _ada(c, w_ada, b_ada):
    depth, d, n = w_ada.shape
    bsz = c.shape[0]
    tn = 1536
    return pl.pallas_call(
        _ada_kernel,
        grid=(depth, n // tn),
        in_specs=[
            pl.BlockSpec((bsz, d), lambda l, j: (0, 0)),
            pl.BlockSpec((1, d, tn), lambda l, j: (l, 0, j)),
            pl.BlockSpec((1, 1, tn), lambda l, j: (l, 0, j)),
        ],
        out_specs=pl.BlockSpec((1, bsz, tn), lambda l, j: (l, 0, j)),
        out_shape=jax.ShapeDtypeStruct((depth, bsz, n), F32),
        compiler_params=_params(("arbitrary", "arbitrary"), 40),
        name="ada",
    )(c, w_ada, b_ada.reshape(depth, 1, n))


def _inproj_kernel(x_ref, sc_ref, sh_ref, wfox_ref, wdil_ref, wgate_ref, wfg_ref, bfg_ref, bgate_ref,
                   cos_ref, sina_ref, sinb_ref, selq_ref, selk_ref, oneq_ref, onek_ref,
                   qa_ref, ka_ref, vt_ref, *rest):
    dil_refs = rest[:9]
    gf_ref, gd_ref, carry_ref, perm_ref = rest[9:]
    j = pl.program_id(1)
    tm = x_ref.shape[1]
    u = (x_ref[0] * (1.0 + sc_ref[0]) + sh_ref[0]).astype(BF16)

    ff = _dot(u, wfg_ref[...]) + bfg_ref[...]
    lf = jnp.minimum(ff, 0.0) - jnp.log(1.0 + jnp.exp(-jnp.abs(ff)))
    r = lax.broadcasted_iota(jnp.int32, (tm, tm), 0)
    c = lax.broadcasted_iota(jnp.int32, (tm, tm), 1)
    tri = jnp.where(r >= c, 1.0, 0.0).astype(BF16)
    hi, mid, lo = _split3(lf)
    csum = _dot(tri, hi) + _dot(tri, mid) + _dot(tri, lo)

    @pl.when(j == 0)
    def _():
        carry_ref[...] = jnp.zeros_like(carry_ref)

    cum = csum + carry_ref[...]
    carry_ref[...] = cum[tm - 1:tm, :]

    lane = lax.broadcasted_iota(jnp.int32, (1, LANES), 1)
    c_hi, c_mid, c_lo = (t.astype(F32) for t in _split3(cum * LOG2_E))
    parts = jnp.where(lane < N_HEADS, c_hi,
                      jnp.where(lane < 2 * N_HEADS, pltpu.roll(c_mid, N_HEADS, 1),
                                jnp.where(lane < 3 * N_HEADS, pltpu.roll(c_lo, 2 * N_HEADS, 1), 0.0))).astype(BF16)
    bias_q = _dot(parts, selq_ref[...]) + oneq_ref[...]
    bias_k = _dot(parts, selk_ref[...]) + onek_ref[...]
    pf = _dot(u, wfox_ref[...])
    data = lane < HEAD_DIM
    for dst, off, bias, scale in ((qa_ref, 0, bias_q, SCORE_SCALE * LOG2_E), (ka_ref, HEADS_WIDTH, bias_k, 1.0)):
        for p in range(N_PAIRS):
            pair = pf[:, off + p * PAIR: off + (p + 1) * PAIR] * scale
            for hh, src in ((0, pair), (1, pltpu.roll(pair, HEAD_DIM, 1))):
                t = (2 * p + hh) * LANES
                dst[0, :, t:t + LANES] = jnp.where(data, src, bias[:, t:t + LANES]).astype(BF16)
    vt = pf[:, 2 * HEADS_WIDTH:].T
    for cb in range(tm // FOX_K):
        vt_ref[0, cb] = vt[:, cb * FOX_K:(cb + 1) * FOX_K].astype(BF16)

    pd = _dot(u, wdil_ref[...])
    rows = pl.ds(pl.multiple_of(j * tm, tm), tm)
    cos = cos_ref[rows, :]
    sina = sina_ref[rows, :]
    sinb = sinb_ref[rows, :]
    for i, off in ((0, 0), (1, HEADS_WIDTH)):
        for g in range(N_PAIRS):
            xg = pd[:, off + g * LANES: off + (g + 1) * LANES]
            ahead = pltpu.roll(xg, LANES - HEAD_DIM // 2, 1)
            behind = pltpu.roll(xg, HEAD_DIM // 2, 1)
            rot = xg * cos + ahead * sina + behind * sinb
            perm_ref[i, g] = rot * (SCORE_SCALE * LOG2_E) if i == 0 else rot
    for g in range(N_PAIRS):
        perm_ref[2, g] = pd[:, 2 * HEADS_WIDTH + g * LANES:2 * HEADS_WIDTH + (g + 1) * LANES]
    for i in range(3):
        for (_, dilation), dst in zip(DIL_PATTERNS, dil_refs[i::3]):
            for res in range(dilation):
                for g in range(N_PAIRS):
                    dst[0, res, :, g * LANES:(g + 1) * LANES] = perm_ref[
                        i, g, pl.ds(res, tm // dilation, stride=dilation), :].astype(BF16)

    pg = _sigmoid(_dot(u, wgate_ref[...]) + bgate_ref[...])
    gf_ref[0] = pg[:, :D_MODEL].astype(BF16)
    gd_ref[0] = pg[:, D_MODEL:].astype(BF16)


def _inproj(x, mod3, wfox, wdil, wgate, wfg, bfg, bgate, cos_t, sina_t, sinb_t, selq, selk, oneq, onek):
    bsz, seq, d = x.shape
    tm = ROW_TILE
    const = lambda shape: pl.BlockSpec(shape, lambda b, j: (0,) * len(shape))
    tok = lambda w: pl.BlockSpec((1, tm, w), lambda b, j: (b, j, 0))
    hw = HEADS_WIDTH
    aw = N_HEADS * LANES
    dil_shapes, dil_specs = [], []
    for _, dil in DIL_PATTERNS:
        dil_shapes += [jax.ShapeDtypeStruct((bsz, dil, seq // dil, hw), BF16)] * 3
        dil_specs += [pl.BlockSpec((1, dil, tm // dil, hw), lambda b, j: (b, 0, j, 0))] * 3
    outs = ([jax.ShapeDtypeStruct((bsz, seq, aw), BF16)] * 2
            + [jax.ShapeDtypeStruct((bsz, seq // FOX_K, hw, FOX_K), BF16)]
            + dil_shapes + [jax.ShapeDtypeStruct((bsz, seq, d), BF16)] * 2)
    return pl.pallas_call(
        _inproj_kernel,
        grid=(bsz, seq // tm),
        in_specs=[
            tok(d),
            pl.BlockSpec((1, 1, d), lambda b, j: (b, 0, 1)),
            pl.BlockSpec((1, 1, d), lambda b, j: (b, 0, 0)),
            const(wfox.shape), const(wdil.shape), const(wgate.shape), const(wfg.shape),
            const(bfg.shape), const(bgate.shape),
            const(cos_t.shape), const(sina_t.shape), const(sinb_t.shape),
            const(selq.shape), const(selk.shape), const(oneq.shape), const(onek.shape),
        ],
        out_specs=[tok(aw)] * 2 + [pl.BlockSpec((1, tm // FOX_K, hw, FOX_K), lambda b, j: (b, j, 0, 0))]
                  + dil_specs + [tok(d)] * 2,
        out_shape=outs,
        scratch_shapes=[pltpu.VMEM((1, LANES), F32), pltpu.VMEM((3, N_PAIRS, tm, LANES), F32)],
        compiler_params=_params(("arbitrary", "arbitrary"), 56),
        name="inproj",
    )(x, mod3, mod3, wfox, wdil, wgate, wfg, bfg, bgate, cos_t, sina_t, sinb_t, selq, selk, oneq, onek)


def _fox_kernel(qa_ref, ka_ref, vt_ref, o_ref, m_ref, l_ref, acc_ref):
    seq = qa_ref.shape[1]
    tq, tk = FOX_Q, FOX_K
    k_pos = lax.broadcasted_iota(jnp.int32, (tk, tq), 0)
    q_pos = lax.broadcasted_iota(jnp.int32, (tk, tq), 1)
    ones = jnp.ones((16, tk), BF16)

    def q_body(qi, _):
        q0 = pl.multiple_of(qi * tq, tq)
        m_ref[...] = jnp.full(m_ref.shape, NEG_INF, F32)
        l_ref[...] = jnp.zeros(l_ref.shape, F32)
        acc_ref[...] = jnp.zeros(acc_ref.shape, F32)

        def kv_step(kb, k0, visible):
            scores = []
            for h in range(N_HEADS):
                ls = slice(h * LANES, (h + 1) * LANES)
                scores.append(_dot_nt(ka_ref[0, pl.ds(k0, tk), ls], qa_ref[0, pl.ds(q0, tq), ls]))
            m_all = m_ref[...]
            l_all = l_ref[...]
            probs, decay, m_rows, l_rows = [], [], [], []
            for h in range(N_HEADS):
                s = scores[h] if visible is None else jnp.where(visible, scores[h], NEG_INF)
                m_old = m_all[h:h + 1, :]
                m_new = jnp.maximum(m_old, jnp.max(s, axis=0, keepdims=True))
                m_rows.append(m_new)
                decay.append(jnp.exp2(m_old - m_new))
                probs.append(jnp.exp2((s - m_new).astype(BF16)))
            m_ref[...] = jnp.concatenate(m_rows, axis=0)
            for h in range(N_HEADS):
                vh = vt_ref[0, kb, h * HEAD_DIM:(h + 1) * HEAD_DIM, :]
                acc_ref[h] = decay[h] * acc_ref[h] + _dot(vh, probs[h])
                l_rows.append(decay[h] * l_all[h:h + 1, :] + _dot(ones, probs[h])[0:1])
            l_ref[...] = jnp.concatenate(l_rows, axis=0)

        def full_block(jb, carry):
            kv_step(jb, pl.multiple_of(jb * tk, tk), None)
            return carry

        lax.fori_loop(0, qi * (tq // tk), full_block, 0)
        for dblk in range(tq // tk):
            kb = qi * (tq // tk) + dblk
            kv_step(kb, pl.multiple_of(kb * tk, tk), k_pos + dblk * tk <= q_pos)

        for p in range(N_PAIRS):
            pair = jnp.concatenate([acc_ref[2 * p + hh] / l_ref[2 * p + hh:2 * p + hh + 1, :] for hh in range(2)],
                                   axis=0)
            o_ref[0, pl.ds(q0, tq), p * PAIR:(p + 1) * PAIR] = pair.T.astype(BF16)
        return 0

    lax.fori_loop(0, seq // tq, q_body, 0)


def _fox(qa, ka, vt):
    bsz, seq, aw = qa.shape
    hw = HEADS_WIDTH
    full = pl.BlockSpec((1, seq, aw), lambda b: (b, 0, 0))
    return pl.pallas_call(
        _fox_kernel,
        grid=(bsz,),
        in_specs=[full, full, pl.BlockSpec((1, seq // FOX_K, hw, FOX_K), lambda b: (b, 0, 0, 0))],
        out_specs=pl.BlockSpec((1, seq, hw), lambda b: (b, 0, 0)),
        out_shape=jax.ShapeDtypeStruct((bsz, seq, hw), BF16),
        scratch_shapes=[pltpu.VMEM((N_HEADS, FOX_Q), F32), pltpu.VMEM((N_HEADS, FOX_Q), F32),
                        pltpu.VMEM((N_HEADS, HEAD_DIM, FOX_Q), F32)],
        compiler_params=_params(("arbitrary",), 48),
        name="fox",
    )(qa, ka, vt)


def _dil_kernel(q_ref, kc_ref, kp_ref, vc_ref, vp_ref, o_ref, lse_ref):
    n = pl.program_id(2)
    blk = ATTN_BLOCK
    n_res = q_ref.shape[0]
    n_sub = q_ref.shape[1] // blk
    lane = lax.broadcasted_iota(jnp.int32, (1, PAIR), 1)
    first = lane < HEAD_DIM
    i = lax.broadcasted_iota(jnp.int32, (blk, 2 * blk), 0)
    c = lax.broadcasted_iota(jnp.int32, (blk, 2 * blk), 1)
    in_window = (c >= i) & (c <= i + blk)
    has_prev = (c >= blk) | (n > 0)
    lane_head = lax.broadcasted_iota(jnp.int32, (1, LANES), 1) // LSE_LANES

    for res in range(n_res):
        for sub in range(n_sub):
            rows = slice(sub * blk, (sub + 1) * blk)
            before = slice((sub - 1) * blk, sub * blk)
            valid = in_window & has_prev if sub == 0 else in_window
            scores, values = [], []
            for p in range(N_PAIRS):
                ls = slice(p * PAIR, (p + 1) * PAIR)
                qb = q_ref[res, rows, ls]
                zero = jnp.zeros_like(qb)
                k_prev = kp_ref[res, :, ls] if sub == 0 else kc_ref[res, before, ls]
                v_prev = vp_ref[res, :, ls] if sub == 0 else vc_ref[res, before, ls]
                kb = jnp.concatenate([k_prev, kc_ref[res, rows, ls]], axis=0)
                values.append(jnp.concatenate([v_prev, vc_ref[res, rows, ls]], axis=0))
                scores.append(_dot_nt(jnp.where(first, qb, zero), kb))
                scores.append(_dot_nt(jnp.where(first, zero, qb), kb))
            probs, denom = [], []
            lse_tile = jnp.zeros((blk, LANES), F32)
            for h, s in enumerate(scores):
                s = jnp.where(valid, s, NEG_INF)
                m = jnp.max(s, axis=-1, keepdims=True)
                e = jnp.exp2(s - m)
                l = jnp.sum(e, axis=-1, keepdims=True)
                probs.append(e.astype(BF16))
                denom.append(l)
                lse_tile = jnp.where(lane_head == h, m * LN_2 + jnp.log(l), lse_tile)
            for p in range(N_PAIRS):
                ls = slice(p * PAIR, (p + 1) * PAIR)
                o0 = _dot(probs[2 * p], values[p]) / denom[2 * p]
                o1 = _dot(probs[2 * p + 1], values[p]) / denom[2 * p + 1]
                o_ref[res, rows, ls] = jnp.where(first, o0, o1).astype(BF16)
            lse_ref[res, rows, :] = lse_tile


def _dil(qd, kd, vd):
    bsz, dilation, length, hw = qd.shape
    nb = length // ATTN_BLOCK
    n_sub = min(nb, DIL_STEP_BLOCKS)
    n_res = min(dilation, DIL_STEP_BLOCKS // n_sub)
    cur = pl.BlockSpec((None, n_res, n_sub * ATTN_BLOCK, hw), lambda b, r, n: (b, r, n, 0))
    cur_lse = pl.BlockSpec((None, n_res, n_sub * ATTN_BLOCK, LANES), lambda b, r, n: (b, r, n, 0))
    prev = pl.BlockSpec((None, n_res, ATTN_BLOCK, hw), lambda b, r, n: (b, r, jnp.maximum(n * n_sub - 1, 0), 0))
    return pl.pallas_call(
        _dil_kernel,
        grid=(bsz, dilation // n_res, nb // n_sub),
        in_specs=[cur, cur, prev, cur, prev],
        out_specs=[cur, cur_lse],
        out_shape=[jax.ShapeDtypeStruct(qd.shape, BF16),
                   jax.ShapeDtypeStruct((bsz, dilation, length, LANES), F32)],
        compiler_params=_params(("arbitrary", "arbitrary", "arbitrary"), 32),
        name=f"dil{dilation}",
    )(qd, kd, kd, vd, vd)


def _mix_kernel(x_ref, yf_ref, o1_ref, o2_ref, o3_ref, l1_ref, l2_ref, l3_ref, gf_ref, gd_ref, g1_ref,
                wbf_ref, wbd_ref, wo_ref, lng_ref, lnb_ref, spread_ref, out_ref, nat_ref):
    tm = x_ref.shape[1]

    def natural(i, src_ref):
        dilation = src_ref.shape[1]
        groups = src_ref.shape[3] // LANES
        for res in range(dilation):
            for g in range(groups):
                nat_ref[i, g, pl.ds(res, tm // dilation, stride=dilation), :] = src_ref[
                    0, res, :, g * LANES:(g + 1) * LANES].astype(F32)
        return jnp.concatenate([nat_ref[i, g] for g in range(groups)], axis=1)

    os_ = [natural(i, ref) for i, ref in enumerate((o1_ref, o2_ref, o3_ref))]
    l1, l2, l3 = [natural(3 + i, ref) for i, ref in enumerate((l1_ref, l2_ref, l3_ref))]
    mx = jnp.maximum(jnp.maximum(l1, l2), l3)
    e1, e2, e3 = jnp.exp(l1 - mx), jnp.exp(l2 - mx), jnp.exp(l3 - mx)
    tot = e1 + e2 + e3
    yd = sum(_dot((e / tot).astype(BF16), spread_ref[...]) * o for e, o in zip((e1, e2, e3), os_))
    merged = (gf_ref[0].astype(F32) * _dot(yf_ref[0], wbf_ref[...])
              + gd_ref[0].astype(F32) * _dot(yd.astype(BF16), wbd_ref[...]))
    y = _dot(merged.astype(BF16), wo_ref[...])
    z = DN_ALPHA * x_ref[0] + (1.0 + g1_ref[0]) * y
    out_ref[0] = _layer_norm(z, lng_ref[...], lnb_ref[...])


def _mix(x, yf, os_, lses, gf, gd, mod3, wbf, wbd, wo, lng, lnb):
    bsz, seq, d = x.shape
    tm = ROW_TILE
    const = lambda shape: pl.BlockSpec(shape, lambda b, j: (0,) * len(shape))
    tok = lambda w: pl.BlockSpec((1, tm, w), lambda b, j: (b, j, 0))
    hw = HEADS_WIDTH
    res_major = lambda w: [pl.BlockSpec((1, dil, tm // dil, w), lambda b, j: (b, 0, j, 0)) for _, dil in DIL_PATTERNS]
    heads = jnp.arange(N_HEADS)
    spread = jnp.zeros((LANES, hw), F32).at[
        (heads * LSE_LANES)[:, None], heads[:, None] * HEAD_DIM + jnp.arange(HEAD_DIM)[None, :]].set(1.0).astype(BF16)
    return pl.pallas_call(
        _mix_kernel,
        grid=(bsz, seq // tm),
        in_specs=[tok(d), tok(hw)] + res_major(hw) + res_major(LANES) + [tok(d), tok(d),
                  pl.BlockSpec((1, 1, d), lambda b, j: (b, 0, 2)),
                  const(wbf.shape), const(wbd.shape), const(wo.shape), const(lng.shape), const(lnb.shape),
                  const(spread.shape)],
        out_specs=tok(d),
        out_shape=jax.ShapeDtypeStruct((bsz, seq, d), F32),
        scratch_shapes=[pltpu.VMEM((6, N_PAIRS, tm, LANES), F32)],
        compiler_params=_params(("arbitrary", "arbitrary"), 48),
        name="mix",
    )(x, yf, *os_, *lses, gf, gd, mod3, wbf, wbd, wo, lng, lnb, spread)


def _router_kernel(x_ref, sc_ref, sh_ref, wrh_ref, wrl_ref, bias_ref,
                   u_ref, up_ref, eidx_ref, rank_ref, gw_ref, cnt_ref):
    seq = x_ref.shape[1]
    tc = ROUTE_CHUNK
    e_iota = lax.broadcasted_iota(jnp.int32, (N_EXPERTS, tc), 0)
    g_iota = lax.broadcasted_iota(jnp.int32, (N_GROUPS, tc), 0)
    s_iota = lax.broadcasted_iota(jnp.int32, (GROUP_SIZE, tc), 0)
    r = lax.broadcasted_iota(jnp.int32, (tc, tc), 0)
    c = lax.broadcasted_iota(jnp.int32, (tc, tc), 1)
    upper = jnp.where(r <= c, 1.0, 0.0).astype(BF16)
    ones = jnp.ones((16, tc), BF16)
    carry = jnp.zeros((N_EXPERTS, 1), F32)
    cnt = jnp.zeros((16, N_EXPERTS), F32)

    for ch in range(seq // tc):
        rows = slice(ch * tc, (ch + 1) * tc)
        u = x_ref[0, rows, :] * (1.0 + sc_ref[0]) + sh_ref[0]
        u_ref[0, rows, :] = u.astype(BF16)
        up_ref[0, rows, :] = _pack_words(u)
        u_hi, u_lo = _split2(u)
        logits = _dot_nt(wrh_ref[...], u_hi) + _dot_nt(wrh_ref[...], u_lo) + _dot_nt(wrl_ref[...], u_hi)
        scores = _sigmoid(logits)
        sel = scores + bias_ref[...]

        gs_rows = []
        for g in range(N_GROUPS):
            blk = sel[g * GROUP_SIZE:(g + 1) * GROUP_SIZE]
            m1 = jnp.max(blk, axis=0, keepdims=True)
            i1 = jnp.min(jnp.where(blk == m1, s_iota, GROUP_SIZE), axis=0, keepdims=True)
            m2 = jnp.max(jnp.where(s_iota == i1, -jnp.inf, blk), axis=0, keepdims=True)
            gs_rows.append(m1 + m2)
        gs = jnp.concatenate(gs_rows, axis=0)
        beaten = jnp.zeros((N_GROUPS, tc), F32)
        for g in range(N_GROUPS):
            other = gs[g:g + 1]
            wins = (other > gs) | ((other == gs) & (g_iota > g))
            beaten = beaten + jnp.where(wins, 1.0, 0.0)
        keep = beaten < TOPK_GROUPS
        cand = jnp.concatenate(
            [jnp.where(keep[g:g + 1], sel[g * GROUP_SIZE:(g + 1) * GROUP_SIZE], NEG_INF) for g in range(N_GROUPS)],
            axis=0)
        chosen = jnp.zeros((N_EXPERTS, tc), F32)
        picks = []
        for _ in range(TOP_K):
            m = jnp.max(cand, axis=0, keepdims=True)
            idx = jnp.min(jnp.where(cand == m, e_iota, N_EXPERTS), axis=0, keepdims=True)
            hit = e_iota == idx
            chosen = jnp.where(hit, 1.0, chosen)
            cand = jnp.where(hit, -jnp.inf, cand)
            picks.append(idx)
        g_raw = jnp.where(chosen > 0.0, scores, 0.0)
        gate = g_raw / jnp.sum(g_raw, axis=0, keepdims=True) * ROUTED_SCALE

        chosen_b = chosen.astype(BF16)
        incl = _dot(chosen_b, upper) + carry
        carry = incl[:, tc - 1:tc]
        cnt = cnt + _dot_nt(ones, chosen_b)

        rank_rows, gate_rows = [], []
        for idx in picks:
            hit = e_iota == idx
            rank_rows.append(jnp.sum(jnp.where(hit, incl - 1.0, 0.0), axis=0, keepdims=True))
            gate_rows.append(jnp.sum(jnp.where(hit, gate, 0.0), axis=0, keepdims=True))
        eidx_ref[0, :, rows] = jnp.concatenate(picks, axis=0)
        rank_ref[0, :, rows] = jnp.concatenate(rank_rows, axis=0).astype(jnp.int32)
        gate_t = jnp.concatenate(gate_rows + [jnp.zeros((LANES - TOP_K, tc), F32)], axis=0).T
        gw_ref[0, rows, :] = gate_t[:, :TOP_K]

    cnt_ref[0] = cnt[0:1].astype(jnp.int32)


def _router(x1, mod3, wr_hi, wr_lo, bias):
    bsz, seq, d = x1.shape
    full = pl.BlockSpec((1, seq, d), lambda b: (b, 0, 0))
    ks = pl.BlockSpec((1, TOP_K, seq), lambda b: (b, 0, 0))
    return pl.pallas_call(
        _router_kernel,
        grid=(bsz,),
        in_specs=[full,
                  pl.BlockSpec((1, 1, d), lambda b: (b, 0, 4)),
                  pl.BlockSpec((1, 1, d), lambda b: (b, 0, 3)),
                  pl.BlockSpec(wr_hi.shape, lambda b: (0, 0)),
                  pl.BlockSpec(wr_lo.shape, lambda b: (0, 0)),
                  pl.BlockSpec(bias.shape, lambda b: (0, 0))],
        out_specs=[full, pl.BlockSpec((1, seq, PACKED), lambda b: (b, 0, 0)), ks, ks,
                   pl.BlockSpec((1, seq, TOP_K), lambda b: (b, 0, 0)),
                   pl.BlockSpec((1, 1, N_EXPERTS), lambda b: (b, 0, 0))],
        out_shape=[jax.ShapeDtypeStruct((bsz, seq, d), BF16),
                   jax.ShapeDtypeStruct((bsz, seq, PACKED), F32),
                   jax.ShapeDtypeStruct((bsz, TOP_K, seq), jnp.int32),
                   jax.ShapeDtypeStruct((bsz, TOP_K, seq), jnp.int32),
                   jax.ShapeDtypeStruct((bsz, seq, TOP_K), F32),
                   jax.ShapeDtypeStruct((bsz, 1, N_EXPERTS), jnp.int32)],
        compiler_params=_params(("arbitrary",), 48),
        name="router",
    )(x1, mod3, mod3, wr_hi, wr_lo, bias)


def _slot_kernel(base_ref, eidx_ref, rank_ref, slot_ref):
    b = pl.program_id(0)
    eidx = eidx_ref[0]
    start = jnp.zeros(eidx.shape, jnp.int32)
    for e in range(N_EXPERTS):
        start = jnp.where(eidx == e, base_ref[b * N_EXPERTS + e], start)
    slot_ref[0] = start + rank_ref[0]


def _slots(base, eidx, rank):
    bsz, _, seq = eidx.shape
    ks = pl.BlockSpec((1, TOP_K, seq), lambda b, c: (b, 0, 0))
    return pl.pallas_call(
        _slot_kernel,
        grid_spec=pltpu.PrefetchScalarGridSpec(num_scalar_prefetch=1, grid=(bsz,), in_specs=[ks, ks], out_specs=ks),
        out_shape=jax.ShapeDtypeStruct((bsz, TOP_K, seq), jnp.int32),
        compiler_params=_params(("arbitrary",), 16),
        name="slots",
    )(base.reshape(bsz * N_EXPERTS), eidx, rank)


def _sc_mesh():
    return plsc.VectorSubcoreMesh(core_axis_name="core", subcore_axis_name="subcore")


def _sc_worker():
    return lax.axis_index("subcore") * SC_CORES + lax.axis_index("core")


def _sc_dispatch(rows, slot, n_slots):
    n_tok, width = rows.shape
    seq = slot.shape[1]
    chunk = DISPATCH_CHUNK
    per_worker = n_tok // SC_WORKERS
    assert per_worker % chunk == 0 and seq % chunk == 0

    @functools.partial(
        pl.kernel, mesh=_sc_mesh(),
        out_type=jax.ShapeDtypeStruct((n_slots, width), rows.dtype),
        scratch_types=[pltpu.VMEM((TOP_K, chunk), jnp.int32), pltpu.VMEM((chunk, width), rows.dtype),
                       pltpu.SemaphoreType.DMA],
    )
    def dispatch(rows_hbm, slot_hbm, out_hbm, idx_v, rows_v, sem):
        base = _sc_worker() * per_worker

        @pl.loop(0, per_worker // chunk)
        def _(i):
            off = base + i * chunk
            b = off // seq
            n0 = off - b * seq
            pltpu.sync_copy(slot_hbm.at[pl.ds(b * TOP_K, TOP_K), pl.ds(n0, chunk)], idx_v)
            pltpu.sync_copy(rows_hbm.at[pl.ds(off, chunk)], rows_v)
            copies = [pltpu.async_copy(rows_v, out_hbm.at[idx_v.at[k]], sem) for k in range(TOP_K)]
            for cp in copies:
                cp.wait()

    return dispatch(rows, slot)


def _sc_gather(table, idx):
    n_out = idx.shape[0]
    width = table.shape[1]
    chunk = GATHER_CHUNK
    per_worker = n_out // SC_WORKERS
    steps = per_worker // chunk
    assert per_worker % chunk == 0 and steps % 2 == 0

    @functools.partial(
        pl.kernel, mesh=_sc_mesh(),
        out_type=jax.ShapeDtypeStruct((n_out, width), table.dtype),
        scratch_types=[pltpu.VMEM((2, chunk), jnp.int32), pltpu.VMEM((2, chunk, width), table.dtype),
                       pltpu.SemaphoreType.DMA, pltpu.SemaphoreType.DMA((2,))],
    )
    def gather(table_hbm, idx_hbm, out_hbm, idx_v, rows_v, gather_sem, out_sems):
        base = _sc_worker() * per_worker

        def write_out(buf, off):
            return pltpu.make_async_copy(rows_v.at[buf], out_hbm.at[pl.ds(off, chunk)], out_sems.at[buf])

        @pl.loop(0, steps, step=2)
        def _(i):
            for buf in range(2):
                off = base + (i + buf) * chunk

                @pl.when(i >= 2)
                def _():
                    write_out(buf, off - 2 * chunk).wait()

                pltpu.sync_copy(idx_hbm.at[pl.ds(off, chunk)], idx_v.at[buf])
                pltpu.async_copy(table_hbm.at[idx_v.at[buf]], rows_v.at[buf], gather_sem).wait()
                write_out(buf, off).start()

        for buf in range(2):
            write_out(buf, base + (steps - 2 + buf) * chunk).wait()

    return gather(table, idx)


def _gmm_kernel(blk_e_ref, used_ref, xs_ref, wgu_ref, wd_ref, out_ref):
    half = PACKED

    @pl.when(pl.program_id(0) < used_ref[0])
    def _():
        lo, hi = _unpack_words(xs_ref[...])
        hgu = _dot(lo.astype(BF16), wgu_ref[0, :half, :]) + _dot(hi.astype(BF16), wgu_ref[0, half:, :])
        hg = hgu[:, :EXPERT_FF]
        h = (hg * _sigmoid(hg) * hgu[:, EXPERT_FF:]).astype(BF16)
        out_ref[...] = _pack_words(_dot(h, wd_ref[0]))


def _gmm(xs, blk_e, used, wgu, wd):
    n_slots, width = xs.shape
    bm = EXPERT_BLOCK
    d = wgu.shape[1]
    live = lambda i, blk_e, used: (jnp.minimum(i, used[0] - 1), 0)
    return pl.pallas_call(
        _gmm_kernel,
        grid_spec=pltpu.PrefetchScalarGridSpec(
            num_scalar_prefetch=2,
            grid=(n_slots // bm,),
            in_specs=[pl.BlockSpec((bm, width), live),
                      pl.BlockSpec((1, d, 2 * EXPERT_FF), lambda i, blk_e, used: (blk_e[i], 0, 0)),
                      pl.BlockSpec((1, EXPERT_FF, d), lambda i, blk_e, used: (blk_e[i], 0, 0))],
            out_specs=pl.BlockSpec((bm, width), live)),
        out_shape=jax.ShapeDtypeStruct((n_slots, width), F32),
        compiler_params=_params(("arbitrary",), 32),
        name="gmm",
    )(blk_e, used, xs, wgu, wd)


def _ffn_out_kernel(x_ref, u_ref, yg_ref, gw_ref, g2_ref, wsgu_ref, wsd_ref, lng_ref, lnb_ref, out_ref):
    hgu = _dot(u_ref[0], wsgu_ref[...])
    hg = hgu[:, :SHARED_FF]
    h = (hg * _sigmoid(hg) * hgu[:, SHARED_FF:]).astype(BF16)
    shared = _dot(h, wsd_ref[...])
    gw = gw_ref[0]
    acc_lo = shared[:, :PACKED]
    acc_hi = shared[:, PACKED:]
    for k in range(TOP_K):
        lo, hi = _unpack_words(yg_ref[0, k])
        w = gw[:, k:k + 1]
        acc_lo = acc_lo + w * lo
        acc_hi = acc_hi + w * hi
    y = jnp.concatenate([acc_lo, acc_hi], axis=1)
    z = DN_ALPHA * x_ref[0] + (1.0 + g2_ref[0]) * y
    out_ref[0] = _layer_norm(z, lng_ref[...], lnb_ref[...])


def _ffn_out(x1, u2, yg, gw, mod3, wsgu, wsd, lng, lnb):
    bsz, seq, d = x1.shape
    tm = ROW_TILE
    const = lambda shape: pl.BlockSpec(shape, lambda b, j: (0,) * len(shape))
    tok = pl.BlockSpec((1, tm, d), lambda b, j: (b, j, 0))
    return pl.pallas_call(
        _ffn_out_kernel,
        grid=(bsz, seq // tm),
        in_specs=[tok, tok,
                  pl.BlockSpec((1, TOP_K, tm, PACKED), lambda b, j: (b, 0, j, 0)),
                  pl.BlockSpec((1, tm, TOP_K), lambda b, j: (b, j, 0)),
                  pl.BlockSpec((1, 1, d), lambda b, j: (b, 0, 5)),
                  const(wsgu.shape), const(wsd.shape), const(lng.shape), const(lnb.shape)],
        out_specs=tok,
        out_shape=jax.ShapeDtypeStruct((bsz, seq, d), F32),
        compiler_params=_params(("arbitrary", "arbitrary"), 48),
        name="ffn_out",
    )(x1, u2, yg, gw, mod3, wsgu, wsd, lng, lnb)


def _moe_layout(cnt, n_blocks):
    bm = EXPERT_BLOCK
    total = jnp.sum(cnt, axis=0)
    padded = (total + bm - 1) // bm * bm
    ends = jnp.cumsum(padded)
    base = (ends - padded)[None, :] + jnp.cumsum(cnt, axis=0) - cnt
    first_row = jnp.arange(n_blocks, dtype=ends.dtype) * bm
    blk_e = jnp.minimum(jnp.sum(ends[None, :] <= first_row[:, None], axis=1), N_EXPERTS - 1)
    used = (ends[-1:] // bm)
    return base.astype(jnp.int32), blk_e.astype(jnp.int32), used.astype(jnp.int32)


def _rope_tables(seq):
    half = HEAD_DIM // 2
    inv_freq = ROPE_THETA ** (-jnp.arange(half, dtype=F32) * 2.0 / HEAD_DIM)
    ang = jnp.arange(seq, dtype=F32)[:, None] * inv_freq[None, :]
    cos, sin = jnp.cos(ang), jnp.sin(ang)
    zero = jnp.zeros_like(sin)
    reps = LANES // HEAD_DIM
    cos_t = jnp.tile(jnp.concatenate([cos, cos], axis=1), (1, reps))
    sina_t = jnp.tile(jnp.concatenate([-sin, zero], axis=1), (1, reps))
    sinb_t = jnp.tile(jnp.concatenate([zero, sin], axis=1), (1, reps))
    return cos_t, sina_t, sinb_t


def _fox_bias_tables():
    n_split = 3
    aw = N_HEADS * LANES
    heads = jnp.arange(N_HEADS)
    selq = jnp.zeros((LANES, aw), F32)
    selk = jnp.zeros((LANES, aw), F32)
    oneq = jnp.zeros((1, aw), F32)
    onek = jnp.zeros((1, aw), F32)
    for j in range(n_split):
        selk = selk.at[j * N_HEADS + heads, heads * LANES + FOX_BIAS_LANE + j].set(-1.0)
        selq = selq.at[j * N_HEADS + heads, heads * LANES + FOX_BIAS_LANE + n_split + j].set(1.0)
        oneq = oneq.at[0, heads * LANES + FOX_BIAS_LANE + j].set(1.0)
        onek = onek.at[0, heads * LANES + FOX_BIAS_LANE + n_split + j].set(1.0)
    return selq.astype(BF16), selk.astype(BF16), oneq, onek


def kernel(x, c, w_ada, b_ada, w_in, b_forget, b_gate, w_br_fox, w_br_dil, w_o, ln_g, ln_b, w_router, router_bias,
           w_exp_gate, w_exp_up, w_exp_down, w_sh_gate, w_sh_up, w_sh_down):
    bsz, seq, d = x.shape
    depth = w_ada.shape[0]
    hw = HEADS_WIDTH
    n_tok = bsz * seq

    mod = _ada(c, w_ada, b_ada)

    o_f, o_d, o_g = 3 * hw, 3 * hw + N_HEADS, 6 * hw + N_HEADS
    wfox = w_in[:, :, :o_f].astype(BF16)
    wfg = jnp.pad(w_in[:, :, o_f:o_d], ((0, 0), (0, 0), (0, LANES - N_HEADS))).astype(BF16)
    wdil = w_in[:, :, o_d:o_g].astype(BF16)
    wgate = w_in[:, :, o_g:].astype(BF16)
    bfg = jnp.pad(b_forget, ((0, 0), (0, LANES - N_HEADS))).reshape(depth, 1, LANES)
    bgate = b_gate.reshape(depth, 1, 2 * d)
    wr_t = jnp.swapaxes(w_router, 1, 2)
    wr_hi = wr_t.astype(BF16)
    wr_lo = (wr_t - wr_hi.astype(F32)).astype(BF16)
    rbias = router_bias.reshape(depth, N_EXPERTS, 1)
    wgu = jnp.concatenate([w_exp_gate, w_exp_up], axis=-1).astype(BF16)
    wd = w_exp_down.astype(BF16)
    wsgu = jnp.concatenate([w_sh_gate, w_sh_up], axis=-1).astype(BF16)
    wsd = w_sh_down.astype(BF16)
    cos_t, sina_t, sinb_t = _rope_tables(seq)
    selq, selk, oneq, onek = _fox_bias_tables()

    layer_params = (mod, wfox, wdil, wgate, wfg, bfg, bgate, w_br_fox.astype(BF16), w_br_dil.astype(BF16),
                    w_o.astype(BF16), ln_g, ln_b, wr_hi, wr_lo, rbias, wgu, wd, wsgu, wsd)

    n_streams = 2 if n_tok % (2 * SC_WORKERS * DISPATCH_CHUNK) == 0 and bsz % 2 == 0 else 1
    sb = bsz // n_streams
    s_tok = sb * seq
    n_blocks = s_tok * TOP_K // EXPERT_BLOCK + N_EXPERTS

    def stream_layer(xc, mod3, prm):
        (_, wfox_l, wdil_l, wgate_l, wfg_l, bfg_l, bgate_l, wbf_l, wbd_l, wo_l, lng_l, lnb_l,
         wrh_l, wrl_l, rb_l, wgu_l, wd_l, wsgu_l, wsd_l) = prm
        qa, ka, vt, *rest = _inproj(
            xc, mod3, wfox_l, wdil_l, wgate_l, wfg_l, bfg_l, bgate_l, cos_t, sina_t, sinb_t, selq, selk, oneq, onek)
        gf, gd = rest[9:]
        yf = _fox(qa, ka, vt)
        dil = [_dil(*rest[3 * i:3 * i + 3]) for i in range(len(DIL_PATTERNS))]
        x1 = _mix(xc, yf, [o for o, _ in dil], [l for _, l in dil], gf, gd, mod3, wbf_l, wbd_l, wo_l,
                  lng_l[0:1], lnb_l[0:1])
        u2, u2p, eidx, rank, gw, cnt = _router(x1, mod3, wrh_l, wrl_l, rb_l)
        base, blk_e, used = _moe_layout(cnt.reshape(sb, N_EXPERTS), n_blocks)
        slot = _slots(base, eidx, rank)
        xs = _sc_dispatch(u2p.reshape(s_tok, PACKED), slot.reshape(sb * TOP_K, seq), n_blocks * EXPERT_BLOCK)
        ys = _gmm(xs, blk_e, used, wgu_l, wd_l)
        yg = _sc_gather(ys, slot.reshape(s_tok * TOP_K)).reshape(sb, TOP_K, seq, PACKED)
        return _ffn_out(x1, u2, yg, gw, mod3, wsgu_l, wsd_l, lng_l[1:2], lnb_l[1:2])

    def layer(xcs, prm):
        mod3 = prm[0].reshape(bsz, 1, 6 * d)
        return tuple(stream_layer(xc, mod3[i * sb:(i + 1) * sb], prm) for i, xc in enumerate(xcs)), None

    outs, _ = lax.scan(layer, tuple(x[i * sb:(i + 1) * sb] for i in range(n_streams)), layer_params)
    return outs[0] if n_streams == 1 else jnp.concatenate(outs, axis=0)
```

```python
import functools

import jax
import jax.numpy as jnp
from jax import lax
from jax.experimental import pallas as pl
from jax.experimental.pallas import tpu as pltpu
from jax.experimental.pallas import tpu_sc as plsc

D_MODEL = 1024
DEPTH = 4
HEAD_DIM = 64
N_HEADS = 8
HEADS_WIDTH = N_HEADS * HEAD_DIM
DIL_PATTERNS = ((128, 1), (512, 4), (2048, 16))
ATTN_BLOCK = 128
ROPE_THETA = 10000.0
N_EXPERTS = 64
TOP_K = 8
N_GROUPS = 8
GROUP_SIZE = N_EXPERTS // N_GROUPS
TOPK_GROUPS = 4
EXPERT_FF = 256
SHARED_FF = 256
ROUTED_SCALE = 2.5
DN_ALPHA = (2 * DEPTH) ** 0.25
LN_EPS = 1e-5
NEG_INF = -1e30
SCORE_SCALE = HEAD_DIM ** -0.5
LOG2_E = 1.4426950408889634
LN_2 = 0.6931471805599453

LANES = 128
PAIR = 2 * HEAD_DIM
N_PAIRS = HEADS_WIDTH // PAIR
LSE_LANES = LANES // N_HEADS

ROW_TILE = 512
FOX_Q = 256
FOX_K = 256
FOX_BIAS_LANE = HEAD_DIM
DIL_STEP_BLOCKS = 4
ROUTE_CHUNK = 512
EXPERT_BLOCK = 512
PACKED = D_MODEL // 2

SC_CORES = 2
SC_SUBCORES = 16
SC_WORKERS = SC_CORES * SC_SUBCORES
N_STREAMS = 4
DISPATCH_CHUNK = 128
GATHER_CHUNK = 64

BF16 = jnp.bfloat16
F32 = jnp.float32


def _dot(a, b):
    return jnp.dot(a, b, preferred_element_type=F32)


def _dot_nt(a, b):
    return lax.dot_general(a, b, (((1,), (1,)), ((), ())), preferred_element_type=F32)


def _split2(x):
    hi = x.astype(BF16)
    lo = (x - hi.astype(F32)).astype(BF16)
    return hi, lo


def _split3(x):
    hi = x.astype(BF16)
    r = x - hi.astype(F32)
    mid = r.astype(BF16)
    lo = (r - mid.astype(F32)).astype(BF16)
    return hi, mid, lo


def _sigmoid(x):
    return 1.0 / (1.0 + jnp.exp(-x))


def _layer_norm(z, g, b):
    mu = jnp.mean(z, axis=-1, keepdims=True)
    zc = z - mu
    var = jnp.mean(zc * zc, axis=-1, keepdims=True)
    return zc * lax.rsqrt(var + LN_EPS) * g + b


def _pack_words(v):
    half = v.shape[1] // 2
    bits = pltpu.bitcast(v.astype(BF16).astype(F32), jnp.uint32)
    word = (bits[:, half:] & jnp.uint32(0xFFFF0000)) | (bits[:, :half] >> 16)
    return pltpu.bitcast(word, F32)


def _unpack_words(w):
    bits = pltpu.bitcast(w, jnp.uint32)
    return pltpu.bitcast(bits << 16, F32), pltpu.bitcast(bits & jnp.uint32(0xFFFF0000), F32)


def _params(sem, vmem_mb):
    return pltpu.CompilerParams(dimension_semantics=sem, vmem_limit_bytes=vmem_mb * 1024 * 1024)


def _ada_kernel(c_ref, w_ref, b_ref, o_ref):
    c = c_ref[...]
    cs = c * _sigmoid(c)
    c_hi, c_lo = _split2(cs)
    w_hi, w_lo = _split2(w_ref[0])
    o_ref[0] = _dot(c_hi, w_hi) + _dot(c_hi, w_lo) + _dot(c_lo, w_hi) + b_ref[0]


def _ada(c, w_ada, b_ada):
    depth, d, n = w_ada.shape
    bsz = c.shape[0]
    tn = 1536
    return pl.pallas_call(
        _ada_kernel,
        grid=(depth, n // tn),
        in_specs=[
            pl.BlockSpec((bsz, d), lambda l, j: (0, 0)),
            pl.BlockSpec((1, d, tn), lambda l, j: (l, 0, j)),
            pl.BlockSpec((1, 1, tn), lambda l, j: (l, 0, j)),
        ],
        out_specs=pl.BlockSpec((1, bsz, tn), lambda l, j: (l, 0, j)),
        out_shape=jax.ShapeDtypeStruct((depth, bsz, n), F32),
        compiler_params=_params(("arbitrary", "arbitrary"), 40),
        name="ada",
    )(c, w_ada, b_ada.reshape(depth, 1, n))


def _inproj_kernel(x_ref, sc_ref, sh_ref, wfox_ref, wdil_ref, wgate_ref, wfg_ref, bfg_ref, bgate_ref,
                   cos_ref, sina_ref, sinb_ref, selq_ref, selk_ref, oneq_ref, onek_ref,
                   qa_ref, ka_ref, vt_ref, *rest):
    dil_refs = rest[:9]
    gf_ref, gd_ref, carry_ref, perm_ref = rest[9:]
    j = pl.program_id(1)
    tm = x_ref.shape[1]
    u = (x_ref[0] * (1.0 + sc_ref[0]) + sh_ref[0]).astype(BF16)

    ff = _dot(u, wfg_ref[...]) + bfg_ref[...]
    lf = jnp.minimum(ff, 0.0) - jnp.log(1.0 + jnp.exp(-jnp.abs(ff)))
    r = lax.broadcasted_iota(jnp.int32, (tm, tm), 0)
    c = lax.broadcasted_iota(jnp.int32, (tm, tm), 1)
    tri = jnp.where(r >= c, 1.0, 0.0).astype(BF16)
    hi, mid, lo = _split3(lf)
    csum = _dot(tri, hi) + _dot(tri, mid) + _dot(tri, lo)

    @pl.when(j == 0)
    def _():
        carry_ref[...] = jnp.zeros_like(carry_ref)

    cum = csum + carry_ref[...]
    carry_ref[...] = cum[tm - 1:tm, :]

    lane = lax.broadcasted_iota(jnp.int32, (1, LANES), 1)
    c_hi, c_mid, c_lo = (t.astype(F32) for t in _split3(cum * LOG2_E))
    parts = jnp.where(lane < N_HEADS, c_hi,
                      jnp.where(lane < 2 * N_HEADS, pltpu.roll(c_mid, N_HEADS, 1),
                                jnp.where(lane < 3 * N_HEADS, pltpu.roll(c_lo, 2 * N_HEADS, 1), 0.0))).astype(BF16)
    bias_q = _dot(parts, selq_ref[...]) + oneq_ref[...]
    bias_k = _dot(parts, selk_ref[...]) + onek_ref[...]
    pf = _dot(u, wfox_ref[...])
    data = lane < HEAD_DIM
    for dst, off, bias, scale in ((qa_ref, 0, bias_q, SCORE_SCALE * LOG2_E), (ka_ref, HEADS_WIDTH, bias_k, 1.0)):
        for p in range(N_PAIRS):
            pair = pf[:, off + p * PAIR: off + (p + 1) * PAIR] * scale
            for hh, src in ((0, pair), (1, pltpu.roll(pair, HEAD_DIM, 1))):
                t = (2 * p + hh) * LANES
                dst[0, :, t:t + LANES] = jnp.where(data, src, bias[:, t:t + LANES]).astype(BF16)
    vt = pf[:, 2 * HEADS_WIDTH:].T
    for cb in range(tm // FOX_K):
        vt_ref[0, cb] = vt[:, cb * FOX_K:(cb + 1) * FOX_K].astype(BF16)

    pd = _dot(u, wdil_ref[...])
    rows = pl.ds(pl.multiple_of(j * tm, tm), tm)
    cos = cos_ref[rows, :]
    sina = sina_ref[rows, :]
    sinb = sinb_ref[rows, :]
    for i, off in ((0, 0), (1, HEADS_WIDTH)):
        for g in range(N_PAIRS):
            xg = pd[:, off + g * LANES: off + (g + 1) * LANES]
            ahead = pltpu.roll(xg, LANES - HEAD_DIM // 2, 1)
            behind = pltpu.roll(xg, HEAD_DIM // 2, 1)
            rot = xg * cos + ahead * sina + behind * sinb
            perm_ref[i, g] = rot * (SCORE_SCALE * LOG2_E) if i == 0 else rot
    for g in range(N_PAIRS):
        perm_ref[2, g] = pd[:, 2 * HEADS_WIDTH + g * LANES:2 * HEADS_WIDTH + (g + 1) * LANES]
    for i in range(3):
        for (_, dilation), dst in zip(DIL_PATTERNS, dil_refs[i::3]):
            for res in range(dilation):
                for g in range(N_PAIRS):
                    dst[0, res, :, g * LANES:(g + 1) * LANES] = perm_ref[
                        i, g, pl.ds(res, tm // dilation, stride=dilation), :].astype(BF16)

    pg = _sigmoid(_dot(u, wgate_ref[...]) + bgate_ref[...])
    gf_ref[0] = pg[:, :D_MODEL].astype(BF16)
    gd_ref[0] = pg[:, D_MODEL:].astype(BF16)


def _inproj(x, mod3, wfox, wdil, wgate, wfg, bfg, bgate, cos_t, sina_t, sinb_t, selq, selk, oneq, onek):
    bsz, seq, d = x.shape
    tm = ROW_TILE
    const = lambda shape: pl.BlockSpec(shape, lambda b, j: (0,) * len(shape))
    tok = lambda w: pl.BlockSpec((1, tm, w), lambda b, j: (b, j, 0))
    hw = HEADS_WIDTH
    aw = N_HEADS * LANES
    dil_shapes, dil_specs = [], []
    for _, dil in DIL_PATTERNS:
        dil_shapes += [jax.ShapeDtypeStruct((bsz, dil, seq // dil, hw), BF16)] * 3
        dil_specs += [pl.BlockSpec((1, dil, tm // dil, hw), lambda b, j: (b, 0, j, 0))] * 3
    outs = ([jax.ShapeDtypeStruct((bsz, seq, aw), BF16)] * 2
            + [jax.ShapeDtypeStruct((bsz, seq // FOX_K, hw, FOX_K), BF16)]
            + dil_shapes + [jax.ShapeDtypeStruct((bsz, seq, d), BF16)] * 2)
    return pl.pallas_call(
        _inproj_kernel,
        grid=(bsz, seq // tm),
        in_specs=[
            tok(d),
            pl.BlockSpec((1, 1, d), lambda b, j: (b, 0, 1)),
            pl.BlockSpec((1, 1, d), lambda b, j: (b, 0, 0)),
            const(wfox.shape), const(wdil.shape), const(wgate.shape), const(wfg.shape),
            const(bfg.shape), const(bgate.shape),
            const(cos_t.shape), const(sina_t.shape), const(sinb_t.shape),
            const(selq.shape), const(selk.shape), const(oneq.shape), const(onek.shape),
        ],
        out_specs=[tok(aw)] * 2 + [pl.BlockSpec((1, tm // FOX_K, hw, FOX_K), lambda b, j: (b, j, 0, 0))]
                  + dil_specs + [tok(d)] * 2,
        out_shape=outs,
        scratch_shapes=[pltpu.VMEM((1, LANES), F32), pltpu.VMEM((3, N_PAIRS, tm, LANES), F32)],
        compiler_params=_params(("arbitrary", "arbitrary"), 56),
        name="inproj",
    )(x, mod3, mod3, wfox, wdil, wgate, wfg, bfg, bgate, cos_t, sina_t, sinb_t, selq, selk, oneq, onek)


def _fox_kernel(qa_ref, ka_ref, vt_ref, o_ref, m_ref, l_ref, acc_ref):
    seq = qa_ref.shape[1]
    tq, tk = FOX_Q, FOX_K
    k_pos = lax.broadcasted_iota(jnp.int32, (tk, tq), 0)
    q_pos = lax.broadcasted_iota(jnp.int32, (tk, tq), 1)
    ones = jnp.ones((16, tk), BF16)

    def q_body(qi, _):
        q0 = pl.multiple_of(qi * tq, tq)
        m_ref[...] = jnp.full(m_ref.shape, NEG_INF, F32)
        l_ref[...] = jnp.zeros(l_ref.shape, F32)
        acc_ref[...] = jnp.zeros(acc_ref.shape, F32)

        def kv_step(kb, k0, visible):
            scores = []
            for h in range(N_HEADS):
                ls = slice(h * LANES, (h + 1) * LANES)
                scores.append(_dot_nt(ka_ref[0, pl.ds(k0, tk), ls], qa_ref[0, pl.ds(q0, tq), ls]))
            m_all = m_ref[...]
            l_all = l_ref[...]
            probs, decay, m_rows, l_rows = [], [], [], []
            for h in range(N_HEADS):
                s = scores[h] if visible is None else jnp.where(visible, scores[h], NEG_INF)
                m_old = m_all[h:h + 1, :]
                m_new = jnp.maximum(m_old, jnp.max(s, axis=0, keepdims=True))
                m_rows.append(m_new)
                decay.append(jnp.exp2(m_old - m_new))
                probs.append(jnp.exp2((s - m_new).astype(BF16)))
            m_ref[...] = jnp.concatenate(m_rows, axis=0)
            for h in range(N_HEADS):
                vh = vt_ref[0, kb, h * HEAD_DIM:(h + 1) * HEAD_DIM, :]
                acc_ref[h] = decay[h] * acc_ref[h] + _dot(vh, probs[h])
                l_rows.append(decay[h] * l_all[h:h + 1, :] + _dot(ones, probs[h])[0:1])
            l_ref[...] = jnp.concatenate(l_rows, axis=0)

        def full_block(jb, carry):
            kv_step(jb, pl.multiple_of(jb * tk, tk), None)
            return carry

        lax.fori_loop(0, qi * (tq // tk), full_block, 0)
        for dblk in range(tq // tk):
            kb = qi * (tq // tk) + dblk
            kv_step(kb, pl.multiple_of(kb * tk, tk), k_pos + dblk * tk <= q_pos)

        for p in range(N_PAIRS):
            pair = jnp.concatenate([acc_ref[2 * p + hh] / l_ref[2 * p + hh:2 * p + hh + 1, :] for hh in range(2)],
                                   axis=0)
            o_ref[0, pl.ds(q0, tq), p * PAIR:(p + 1) * PAIR] = pair.T.astype(BF16)
        return 0

    lax.fori_loop(0, seq // tq, q_body, 0)


def _fox(qa, ka, vt):
    bsz, seq, aw = qa.shape
    hw = HEADS_WIDTH
    full = pl.BlockSpec((1, seq, aw), lambda b: (b, 0, 0))
    return pl.pallas_call(
        _fox_kernel,
        grid=(bsz,),
        in_specs=[full, full, pl.BlockSpec((1, seq // FOX_K, hw, FOX_K), lambda b: (b, 0, 0, 0))],
        out_specs=pl.BlockSpec((1, seq, hw), lambda b: (b, 0, 0)),
        out_shape=jax.ShapeDtypeStruct((bsz, seq, hw), BF16),
        scratch_shapes=[pltpu.VMEM((N_HEADS, FOX_Q), F32), pltpu.VMEM((N_HEADS, FOX_Q), F32),
                        pltpu.VMEM((N_HEADS, HEAD_DIM, FOX_Q), F32)],
        compiler_params=_params(("arbitrary",), 48),
        name="fox",
    )(qa, ka, vt)


def _dil_kernel(q_ref, kc_ref, kp_ref, vc_ref, vp_ref, o_ref, lse_ref):
    n = pl.program_id(2)
    blk = ATTN_BLOCK
    n_res = q_ref.shape[0]
    n_sub = q_ref.shape[1] // blk
    lane = lax.broadcasted_iota(jnp.int32, (1, PAIR), 1)
    first = lane < HEAD_DIM
    i = lax.broadcasted_iota(jnp.int32, (blk, 2 * blk), 0)
    c = lax.broadcasted_iota(jnp.int32, (blk, 2 * blk), 1)
    in_window = (c >= i) & (c <= i + blk)
    has_prev = (c >= blk) | (n > 0)
    lane_head = lax.broadcasted_iota(jnp.int32, (1, LANES), 1) // LSE_LANES

    for res in range(n_res):
        for sub in range(n_sub):
            rows = slice(sub * blk, (sub + 1) * blk)
            before = slice((sub - 1) * blk, sub * blk)
            valid = in_window & has_prev if sub == 0 else in_window
            scores, values = [], []
            for p in range(N_PAIRS):
                ls = slice(p * PAIR, (p + 1) * PAIR)
                qb = q_ref[res, rows, ls]
                zero = jnp.zeros_like(qb)
                k_prev = kp_ref[res, :, ls] if sub == 0 else kc_ref[res, before, ls]
                v_prev = vp_ref[res, :, ls] if sub == 0 else vc_ref[res, before, ls]
                kb = jnp.concatenate([k_prev, kc_ref[res, rows, ls]], axis=0)
                values.append(jnp.concatenate([v_prev, vc_ref[res, rows, ls]], axis=0))
                scores.append(_dot_nt(jnp.where(first, qb, zero), kb))
                scores.append(_dot_nt(jnp.where(first, zero, qb), kb))
            probs, denom = [], []
            lse_tile = jnp.zeros((blk, LANES), F32)
            for h, s in enumerate(scores):
                s = jnp.where(valid, s, NEG_INF)
                m = jnp.max(s, axis=-1, keepdims=True)
                e = jnp.exp2(s - m)
                l = jnp.sum(e, axis=-1, keepdims=True)
                probs.append(e.astype(BF16))
                denom.append(l)
                lse_tile = jnp.where(lane_head == h, m * LN_2 + jnp.log(l), lse_tile)
            for p in range(N_PAIRS):
                ls = slice(p * PAIR, (p + 1) * PAIR)
                o0 = _dot(probs[2 * p], values[p]) / denom[2 * p]
                o1 = _dot(probs[2 * p + 1], values[p]) / denom[2 * p + 1]
                o_ref[res, rows, ls] = jnp.where(first, o0, o1).astype(BF16)
            lse_ref[res, rows, :] = lse_tile


def _dil(qd, kd, vd):
    bsz, dilation, length, hw = qd.shape
    nb = length // ATTN_BLOCK
    n_sub = min(nb, DIL_STEP_BLOCKS)
    n_res = min(dilation, DIL_STEP_BLOCKS // n_sub)
    cur = pl.BlockSpec((None, n_res, n_sub * ATTN_BLOCK, hw), lambda b, r, n: (b, r, n, 0))
    cur_lse = pl.BlockSpec((None, n_res, n_sub * ATTN_BLOCK, LANES), lambda b, r, n: (b, r, n, 0))
    prev = pl.BlockSpec((None, n_res, ATTN_BLOCK, hw), lambda b, r, n: (b, r, jnp.maximum(n * n_sub - 1, 0), 0))
    return pl.pallas_call(
        _dil_kernel,
        grid=(bsz, dilation // n_res, nb // n_sub),
        in_specs=[cur, cur, prev, cur, prev],
        out_specs=[cur, cur_lse],
        out_shape=[jax.ShapeDtypeStruct(qd.shape, BF16),
                   jax.ShapeDtypeStruct((bsz, dilation, length, LANES), F32)],
        compiler_params=_params(("arbitrary", "arbitrary", "arbitrary"), 32),
        name=f"dil{dilation}",
    )(qd, kd, kd, vd, vd)


def _mix_kernel(x_ref, yf_ref, o1_ref, o2_ref, o3_ref, l1_ref, l2_ref, l3_ref, gf_ref, gd_ref, g1_ref,
                wbf_ref, wbd_ref, wo_ref, lng_ref, lnb_ref, spread_ref, out_ref, nat_ref):
    tm = x_ref.shape[1]

    def natural(i, src_ref):
        dilation = src_ref.shape[1]
        groups = src_ref.shape[3] // LANES
        for res in range(dilation):
            for g in range(groups):
                nat_ref[i, g, pl.ds(res, tm // dilation, stride=dilation), :] = src_ref[
                    0, res, :, g * LANES:(g + 1) * LANES].astype(F32)
        return jnp.concatenate([nat_ref[i, g] for g in range(groups)], axis=1)

    os_ = [natural(i, ref) for i, ref in enumerate((o1_ref, o2_ref, o3_ref))]
    l1, l2, l3 = [natural(3 + i, ref) for i, ref in enumerate((l1_ref, l2_ref, l3_ref))]
    mx = jnp.maximum(jnp.maximum(l1, l2), l3)
    e1, e2, e3 = jnp.exp(l1 - mx), jnp.exp(l2 - mx), jnp.exp(l3 - mx)
    tot = e1 + e2 + e3
    yd = sum(_dot((e / tot).astype(BF16), spread_ref[...]) * o for e, o in zip((e1, e2, e3), os_))
    merged = (gf_ref[0].astype(F32) * _dot(yf_ref[0], wbf_ref[...])
              + gd_ref[0].astype(F32) * _dot(yd.astype(BF16), wbd_ref[...]))
    y = _dot(merged.astype(BF16), wo_ref[...])
    z = DN_ALPHA * x_ref[0] + (1.0 + g1_ref[0]) * y
    out_ref[0] = _layer_norm(z, lng_ref[...], lnb_ref[...])


def _mix(x, yf, os_, lses, gf, gd, mod3, wbf, wbd, wo, lng, lnb):
    bsz, seq, d = x.shape
    tm = ROW_TILE
    const = lambda shape: pl.BlockSpec(shape, lambda b, j: (0,) * len(shape))
    tok = lambda w: pl.BlockSpec((1, tm, w), lambda b, j: (b, j, 0))
    hw = HEADS_WIDTH
    res_major = lambda w: [pl.BlockSpec((1, dil, tm // dil, w), lambda b, j: (b, 0, j, 0)) for _, dil in DIL_PATTERNS]
    heads = jnp.arange(N_HEADS)
    spread = jnp.zeros((LANES, hw), F32).at[
        (heads * LSE_LANES)[:, None], heads[:, None] * HEAD_DIM + jnp.arange(HEAD_DIM)[None, :]].set(1.0).astype(BF16)
    return pl.pallas_call(
        _mix_kernel,
        grid=(bsz, seq // tm),
        in_specs=[tok(d), tok(hw)] + res_major(hw) + res_major(LANES) + [tok(d), tok(d),
                  pl.BlockSpec((1, 1, d), lambda b, j: (b, 0, 2)),
                  const(wbf.shape), const(wbd.shape), const(wo.shape), const(lng.shape), const(lnb.shape),
                  const(spread.shape)],
        out_specs=tok(d),
        out_shape=jax.ShapeDtypeStruct((bsz, seq, d), F32),
        scratch_shapes=[pltpu.VMEM((6, N_PAIRS, tm, LANES), F32)],
        compiler_params=_params(("arbitrary", "arbitrary"), 48),
        name="mix",
    )(x, yf, *os_, *lses, gf, gd, mod3, wbf, wbd, wo, lng, lnb, spread)


def _router_kernel(x_ref, sc_ref, sh_ref, wrh_ref, wrl_ref, bias_ref,
                   u_ref, up_ref, eidx_ref, rank_ref, gw_ref, cnt_ref):
    seq = x_ref.shape[1]
    tc = ROUTE_CHUNK
    e_iota = lax.broadcasted_iota(jnp.int32, (N_EXPERTS, tc), 0)
    g_iota = lax.broadcasted_iota(jnp.int32, (N_GROUPS, tc), 0)
    s_iota = lax.broadcasted_iota(jnp.int32, (GROUP_SIZE, tc), 0)
    r = lax.broadcasted_iota(jnp.int32, (tc, tc), 0)
    c = lax.broadcasted_iota(jnp.int32, (tc, tc), 1)
    upper = jnp.where(r <= c, 1.0, 0.0).astype(BF16)
    ones = jnp.ones((16, tc), BF16)
    carry = jnp.zeros((N_EXPERTS, 1), F32)
    cnt = jnp.zeros((16, N_EXPERTS), F32)

    for ch in range(seq // tc):
        rows = slice(ch * tc, (ch + 1) * tc)
        u = x_ref[0, rows, :] * (1.0 + sc_ref[0]) + sh_ref[0]
        u_ref[0, rows, :] = u.astype(BF16)
        up_ref[0, rows, :] = _pack_words(u)
        u_hi, u_lo = _split2(u)
        logits = _dot_nt(wrh_ref[...], u_hi) + _dot_nt(wrh_ref[...], u_lo) + _dot_nt(wrl_ref[...], u_hi)
        scores = _sigmoid(logits)
        sel = scores + bias_ref[...]

        gs_rows = []
        for g in range(N_GROUPS):
            blk = sel[g * GROUP_SIZE:(g + 1) * GROUP_SIZE]
            m1 = jnp.max(blk, axis=0, keepdims=True)
            i1 = jnp.min(jnp.where(blk == m1, s_iota, GROUP_SIZE), axis=0, keepdims=True)
            m2 = jnp.max(jnp.where(s_iota == i1, -jnp.inf, blk), axis=0, keepdims=True)
            gs_rows.append(m1 + m2)
        gs = jnp.concatenate(gs_rows, axis=0)
        beaten = jnp.zeros((N_GROUPS, tc), F32)
        for g in range(N_GROUPS):
            other = gs[g:g + 1]
            wins = (other > gs) | ((other == gs) & (g_iota > g))
            beaten = beaten + jnp.where(wins, 1.0, 0.0)
        keep = beaten < TOPK_GROUPS
        cand = jnp.concatenate(
            [jnp.where(keep[g:g + 1], sel[g * GROUP_SIZE:(g + 1) * GROUP_SIZE], NEG_INF) for g in range(N_GROUPS)],
            axis=0)
        chosen = jnp.zeros((N_EXPERTS, tc), F32)
        picks = []
        for _ in range(TOP_K):
            m = jnp.max(cand, axis=0, keepdims=True)
            idx = jnp.min(jnp.where(cand == m, e_iota, N_EXPERTS), axis=0, keepdims=True)
            hit = e_iota == idx
            chosen = jnp.where(hit, 1.0, chosen)
            cand = jnp.where(hit, -jnp.inf, cand)
            picks.append(idx)
        g_raw = jnp.where(chosen > 0.0, scores, 0.0)
        gate = g_raw / jnp.sum(g_raw, axis=0, keepdims=True) * ROUTED_SCALE

        chosen_b = chosen.astype(BF16)
        incl = _dot(chosen_b, upper) + carry
        carry = incl[:, tc - 1:tc]
        cnt = cnt + _dot_nt(ones, chosen_b)

        rank_rows, gate_rows = [], []
        for idx in picks:
            hit = e_iota == idx
            rank_rows.append(jnp.sum(jnp.where(hit, incl - 1.0, 0.0), axis=0, keepdims=True))
            gate_rows.append(jnp.sum(jnp.where(hit, gate, 0.0), axis=0, keepdims=True))
        eidx_ref[0, :, rows] = jnp.concatenate(picks, axis=0)
        rank_ref[0, :, rows] = jnp.concatenate(rank_rows, axis=0).astype(jnp.int32)
        gate_t = jnp.concatenate(gate_rows + [jnp.zeros((LANES - TOP_K, tc), F32)], axis=0).T
        gw_ref[0, rows, :] = gate_t[:, :TOP_K]

    cnt_ref[0] = cnt[0:1].astype(jnp.int32)


def _router(x1, mod3, wr_hi, wr_lo, bias):
    bsz, seq, d = x1.shape
    full = pl.BlockSpec((1, seq, d), lambda b: (b, 0, 0))
    ks = pl.BlockSpec((1, TOP_K, seq), lambda b: (b, 0, 0))
    return pl.pallas_call(
        _router_kernel,
        grid=(bsz,),
        in_specs=[full,
                  pl.BlockSpec((1, 1, d), lambda b: (b, 0, 4)),
                  pl.BlockSpec((1, 1, d), lambda b: (b, 0, 3)),
                  pl.BlockSpec(wr_hi.shape, lambda b: (0, 0)),
                  pl.BlockSpec(wr_lo.shape, lambda b: (0, 0)),
                  pl.BlockSpec(bias.shape, lambda b: (0, 0))],
        out_specs=[full, pl.BlockSpec((1, seq, PACKED), lambda b: (b, 0, 0)), ks, ks,
                   pl.BlockSpec((1, seq, TOP_K), lambda b: (b, 0, 0)),
                   pl.BlockSpec((1, 1, N_EXPERTS), lambda b: (b, 0, 0))],
        out_shape=[jax.ShapeDtypeStruct((bsz, seq, d), BF16),
                   jax.ShapeDtypeStruct((bsz, seq, PACKED), F32),
                   jax.ShapeDtypeStruct((bsz, TOP_K, seq), jnp.int32),
                   jax.ShapeDtypeStruct((bsz, TOP_K, seq), jnp.int32),
                   jax.ShapeDtypeStruct((bsz, seq, TOP_K), F32),
                   jax.ShapeDtypeStruct((bsz, 1, N_EXPERTS), jnp.int32)],
        compiler_params=_params(("arbitrary",), 48),
        name="router",
    )(x1, mod3, mod3, wr_hi, wr_lo, bias)


def _slot_kernel(base_ref, eidx_ref, rank_ref, slot_ref):
    b = pl.program_id(0)
    eidx = eidx_ref[0]
    start = jnp.zeros(eidx.shape, jnp.int32)
    for e in range(N_EXPERTS):
        start = jnp.where(eidx == e, base_ref[b * N_EXPERTS + e], start)
    slot_ref[0] = start + rank_ref[0]


def _slots(base, eidx, rank):
    bsz, _, seq = eidx.shape
    ks = pl.BlockSpec((1, TOP_K, seq), lambda b, c: (b, 0, 0))
    return pl.pallas_call(
        _slot_kernel,
        grid_spec=pltpu.PrefetchScalarGridSpec(num_scalar_prefetch=1, grid=(bsz,), in_specs=[ks, ks], out_specs=ks),
        out_shape=jax.ShapeDtypeStruct((bsz, TOP_K, seq), jnp.int32),
        compiler_params=_params(("arbitrary",), 16),
        name="slots",
    )(base.reshape(bsz * N_EXPERTS), eidx, rank)


def _sc_mesh():
    return plsc.VectorSubcoreMesh(core_axis_name="core", subcore_axis_name="subcore")


def _sc_worker():
    return lax.axis_index("subcore") * SC_CORES + lax.axis_index("core")


def _sc_dispatch(rows, slot, n_slots):
    n_tok, width = rows.shape
    seq = slot.shape[1]
    chunk = DISPATCH_CHUNK
    per_worker = n_tok // SC_WORKERS
    assert per_worker % chunk == 0 and seq % chunk == 0

    @functools.partial(
        pl.kernel, mesh=_sc_mesh(),
        out_type=jax.ShapeDtypeStruct((n_slots, width), rows.dtype),
        scratch_types=[pltpu.VMEM((TOP_K, chunk), jnp.int32), pltpu.VMEM((chunk, width), rows.dtype),
                       pltpu.SemaphoreType.DMA],
    )
    def dispatch(rows_hbm, slot_hbm, out_hbm, idx_v, rows_v, sem):
        base = _sc_worker() * per_worker

        @pl.loop(0, per_worker // chunk)
        def _(i):
            off = base + i * chunk
            b = off // seq
            n0 = off - b * seq
            pltpu.sync_copy(slot_hbm.at[pl.ds(b * TOP_K, TOP_K), pl.ds(n0, chunk)], idx_v)
            pltpu.sync_copy(rows_hbm.at[pl.ds(off, chunk)], rows_v)
            copies = [pltpu.async_copy(rows_v, out_hbm.at[idx_v.at[k]], sem) for k in range(TOP_K)]
            for cp in copies:
                cp.wait()

    return dispatch(rows, slot)


def _sc_gather(table, idx):
    n_out = idx.shape[0]
    width = table.shape[1]
    chunk = GATHER_CHUNK
    per_worker = n_out // SC_WORKERS
    steps = per_worker // chunk
    assert per_worker % chunk == 0 and steps % 2 == 0

    @functools.partial(
        pl.kernel, mesh=_sc_mesh(),
        out_type=jax.ShapeDtypeStruct((n_out, width), table.dtype),
        scratch_types=[pltpu.VMEM((2, chunk), jnp.int32), pltpu.VMEM((2, chunk, width), table.dtype),
                       pltpu.SemaphoreType.DMA, pltpu.SemaphoreType.DMA((2,))],
    )
    def gather(table_hbm, idx_hbm, out_hbm, idx_v, rows_v, gather_sem, out_sems):
        base = _sc_worker() * per_worker

        def write_out(buf, off):
            return pltpu.make_async_copy(rows_v.at[buf], out_hbm.at[pl.ds(off, chunk)], out_sems.at[buf])

        @pl.loop(0, steps, step=2)
        def _(i):
            for buf in range(2):
                off = base + (i + buf) * chunk

                @pl.when(i >= 2)
                def _():
                    write_out(buf, off - 2 * chunk).wait()

                pltpu.sync_copy(idx_hbm.at[pl.ds(off, chunk)], idx_v.at[buf])
                pltpu.async_copy(table_hbm.at[idx_v.at[buf]], rows_v.at[buf], gather_sem).wait()
                write_out(buf, off).start()

        for buf in range(2):
            write_out(buf, base + (steps - 2 + buf) * chunk).wait()

    return gather(table, idx)


def _gmm_kernel(blk_e_ref, used_ref, xs_ref, wgu_ref, wd_ref, out_ref):
    half = PACKED

    @pl.when(pl.program_id(0) < used_ref[0])
    def _():
        lo, hi = _unpack_words(xs_ref[...])
        hgu = _dot(lo.astype(BF16), wgu_ref[0, :half, :]) + _dot(hi.astype(BF16), wgu_ref[0, half:, :])
        hg = hgu[:, :EXPERT_FF]
        h = (hg * _sigmoid(hg) * hgu[:, EXPERT_FF:]).astype(BF16)
        out_ref[...] = _pack_words(_dot(h, wd_ref[0]))


def _gmm(xs, blk_e, used, wgu, wd):
    n_slots, width = xs.shape
    bm = EXPERT_BLOCK
    d = wgu.shape[1]
    live = lambda i, blk_e, used: (jnp.minimum(i, used[0] - 1), 0)
    return pl.pallas_call(
        _gmm_kernel,
        grid_spec=pltpu.PrefetchScalarGridSpec(
            num_scalar_prefetch=2,
            grid=(n_slots // bm,),
            in_specs=[pl.BlockSpec((bm, width), live),
                      pl.BlockSpec((1, d, 2 * EXPERT_FF), lambda i, blk_e, used: (blk_e[i], 0, 0)),
                      pl.BlockSpec((1, EXPERT_FF, d), lambda i, blk_e, used: (blk_e[i], 0, 0))],
            out_specs=pl.BlockSpec((bm, width), live)),
        out_shape=jax.ShapeDtypeStruct((n_slots, width), F32),
        compiler_params=_params(("arbitrary",), 32),
        name="gmm",
    )(blk_e, used, xs, wgu, wd)


def _ffn_out_kernel(x_ref, u_ref, yg_ref, gw_ref, g2_ref, wsgu_ref, wsd_ref, lng_ref, lnb_ref, out_ref):
    hgu = _dot(u_ref[0], wsgu_ref[...])
    hg = hgu[:, :SHARED_FF]
    h = (hg * _sigmoid(hg) * hgu[:, SHARED_FF:]).astype(BF16)
    shared = _dot(h, wsd_ref[...])
    gw = gw_ref[0]
    acc_lo = shared[:, :PACKED]
    acc_hi = shared[:, PACKED:]
    for k in range(TOP_K):
        lo, hi = _unpack_words(yg_ref[0, k])
        w = gw[:, k:k + 1]
        acc_lo = acc_lo + w * lo
        acc_hi = acc_hi + w * hi
    y = jnp.concatenate([acc_lo, acc_hi], axis=1)
    z = DN_ALPHA * x_ref[0] + (1.0 + g2_ref[0]) * y
    out_ref[0] = _layer_norm(z, lng_ref[...], lnb_ref[...])


def _ffn_out(x1, u2, yg, gw, mod3, wsgu, wsd, lng, lnb):
    bsz, seq, d = x1.shape
    tm = ROW_TILE
    const = lambda shape: pl.BlockSpec(shape, lambda b, j: (0,) * len(shape))
    tok = pl.BlockSpec((1, tm, d), lambda b, j: (b, j, 0))
    return pl.pallas_call(
        _ffn_out_kernel,
        grid=(bsz, seq // tm),
        in_specs=[tok, tok,
                  pl.BlockSpec((1, TOP_K, tm, PACKED), lambda b, j: (b, 0, j, 0)),
                  pl.BlockSpec((1, tm, TOP_K), lambda b, j: (b, j, 0)),
                  pl.BlockSpec((1, 1, d), lambda b, j: (b, 0, 5)),
                  const(wsgu.shape), const(wsd.shape), const(lng.shape), const(lnb.shape)],
        out_specs=tok,
        out_shape=jax.ShapeDtypeStruct((bsz, seq, d), F32),
        compiler_params=_params(("arbitrary", "arbitrary"), 48),
        name="ffn_out",
    )(x1, u2, yg, gw, mod3, wsgu, wsd, lng, lnb)


def _moe_layout(cnt, n_blocks):
    bm = EXPERT_BLOCK
    total = jnp.sum(cnt, axis=0)
    padded = (total + bm - 1) // bm * bm
    ends = jnp.cumsum(padded)
    base = (ends - padded)[None, :] + jnp.cumsum(cnt, axis=0) - cnt
    first_row = jnp.arange(n_blocks, dtype=ends.dtype) * bm
    blk_e = jnp.minimum(jnp.sum(ends[None, :] <= first_row[:, None], axis=1), N_EXPERTS - 1)
    used = (ends[-1:] // bm)
    return base.astype(jnp.int32), blk_e.astype(jnp.int32), used.astype(jnp.int32)


def _rope_tables(seq):
    half = HEAD_DIM // 2
    inv_freq = ROPE_THETA ** (-jnp.arange(half, dtype=F32) * 2.0 / HEAD_DIM)
    ang = jnp.arange(seq, dtype=F32)[:, None] * inv_freq[None, :]
    cos, sin = jnp.cos(ang), jnp.sin(ang)
    zero = jnp.zeros_like(sin)
    reps = LANES // HEAD_DIM
    cos_t = jnp.tile(jnp.concatenate([cos, cos], axis=1), (1, reps))
    sina_t = jnp.tile(jnp.concatenate([-sin, zero], axis=1), (1, reps))
    sinb_t = jnp.tile(jnp.concatenate([zero, sin], axis=1), (1, reps))
    return cos_t, sina_t, sinb_t


def _fox_bias_tables():
    n_split = 3
    aw = N_HEADS * LANES
    heads = jnp.arange(N_HEADS)
    selq = jnp.zeros((LANES, aw), F32)
    selk = jnp.zeros((LANES, aw), F32)
    oneq = jnp.zeros((1, aw), F32)
    onek = jnp.zeros((1, aw), F32)
    for j in range(n_split):
        selk = selk.at[j * N_HEADS + heads, heads * LANES + FOX_BIAS_LANE + j].set(-1.0)
        selq = selq.at[j * N_HEADS + heads, heads * LANES + FOX_BIAS_LANE + n_split + j].set(1.0)
        oneq = oneq.at[0, heads * LANES + FOX_BIAS_LANE + j].set(1.0)
        onek = onek.at[0, heads * LANES + FOX_BIAS_LANE + n_split + j].set(1.0)
    return selq.astype(BF16), selk.astype(BF16), oneq, onek


def kernel(x, c, w_ada, b_ada, w_in, b_forget, b_gate, w_br_fox, w_br_dil, w_o, ln_g, ln_b, w_router, router_bias,
           w_exp_gate, w_exp_up, w_exp_down, w_sh_gate, w_sh_up, w_sh_down):
    bsz, seq, d = x.shape
    depth = w_ada.shape[0]
    hw = HEADS_WIDTH
    n_tok = bsz * seq

    mod = _ada(c, w_ada, b_ada)

    o_f, o_d, o_g = 3 * hw, 3 * hw + N_HEADS, 6 * hw + N_HEADS
    wfox = w_in[:, :, :o_f].astype(BF16)
    wfg = jnp.pad(w_in[:, :, o_f:o_d], ((0, 0), (0, 0), (0, LANES - N_HEADS))).astype(BF16)
    wdil = w_in[:, :, o_d:o_g].astype(BF16)
    wgate = w_in[:, :, o_g:].astype(BF16)
    bfg = jnp.pad(b_forget, ((0, 0), (0, LANES - N_HEADS))).reshape(depth, 1, LANES)
    bgate = b_gate.reshape(depth, 1, 2 * d)
    wr_t = jnp.swapaxes(w_router, 1, 2)
    wr_hi = wr_t.astype(BF16)
    wr_lo = (wr_t - wr_hi.astype(F32)).astype(BF16)
    rbias = router_bias.reshape(depth, N_EXPERTS, 1)
    wgu = jnp.concatenate([w_exp_gate, w_exp_up], axis=-1).astype(BF16)
    wd = w_exp_down.astype(BF16)
    wsgu = jnp.concatenate([w_sh_gate, w_sh_up], axis=-1).astype(BF16)
    wsd = w_sh_down.astype(BF16)
    cos_t, sina_t, sinb_t = _rope_tables(seq)
    selq, selk, oneq, onek = _fox_bias_tables()

    layer_params = (mod, wfox, wdil, wgate, wfg, bfg, bgate, w_br_fox.astype(BF16), w_br_dil.astype(BF16),
                    w_o.astype(BF16), ln_g, ln_b, wr_hi, wr_lo, rbias, wgu, wd, wsgu, wsd)

    n_streams = max(n for n in (1, 2, N_STREAMS) if n_tok % (n * SC_WORKERS * DISPATCH_CHUNK) == 0 and bsz % n == 0)
    sb = bsz // n_streams
    s_tok = sb * seq
    n_blocks = s_tok * TOP_K // EXPERT_BLOCK + N_EXPERTS

    def stream_layer(xc, mod3, prm):
        (_, wfox_l, wdil_l, wgate_l, wfg_l, bfg_l, bgate_l, wbf_l, wbd_l, wo_l, lng_l, lnb_l,
         wrh_l, wrl_l, rb_l, wgu_l, wd_l, wsgu_l, wsd_l) = prm
        qa, ka, vt, *rest = _inproj(
            xc, mod3, wfox_l, wdil_l, wgate_l, wfg_l, bfg_l, bgate_l, cos_t, sina_t, sinb_t, selq, selk, oneq, onek)
        gf, gd = rest[9:]
        yf = _fox(qa, ka, vt)
        dil = [_dil(*rest[3 * i:3 * i + 3]) for i in range(len(DIL_PATTERNS))]
        x1 = _mix(xc, yf, [o for o, _ in dil], [l for _, l in dil], gf, gd, mod3, wbf_l, wbd_l, wo_l,
                  lng_l[0:1], lnb_l[0:1])
        u2, u2p, eidx, rank, gw, cnt = _router(x1, mod3, wrh_l, wrl_l, rb_l)
        base, blk_e, used = _moe_layout(cnt.reshape(sb, N_EXPERTS), n_blocks)
        slot = _slots(base, eidx, rank)
        xs = _sc_dispatch(u2p.reshape(s_tok, PACKED), slot.reshape(sb * TOP_K, seq), n_blocks * EXPERT_BLOCK)
        ys = _gmm(xs, blk_e, used, wgu_l, wd_l)
        yg = _sc_gather(ys, slot.reshape(s_tok * TOP_K)).reshape(sb, TOP_K, seq, PACKED)
        return _ffn_out(x1, u2, yg, gw, mod3, wsgu_l, wsd_l, lng_l[1:2], lnb_l[1:2])

    def layer(xcs, prm):
        mod3 = prm[0].reshape(bsz, 1, 6 * d)
        return tuple(stream_layer(xc, mod3[i * sb:(i + 1) * sb], prm) for i, xc in enumerate(xcs)), None

    outs, _ = lax.scan(layer, tuple(x[i * sb:(i + 1) * sb] for i in range(n_streams)), layer_params)
    return outs[0] if n_streams == 1 else jnp.concatenate(outs, axis=0)
```

```python
import functools

import jax
import jax.numpy as jnp
from jax import lax
from jax.experimental import pallas as pl
from jax.experimental.pallas import tpu as pltpu
from jax.experimental.pallas import tpu_sc as plsc

D_MODEL = 1024
DEPTH = 4
HEAD_DIM = 64
N_HEADS = 8
HEADS_WIDTH = N_HEADS * HEAD_DIM
DIL_PATTERNS = ((128, 1), (512, 4), (2048, 16))
ATTN_BLOCK = 128
ROPE_THETA = 10000.0
N_EXPERTS = 64
TOP_K = 8
N_GROUPS = 8
GROUP_SIZE = N_EXPERTS // N_GROUPS
TOPK_GROUPS = 4
EXPERT_FF = 256
SHARED_FF = 256
ROUTED_SCALE = 2.5
DN_ALPHA = (2 * DEPTH) ** 0.25
LN_EPS = 1e-5
NEG_INF = -1e30
SCORE_SCALE = HEAD_DIM ** -0.5
LOG2_E = 1.4426950408889634

LANES = 128
PAIR = 2 * HEAD_DIM
N_PAIRS = HEADS_WIDTH // PAIR

ROW_TILE = 512
FOX_Q = 256
FOX_K = 256
FOX_BIAS_LANE = HEAD_DIM
DIL_STEP_BLOCKS = 4
ROUTE_CHUNK = 512
EXPERT_BLOCK = 512
PACKED = D_MODEL // 2

SC_CORES = 2
SC_SUBCORES = 16
SC_WORKERS = SC_CORES * SC_SUBCORES
DISPATCH_CHUNK = 128
GATHER_CHUNK = 64

BF16 = jnp.bfloat16
F32 = jnp.float32


def _dot(a, b):
    return jnp.dot(a, b, preferred_element_type=F32)


def _dot_nt(a, b):
    return lax.dot_general(a, b, (((1,), (1,)), ((), ())), preferred_element_type=F32)


def _split2(x):
    hi = x.astype(BF16)
    lo = (x - hi.astype(F32)).astype(BF16)
    return hi, lo


def _split3(x):
    hi = x.astype(BF16)
    r = x - hi.astype(F32)
    mid = r.astype(BF16)
    lo = (r - mid.astype(F32)).astype(BF16)
    return hi, mid, lo


def _sigmoid(x):
    return 1.0 / (1.0 + jnp.exp(-x))


def _layer_norm(z, g, b):
    mu = jnp.mean(z, axis=-1, keepdims=True)
    zc = z - mu
    var = jnp.mean(zc * zc, axis=-1, keepdims=True)
    return zc * lax.rsqrt(var + LN_EPS) * g + b


def _pack_words(v):
    half = v.shape[1] // 2
    bits = pltpu.bitcast(v.astype(BF16).astype(F32), jnp.uint32)
    word = (bits[:, half:] & jnp.uint32(0xFFFF0000)) | (bits[:, :half] >> 16)
    return pltpu.bitcast(word, F32)


def _unpack_words(w):
    bits = pltpu.bitcast(w, jnp.uint32)
    return pltpu.bitcast(bits << 16, F32), pltpu.bitcast(bits & jnp.uint32(0xFFFF0000), F32)


def _params(sem, vmem_mb):
    return pltpu.CompilerParams(dimension_semantics=sem, vmem_limit_bytes=vmem_mb * 1024 * 1024)


def _ada_kernel(c_ref, w_ref, b_ref, o_ref):
    c = c_ref[...]
    cs = c * _sigmoid(c)
    c_hi, c_lo = _split2(cs)
    w_hi, w_lo = _split2(w_ref[0])
    o_ref[0] = _dot(c_hi, w_hi) + _dot(c_hi, w_lo) + _dot(c_lo, w_hi) + b_ref[0]


def _ada(c, w_ada, b_ada):
    depth, d, n = w_ada.shape
    bsz = c.shape[0]
    tn = 1536
    return pl.pallas_call(
        _ada_kernel,
        grid=(depth, n // tn),
        in_specs=[
            pl.BlockSpec((bsz, d), lambda l, j: (0, 0)),
            pl.BlockSpec((1, d, tn), lambda l, j: (l, 0, j)),
            pl.BlockSpec((1, 1, tn), lambda l, j: (l, 0, j)),
        ],
        out_specs=pl.BlockSpec((1, bsz, tn), lambda l, j: (l, 0, j)),
        out_shape=jax.ShapeDtypeStruct((depth, bsz, n), F32),
        compiler_params=_params(("arbitrary", "arbitrary"), 40),
        name="ada",
    )(c, w_ada, b_ada.reshape(depth, 1, n))


def _inproj_kernel(x_ref, sc_ref, sh_ref, wfox_ref, wdil_ref, wgate_ref, wfg_ref, bfg_ref, bgate_ref,
                   cos_ref, sina_ref, sinb_ref, selq_ref, selk_ref, oneq_ref, onek_ref,
                   qa_ref, ka_ref, vt_ref, *rest):
    dil_refs = rest[:9]
    gf_ref, gd_ref, carry_ref, perm_ref = rest[9:]
    j = pl.program_id(1)
    tm = x_ref.shape[1]
    u = (x_ref[0] * (1.0 + sc_ref[0]) + sh_ref[0]).astype(BF16)

    ff = _dot(u, wfg_ref[...]) + bfg_ref[...]
    lf = jnp.minimum(ff, 0.0) - jnp.log(1.0 + jnp.exp(-jnp.abs(ff)))
    r = lax.broadcasted_iota(jnp.int32, (tm, tm), 0)
    c = lax.broadcasted_iota(jnp.int32, (tm, tm), 1)
    tri = jnp.where(r >= c, 1.0, 0.0).astype(BF16)
    hi, mid, lo = _split3(lf)
    csum = _dot(tri, hi) + _dot(tri, mid) + _dot(tri, lo)

    @pl.when(j == 0)
    def _():
        carry_ref[...] = jnp.zeros_like(carry_ref)

    cum = csum + carry_ref[...]
    carry_ref[...] = cum[tm - 1:tm, :]

    lane = lax.broadcasted_iota(jnp.int32, (1, LANES), 1)
    c_hi, c_mid, c_lo = (t.astype(F32) for t in _split3(cum * LOG2_E))
    parts = jnp.where(lane < N_HEADS, c_hi,
                      jnp.where(lane < 2 * N_HEADS, pltpu.roll(c_mid, N_HEADS, 1),
                                jnp.where(lane < 3 * N_HEADS, pltpu.roll(c_lo, 2 * N_HEADS, 1), 0.0))).astype(BF16)
    bias_q = _dot(parts, selq_ref[...]) + oneq_ref[...]
    bias_k = _dot(parts, selk_ref[...]) + onek_ref[...]
    pf = _dot(u, wfox_ref[...])
    data = lane < HEAD_DIM
    for dst, off, bias, scale in ((qa_ref, 0, bias_q, SCORE_SCALE * LOG2_E), (ka_ref, HEADS_WIDTH, bias_k, 1.0)):
        for p in range(N_PAIRS):
            pair = pf[:, off + p * PAIR: off + (p + 1) * PAIR] * scale
            for hh, src in ((0, pair), (1, pltpu.roll(pair, HEAD_DIM, 1))):
                t = (2 * p + hh) * LANES
                dst[0, :, t:t + LANES] = jnp.where(data, src, bias[:, t:t + LANES]).astype(BF16)
    vt = pf[:, 2 * HEADS_WIDTH:].T
    for cb in range(tm // FOX_K):
        vt_ref[0, cb] = vt[:, cb * FOX_K:(cb + 1) * FOX_K].astype(BF16)

    pd = _dot(u, wdil_ref[...])
    rows = pl.ds(pl.multiple_of(j * tm, tm), tm)
    cos = cos_ref[rows, :]
    sina = sina_ref[rows, :]
    sinb = sinb_ref[rows, :]
    for i, off in ((0, 0), (1, HEADS_WIDTH)):
        for g in range(N_PAIRS):
            xg = pd[:, off + g * LANES: off + (g + 1) * LANES]
            ahead = pltpu.roll(xg, LANES - HEAD_DIM // 2, 1)
            behind = pltpu.roll(xg, HEAD_DIM // 2, 1)
            rot = xg * cos + ahead * sina + behind * sinb
            perm_ref[i, g] = rot * SCORE_SCALE if i == 0 else rot
    for g in range(N_PAIRS):
        perm_ref[2, g] = pd[:, 2 * HEADS_WIDTH + g * LANES:2 * HEADS_WIDTH + (g + 1) * LANES]
    for i in range(3):
        for (_, dilation), dst in zip(DIL_PATTERNS, dil_refs[i::3]):
            for res in range(dilation):
                for g in range(N_PAIRS):
                    dst[0, res, :, g * LANES:(g + 1) * LANES] = perm_ref[
                        i, g, pl.ds(res, tm // dilation, stride=dilation), :].astype(BF16)

    pg = _sigmoid(_dot(u, wgate_ref[...]) + bgate_ref[...])
    gf_ref[0] = pg[:, :D_MODEL].astype(BF16)
    gd_ref[0] = pg[:, D_MODEL:].astype(BF16)


def _inproj(x, mod3, wfox, wdil, wgate, wfg, bfg, bgate, cos_t, sina_t, sinb_t, selq, selk, oneq, onek):
    bsz, seq, d = x.shape
    tm = ROW_TILE
    const = lambda shape: pl.BlockSpec(shape, lambda b, j: (0,) * len(shape))
    tok = lambda w: pl.BlockSpec((1, tm, w), lambda b, j: (b, j, 0))
    hw = HEADS_WIDTH
    aw = N_HEADS * LANES
    dil_shapes, dil_specs = [], []
    for _, dil in DIL_PATTERNS:
        dil_shapes += [jax.ShapeDtypeStruct((bsz, dil, seq // dil, hw), BF16)] * 3
        dil_specs += [pl.BlockSpec((1, dil, tm // dil, hw), lambda b, j: (b, 0, j, 0))] * 3
    outs = ([jax.ShapeDtypeStruct((bsz, seq, aw), BF16)] * 2
            + [jax.ShapeDtypeStruct((bsz, seq // FOX_K, hw, FOX_K), BF16)]
            + dil_shapes + [jax.ShapeDtypeStruct((bsz, seq, d), BF16)] * 2)
    return pl.pallas_call(
        _inproj_kernel,
        grid=(bsz, seq // tm),
        in_specs=[
            tok(d),
            pl.BlockSpec((1, 1, d), lambda b, j: (b, 0, 1)),
            pl.BlockSpec((1, 1, d), lambda b, j: (b, 0, 0)),
            const(wfox.shape), const(wdil.shape), const(wgate.shape), const(wfg.shape),
            const(bfg.shape), const(bgate.shape),
            const(cos_t.shape), const(sina_t.shape), const(sinb_t.shape),
            const(selq.shape), const(selk.shape), const(oneq.shape), const(onek.shape),
        ],
        out_specs=[tok(aw)] * 2 + [pl.BlockSpec((1, tm // FOX_K, hw, FOX_K), lambda b, j: (b, j, 0, 0))]
                  + dil_specs + [tok(d)] * 2,
        out_shape=outs,
        scratch_shapes=[pltpu.VMEM((1, LANES), F32), pltpu.VMEM((3, N_PAIRS, tm, LANES), F32)],
        compiler_params=_params(("arbitrary", "arbitrary"), 56),
        name="inproj",
    )(x, mod3, mod3, wfox, wdil, wgate, wfg, bfg, bgate, cos_t, sina_t, sinb_t, selq, selk, oneq, onek)


def _fox_kernel(qa_ref, ka_ref, vt_ref, o_ref, m_ref, l_ref, acc_ref):
    seq = qa_ref.shape[1]
    tq, tk = FOX_Q, FOX_K
    k_pos = lax.broadcasted_iota(jnp.int32, (tk, tq), 0)
    q_pos = lax.broadcasted_iota(jnp.int32, (tk, tq), 1)
    ones = jnp.ones((16, tk), BF16)

    def q_body(qi, _):
        q0 = pl.multiple_of(qi * tq, tq)
        m_ref[...] = jnp.full(m_ref.shape, NEG_INF, F32)
        l_ref[...] = jnp.zeros(l_ref.shape, F32)
        acc_ref[...] = jnp.zeros(acc_ref.shape, F32)

        def kv_step(kb, k0, visible):
            scores = []
            for h in range(N_HEADS):
                ls = slice(h * LANES, (h + 1) * LANES)
                scores.append(_dot_nt(ka_ref[0, pl.ds(k0, tk), ls], qa_ref[0, pl.ds(q0, tq), ls]))
            m_all = m_ref[...]
            l_all = l_ref[...]
            probs, decay, m_rows, l_rows = [], [], [], []
            for h in range(N_HEADS):
                s = scores[h] if visible is None else jnp.where(visible, scores[h], NEG_INF)
                m_old = m_all[h:h + 1, :]
                m_new = jnp.maximum(m_old, jnp.max(s, axis=0, keepdims=True))
                m_rows.append(m_new)
                decay.append(jnp.exp2(m_old - m_new))
                probs.append(jnp.exp2((s - m_new).astype(BF16)))
            m_ref[...] = jnp.concatenate(m_rows, axis=0)
            for h in range(N_HEADS):
                vh = vt_ref[0, kb, h * HEAD_DIM:(h + 1) * HEAD_DIM, :]
                acc_ref[h] = decay[h] * acc_ref[h] + _dot(vh, probs[h])
                l_rows.append(decay[h] * l_all[h:h + 1, :] + _dot(ones, probs[h])[0:1])
            l_ref[...] = jnp.concatenate(l_rows, axis=0)

        def full_block(jb, carry):
            kv_step(jb, pl.multiple_of(jb * tk, tk), None)
            return carry

        lax.fori_loop(0, qi * (tq // tk), full_block, 0)
        for dblk in range(tq // tk):
            kb = qi * (tq // tk) + dblk
            kv_step(kb, pl.multiple_of(kb * tk, tk), k_pos + dblk * tk <= q_pos)

        for p in range(N_PAIRS):
            pair = jnp.concatenate([acc_ref[2 * p + hh] / l_ref[2 * p + hh:2 * p + hh + 1, :] for hh in range(2)],
                                   axis=0)
            o_ref[0, pl.ds(q0, tq), p * PAIR:(p + 1) * PAIR] = pair.T.astype(BF16)
        return 0

    lax.fori_loop(0, seq // tq, q_body, 0)


def _fox(qa, ka, vt):
    bsz, seq, aw = qa.shape
    hw = HEADS_WIDTH
    full = pl.BlockSpec((1, seq, aw), lambda b: (b, 0, 0))
    return pl.pallas_call(
        _fox_kernel,
        grid=(bsz,),
        in_specs=[full, full, pl.BlockSpec((1, seq // FOX_K, hw, FOX_K), lambda b: (b, 0, 0, 0))],
        out_specs=pl.BlockSpec((1, seq, hw), lambda b: (b, 0, 0)),
        out_shape=jax.ShapeDtypeStruct((bsz, seq, hw), BF16),
        scratch_shapes=[pltpu.VMEM((N_HEADS, FOX_Q), F32), pltpu.VMEM((N_HEADS, FOX_Q), F32),
                        pltpu.VMEM((N_HEADS, HEAD_DIM, FOX_Q), F32)],
        compiler_params=_params(("arbitrary",), 48),
        name="fox",
    )(qa, ka, vt)


def _dil_kernel(q_ref, kc_ref, kp_ref, vc_ref, vp_ref, o_ref, lse_ref):
    n = pl.program_id(2)
    blk = ATTN_BLOCK
    n_res = q_ref.shape[0]
    n_sub = q_ref.shape[1] // blk
    lane = lax.broadcasted_iota(jnp.int32, (1, PAIR), 1)
    first = lane < HEAD_DIM
    i = lax.broadcasted_iota(jnp.int32, (blk, 2 * blk), 0)
    c = lax.broadcasted_iota(jnp.int32, (blk, 2 * blk), 1)
    in_window = (c >= i) & (c <= i + blk)
    has_prev = (c >= blk) | (n > 0)

    for res in range(n_res):
        for sub in range(n_sub):
            rows = slice(sub * blk, (sub + 1) * blk)
            before = slice((sub - 1) * blk, sub * blk)
            valid = in_window & has_prev if sub == 0 else in_window
            scores, values = [], []
            for p in range(N_PAIRS):
                ls = slice(p * PAIR, (p + 1) * PAIR)
                qb = q_ref[res, rows, ls]
                zero = jnp.zeros_like(qb)
                k_prev = kp_ref[res, :, ls] if sub == 0 else kc_ref[res, before, ls]
                v_prev = vp_ref[res, :, ls] if sub == 0 else vc_ref[res, before, ls]
                kb = jnp.concatenate([k_prev, kc_ref[res, rows, ls]], axis=0)
                values.append(jnp.concatenate([v_prev, vc_ref[res, rows, ls]], axis=0))
                scores.append(_dot_nt(jnp.where(first, qb, zero), kb))
                scores.append(_dot_nt(jnp.where(first, zero, qb), kb))
            probs, denom, lses = [], [], []
            for s in scores:
                s = jnp.where(valid, s, NEG_INF)
                m = jnp.max(s, axis=-1, keepdims=True)
                e = jnp.exp(s - m)
                l = jnp.sum(e, axis=-1, keepdims=True)
                probs.append(e.astype(BF16))
                denom.append(l)
                lses.append(m + jnp.log(l))
            for p in range(N_PAIRS):
                ls = slice(p * PAIR, (p + 1) * PAIR)
                o0 = _dot(probs[2 * p], values[p]) / denom[2 * p]
                o1 = _dot(probs[2 * p + 1], values[p]) / denom[2 * p + 1]
                o_ref[res, rows, ls] = jnp.where(first, o0, o1).astype(BF16)
                lse_ref[res, rows, ls] = jnp.where(first, lses[2 * p], lses[2 * p + 1])


def _dil(qd, kd, vd):
    bsz, dilation, length, hw = qd.shape
    nb = length // ATTN_BLOCK
    n_sub = min(nb, DIL_STEP_BLOCKS)
    n_res = min(dilation, DIL_STEP_BLOCKS // n_sub)
    cur = pl.BlockSpec((None, n_res, n_sub * ATTN_BLOCK, hw), lambda b, r, n: (b, r, n, 0))
    prev = pl.BlockSpec((None, n_res, ATTN_BLOCK, hw), lambda b, r, n: (b, r, jnp.maximum(n * n_sub - 1, 0), 0))
    return pl.pallas_call(
        _dil_kernel,
        grid=(bsz, dilation // n_res, nb // n_sub),
        in_specs=[cur, cur, prev, cur, prev],
        out_specs=[cur, cur],
        out_shape=[jax.ShapeDtypeStruct(qd.shape, BF16), jax.ShapeDtypeStruct(qd.shape, F32)],
        compiler_params=_params(("arbitrary", "arbitrary", "arbitrary"), 32),
        name=f"dil{dilation}",
    )(qd, kd, kd, vd, vd)


def _mix_kernel(x_ref, yf_ref, o1_ref, o2_ref, o3_ref, l1_ref, l2_ref, l3_ref, gf_ref, gd_ref, g1_ref,
                wbf_ref, wbd_ref, wo_ref, lng_ref, lnb_ref, out_ref, nat_ref):
    tm = x_ref.shape[1]

    def natural(i, src_ref):
        dilation = src_ref.shape[1]
        for res in range(dilation):
            for g in range(N_PAIRS):
                nat_ref[i, g, pl.ds(res, tm // dilation, stride=dilation), :] = src_ref[
                    0, res, :, g * LANES:(g + 1) * LANES].astype(F32)
        return jnp.concatenate([nat_ref[i, g] for g in range(N_PAIRS)], axis=1)

    os_ = [natural(i, ref) for i, ref in enumerate((o1_ref, o2_ref, o3_ref))]
    l1, l2, l3 = [natural(3 + i, ref) for i, ref in enumerate((l1_ref, l2_ref, l3_ref))]
    mx = jnp.maximum(jnp.maximum(l1, l2), l3)
    e1, e2, e3 = jnp.exp(l1 - mx), jnp.exp(l2 - mx), jnp.exp(l3 - mx)
    yd = (e1 * os_[0] + e2 * os_[1] + e3 * os_[2]) / (e1 + e2 + e3)
    merged = (gf_ref[0].astype(F32) * _dot(yf_ref[0], wbf_ref[...])
              + gd_ref[0].astype(F32) * _dot(yd.astype(BF16), wbd_ref[...]))
    y = _dot(merged.astype(BF16), wo_ref[...])
    z = DN_ALPHA * x_ref[0] + (1.0 + g1_ref[0]) * y
    out_ref[0] = _layer_norm(z, lng_ref[...], lnb_ref[...])


def _mix(x, yf, os_, lses, gf, gd, mod3, wbf, wbd, wo, lng, lnb):
    bsz, seq, d = x.shape
    tm = ROW_TILE
    const = lambda shape: pl.BlockSpec(shape, lambda b, j: (0,) * len(shape))
    tok = lambda w: pl.BlockSpec((1, tm, w), lambda b, j: (b, j, 0))
    hw = HEADS_WIDTH
    res_major = [pl.BlockSpec((1, dil, tm // dil, hw), lambda b, j: (b, 0, j, 0)) for _, dil in DIL_PATTERNS]
    return pl.pallas_call(
        _mix_kernel,
        grid=(bsz, seq // tm),
        in_specs=[tok(d), tok(hw)] + res_major * 2 + [tok(d), tok(d),
                  pl.BlockSpec((1, 1, d), lambda b, j: (b, 0, 2)),
                  const(wbf.shape), const(wbd.shape), const(wo.shape), const(lng.shape), const(lnb.shape)],
        out_specs=tok(d),
        out_shape=jax.ShapeDtypeStruct((bsz, seq, d), F32),
        scratch_shapes=[pltpu.VMEM((6, N_PAIRS, tm, LANES), F32)],
        compiler_params=_params(("arbitrary", "arbitrary"), 48),
        name="mix",
    )(x, yf, *os_, *lses, gf, gd, mod3, wbf, wbd, wo, lng, lnb)


def _router_kernel(x_ref, sc_ref, sh_ref, wrh_ref, wrl_ref, bias_ref,
                   u_ref, up_ref, eidx_ref, rank_ref, gw_ref, cnt_ref):
    seq = x_ref.shape[1]
    tc = ROUTE_CHUNK
    e_iota = lax.broadcasted_iota(jnp.int32, (N_EXPERTS, tc), 0)
    g_iota = lax.broadcasted_iota(jnp.int32, (N_GROUPS, tc), 0)
    s_iota = lax.broadcasted_iota(jnp.int32, (GROUP_SIZE, tc), 0)
    r = lax.broadcasted_iota(jnp.int32, (tc, tc), 0)
    c = lax.broadcasted_iota(jnp.int32, (tc, tc), 1)
    upper = jnp.where(r <= c, 1.0, 0.0).astype(BF16)
    ones = jnp.ones((16, tc), BF16)
    carry = jnp.zeros((N_EXPERTS, 1), F32)
    cnt = jnp.zeros((16, N_EXPERTS), F32)

    for ch in range(seq // tc):
        rows = slice(ch * tc, (ch + 1) * tc)
        u = x_ref[0, rows, :] * (1.0 + sc_ref[0]) + sh_ref[0]
        u_ref[0, rows, :] = u.astype(BF16)
        up_ref[0, rows, :] = _pack_words(u)
        u_hi, u_lo = _split2(u)
        logits = _dot_nt(wrh_ref[...], u_hi) + _dot_nt(wrh_ref[...], u_lo) + _dot_nt(wrl_ref[...], u_hi)
        scores = _sigmoid(logits)
        sel = scores + bias_ref[...]

        gs_rows = []
        for g in range(N_GROUPS):
            blk = sel[g * GROUP_SIZE:(g + 1) * GROUP_SIZE]
            m1 = jnp.max(blk, axis=0, keepdims=True)
            i1 = jnp.min(jnp.where(blk == m1, s_iota, GROUP_SIZE), axis=0, keepdims=True)
            m2 = jnp.max(jnp.where(s_iota == i1, -jnp.inf, blk), axis=0, keepdims=True)
            gs_rows.append(m1 + m2)
        gs = jnp.concatenate(gs_rows, axis=0)
        beaten = jnp.zeros((N_GROUPS, tc), F32)
        for g in range(N_GROUPS):
            other = gs[g:g + 1]
            wins = (other > gs) | ((other == gs) & (g_iota > g))
            beaten = beaten + jnp.where(wins, 1.0, 0.0)
        keep = beaten < TOPK_GROUPS
        cand = jnp.concatenate(
            [jnp.where(keep[g:g + 1], sel[g * GROUP_SIZE:(g + 1) * GROUP_SIZE], NEG_INF) for g in range(N_GROUPS)],
            axis=0)
        chosen = jnp.zeros((N_EXPERTS, tc), F32)
        picks = []
        for _ in range(TOP_K):
            m = jnp.max(cand, axis=0, keepdims=True)
            idx = jnp.min(jnp.where(cand == m, e_iota, N_EXPERTS), axis=0, keepdims=True)
            hit = e_iota == idx
            chosen = jnp.where(hit, 1.0, chosen)
            cand = jnp.where(hit, -jnp.inf, cand)
            picks.append(idx)
        g_raw = jnp.where(chosen > 0.0, scores, 0.0)
        gate = g_raw / jnp.sum(g_raw, axis=0, keepdims=True) * ROUTED_SCALE

        chosen_b = chosen.astype(BF16)
        incl = _dot(chosen_b, upper) + carry
        carry = incl[:, tc - 1:tc]
        cnt = cnt + _dot_nt(ones, chosen_b)

        rank_rows, gate_rows = [], []
        for idx in picks:
            hit = e_iota == idx
            rank_rows.append(jnp.sum(jnp.where(hit, incl - 1.0, 0.0), axis=0, keepdims=True))
            gate_rows.append(jnp.sum(jnp.where(hit, gate, 0.0), axis=0, keepdims=True))
        eidx_ref[0, :, rows] = jnp.concatenate(picks, axis=0)
        rank_ref[0, :, rows] = jnp.concatenate(rank_rows, axis=0).astype(jnp.int32)
        gate_t = jnp.concatenate(gate_rows + [jnp.zeros((LANES - TOP_K, tc), F32)], axis=0).T
        gw_ref[0, rows, :] = gate_t[:, :TOP_K]

    cnt_ref[0] = cnt[0:1].astype(jnp.int32)


def _router(x1, mod3, wr_hi, wr_lo, bias):
    bsz, seq, d = x1.shape
    full = pl.BlockSpec((1, seq, d), lambda b: (b, 0, 0))
    ks = pl.BlockSpec((1, TOP_K, seq), lambda b: (b, 0, 0))
    return pl.pallas_call(
        _router_kernel,
        grid=(bsz,),
        in_specs=[full,
                  pl.BlockSpec((1, 1, d), lambda b: (b, 0, 4)),
                  pl.BlockSpec((1, 1, d), lambda b: (b, 0, 3)),
                  pl.BlockSpec(wr_hi.shape, lambda b: (0, 0)),
                  pl.BlockSpec(wr_lo.shape, lambda b: (0, 0)),
                  pl.BlockSpec(bias.shape, lambda b: (0, 0))],
        out_specs=[full, pl.BlockSpec((1, seq, PACKED), lambda b: (b, 0, 0)), ks, ks,
                   pl.BlockSpec((1, seq, TOP_K), lambda b: (b, 0, 0)),
                   pl.BlockSpec((1, 1, N_EXPERTS), lambda b: (b, 0, 0))],
        out_shape=[jax.ShapeDtypeStruct((bsz, seq, d), BF16),
                   jax.ShapeDtypeStruct((bsz, seq, PACKED), F32),
                   jax.ShapeDtypeStruct((bsz, TOP_K, seq), jnp.int32),
                   jax.ShapeDtypeStruct((bsz, TOP_K, seq), jnp.int32),
                   jax.ShapeDtypeStruct((bsz, seq, TOP_K), F32),
                   jax.ShapeDtypeStruct((bsz, 1, N_EXPERTS), jnp.int32)],
        compiler_params=_params(("arbitrary",), 48),
        name="router",
    )(x1, mod3, mod3, wr_hi, wr_lo, bias)


def _slot_kernel(base_ref, eidx_ref, rank_ref, slot_ref):
    b = pl.program_id(0)
    eidx = eidx_ref[0]
    start = jnp.zeros(eidx.shape, jnp.int32)
    for e in range(N_EXPERTS):
        start = jnp.where(eidx == e, base_ref[b * N_EXPERTS + e], start)
    slot_ref[0] = start + rank_ref[0]


def _slots(base, eidx, rank):
    bsz, _, seq = eidx.shape
    ks = pl.BlockSpec((1, TOP_K, seq), lambda b, c: (b, 0, 0))
    return pl.pallas_call(
        _slot_kernel,
        grid_spec=pltpu.PrefetchScalarGridSpec(num_scalar_prefetch=1, grid=(bsz,), in_specs=[ks, ks], out_specs=ks),
        out_shape=jax.ShapeDtypeStruct((bsz, TOP_K, seq), jnp.int32),
        compiler_params=_params(("arbitrary",), 16),
        name="slots",
    )(base.reshape(bsz * N_EXPERTS), eidx, rank)


def _sc_mesh():
    return plsc.VectorSubcoreMesh(core_axis_name="core", subcore_axis_name="subcore")


def _sc_worker():
    return lax.axis_index("subcore") * SC_CORES + lax.axis_index("core")


def _sc_dispatch(rows, slot, n_slots):
    n_tok, width = rows.shape
    seq = slot.shape[1]
    chunk = DISPATCH_CHUNK
    per_worker = n_tok // SC_WORKERS
    assert per_worker % chunk == 0 and seq % chunk == 0

    @functools.partial(
        pl.kernel, mesh=_sc_mesh(),
        out_type=jax.ShapeDtypeStruct((n_slots, width), rows.dtype),
        scratch_types=[pltpu.VMEM((TOP_K, chunk), jnp.int32), pltpu.VMEM((chunk, width), rows.dtype),
                       pltpu.SemaphoreType.DMA],
    )
    def dispatch(rows_hbm, slot_hbm, out_hbm, idx_v, rows_v, sem):
        base = _sc_worker() * per_worker

        @pl.loop(0, per_worker // chunk)
        def _(i):
            off = base + i * chunk
            b = off // seq
            n0 = off - b * seq
            pltpu.sync_copy(slot_hbm.at[pl.ds(b * TOP_K, TOP_K), pl.ds(n0, chunk)], idx_v)
            pltpu.sync_copy(rows_hbm.at[pl.ds(off, chunk)], rows_v)
            copies = [pltpu.async_copy(rows_v, out_hbm.at[idx_v.at[k]], sem) for k in range(TOP_K)]
            for cp in copies:
                cp.wait()

    return dispatch(rows, slot)


def _sc_gather(table, idx):
    n_out = idx.shape[0]
    width = table.shape[1]
    chunk = GATHER_CHUNK
    per_worker = n_out // SC_WORKERS
    steps = per_worker // chunk
    assert per_worker % chunk == 0 and steps % 2 == 0

    @functools.partial(
        pl.kernel, mesh=_sc_mesh(),
        out_type=jax.ShapeDtypeStruct((n_out, width), table.dtype),
        scratch_types=[pltpu.VMEM((2, chunk), jnp.int32), pltpu.VMEM((2, chunk, width), table.dtype),
                       pltpu.SemaphoreType.DMA, pltpu.SemaphoreType.DMA((2,))],
    )
    def gather(table_hbm, idx_hbm, out_hbm, idx_v, rows_v, gather_sem, out_sems):
        base = _sc_worker() * per_worker

        def write_out(buf, off):
            return pltpu.make_async_copy(rows_v.at[buf], out_hbm.at[pl.ds(off, chunk)], out_sems.at[buf])

        @pl.loop(0, steps, step=2)
        def _(i):
            for buf in range(2):
                off = base + (i + buf) * chunk

                @pl.when(i >= 2)
                def _():
                    write_out(buf, off - 2 * chunk).wait()

                pltpu.sync_copy(idx_hbm.at[pl.ds(off, chunk)], idx_v.at[buf])
                pltpu.async_copy(table_hbm.at[idx_v.at[buf]], rows_v.at[buf], gather_sem).wait()
                write_out(buf, off).start()

        for buf in range(2):
            write_out(buf, base + (steps - 2 + buf) * chunk).wait()

    return gather(table, idx)


def _gmm_kernel(blk_w_ref, used_ref, xs_ref, wg_ref, wu_ref, wd_ref, out_ref):
    half = PACKED

    @pl.when(pl.program_id(0) < used_ref[0])
    def _():
        lo, hi = (t.astype(BF16) for t in _unpack_words(xs_ref[...]))
        hg = _dot(lo, wg_ref[0, :half, :]) + _dot(hi, wg_ref[0, half:, :])
        hu = _dot(lo, wu_ref[0, :half, :]) + _dot(hi, wu_ref[0, half:, :])
        h = (hg * _sigmoid(hg) * hu).astype(BF16)
        out_ref[...] = _pack_words(_dot(h, wd_ref[0]))


def _gmm(xs, blk_w, used, wg, wu, wd):
    n_slots, width = xs.shape
    bm = EXPERT_BLOCK
    d = wg.shape[1]
    live = lambda i, blk_w, used: (jnp.minimum(i, used[0] - 1), 0)
    slab = lambda i, blk_w, used: (blk_w[i], 0, 0)
    return pl.pallas_call(
        _gmm_kernel,
        grid_spec=pltpu.PrefetchScalarGridSpec(
            num_scalar_prefetch=2,
            grid=(n_slots // bm,),
            in_specs=[pl.BlockSpec((bm, width), live),
                      pl.BlockSpec((1, d, EXPERT_FF), slab),
                      pl.BlockSpec((1, d, EXPERT_FF), slab),
                      pl.BlockSpec((1, EXPERT_FF, d), slab)],
            out_specs=pl.BlockSpec((bm, width), live)),
        out_shape=jax.ShapeDtypeStruct((n_slots, width), F32),
        compiler_params=_params(("arbitrary",), 32),
        name="gmm",
    )(blk_w, used, xs, wg, wu, wd)


def _ffn_out_kernel(x_ref, u_ref, yg_ref, gw_ref, g2_ref, wsgu_ref, wsd_ref, lng_ref, lnb_ref, out_ref):
    hgu = _dot(u_ref[0], wsgu_ref[...])
    hg = hgu[:, :SHARED_FF]
    h = (hg * _sigmoid(hg) * hgu[:, SHARED_FF:]).astype(BF16)
    shared = _dot(h, wsd_ref[...])
    gw = gw_ref[0]
    acc_lo = shared[:, :PACKED]
    acc_hi = shared[:, PACKED:]
    for k in range(TOP_K):
        lo, hi = _unpack_words(yg_ref[0, k])
        w = gw[:, k:k + 1]
        acc_lo = acc_lo + w * lo
        acc_hi = acc_hi + w * hi
    y = jnp.concatenate([acc_lo, acc_hi], axis=1)
    z = DN_ALPHA * x_ref[0] + (1.0 + g2_ref[0]) * y
    out_ref[0] = _layer_norm(z, lng_ref[...], lnb_ref[...])


def _ffn_out(x1, u2, yg, gw, mod3, wsgu, wsd, lng, lnb):
    bsz, seq, d = x1.shape
    tm = ROW_TILE
    const = lambda shape: pl.BlockSpec(shape, lambda b, j: (0,) * len(shape))
    tok = pl.BlockSpec((1, tm, d), lambda b, j: (b, j, 0))
    return pl.pallas_call(
        _ffn_out_kernel,
        grid=(bsz, seq // tm),
        in_specs=[tok, tok,
                  pl.BlockSpec((1, TOP_K, tm, PACKED), lambda b, j: (b, 0, j, 0)),
                  pl.BlockSpec((1, tm, TOP_K), lambda b, j: (b, j, 0)),
                  pl.BlockSpec((1, 1, d), lambda b, j: (b, 0, 5)),
                  const(wsgu.shape), const(wsd.shape), const(lng.shape), const(lnb.shape)],
        out_specs=tok,
        out_shape=jax.ShapeDtypeStruct((bsz, seq, d), F32),
        compiler_params=_params(("arbitrary", "arbitrary"), 48),
        name="ffn_out",
    )(x1, u2, yg, gw, mod3, wsgu, wsd, lng, lnb)


def _moe_layout(cnt, n_blocks):
    bm = EXPERT_BLOCK
    total = jnp.sum(cnt, axis=0)
    padded = (total + bm - 1) // bm * bm
    ends = jnp.cumsum(padded)
    base = (ends - padded)[None, :] + jnp.cumsum(cnt, axis=0) - cnt
    first_row = jnp.arange(n_blocks, dtype=ends.dtype) * bm
    blk_e = jnp.minimum(jnp.sum(ends[None, :] <= first_row[:, None], axis=1), N_EXPERTS - 1)
    used = (ends[-1:] // bm)
    return base.astype(jnp.int32), blk_e.astype(jnp.int32), used.astype(jnp.int32)


def _rope_tables(seq):
    half = HEAD_DIM // 2
    inv_freq = ROPE_THETA ** (-jnp.arange(half, dtype=F32) * 2.0 / HEAD_DIM)
    ang = jnp.arange(seq, dtype=F32)[:, None] * inv_freq[None, :]
    cos, sin = jnp.cos(ang), jnp.sin(ang)
    zero = jnp.zeros_like(sin)
    reps = LANES // HEAD_DIM
    cos_t = jnp.tile(jnp.concatenate([cos, cos], axis=1), (1, reps))
    sina_t = jnp.tile(jnp.concatenate([-sin, zero], axis=1), (1, reps))
    sinb_t = jnp.tile(jnp.concatenate([zero, sin], axis=1), (1, reps))
    return cos_t, sina_t, sinb_t


def _fox_bias_tables():
    n_split = 3
    aw = N_HEADS * LANES
    heads = jnp.arange(N_HEADS)
    selq = jnp.zeros((LANES, aw), F32)
    selk = jnp.zeros((LANES, aw), F32)
    oneq = jnp.zeros((1, aw), F32)
    onek = jnp.zeros((1, aw), F32)
    for j in range(n_split):
        selk = selk.at[j * N_HEADS + heads, heads * LANES + FOX_BIAS_LANE + j].set(-1.0)
        selq = selq.at[j * N_HEADS + heads, heads * LANES + FOX_BIAS_LANE + n_split + j].set(1.0)
        oneq = oneq.at[0, heads * LANES + FOX_BIAS_LANE + j].set(1.0)
        onek = onek.at[0, heads * LANES + FOX_BIAS_LANE + n_split + j].set(1.0)
    return selq.astype(BF16), selk.astype(BF16), oneq, onek


def kernel(x, c, w_ada, b_ada, w_in, b_forget, b_gate, w_br_fox, w_br_dil, w_o, ln_g, ln_b, w_router, router_bias,
           w_exp_gate, w_exp_up, w_exp_down, w_sh_gate, w_sh_up, w_sh_down):
    bsz, seq, d = x.shape
    depth = w_ada.shape[0]
    hw = HEADS_WIDTH
    n_tok = bsz * seq

    mod = _ada(c, w_ada, b_ada)

    o_f, o_d, o_g = 3 * hw, 3 * hw + N_HEADS, 6 * hw + N_HEADS
    wfox = w_in[:, :, :o_f].astype(BF16)
    wfg = jnp.pad(w_in[:, :, o_f:o_d], ((0, 0), (0, 0), (0, LANES - N_HEADS))).astype(BF16)
    wdil = w_in[:, :, o_d:o_g].astype(BF16)
    wgate = w_in[:, :, o_g:].astype(BF16)
    bfg = jnp.pad(b_forget, ((0, 0), (0, LANES - N_HEADS))).reshape(depth, 1, LANES)
    bgate = b_gate.reshape(depth, 1, 2 * d)
    wr_t = jnp.swapaxes(w_router, 1, 2)
    wr_hi = wr_t.astype(BF16)
    wr_lo = (wr_t - wr_hi.astype(F32)).astype(BF16)
    rbias = router_bias.reshape(depth, N_EXPERTS, 1)
    wg = w_exp_gate.astype(BF16).reshape(depth * N_EXPERTS, d, EXPERT_FF)
    wu = w_exp_up.astype(BF16).reshape(depth * N_EXPERTS, d, EXPERT_FF)
    wd = w_exp_down.astype(BF16).reshape(depth * N_EXPERTS, EXPERT_FF, d)
    wsgu = jnp.concatenate([w_sh_gate, w_sh_up], axis=-1).astype(BF16)
    wsd = w_sh_down.astype(BF16)
    cos_t, sina_t, sinb_t = _rope_tables(seq)
    selq, selk, oneq, onek = _fox_bias_tables()

    layer_params = (mod, wfox, wdil, wgate, wfg, bfg, bgate, w_br_fox.astype(BF16), w_br_dil.astype(BF16),
                    w_o.astype(BF16), ln_g, ln_b, wr_hi, wr_lo, rbias, wsgu, wsd,
                    jnp.arange(depth, dtype=jnp.int32) * N_EXPERTS)

    n_streams = 2 if n_tok % (2 * SC_WORKERS * DISPATCH_CHUNK) == 0 and bsz % 2 == 0 else 1
    sb = bsz // n_streams
    s_tok = sb * seq
    n_blocks = s_tok * TOP_K // EXPERT_BLOCK + N_EXPERTS

    def stream_layer(xc, mod3, prm):
        (_, wfox_l, wdil_l, wgate_l, wfg_l, bfg_l, bgate_l, wbf_l, wbd_l, wo_l, lng_l, lnb_l,
         wrh_l, wrl_l, rb_l, wsgu_l, wsd_l, first_slab) = prm
        qa, ka, vt, *rest = _inproj(
            xc, mod3, wfox_l, wdil_l, wgate_l, wfg_l, bfg_l, bgate_l, cos_t, sina_t, sinb_t, selq, selk, oneq, onek)
        gf, gd = rest[9:]
        yf = _fox(qa, ka, vt)
        dil = [_dil(*rest[3 * i:3 * i + 3]) for i in range(len(DIL_PATTERNS))]
        x1 = _mix(xc, yf, [o for o, _ in dil], [l for _, l in dil], gf, gd, mod3, wbf_l, wbd_l, wo_l,
                  lng_l[0:1], lnb_l[0:1])
        u2, u2p, eidx, rank, gw, cnt = _router(x1, mod3, wrh_l, wrl_l, rb_l)
        base, blk_e, used = _moe_layout(cnt.reshape(sb, N_EXPERTS), n_blocks)
        slot = _slots(base, eidx, rank)
        xs = _sc_dispatch(u2p.reshape(s_tok, PACKED), slot.reshape(sb * TOP_K, seq), n_blocks * EXPERT_BLOCK)
        ys = _gmm(xs, blk_e + first_slab, used, wg, wu, wd)
        yg = _sc_gather(ys, slot.reshape(s_tok * TOP_K)).reshape(sb, TOP_K, seq, PACKED)
        return _ffn_out(x1, u2, yg, gw, mod3, wsgu_l, wsd_l, lng_l[1:2], lnb_l[1:2])

    def layer(xcs, prm):
        mod3 = prm[0].reshape(bsz, 1, 6 * d)
        return tuple(stream_layer(xc, mod3[i * sb:(i + 1) * sb], prm) for i, xc in enumerate(xcs)), None

    outs, _ = lax.scan(layer, tuple(x[i * sb:(i + 1) * sb] for i in range(n_streams)), layer_params)
    return outs[0] if n_streams == 1 else jnp.concatenate(outs, axis=0)
```

```python
import functools

import jax
import jax.numpy as jnp
from jax import lax
from jax.experimental import pallas as pl
from jax.experimental.pallas import tpu as pltpu
from jax.experimental.pallas import tpu_sc as plsc

D_MODEL = 1024
DEPTH = 4
HEAD_DIM = 64
N_HEADS = 8
HEADS_WIDTH = N_HEADS * HEAD_DIM
DIL_PATTERNS = ((128, 1), (512, 4), (2048, 16))
ATTN_BLOCK = 128
ROPE_THETA = 10000.0
N_EXPERTS = 64
TOP_K = 8
N_GROUPS = 8
GROUP_SIZE = N_EXPERTS // N_GROUPS
TOPK_GROUPS = 4
EXPERT_FF = 256
SHARED_FF = 256
ROUTED_SCALE = 2.5
DN_ALPHA = (2 * DEPTH) ** 0.25
LN_EPS = 1e-5
NEG_INF = -1e30
SCORE_SCALE = HEAD_DIM ** -0.5
LOG2_E = 1.4426950408889634

LANES = 128
PAIR = 2 * HEAD_DIM
N_PAIRS = HEADS_WIDTH // PAIR

ROW_TILE = 512
FOX_Q = 256
FOX_K = 256
FOX_BIAS_LANE = HEAD_DIM
DIL_STEP_BLOCKS = 4
ROUTE_CHUNK = 512
EXPERT_BLOCK = 512
PACKED = D_MODEL // 2

SC_CORES = 2
SC_SUBCORES = 16
SC_WORKERS = SC_CORES * SC_SUBCORES
DISPATCH_CHUNK = 128
GATHER_CHUNK = 64

BF16 = jnp.bfloat16
F32 = jnp.float32


def _dot(a, b):
    return jnp.dot(a, b, preferred_element_type=F32)


def _dot_nt(a, b):
    return lax.dot_general(a, b, (((1,), (1,)), ((), ())), preferred_element_type=F32)


def _split2(x):
    hi = x.astype(BF16)
    lo = (x - hi.astype(F32)).astype(BF16)
    return hi, lo


def _split3(x):
    hi = x.astype(BF16)
    r = x - hi.astype(F32)
    mid = r.astype(BF16)
    lo = (r - mid.astype(F32)).astype(BF16)
    return hi, mid, lo


def _sigmoid(x):
    return 1.0 / (1.0 + jnp.exp(-x))


def _layer_norm(z, g, b):
    mu = jnp.mean(z, axis=-1, keepdims=True)
    zc = z - mu
    var = jnp.mean(zc * zc, axis=-1, keepdims=True)
    return zc * lax.rsqrt(var + LN_EPS) * g + b


def _pack_words(v):
    half = v.shape[1] // 2
    bits = pltpu.bitcast(v.astype(BF16).astype(F32), jnp.uint32)
    word = (bits[:, half:] & jnp.uint32(0xFFFF0000)) | (bits[:, :half] >> 16)
    return pltpu.bitcast(word, F32)


def _unpack_words(w):
    bits = pltpu.bitcast(w, jnp.uint32)
    return pltpu.bitcast(bits << 16, F32), pltpu.bitcast(bits & jnp.uint32(0xFFFF0000), F32)


def _params(sem, vmem_mb):
    return pltpu.CompilerParams(dimension_semantics=sem, vmem_limit_bytes=vmem_mb * 1024 * 1024)


def _ada_kernel(c_ref, w_ref, b_ref, o_ref):
    c = c_ref[...]
    cs = c * _sigmoid(c)
    c_hi, c_lo = _split2(cs)
    w_hi, w_lo = _split2(w_ref[0])
    o_ref[0] = _dot(c_hi, w_hi) + _dot(c_hi, w_lo) + _dot(c_lo, w_hi) + b_ref[0]


def _ada(c, w_ada, b_ada):
    depth, d, n = w_ada.shape
    bsz = c.shape[0]
    tn = 1536
    return pl.pallas_call(
        _ada_kernel,
        grid=(depth, n // tn),
        in_specs=[
            pl.BlockSpec((bsz, d), lambda l, j: (0, 0)),
            pl.BlockSpec((1, d, tn), lambda l, j: (l, 0, j)),
            pl.BlockSpec((1, 1, tn), lambda l, j: (l, 0, j)),
        ],
        out_specs=pl.BlockSpec((1, bsz, tn), lambda l, j: (l, 0, j)),
        out_shape=jax.ShapeDtypeStruct((depth, bsz, n), F32),
        compiler_params=_params(("arbitrary", "arbitrary"), 40),
        name="ada",
    )(c, w_ada, b_ada.reshape(depth, 1, n))


def _inproj_kernel(x_ref, sc_ref, sh_ref, wfox_ref, wdil_ref, wgate_ref, wfg_ref, bfg_ref, bgate_ref,
                   cos_ref, sina_ref, sinb_ref, selq_ref, selk_ref, oneq_ref, onek_ref,
                   qa_ref, ka_ref, vt_ref, *rest):
    dil_refs = rest[:9]
    gf_ref, gd_ref, carry_ref, perm_ref = rest[9:]
    j = pl.program_id(1)
    tm = x_ref.shape[1]
    u = (x_ref[0] * (1.0 + sc_ref[0]) + sh_ref[0]).astype(BF16)

    ff = _dot(u, wfg_ref[...]) + bfg_ref[...]
    lf = jnp.minimum(ff, 0.0) - jnp.log(1.0 + jnp.exp(-jnp.abs(ff)))
    r = lax.broadcasted_iota(jnp.int32, (tm, tm), 0)
    c = lax.broadcasted_iota(jnp.int32, (tm, tm), 1)
    tri = jnp.where(r >= c, 1.0, 0.0).astype(BF16)
    hi, mid, lo = _split3(lf)
    csum = _dot(tri, hi) + _dot(tri, mid) + _dot(tri, lo)

    @pl.when(j == 0)
    def _():
        carry_ref[...] = jnp.zeros_like(carry_ref)

    cum = csum + carry_ref[...]
    carry_ref[...] = cum[tm - 1:tm, :]

    lane = lax.broadcasted_iota(jnp.int32, (1, LANES), 1)
    c_hi, c_mid, c_lo = (t.astype(F32) for t in _split3(cum * LOG2_E))
    parts = jnp.where(lane < N_HEADS, c_hi,
                      jnp.where(lane < 2 * N_HEADS, pltpu.roll(c_mid, N_HEADS, 1),
                                jnp.where(lane < 3 * N_HEADS, pltpu.roll(c_lo, 2 * N_HEADS, 1), 0.0))).astype(BF16)
    bias_q = _dot(parts, selq_ref[...]) + oneq_ref[...]
    bias_k = _dot(parts, selk_ref[...]) + onek_ref[...]
    pf = _dot(u, wfox_ref[...])
    data = lane < HEAD_DIM
    for dst, off, bias, scale in ((qa_ref, 0, bias_q, SCORE_SCALE * LOG2_E), (ka_ref, HEADS_WIDTH, bias_k, 1.0)):
        for p in range(N_PAIRS):
            pair = pf[:, off + p * PAIR: off + (p + 1) * PAIR] * scale
            for hh, src in ((0, pair), (1, pltpu.roll(pair, HEAD_DIM, 1))):
                t = (2 * p + hh) * LANES
                dst[0, :, t:t + LANES] = jnp.where(data, src, bias[:, t:t + LANES]).astype(BF16)
    vt = pf[:, 2 * HEADS_WIDTH:].T
    for cb in range(tm // FOX_K):
        vt_ref[0, cb] = vt[:, cb * FOX_K:(cb + 1) * FOX_K].astype(BF16)

    pd = _dot(u, wdil_ref[...])
    rows = pl.ds(pl.multiple_of(j * tm, tm), tm)
    cos = cos_ref[rows, :]
    sina = sina_ref[rows, :]
    sinb = sinb_ref[rows, :]
    for i, off in ((0, 0), (1, HEADS_WIDTH)):
        for g in range(N_PAIRS):
            xg = pd[:, off + g * LANES: off + (g + 1) * LANES]
            ahead = pltpu.roll(xg, LANES - HEAD_DIM // 2, 1)
            behind = pltpu.roll(xg, HEAD_DIM // 2, 1)
            rot = xg * cos + ahead * sina + behind * sinb
            perm_ref[i, g] = rot * SCORE_SCALE if i == 0 else rot
    for g in range(N_PAIRS):
        perm_ref[2, g] = pd[:, 2 * HEADS_WIDTH + g * LANES:2 * HEADS_WIDTH + (g + 1) * LANES]
    for i in range(3):
        for (_, dilation), dst in zip(DIL_PATTERNS, dil_refs[i::3]):
            for res in range(dilation):
                for g in range(N_PAIRS):
                    dst[0, res, :, g * LANES:(g + 1) * LANES] = perm_ref[
                        i, g, pl.ds(res, tm // dilation, stride=dilation), :].astype(BF16)

    pg = _sigmoid(_dot(u, wgate_ref[...]) + bgate_ref[...])
    gf_ref[0] = pg[:, :D_MODEL].astype(BF16)
    gd_ref[0] = pg[:, D_MODEL:].astype(BF16)


def _inproj(x, first_seq, mod3, wfox, wdil, wgate, wfg, bfg, bgate, cos_t, sina_t, sinb_t, selq, selk, oneq, onek):
    _, seq, d = x.shape
    bsz = mod3.shape[0]
    tm = ROW_TILE
    const = lambda shape: pl.BlockSpec(shape, lambda b, j: (0,) * len(shape))
    tok = lambda w: pl.BlockSpec((1, tm, w), lambda b, j: (b, j, 0))
    hw = HEADS_WIDTH
    aw = N_HEADS * LANES
    dil_shapes, dil_specs = [], []
    for _, dil in DIL_PATTERNS:
        dil_shapes += [jax.ShapeDtypeStruct((bsz, dil, seq // dil, hw), BF16)] * 3
        dil_specs += [pl.BlockSpec((1, dil, tm // dil, hw), lambda b, j: (b, 0, j, 0))] * 3
    outs = ([jax.ShapeDtypeStruct((bsz, seq, aw), BF16)] * 2
            + [jax.ShapeDtypeStruct((bsz, seq // FOX_K, hw, FOX_K), BF16)]
            + dil_shapes + [jax.ShapeDtypeStruct((bsz, seq, d), BF16)] * 2)
    return pl.pallas_call(
        _inproj_kernel,
        grid=(bsz, seq // tm),
        in_specs=[
            pl.BlockSpec((1, tm, d), lambda b, j: (b + first_seq, j, 0)),
            pl.BlockSpec((1, 1, d), lambda b, j: (b, 0, 1)),
            pl.BlockSpec((1, 1, d), lambda b, j: (b, 0, 0)),
            const(wfox.shape), const(wdil.shape), const(wgate.shape), const(wfg.shape),
            const(bfg.shape), const(bgate.shape),
            const(cos_t.shape), const(sina_t.shape), const(sinb_t.shape),
            const(selq.shape), const(selk.shape), const(oneq.shape), const(onek.shape),
        ],
        out_specs=[tok(aw)] * 2 + [pl.BlockSpec((1, tm // FOX_K, hw, FOX_K), lambda b, j: (b, j, 0, 0))]
                  + dil_specs + [tok(d)] * 2,
        out_shape=outs,
        scratch_shapes=[pltpu.VMEM((1, LANES), F32), pltpu.VMEM((3, N_PAIRS, tm, LANES), F32)],
        compiler_params=_params(("arbitrary", "arbitrary"), 56),
        name="inproj",
    )(x, mod3, mod3, wfox, wdil, wgate, wfg, bfg, bgate, cos_t, sina_t, sinb_t, selq, selk, oneq, onek)


def _fox_kernel(qa_ref, ka_ref, vt_ref, o_ref, m_ref, l_ref, acc_ref):
    seq = qa_ref.shape[1]
    tq, tk = FOX_Q, FOX_K
    k_pos = lax.broadcasted_iota(jnp.int32, (tk, tq), 0)
    q_pos = lax.broadcasted_iota(jnp.int32, (tk, tq), 1)
    ones = jnp.ones((16, tk), BF16)

    def q_body(qi, _):
        q0 = pl.multiple_of(qi * tq, tq)
        m_ref[...] = jnp.full(m_ref.shape, NEG_INF, F32)
        l_ref[...] = jnp.zeros(l_ref.shape, F32)
        acc_ref[...] = jnp.zeros(acc_ref.shape, F32)

        def kv_step(kb, k0, visible):
            scores = []
            for h in range(N_HEADS):
                ls = slice(h * LANES, (h + 1) * LANES)
                scores.append(_dot_nt(ka_ref[0, pl.ds(k0, tk), ls], qa_ref[0, pl.ds(q0, tq), ls]))
            m_all = m_ref[...]
            l_all = l_ref[...]
            probs, decay, m_rows, l_rows = [], [], [], []
            for h in range(N_HEADS):
                s = scores[h] if visible is None else jnp.where(visible, scores[h], NEG_INF)
                m_old = m_all[h:h + 1, :]
                m_new = jnp.maximum(m_old, jnp.max(s, axis=0, keepdims=True))
                m_rows.append(m_new)
                decay.append(jnp.exp2(m_old - m_new))
                probs.append(jnp.exp2((s - m_new).astype(BF16)))
            m_ref[...] = jnp.concatenate(m_rows, axis=0)
            for h in range(N_HEADS):
                vh = vt_ref[0, kb, h * HEAD_DIM:(h + 1) * HEAD_DIM, :]
                acc_ref[h] = decay[h] * acc_ref[h] + _dot(vh, probs[h])
                l_rows.append(decay[h] * l_all[h:h + 1, :] + _dot(ones, probs[h])[0:1])
            l_ref[...] = jnp.concatenate(l_rows, axis=0)

        def full_block(jb, carry):
            kv_step(jb, pl.multiple_of(jb * tk, tk), None)
            return carry

        lax.fori_loop(0, qi * (tq // tk), full_block, 0)
        for dblk in range(tq // tk):
            kb = qi * (tq // tk) + dblk
            kv_step(kb, pl.multiple_of(kb * tk, tk), k_pos + dblk * tk <= q_pos)

        for p in range(N_PAIRS):
            pair = jnp.concatenate([acc_ref[2 * p + hh] / l_ref[2 * p + hh:2 * p + hh + 1, :] for hh in range(2)],
                                   axis=0)
            o_ref[0, pl.ds(q0, tq), p * PAIR:(p + 1) * PAIR] = pair.T.astype(BF16)
        return 0

    lax.fori_loop(0, seq // tq, q_body, 0)


def _fox(qa, ka, vt):
    bsz, seq, aw = qa.shape
    hw = HEADS_WIDTH
    full = pl.BlockSpec((1, seq, aw), lambda b: (b, 0, 0))
    return pl.pallas_call(
        _fox_kernel,
        grid=(bsz,),
        in_specs=[full, full, pl.BlockSpec((1, seq // FOX_K, hw, FOX_K), lambda b: (b, 0, 0, 0))],
        out_specs=pl.BlockSpec((1, seq, hw), lambda b: (b, 0, 0)),
        out_shape=jax.ShapeDtypeStruct((bsz, seq, hw), BF16),
        scratch_shapes=[pltpu.VMEM((N_HEADS, FOX_Q), F32), pltpu.VMEM((N_HEADS, FOX_Q), F32),
                        pltpu.VMEM((N_HEADS, HEAD_DIM, FOX_Q), F32)],
        compiler_params=_params(("arbitrary",), 48),
        name="fox",
    )(qa, ka, vt)


def _dil_kernel(q_ref, kc_ref, kp_ref, vc_ref, vp_ref, o_ref, lse_ref):
    n = pl.program_id(2)
    blk = ATTN_BLOCK
    n_res = q_ref.shape[0]
    n_sub = q_ref.shape[1] // blk
    lane = lax.broadcasted_iota(jnp.int32, (1, PAIR), 1)
    first = lane < HEAD_DIM
    i = lax.broadcasted_iota(jnp.int32, (blk, 2 * blk), 0)
    c = lax.broadcasted_iota(jnp.int32, (blk, 2 * blk), 1)
    in_window = (c >= i) & (c <= i + blk)
    has_prev = (c >= blk) | (n > 0)

    for res in range(n_res):
        for sub in range(n_sub):
            rows = slice(sub * blk, (sub + 1) * blk)
            before = slice((sub - 1) * blk, sub * blk)
            valid = in_window & has_prev if sub == 0 else in_window
            scores, values = [], []
            for p in range(N_PAIRS):
                ls = slice(p * PAIR, (p + 1) * PAIR)
                qb = q_ref[res, rows, ls]
                zero = jnp.zeros_like(qb)
                k_prev = kp_ref[res, :, ls] if sub == 0 else kc_ref[res, before, ls]
                v_prev = vp_ref[res, :, ls] if sub == 0 else vc_ref[res, before, ls]
                kb = jnp.concatenate([k_prev, kc_ref[res, rows, ls]], axis=0)
                values.append(jnp.concatenate([v_prev, vc_ref[res, rows, ls]], axis=0))
                scores.append(_dot_nt(jnp.where(first, qb, zero), kb))
                scores.append(_dot_nt(jnp.where(first, zero, qb), kb))
            probs, denom, lses = [], [], []
            for s in scores:
                s = jnp.where(valid, s, NEG_INF)
                m = jnp.max(s, axis=-1, keepdims=True)
                e = jnp.exp(s - m)
                l = jnp.sum(e, axis=-1, keepdims=True)
                probs.append(e.astype(BF16))
                denom.append(l)
                lses.append(m + jnp.log(l))
            for p in range(N_PAIRS):
                ls = slice(p * PAIR, (p + 1) * PAIR)
                o0 = _dot(probs[2 * p], values[p]) / denom[2 * p]
                o1 = _dot(probs[2 * p + 1], values[p]) / denom[2 * p + 1]
                o_ref[res, rows, ls] = jnp.where(first, o0, o1).astype(BF16)
                lse_ref[res, rows, ls] = jnp.where(first, lses[2 * p], lses[2 * p + 1])


def _dil(qd, kd, vd):
    bsz, dilation, length, hw = qd.shape
    nb = length // ATTN_BLOCK
    n_sub = min(nb, DIL_STEP_BLOCKS)
    n_res = min(dilation, DIL_STEP_BLOCKS // n_sub)
    cur = pl.BlockSpec((None, n_res, n_sub * ATTN_BLOCK, hw), lambda b, r, n: (b, r, n, 0))
    prev = pl.BlockSpec((None, n_res, ATTN_BLOCK, hw), lambda b, r, n: (b, r, jnp.maximum(n * n_sub - 1, 0), 0))
    return pl.pallas_call(
        _dil_kernel,
        grid=(bsz, dilation // n_res, nb // n_sub),
        in_specs=[cur, cur, prev, cur, prev],
        out_specs=[cur, cur],
        out_shape=[jax.ShapeDtypeStruct(qd.shape, BF16), jax.ShapeDtypeStruct(qd.shape, F32)],
        compiler_params=_params(("arbitrary", "arbitrary", "arbitrary"), 32),
        name=f"dil{dilation}",
    )(qd, kd, kd, vd, vd)


def _mix_kernel(x_ref, yf_ref, o1_ref, o2_ref, o3_ref, l1_ref, l2_ref, l3_ref, gf_ref, gd_ref, g1_ref,
                wbf_ref, wbd_ref, wo_ref, lng_ref, lnb_ref, out_ref, nat_ref):
    tm = x_ref.shape[1]

    def natural(i, src_ref):
        dilation = src_ref.shape[1]
        for res in range(dilation):
            for g in range(N_PAIRS):
                nat_ref[i, g, pl.ds(res, tm // dilation, stride=dilation), :] = src_ref[
                    0, res, :, g * LANES:(g + 1) * LANES].astype(F32)
        return jnp.concatenate([nat_ref[i, g] for g in range(N_PAIRS)], axis=1)

    os_ = [natural(i, ref) for i, ref in enumerate((o1_ref, o2_ref, o3_ref))]
    l1, l2, l3 = [natural(3 + i, ref) for i, ref in enumerate((l1_ref, l2_ref, l3_ref))]
    mx = jnp.maximum(jnp.maximum(l1, l2), l3)
    e1, e2, e3 = jnp.exp(l1 - mx), jnp.exp(l2 - mx), jnp.exp(l3 - mx)
    yd = (e1 * os_[0] + e2 * os_[1] + e3 * os_[2]) / (e1 + e2 + e3)
    merged = (gf_ref[0].astype(F32) * _dot(yf_ref[0], wbf_ref[...])
              + gd_ref[0].astype(F32) * _dot(yd.astype(BF16), wbd_ref[...]))
    y = _dot(merged.astype(BF16), wo_ref[...])
    z = DN_ALPHA * x_ref[0] + (1.0 + g1_ref[0]) * y
    out_ref[0] = _layer_norm(z, lng_ref[...], lnb_ref[...])


def _mix(x, first_seq, yf, os_, lses, gf, gd, mod3, wbf, wbd, wo, lng, lnb):
    _, seq, d = x.shape
    bsz = mod3.shape[0]
    tm = ROW_TILE
    const = lambda shape: pl.BlockSpec(shape, lambda b, j: (0,) * len(shape))
    tok = lambda w: pl.BlockSpec((1, tm, w), lambda b, j: (b, j, 0))
    hw = HEADS_WIDTH
    res_major = [pl.BlockSpec((1, dil, tm // dil, hw), lambda b, j: (b, 0, j, 0)) for _, dil in DIL_PATTERNS]
    return pl.pallas_call(
        _mix_kernel,
        grid=(bsz, seq // tm),
        in_specs=[pl.BlockSpec((1, tm, d), lambda b, j: (b + first_seq, j, 0)), tok(hw)] + res_major * 2 + [
                  tok(d), tok(d),
                  pl.BlockSpec((1, 1, d), lambda b, j: (b, 0, 2)),
                  const(wbf.shape), const(wbd.shape), const(wo.shape), const(lng.shape), const(lnb.shape)],
        out_specs=tok(d),
        out_shape=jax.ShapeDtypeStruct((bsz, seq, d), F32),
        scratch_shapes=[pltpu.VMEM((6, N_PAIRS, tm, LANES), F32)],
        compiler_params=_params(("arbitrary", "arbitrary"), 48),
        name="mix",
    )(x, yf, *os_, *lses, gf, gd, mod3, wbf, wbd, wo, lng, lnb)


def _router_kernel(x_ref, sc_ref, sh_ref, wrh_ref, wrl_ref, bias_ref,
                   u_ref, up_ref, eidx_ref, rank_ref, gw_ref, cnt_ref):
    seq = x_ref.shape[1]
    tc = ROUTE_CHUNK
    e_iota = lax.broadcasted_iota(jnp.int32, (N_EXPERTS, tc), 0)
    g_iota = lax.broadcasted_iota(jnp.int32, (N_GROUPS, tc), 0)
    s_iota = lax.broadcasted_iota(jnp.int32, (GROUP_SIZE, tc), 0)
    r = lax.broadcasted_iota(jnp.int32, (tc, tc), 0)
    c = lax.broadcasted_iota(jnp.int32, (tc, tc), 1)
    upper = jnp.where(r <= c, 1.0, 0.0).astype(BF16)
    ones = jnp.ones((16, tc), BF16)
    carry = jnp.zeros((N_EXPERTS, 1), F32)
    cnt = jnp.zeros((16, N_EXPERTS), F32)

    for ch in range(seq // tc):
        rows = slice(ch * tc, (ch + 1) * tc)
        u = x_ref[0, rows, :] * (1.0 + sc_ref[0]) + sh_ref[0]
        u_ref[0, rows, :] = u.astype(BF16)
        up_ref[0, rows, :] = _pack_words(u)
        u_hi, u_lo = _split2(u)
        logits = _dot_nt(wrh_ref[...], u_hi) + _dot_nt(wrh_ref[...], u_lo) + _dot_nt(wrl_ref[...], u_hi)
        scores = _sigmoid(logits)
        sel = scores + bias_ref[...]

        gs_rows = []
        for g in range(N_GROUPS):
            blk = sel[g * GROUP_SIZE:(g + 1) * GROUP_SIZE]
            m1 = jnp.max(blk, axis=0, keepdims=True)
            i1 = jnp.min(jnp.where(blk == m1, s_iota, GROUP_SIZE), axis=0, keepdims=True)
            m2 = jnp.max(jnp.where(s_iota == i1, -jnp.inf, blk), axis=0, keepdims=True)
            gs_rows.append(m1 + m2)
        gs = jnp.concatenate(gs_rows, axis=0)
        beaten = jnp.zeros((N_GROUPS, tc), F32)
        for g in range(N_GROUPS):
            other = gs[g:g + 1]
            wins = (other > gs) | ((other == gs) & (g_iota > g))
            beaten = beaten + jnp.where(wins, 1.0, 0.0)
        keep = beaten < TOPK_GROUPS
        cand = jnp.concatenate(
            [jnp.where(keep[g:g + 1], sel[g * GROUP_SIZE:(g + 1) * GROUP_SIZE], NEG_INF) for g in range(N_GROUPS)],
            axis=0)
        chosen = jnp.zeros((N_EXPERTS, tc), F32)
        picks = []
        for _ in range(TOP_K):
            m = jnp.max(cand, axis=0, keepdims=True)
            idx = jnp.min(jnp.where(cand == m, e_iota, N_EXPERTS), axis=0, keepdims=True)
            hit = e_iota == idx
            chosen = jnp.where(hit, 1.0, chosen)
            cand = jnp.where(hit, -jnp.inf, cand)
            picks.append(idx)
        g_raw = jnp.where(chosen > 0.0, scores, 0.0)
        gate = g_raw / jnp.sum(g_raw, axis=0, keepdims=True) * ROUTED_SCALE

        chosen_b = chosen.astype(BF16)
        incl = _dot(chosen_b, upper) + carry
        carry = incl[:, tc - 1:tc]
        cnt = cnt + _dot_nt(ones, chosen_b)

        rank_rows, gate_rows = [], []
        for idx in picks:
            hit = e_iota == idx
            rank_rows.append(jnp.sum(jnp.where(hit, incl - 1.0, 0.0), axis=0, keepdims=True))
            gate_rows.append(jnp.sum(jnp.where(hit, gate, 0.0), axis=0, keepdims=True))
        eidx_ref[0, :, rows] = jnp.concatenate(picks, axis=0)
        rank_ref[0, :, rows] = jnp.concatenate(rank_rows, axis=0).astype(jnp.int32)
        gate_t = jnp.concatenate(gate_rows + [jnp.zeros((LANES - TOP_K, tc), F32)], axis=0).T
        gw_ref[0, rows, :] = gate_t[:, :TOP_K]

    cnt_ref[0] = cnt[0:1].astype(jnp.int32)


def _router(x1, mod3, wr_hi, wr_lo, bias):
    bsz, seq, d = x1.shape
    full = pl.BlockSpec((1, seq, d), lambda b: (b, 0, 0))
    ks = pl.BlockSpec((1, TOP_K, seq), lambda b: (b, 0, 0))
    return pl.pallas_call(
        _router_kernel,
        grid=(bsz,),
        in_specs=[full,
                  pl.BlockSpec((1, 1, d), lambda b: (b, 0, 4)),
                  pl.BlockSpec((1, 1, d), lambda b: (b, 0, 3)),
                  pl.BlockSpec(wr_hi.shape, lambda b: (0, 0)),
                  pl.BlockSpec(wr_lo.shape, lambda b: (0, 0)),
                  pl.BlockSpec(bias.shape, lambda b: (0, 0))],
        out_specs=[full, pl.BlockSpec((1, seq, PACKED), lambda b: (b, 0, 0)), ks, ks,
                   pl.BlockSpec((1, seq, TOP_K), lambda b: (b, 0, 0)),
                   pl.BlockSpec((1, 1, N_EXPERTS), lambda b: (b, 0, 0))],
        out_shape=[jax.ShapeDtypeStruct((bsz, seq, d), BF16),
                   jax.ShapeDtypeStruct((bsz, seq, PACKED), F32),
                   jax.ShapeDtypeStruct((bsz, TOP_K, seq), jnp.int32),
                   jax.ShapeDtypeStruct((bsz, TOP_K, seq), jnp.int32),
                   jax.ShapeDtypeStruct((bsz, seq, TOP_K), F32),
                   jax.ShapeDtypeStruct((bsz, 1, N_EXPERTS), jnp.int32)],
        compiler_params=_params(("arbitrary",), 48),
        name="router",
    )(x1, mod3, mod3, wr_hi, wr_lo, bias)


def _slot_kernel(base_ref, eidx_ref, rank_ref, slot_ref):
    b = pl.program_id(0)
    eidx = eidx_ref[0]
    start = jnp.zeros(eidx.shape, jnp.int32)
    for e in range(N_EXPERTS):
        start = jnp.where(eidx == e, base_ref[b * N_EXPERTS + e], start)
    slot_ref[0] = start + rank_ref[0]


def _slots(base, eidx, rank):
    bsz, _, seq = eidx.shape
    ks = pl.BlockSpec((1, TOP_K, seq), lambda b, c: (b, 0, 0))
    return pl.pallas_call(
        _slot_kernel,
        grid_spec=pltpu.PrefetchScalarGridSpec(num_scalar_prefetch=1, grid=(bsz,), in_specs=[ks, ks], out_specs=ks),
        out_shape=jax.ShapeDtypeStruct((bsz, TOP_K, seq), jnp.int32),
        compiler_params=_params(("arbitrary",), 16),
        name="slots",
    )(base.reshape(bsz * N_EXPERTS), eidx, rank)


def _sc_mesh():
    return plsc.VectorSubcoreMesh(core_axis_name="core", subcore_axis_name="subcore")


def _sc_worker():
    return lax.axis_index("subcore") * SC_CORES + lax.axis_index("core")


def _sc_dispatch(rows, slot, n_slots):
    n_tok, width = rows.shape
    seq = slot.shape[1]
    chunk = DISPATCH_CHUNK
    per_worker = n_tok // SC_WORKERS
    assert per_worker % chunk == 0 and seq % chunk == 0

    @functools.partial(
        pl.kernel, mesh=_sc_mesh(),
        out_type=jax.ShapeDtypeStruct((n_slots, width), rows.dtype),
        scratch_types=[pltpu.VMEM((TOP_K, chunk), jnp.int32), pltpu.VMEM((chunk, width), rows.dtype),
                       pltpu.SemaphoreType.DMA],
    )
    def dispatch(rows_hbm, slot_hbm, out_hbm, idx_v, rows_v, sem):
        base = _sc_worker() * per_worker

        @pl.loop(0, per_worker // chunk)
        def _(i):
            off = base + i * chunk
            b = off // seq
            n0 = off - b * seq
            pltpu.sync_copy(slot_hbm.at[pl.ds(b * TOP_K, TOP_K), pl.ds(n0, chunk)], idx_v)
            pltpu.sync_copy(rows_hbm.at[pl.ds(off, chunk)], rows_v)
            copies = [pltpu.async_copy(rows_v, out_hbm.at[idx_v.at[k]], sem) for k in range(TOP_K)]
            for cp in copies:
                cp.wait()

    return dispatch(rows, slot)


def _sc_gather(table, idx):
    n_out = idx.shape[0]
    width = table.shape[1]
    chunk = GATHER_CHUNK
    per_worker = n_out // SC_WORKERS
    steps = per_worker // chunk
    assert per_worker % chunk == 0 and steps % 2 == 0

    @functools.partial(
        pl.kernel, mesh=_sc_mesh(),
        out_type=jax.ShapeDtypeStruct((n_out, width), table.dtype),
        scratch_types=[pltpu.VMEM((2, chunk), jnp.int32), pltpu.VMEM((2, chunk, width), table.dtype),
                       pltpu.SemaphoreType.DMA, pltpu.SemaphoreType.DMA((2,))],
    )
    def gather(table_hbm, idx_hbm, out_hbm, idx_v, rows_v, gather_sem, out_sems):
        base = _sc_worker() * per_worker

        def write_out(buf, off):
            return pltpu.make_async_copy(rows_v.at[buf], out_hbm.at[pl.ds(off, chunk)], out_sems.at[buf])

        @pl.loop(0, steps, step=2)
        def _(i):
            for buf in range(2):
                off = base + (i + buf) * chunk

                @pl.when(i >= 2)
                def _():
                    write_out(buf, off - 2 * chunk).wait()

                pltpu.sync_copy(idx_hbm.at[pl.ds(off, chunk)], idx_v.at[buf])
                pltpu.async_copy(table_hbm.at[idx_v.at[buf]], rows_v.at[buf], gather_sem).wait()
                write_out(buf, off).start()

        for buf in range(2):
            write_out(buf, base + (steps - 2 + buf) * chunk).wait()

    return gather(table, idx)


def _gmm_kernel(blk_w_ref, used_ref, xs_ref, wg_ref, wu_ref, wd_ref, out_ref):
    half = PACKED

    @pl.when(pl.program_id(0) < used_ref[0])
    def _():
        lo, hi = (t.astype(BF16) for t in _unpack_words(xs_ref[...]))
        hg = _dot(lo, wg_ref[0, :half, :]) + _dot(hi, wg_ref[0, half:, :])
        hu = _dot(lo, wu_ref[0, :half, :]) + _dot(hi, wu_ref[0, half:, :])
        h = (hg * _sigmoid(hg) * hu).astype(BF16)
        out_ref[...] = _pack_words(_dot(h, wd_ref[0]))


def _gmm(xs, blk_w, used, wg, wu, wd):
    n_slots, width = xs.shape
    bm = EXPERT_BLOCK
    d = wg.shape[1]
    live = lambda i, blk_w, used: (jnp.minimum(i, used[0] - 1), 0)
    slab = lambda i, blk_w, used: (blk_w[i], 0, 0)
    return pl.pallas_call(
        _gmm_kernel,
        grid_spec=pltpu.PrefetchScalarGridSpec(
            num_scalar_prefetch=2,
            grid=(n_slots // bm,),
            in_specs=[pl.BlockSpec((bm, width), live),
                      pl.BlockSpec((1, d, EXPERT_FF), slab),
                      pl.BlockSpec((1, d, EXPERT_FF), slab),
                      pl.BlockSpec((1, EXPERT_FF, d), slab)],
            out_specs=pl.BlockSpec((bm, width), live)),
        out_shape=jax.ShapeDtypeStruct((n_slots, width), F32),
        compiler_params=_params(("arbitrary",), 32),
        name="gmm",
    )(blk_w, used, xs, wg, wu, wd)


def _ffn_out_kernel(x_ref, u_ref, yg_ref, gw_ref, g2_ref, wsgu_ref, wsd_ref, lng_ref, lnb_ref, *rest):
    out_ref = rest[-1]
    hgu = _dot(u_ref[0], wsgu_ref[...])
    hg = hgu[:, :SHARED_FF]
    h = (hg * _sigmoid(hg) * hgu[:, SHARED_FF:]).astype(BF16)
    shared = _dot(h, wsd_ref[...])
    gw = gw_ref[0]
    acc_lo = shared[:, :PACKED]
    acc_hi = shared[:, PACKED:]
    for k in range(TOP_K):
        lo, hi = _unpack_words(yg_ref[0, k])
        w = gw[:, k:k + 1]
        acc_lo = acc_lo + w * lo
        acc_hi = acc_hi + w * hi
    y = jnp.concatenate([acc_lo, acc_hi], axis=1)
    z = DN_ALPHA * x_ref[0] + (1.0 + g2_ref[0]) * y
    out_ref[0] = _layer_norm(z, lng_ref[...], lnb_ref[...])


def _ffn_out(x1, u2, yg, gw, mod3, wsgu, wsd, lng, lnb, first_seq, n_seq, shared):
    bsz, seq, d = x1.shape
    tm = ROW_TILE
    const = lambda shape: pl.BlockSpec(shape, lambda b, j: (0,) * len(shape))
    tok = pl.BlockSpec((1, tm, d), lambda b, j: (b, j, 0))
    in_specs = [tok, tok,
                pl.BlockSpec((1, TOP_K, tm, PACKED), lambda b, j: (b, 0, j, 0)),
                pl.BlockSpec((1, tm, TOP_K), lambda b, j: (b, j, 0)),
                pl.BlockSpec((1, 1, d), lambda b, j: (b, 0, 5)),
                const(wsgu.shape), const(wsd.shape), const(lng.shape), const(lnb.shape)]
    args = [x1, u2, yg, gw, mod3, wsgu, wsd, lng, lnb]
    aliases = {}
    if shared is not None:
        aliases = {len(args): 0}
        in_specs.append(pl.BlockSpec(memory_space=pl.ANY))
        args.append(shared)
    return pl.pallas_call(
        _ffn_out_kernel,
        grid=(bsz, seq // tm),
        in_specs=in_specs,
        out_specs=pl.BlockSpec((1, tm, d), lambda b, j: (b + first_seq, j, 0)),
        out_shape=jax.ShapeDtypeStruct((n_seq, seq, d), F32),
        input_output_aliases=aliases,
        compiler_params=_params(("arbitrary", "arbitrary"), 48),
        name="ffn_out",
    )(*args)


def _moe_layout(cnt, n_blocks):
    bm = EXPERT_BLOCK
    total = jnp.sum(cnt, axis=0)
    padded = (total + bm - 1) // bm * bm
    ends = jnp.cumsum(padded)
    base = (ends - padded)[None, :] + jnp.cumsum(cnt, axis=0) - cnt
    first_row = jnp.arange(n_blocks, dtype=ends.dtype) * bm
    blk_e = jnp.minimum(jnp.sum(ends[None, :] <= first_row[:, None], axis=1), N_EXPERTS - 1)
    used = (ends[-1:] // bm)
    return base.astype(jnp.int32), blk_e.astype(jnp.int32), used.astype(jnp.int32)


def _rope_tables(seq):
    half = HEAD_DIM // 2
    inv_freq = ROPE_THETA ** (-jnp.arange(half, dtype=F32) * 2.0 / HEAD_DIM)
    ang = jnp.arange(seq, dtype=F32)[:, None] * inv_freq[None, :]
    cos, sin = jnp.cos(ang), jnp.sin(ang)
    zero = jnp.zeros_like(sin)
    reps = LANES // HEAD_DIM
    cos_t = jnp.tile(jnp.concatenate([cos, cos], axis=1), (1, reps))
    sina_t = jnp.tile(jnp.concatenate([-sin, zero], axis=1), (1, reps))
    sinb_t = jnp.tile(jnp.concatenate([zero, sin], axis=1), (1, reps))
    return cos_t, sina_t, sinb_t


def _fox_bias_tables():
    n_split = 3
    aw = N_HEADS * LANES
    heads = jnp.arange(N_HEADS)
    selq = jnp.zeros((LANES, aw), F32)
    selk = jnp.zeros((LANES, aw), F32)
    oneq = jnp.zeros((1, aw), F32)
    onek = jnp.zeros((1, aw), F32)
    for j in range(n_split):
        selk = selk.at[j * N_HEADS + heads, heads * LANES + FOX_BIAS_LANE + j].set(-1.0)
        selq = selq.at[j * N_HEADS + heads, heads * LANES + FOX_BIAS_LANE + n_split + j].set(1.0)
        oneq = oneq.at[0, heads * LANES + FOX_BIAS_LANE + j].set(1.0)
        onek = onek.at[0, heads * LANES + FOX_BIAS_LANE + n_split + j].set(1.0)
    return selq.astype(BF16), selk.astype(BF16), oneq, onek


def kernel(x, c, w_ada, b_ada, w_in, b_forget, b_gate, w_br_fox, w_br_dil, w_o, ln_g, ln_b, w_router, router_bias,
           w_exp_gate, w_exp_up, w_exp_down, w_sh_gate, w_sh_up, w_sh_down):
    bsz, seq, d = x.shape
    depth = w_ada.shape[0]
    hw = HEADS_WIDTH
    n_tok = bsz * seq

    mod = _ada(c, w_ada, b_ada)

    o_f, o_d, o_g = 3 * hw, 3 * hw + N_HEADS, 6 * hw + N_HEADS
    wfox = w_in[:, :, :o_f].astype(BF16)
    wfg = jnp.pad(w_in[:, :, o_f:o_d], ((0, 0), (0, 0), (0, LANES - N_HEADS))).astype(BF16)
    wdil = w_in[:, :, o_d:o_g].astype(BF16)
    wgate = w_in[:, :, o_g:].astype(BF16)
    bfg = jnp.pad(b_forget, ((0, 0), (0, LANES - N_HEADS))).reshape(depth, 1, LANES)
    bgate = b_gate.reshape(depth, 1, 2 * d)
    wr_t = jnp.swapaxes(w_router, 1, 2)
    wr_hi = wr_t.astype(BF16)
    wr_lo = (wr_t - wr_hi.astype(F32)).astype(BF16)
    rbias = router_bias.reshape(depth, N_EXPERTS, 1)
    wg = w_exp_gate.astype(BF16).reshape(depth * N_EXPERTS, d, EXPERT_FF)
    wu = w_exp_up.astype(BF16).reshape(depth * N_EXPERTS, d, EXPERT_FF)
    wd = w_exp_down.astype(BF16).reshape(depth * N_EXPERTS, EXPERT_FF, d)
    wsgu = jnp.concatenate([w_sh_gate, w_sh_up], axis=-1).astype(BF16)
    wsd = w_sh_down.astype(BF16)
    cos_t, sina_t, sinb_t = _rope_tables(seq)
    selq, selk, oneq, onek = _fox_bias_tables()

    layer_params = (mod, wfox, wdil, wgate, wfg, bfg, bgate, w_br_fox.astype(BF16), w_br_dil.astype(BF16),
                    w_o.astype(BF16), ln_g, ln_b, wr_hi, wr_lo, rbias, wsgu, wsd,
                    jnp.arange(depth, dtype=jnp.int32) * N_EXPERTS)

    n_streams = 2 if n_tok % (2 * SC_WORKERS * DISPATCH_CHUNK) == 0 and bsz % 2 == 0 else 1
    sb = bsz // n_streams
    s_tok = sb * seq
    n_blocks = s_tok * TOP_K // EXPERT_BLOCK + N_EXPERTS

    def stream_layer(x_all, first_seq, mod3, prm, shared):
        (_, wfox_l, wdil_l, wgate_l, wfg_l, bfg_l, bgate_l, wbf_l, wbd_l, wo_l, lng_l, lnb_l,
         wrh_l, wrl_l, rb_l, wsgu_l, wsd_l, first_slab) = prm
        qa, ka, vt, *rest = _inproj(x_all, first_seq, mod3, wfox_l, wdil_l, wgate_l, wfg_l, bfg_l, bgate_l,
                                    cos_t, sina_t, sinb_t, selq, selk, oneq, onek)
        gf, gd = rest[9:]
        yf = _fox(qa, ka, vt)
        dil = [_dil(*rest[3 * i:3 * i + 3]) for i in range(len(DIL_PATTERNS))]
        x1 = _mix(x_all, first_seq, yf, [o for o, _ in dil], [l for _, l in dil], gf, gd, mod3, wbf_l, wbd_l, wo_l,
                  lng_l[0:1], lnb_l[0:1])
        u2, u2p, eidx, rank, gw, cnt = _router(x1, mod3, wrh_l, wrl_l, rb_l)
        base, blk_e, used = _moe_layout(cnt.reshape(sb, N_EXPERTS), n_blocks)
        slot = _slots(base, eidx, rank)
        xs = _sc_dispatch(u2p.reshape(s_tok, PACKED), slot.reshape(sb * TOP_K, seq), n_blocks * EXPERT_BLOCK)
        ys = _gmm(xs, blk_e + first_slab, used, wg, wu, wd)
        yg = _sc_gather(ys, slot.reshape(s_tok * TOP_K)).reshape(sb, TOP_K, seq, PACKED)
        return _ffn_out(x1, u2, yg, gw, mod3, wsgu_l, wsd_l, lng_l[1:2], lnb_l[1:2], first_seq, bsz, shared)

    def layer(x_all, prm):
        mod3 = prm[0].reshape(bsz, 1, 6 * d)
        shared = None
        for i in range(n_streams):
            shared = stream_layer(x_all, i * sb, mod3[i * sb:(i + 1) * sb], prm, shared)
        return shared, None

    out, _ = lax.scan(layer, x, layer_params)
    return out
```

```python
import functools

import jax
import jax.numpy as jnp
from jax import lax
from jax.experimental import pallas as pl
from jax.experimental.pallas import tpu as pltpu
from jax.experimental.pallas import tpu_sc as plsc

D_MODEL = 1024
DEPTH = 4
HEAD_DIM = 64
N_HEADS = 8
HEADS_WIDTH = N_HEADS * HEAD_DIM
DIL_PATTERNS = ((128, 1), (512, 4), (2048, 16))
ATTN_BLOCK = 128
ROPE_THETA = 10000.0
N_EXPERTS = 64
TOP_K = 8
N_GROUPS = 8
GROUP_SIZE = N_EXPERTS // N_GROUPS
TOPK_GROUPS = 4
EXPERT_FF = 256
SHARED_FF = 256
ROUTED_SCALE = 2.5
DN_ALPHA = (2 * DEPTH) ** 0.25
LN_EPS = 1e-5
NEG_INF = -1e30
SCORE_SCALE = HEAD_DIM ** -0.5
LOG2_E = 1.4426950408889634

LANES = 128
PAIR = 2 * HEAD_DIM
N_PAIRS = HEADS_WIDTH // PAIR

ROW_TILE = 512
FOX_Q = 256
FOX_K = 256
FOX_BIAS_LANE = HEAD_DIM
DIL_STEP_BLOCKS = 4
ROUTE_CHUNK = 512
EXPERT_BLOCK = 1024
PACKED = D_MODEL // 2

SC_CORES = 2
SC_SUBCORES = 16
SC_WORKERS = SC_CORES * SC_SUBCORES
DISPATCH_CHUNK = 128
GATHER_CHUNK = 64

BF16 = jnp.bfloat16
F32 = jnp.float32


def _dot(a, b):
    return jnp.dot(a, b, preferred_element_type=F32)


def _dot_nt(a, b):
    return lax.dot_general(a, b, (((1,), (1,)), ((), ())), preferred_element_type=F32)


def _split2(x):
    hi = x.astype(BF16)
    lo = (x - hi.astype(F32)).astype(BF16)
    return hi, lo


def _split3(x):
    hi = x.astype(BF16)
    r = x - hi.astype(F32)
    mid = r.astype(BF16)
    lo = (r - mid.astype(F32)).astype(BF16)
    return hi, mid, lo


def _sigmoid(x):
    return 1.0 / (1.0 + jnp.exp(-x))


def _layer_norm(z, g, b):
    mu = jnp.mean(z, axis=-1, keepdims=True)
    zc = z - mu
    var = jnp.mean(zc * zc, axis=-1, keepdims=True)
    return zc * lax.rsqrt(var + LN_EPS) * g + b


def _pack_words(v):
    half = v.shape[1] // 2
    bits = pltpu.bitcast(v.astype(BF16).astype(F32), jnp.uint32)
    word = (bits[:, half:] & jnp.uint32(0xFFFF0000)) | (bits[:, :half] >> 16)
    return pltpu.bitcast(word, F32)


def _unpack_words(w):
    bits = pltpu.bitcast(w, jnp.uint32)
    return pltpu.bitcast(bits << 16, F32), pltpu.bitcast(bits & jnp.uint32(0xFFFF0000), F32)


def _params(sem, vmem_mb):
    return pltpu.CompilerParams(dimension_semantics=sem, vmem_limit_bytes=vmem_mb * 1024 * 1024)


def _ada_kernel(c_ref, w_ref, b_ref, o_ref):
    c = c_ref[...]
    cs = c * _sigmoid(c)
    c_hi, c_lo = _split2(cs)
    w_hi, w_lo = _split2(w_ref[0])
    o_ref[0] = _dot(c_hi, w_hi) + _dot(c_hi, w_lo) + _dot(c_lo, w_hi) + b_ref[0]


def _ada(c, w_ada, b_ada):
    depth, d, n = w_ada.shape
    bsz = c.shape[0]
    tn = 1536
    return pl.pallas_call(
        _ada_kernel,
        grid=(depth, n // tn),
        in_specs=[
            pl.BlockSpec((bsz, d), lambda l, j: (0, 0)),
            pl.BlockSpec((1, d, tn), lambda l, j: (l, 0, j)),
            pl.BlockSpec((1, 1, tn), lambda l, j: (l, 0, j)),
        ],
        out_specs=pl.BlockSpec((1, bsz, tn), lambda l, j: (l, 0, j)),
        out_shape=jax.ShapeDtypeStruct((depth, bsz, n), F32),
        compiler_params=_params(("arbitrary", "arbitrary"), 40),
        name="ada",
    )(c, w_ada, b_ada.reshape(depth, 1, n))


def _inproj_kernel(x_ref, sc_ref, sh_ref, wfox_ref, wdil_ref, wgate_ref, wfg_ref, bfg_ref, bgate_ref,
                   cos_ref, sina_ref, sinb_ref, selq_ref, selk_ref, oneq_ref, onek_ref,
                   qa_ref, ka_ref, vt_ref, *rest):
    dil_refs = rest[:9]
    gf_ref, gd_ref, carry_ref, perm_ref = rest[9:]
    j = pl.program_id(1)
    tm = x_ref.shape[1]
    u = (x_ref[0] * (1.0 + sc_ref[0]) + sh_ref[0]).astype(BF16)

    ff = _dot(u, wfg_ref[...]) + bfg_ref[...]
    lf = jnp.minimum(ff, 0.0) - jnp.log(1.0 + jnp.exp(-jnp.abs(ff)))
    r = lax.broadcasted_iota(jnp.int32, (tm, tm), 0)
    c = lax.broadcasted_iota(jnp.int32, (tm, tm), 1)
    tri = jnp.where(r >= c, 1.0, 0.0).astype(BF16)
    hi, mid, lo = _split3(lf)
    csum = _dot(tri, hi) + _dot(tri, mid) + _dot(tri, lo)

    @pl.when(j == 0)
    def _():
        carry_ref[...] = jnp.zeros_like(carry_ref)

    cum = csum + carry_ref[...]
    carry_ref[...] = cum[tm - 1:tm, :]

    lane = lax.broadcasted_iota(jnp.int32, (1, LANES), 1)
    c_hi, c_mid, c_lo = (t.astype(F32) for t in _split3(cum * LOG2_E))
    parts = jnp.where(lane < N_HEADS, c_hi,
                      jnp.where(lane < 2 * N_HEADS, pltpu.roll(c_mid, N_HEADS, 1),
                                jnp.where(lane < 3 * N_HEADS, pltpu.roll(c_lo, 2 * N_HEADS, 1), 0.0))).astype(BF16)
    bias_q = _dot(parts, selq_ref[...]) + oneq_ref[...]
    bias_k = _dot(parts, selk_ref[...]) + onek_ref[...]
    pf = _dot(u, wfox_ref[...])
    data = lane < HEAD_DIM
    for dst, off, bias, scale in ((qa_ref, 0, bias_q, SCORE_SCALE * LOG2_E), (ka_ref, HEADS_WIDTH, bias_k, 1.0)):
        for p in range(N_PAIRS):
            pair = pf[:, off + p * PAIR: off + (p + 1) * PAIR] * scale
            for hh, src in ((0, pair), (1, pltpu.roll(pair, HEAD_DIM, 1))):
                t = (2 * p + hh) * LANES
                dst[0, :, t:t + LANES] = jnp.where(data, src, bias[:, t:t + LANES]).astype(BF16)
    vt = pf[:, 2 * HEADS_WIDTH:].T
    for cb in range(tm // FOX_K):
        vt_ref[0, cb] = vt[:, cb * FOX_K:(cb + 1) * FOX_K].astype(BF16)

    pd = _dot(u, wdil_ref[...])
    rows = pl.ds(pl.multiple_of(j * tm, tm), tm)
    cos = cos_ref[rows, :]
    sina = sina_ref[rows, :]
    sinb = sinb_ref[rows, :]
    for i, off in ((0, 0), (1, HEADS_WIDTH)):
        for g in range(N_PAIRS):
            xg = pd[:, off + g * LANES: off + (g + 1) * LANES]
            ahead = pltpu.roll(xg, LANES - HEAD_DIM // 2, 1)
            behind = pltpu.roll(xg, HEAD_DIM // 2, 1)
            rot = xg * cos + ahead * sina + behind * sinb
            perm_ref[i, g] = rot * SCORE_SCALE if i == 0 else rot
    for g in range(N_PAIRS):
        perm_ref[2, g] = pd[:, 2 * HEADS_WIDTH + g * LANES:2 * HEADS_WIDTH + (g + 1) * LANES]
    for i in range(3):
        for (_, dilation), dst in zip(DIL_PATTERNS, dil_refs[i::3]):
            for res in range(dilation):
                for g in range(N_PAIRS):
                    dst[0, res, :, g * LANES:(g + 1) * LANES] = perm_ref[
                        i, g, pl.ds(res, tm // dilation, stride=dilation), :].astype(BF16)

    pg = _sigmoid(_dot(u, wgate_ref[...]) + bgate_ref[...])
    gf_ref[0] = pg[:, :D_MODEL].astype(BF16)
    gd_ref[0] = pg[:, D_MODEL:].astype(BF16)


def _inproj(x, first_seq, mod3, wfox, wdil, wgate, wfg, bfg, bgate, cos_t, sina_t, sinb_t, selq, selk, oneq, onek):
    _, seq, d = x.shape
    bsz = mod3.shape[0]
    tm = ROW_TILE
    const = lambda shape: pl.BlockSpec(shape, lambda b, j: (0,) * len(shape))
    tok = lambda w: pl.BlockSpec((1, tm, w), lambda b, j: (b, j, 0))
    hw = HEADS_WIDTH
    aw = N_HEADS * LANES
    dil_shapes, dil_specs = [], []
    for _, dil in DIL_PATTERNS:
        dil_shapes += [jax.ShapeDtypeStruct((bsz, dil, seq // dil, hw), BF16)] * 3
        dil_specs += [pl.BlockSpec((1, dil, tm // dil, hw), lambda b, j: (b, 0, j, 0))] * 3
    outs = ([jax.ShapeDtypeStruct((bsz, seq, aw), BF16)] * 2
            + [jax.ShapeDtypeStruct((bsz, seq // FOX_K, hw, FOX_K), BF16)]
            + dil_shapes + [jax.ShapeDtypeStruct((bsz, seq, d), BF16)] * 2)
    return pl.pallas_call(
        _inproj_kernel,
        grid=(bsz, seq // tm),
        in_specs=[
            pl.BlockSpec((1, tm, d), lambda b, j: (b + first_seq, j, 0)),
            pl.BlockSpec((1, 1, d), lambda b, j: (b, 0, 1)),
            pl.BlockSpec((1, 1, d), lambda b, j: (b, 0, 0)),
            const(wfox.shape), const(wdil.shape), const(wgate.shape), const(wfg.shape),
            const(bfg.shape), const(bgate.shape),
            const(cos_t.shape), const(sina_t.shape), const(sinb_t.shape),
            const(selq.shape), const(selk.shape), const(oneq.shape), const(onek.shape),
        ],
        out_specs=[tok(aw)] * 2 + [pl.BlockSpec((1, tm // FOX_K, hw, FOX_K), lambda b, j: (b, j, 0, 0))]
                  + dil_specs + [tok(d)] * 2,
        out_shape=outs,
        scratch_shapes=[pltpu.VMEM((1, LANES), F32), pltpu.VMEM((3, N_PAIRS, tm, LANES), F32)],
        compiler_params=_params(("arbitrary", "arbitrary"), 56),
        name="inproj",
    )(x, mod3, mod3, wfox, wdil, wgate, wfg, bfg, bgate, cos_t, sina_t, sinb_t, selq, selk, oneq, onek)


def _fox_kernel(qa_ref, ka_ref, vt_ref, o_ref, m_ref, l_ref, acc_ref):
    seq = qa_ref.shape[1]
    tq, tk = FOX_Q, FOX_K
    k_pos = lax.broadcasted_iota(jnp.int32, (tk, tq), 0)
    q_pos = lax.broadcasted_iota(jnp.int32, (tk, tq), 1)
    ones = jnp.ones((16, tk), BF16)

    def q_body(qi, _):
        q0 = pl.multiple_of(qi * tq, tq)
        m_ref[...] = jnp.full(m_ref.shape, NEG_INF, F32)
        l_ref[...] = jnp.zeros(l_ref.shape, F32)
        acc_ref[...] = jnp.zeros(acc_ref.shape, F32)

        def kv_step(kb, k0, visible):
            scores = []
            for h in range(N_HEADS):
                ls = slice(h * LANES, (h + 1) * LANES)
                scores.append(_dot_nt(ka_ref[0, pl.ds(k0, tk), ls], qa_ref[0, pl.ds(q0, tq), ls]))
            m_all = m_ref[...]
            l_all = l_ref[...]
            probs, decay, m_rows, l_rows = [], [], [], []
            for h in range(N_HEADS):
                s = scores[h] if visible is None else jnp.where(visible, scores[h], NEG_INF)
                m_old = m_all[h:h + 1, :]
                m_new = jnp.maximum(m_old, jnp.max(s, axis=0, keepdims=True))
                m_rows.append(m_new)
                decay.append(jnp.exp2(m_old - m_new))
                probs.append(jnp.exp2((s - m_new).astype(BF16)))
            m_ref[...] = jnp.concatenate(m_rows, axis=0)
            for h in range(N_HEADS):
                vh = vt_ref[0, kb, h * HEAD_DIM:(h + 1) * HEAD_DIM, :]
                acc_ref[h] = decay[h] * acc_ref[h] + _dot(vh, probs[h])
                l_rows.append(decay[h] * l_all[h:h + 1, :] + _dot(ones, probs[h])[0:1])
            l_ref[...] = jnp.concatenate(l_rows, axis=0)

        def full_block(jb, carry):
            kv_step(jb, pl.multiple_of(jb * tk, tk), None)
            return carry

        lax.fori_loop(0, qi * (tq // tk), full_block, 0)
        for dblk in range(tq // tk):
            kb = qi * (tq // tk) + dblk
            kv_step(kb, pl.multiple_of(kb * tk, tk), k_pos + dblk * tk <= q_pos)

        for p in range(N_PAIRS):
            pair = jnp.concatenate([acc_ref[2 * p + hh] / l_ref[2 * p + hh:2 * p + hh + 1, :] for hh in range(2)],
                                   axis=0)
            o_ref[0, pl.ds(q0, tq), p * PAIR:(p + 1) * PAIR] = pair.T.astype(BF16)
        return 0

    lax.fori_loop(0, seq // tq, q_body, 0)


def _fox(qa, ka, vt):
    bsz, seq, aw = qa.shape
    hw = HEADS_WIDTH
    full = pl.BlockSpec((1, seq, aw), lambda b: (b, 0, 0))
    return pl.pallas_call(
        _fox_kernel,
        grid=(bsz,),
        in_specs=[full, full, pl.BlockSpec((1, seq // FOX_K, hw, FOX_K), lambda b: (b, 0, 0, 0))],
        out_specs=pl.BlockSpec((1, seq, hw), lambda b: (b, 0, 0)),
        out_shape=jax.ShapeDtypeStruct((bsz, seq, hw), BF16),
        scratch_shapes=[pltpu.VMEM((N_HEADS, FOX_Q), F32), pltpu.VMEM((N_HEADS, FOX_Q), F32),
                        pltpu.VMEM((N_HEADS, HEAD_DIM, FOX_Q), F32)],
        compiler_params=_params(("arbitrary",), 48),
        name="fox",
    )(qa, ka, vt)


def _dil_kernel(q_ref, kc_ref, kp_ref, vc_ref, vp_ref, o_ref, lse_ref):
    n = pl.program_id(2)
    blk = ATTN_BLOCK
    n_res = q_ref.shape[0]
    n_sub = q_ref.shape[1] // blk
    lane = lax.broadcasted_iota(jnp.int32, (1, PAIR), 1)
    first = lane < HEAD_DIM
    i = lax.broadcasted_iota(jnp.int32, (blk, 2 * blk), 0)
    c = lax.broadcasted_iota(jnp.int32, (blk, 2 * blk), 1)
    in_window = (c >= i) & (c <= i + blk)
    has_prev = (c >= blk) | (n > 0)

    for res in range(n_res):
        for sub in range(n_sub):
            rows = slice(sub * blk, (sub + 1) * blk)
            before = slice((sub - 1) * blk, sub * blk)
            valid = in_window & has_prev if sub == 0 else in_window
            scores, values = [], []
            for p in range(N_PAIRS):
                ls = slice(p * PAIR, (p + 1) * PAIR)
                qb = q_ref[res, rows, ls]
                zero = jnp.zeros_like(qb)
                k_prev = kp_ref[res, :, ls] if sub == 0 else kc_ref[res, before, ls]
                v_prev = vp_ref[res, :, ls] if sub == 0 else vc_ref[res, before, ls]
                kb = jnp.concatenate([k_prev, kc_ref[res, rows, ls]], axis=0)
                values.append(jnp.concatenate([v_prev, vc_ref[res, rows, ls]], axis=0))
                scores.append(_dot_nt(jnp.where(first, qb, zero), kb))
                scores.append(_dot_nt(jnp.where(first, zero, qb), kb))
            probs, denom, lses = [], [], []
            for s in scores:
                s = jnp.where(valid, s, NEG_INF)
                m = jnp.max(s, axis=-1, keepdims=True)
                e = jnp.exp(s - m)
                l = jnp.sum(e, axis=-1, keepdims=True)
                probs.append(e.astype(BF16))
                denom.append(l)
                lses.append(m + jnp.log(l))
            for p in range(N_PAIRS):
                ls = slice(p * PAIR, (p + 1) * PAIR)
                o0 = _dot(probs[2 * p], values[p]) / denom[2 * p]
                o1 = _dot(probs[2 * p + 1], values[p]) / denom[2 * p + 1]
                o_ref[res, rows, ls] = jnp.where(first, o0, o1).astype(BF16)
                lse_ref[res, rows, ls] = jnp.where(first, lses[2 * p], lses[2 * p + 1])


def _dil(qd, kd, vd):
    bsz, dilation, length, hw = qd.shape
    nb = length // ATTN_BLOCK
    n_sub = min(nb, DIL_STEP_BLOCKS)
    n_res = min(dilation, DIL_STEP_BLOCKS // n_sub)
    cur = pl.BlockSpec((None, n_res, n_sub * ATTN_BLOCK, hw), lambda b, r, n: (b, r, n, 0))
    prev = pl.BlockSpec((None, n_res, ATTN_BLOCK, hw), lambda b, r, n: (b, r, jnp.maximum(n * n_sub - 1, 0), 0))
    return pl.pallas_call(
        _dil_kernel,
        grid=(bsz, dilation // n_res, nb // n_sub),
        in_specs=[cur, cur, prev, cur, prev],
        out_specs=[cur, cur],
        out_shape=[jax.ShapeDtypeStruct(qd.shape, BF16), jax.ShapeDtypeStruct(qd.shape, F32)],
        compiler_params=_params(("arbitrary", "arbitrary", "arbitrary"), 32),
        name=f"dil{dilation}",
    )(qd, kd, kd, vd, vd)


def _mix_kernel(x_ref, yf_ref, o1_ref, o2_ref, o3_ref, l1_ref, l2_ref, l3_ref, gf_ref, gd_ref, g1_ref,
                wbf_ref, wbd_ref, wo_ref, lng_ref, lnb_ref, out_ref, nat_ref):
    tm = x_ref.shape[1]

    def natural(i, src_ref):
        dilation = src_ref.shape[1]
        for res in range(dilation):
            for g in range(N_PAIRS):
                nat_ref[i, g, pl.ds(res, tm // dilation, stride=dilation), :] = src_ref[
                    0, res, :, g * LANES:(g + 1) * LANES].astype(F32)
        return jnp.concatenate([nat_ref[i, g] for g in range(N_PAIRS)], axis=1)

    os_ = [natural(i, ref) for i, ref in enumerate((o1_ref, o2_ref, o3_ref))]
    l1, l2, l3 = [natural(3 + i, ref) for i, ref in enumerate((l1_ref, l2_ref, l3_ref))]
    mx = jnp.maximum(jnp.maximum(l1, l2), l3)
    e1, e2, e3 = jnp.exp(l1 - mx), jnp.exp(l2 - mx), jnp.exp(l3 - mx)
    yd = (e1 * os_[0] + e2 * os_[1] + e3 * os_[2]) / (e1 + e2 + e3)
    merged = (gf_ref[0].astype(F32) * _dot(yf_ref[0], wbf_ref[...])
              + gd_ref[0].astype(F32) * _dot(yd.astype(BF16), wbd_ref[...]))
    y = _dot(merged.astype(BF16), wo_ref[...])
    z = DN_ALPHA * x_ref[0] + (1.0 + g1_ref[0]) * y
    out_ref[0] = _layer_norm(z, lng_ref[...], lnb_ref[...])


def _mix(x, first_seq, yf, os_, lses, gf, gd, mod3, wbf, wbd, wo, lng, lnb):
    _, seq, d = x.shape
    bsz = mod3.shape[0]
    tm = ROW_TILE
    const = lambda shape: pl.BlockSpec(shape, lambda b, j: (0,) * len(shape))
    tok = lambda w: pl.BlockSpec((1, tm, w), lambda b, j: (b, j, 0))
    hw = HEADS_WIDTH
    res_major = [pl.BlockSpec((1, dil, tm // dil, hw), lambda b, j: (b, 0, j, 0)) for _, dil in DIL_PATTERNS]
    return pl.pallas_call(
        _mix_kernel,
        grid=(bsz, seq // tm),
        in_specs=[pl.BlockSpec((1, tm, d), lambda b, j: (b + first_seq, j, 0)), tok(hw)] + res_major * 2 + [
                  tok(d), tok(d),
                  pl.BlockSpec((1, 1, d), lambda b, j: (b, 0, 2)),
                  const(wbf.shape), const(wbd.shape), const(wo.shape), const(lng.shape), const(lnb.shape)],
        out_specs=tok(d),
        out_shape=jax.ShapeDtypeStruct((bsz, seq, d), F32),
        scratch_shapes=[pltpu.VMEM((6, N_PAIRS, tm, LANES), F32)],
        compiler_params=_params(("arbitrary", "arbitrary"), 48),
        name="mix",
    )(x, yf, *os_, *lses, gf, gd, mod3, wbf, wbd, wo, lng, lnb)


def _router_kernel(x_ref, sc_ref, sh_ref, wrh_ref, wrl_ref, bias_ref,
                   u_ref, up_ref, eidx_ref, rank_ref, gw_ref, cnt_ref):
    seq = x_ref.shape[1]
    tc = ROUTE_CHUNK
    e_iota = lax.broadcasted_iota(jnp.int32, (N_EXPERTS, tc), 0)
    g_iota = lax.broadcasted_iota(jnp.int32, (N_GROUPS, tc), 0)
    s_iota = lax.broadcasted_iota(jnp.int32, (GROUP_SIZE, tc), 0)
    r = lax.broadcasted_iota(jnp.int32, (tc, tc), 0)
    c = lax.broadcasted_iota(jnp.int32, (tc, tc), 1)
    upper = jnp.where(r <= c, 1.0, 0.0).astype(BF16)
    ones = jnp.ones((16, tc), BF16)
    carry = jnp.zeros((N_EXPERTS, 1), F32)
    cnt = jnp.zeros((16, N_EXPERTS), F32)

    for ch in range(seq // tc):
        rows = slice(ch * tc, (ch + 1) * tc)
        u = x_ref[0, rows, :] * (1.0 + sc_ref[0]) + sh_ref[0]
        u_ref[0, rows, :] = u.astype(BF16)
        up_ref[0, rows, :] = _pack_words(u)
        u_hi, u_lo = _split2(u)
        logits = _dot_nt(wrh_ref[...], u_hi) + _dot_nt(wrh_ref[...], u_lo) + _dot_nt(wrl_ref[...], u_hi)
        scores = _sigmoid(logits)
        sel = scores + bias_ref[...]

        gs_rows = []
        for g in range(N_GROUPS):
            blk = sel[g * GROUP_SIZE:(g + 1) * GROUP_SIZE]
            m1 = jnp.max(blk, axis=0, keepdims=True)
            i1 = jnp.min(jnp.where(blk == m1, s_iota, GROUP_SIZE), axis=0, keepdims=True)
            m2 = jnp.max(jnp.where(s_iota == i1, -jnp.inf, blk), axis=0, keepdims=True)
            gs_rows.append(m1 + m2)
        gs = jnp.concatenate(gs_rows, axis=0)
        beaten = jnp.zeros((N_GROUPS, tc), F32)
        for g in range(N_GROUPS):
            other = gs[g:g + 1]
            wins = (other > gs) | ((other == gs) & (g_iota > g))
            beaten = beaten + jnp.where(wins, 1.0, 0.0)
        keep = beaten < TOPK_GROUPS
        cand = jnp.concatenate(
            [jnp.where(keep[g:g + 1], sel[g * GROUP_SIZE:(g + 1) * GROUP_SIZE], NEG_INF) for g in range(N_GROUPS)],
            axis=0)
        chosen = jnp.zeros((N_EXPERTS, tc), F32)
        picks = []
        for _ in range(TOP_K):
            m = jnp.max(cand, axis=0, keepdims=True)
            idx = jnp.min(jnp.where(cand == m, e_iota, N_EXPERTS), axis=0, keepdims=True)
            hit = e_iota == idx
            chosen = jnp.where(hit, 1.0, chosen)
            cand = jnp.where(hit, -jnp.inf, cand)
            picks.append(idx)
        g_raw = jnp.where(chosen > 0.0, scores, 0.0)
        gate = g_raw / jnp.sum(g_raw, axis=0, keepdims=True) * ROUTED_SCALE

        chosen_b = chosen.astype(BF16)
        incl = _dot(chosen_b, upper) + carry
        carry = incl[:, tc - 1:tc]
        cnt = cnt + _dot_nt(ones, chosen_b)

        rank_rows, gate_rows = [], []
        for idx in picks:
            hit = e_iota == idx
            rank_rows.append(jnp.sum(jnp.where(hit, incl - 1.0, 0.0), axis=0, keepdims=True))
            gate_rows.append(jnp.sum(jnp.where(hit, gate, 0.0), axis=0, keepdims=True))
        eidx_ref[0, :, rows] = jnp.concatenate(picks, axis=0)
        rank_ref[0, :, rows] = jnp.concatenate(rank_rows, axis=0).astype(jnp.int32)
        gate_t = jnp.concatenate(gate_rows + [jnp.zeros((LANES - TOP_K, tc), F32)], axis=0).T
        gw_ref[0, rows, :] = gate_t[:, :TOP_K]

    cnt_ref[0] = cnt[0:1].astype(jnp.int32)


def _router(x1, mod3, wr_hi, wr_lo, bias):
    bsz, seq, d = x1.shape
    full = pl.BlockSpec((1, seq, d), lambda b: (b, 0, 0))
    ks = pl.BlockSpec((1, TOP_K, seq), lambda b: (b, 0, 0))
    return pl.pallas_call(
        _router_kernel,
        grid=(bsz,),
        in_specs=[full,
                  pl.BlockSpec((1, 1, d), lambda b: (b, 0, 4)),
                  pl.BlockSpec((1, 1, d), lambda b: (b, 0, 3)),
                  pl.BlockSpec(wr_hi.shape, lambda b: (0, 0)),
                  pl.BlockSpec(wr_lo.shape, lambda b: (0, 0)),
                  pl.BlockSpec(bias.shape, lambda b: (0, 0))],
        out_specs=[full, pl.BlockSpec((1, seq, PACKED), lambda b: (b, 0, 0)), ks, ks,
                   pl.BlockSpec((1, seq, TOP_K), lambda b: (b, 0, 0)),
                   pl.BlockSpec((1, 1, N_EXPERTS), lambda b: (b, 0, 0))],
        out_shape=[jax.ShapeDtypeStruct((bsz, seq, d), BF16),
                   jax.ShapeDtypeStruct((bsz, seq, PACKED), F32),
                   jax.ShapeDtypeStruct((bsz, TOP_K, seq), jnp.int32),
                   jax.ShapeDtypeStruct((bsz, TOP_K, seq), jnp.int32),
                   jax.ShapeDtypeStruct((bsz, seq, TOP_K), F32),
                   jax.ShapeDtypeStruct((bsz, 1, N_EXPERTS), jnp.int32)],
        compiler_params=_params(("arbitrary",), 48),
        name="router",
    )(x1, mod3, mod3, wr_hi, wr_lo, bias)


def _slot_kernel(base_ref, eidx_ref, rank_ref, slot_ref):
    b = pl.program_id(0)
    eidx = eidx_ref[0]
    start = jnp.zeros(eidx.shape, jnp.int32)
    for e in range(N_EXPERTS):
        start = jnp.where(eidx == e, base_ref[b * N_EXPERTS + e], start)
    slot_ref[0] = start + rank_ref[0]


def _slots(base, eidx, rank):
    bsz, _, seq = eidx.shape
    ks = pl.BlockSpec((1, TOP_K, seq), lambda b, c: (b, 0, 0))
    return pl.pallas_call(
        _slot_kernel,
        grid_spec=pltpu.PrefetchScalarGridSpec(num_scalar_prefetch=1, grid=(bsz,), in_specs=[ks, ks], out_specs=ks),
        out_shape=jax.ShapeDtypeStruct((bsz, TOP_K, seq), jnp.int32),
        compiler_params=_params(("arbitrary",), 16),
        name="slots",
    )(base.reshape(bsz * N_EXPERTS), eidx, rank)


def _sc_mesh():
    return plsc.VectorSubcoreMesh(core_axis_name="core", subcore_axis_name="subcore")


def _sc_worker():
    return lax.axis_index("subcore") * SC_CORES + lax.axis_index("core")


def _sc_dispatch(rows, slot, n_slots):
    n_tok, width = rows.shape
    seq = slot.shape[1]
    chunk = DISPATCH_CHUNK
    per_worker = n_tok // SC_WORKERS
    assert per_worker % chunk == 0 and seq % chunk == 0

    @functools.partial(
        pl.kernel, mesh=_sc_mesh(),
        out_type=jax.ShapeDtypeStruct((n_slots, width), rows.dtype),
        scratch_types=[pltpu.VMEM((TOP_K, chunk), jnp.int32), pltpu.VMEM((chunk, width), rows.dtype),
                       pltpu.SemaphoreType.DMA],
    )
    def dispatch(rows_hbm, slot_hbm, out_hbm, idx_v, rows_v, sem):
        base = _sc_worker() * per_worker

        @pl.loop(0, per_worker // chunk)
        def _(i):
            off = base + i * chunk
            b = off // seq
            n0 = off - b * seq
            pltpu.sync_copy(slot_hbm.at[pl.ds(b * TOP_K, TOP_K), pl.ds(n0, chunk)], idx_v)
            pltpu.sync_copy(rows_hbm.at[pl.ds(off, chunk)], rows_v)
            copies = [pltpu.async_copy(rows_v, out_hbm.at[idx_v.at[k]], sem) for k in range(TOP_K)]
            for cp in copies:
                cp.wait()

    return dispatch(rows, slot)


def _sc_gather(table, idx):
    n_out = idx.shape[0]
    width = table.shape[1]
    chunk = GATHER_CHUNK
    per_worker = n_out // SC_WORKERS
    steps = per_worker // chunk
    assert per_worker % chunk == 0 and steps % 2 == 0

    @functools.partial(
        pl.kernel, mesh=_sc_mesh(),
        out_type=jax.ShapeDtypeStruct((n_out, width), table.dtype),
        scratch_types=[pltpu.VMEM((2, chunk), jnp.int32), pltpu.VMEM((2, chunk, width), table.dtype),
                       pltpu.SemaphoreType.DMA, pltpu.SemaphoreType.DMA((2,))],
    )
    def gather(table_hbm, idx_hbm, out_hbm, idx_v, rows_v, gather_sem, out_sems):
        base = _sc_worker() * per_worker

        def write_out(buf, off):
            return pltpu.make_async_copy(rows_v.at[buf], out_hbm.at[pl.ds(off, chunk)], out_sems.at[buf])

        @pl.loop(0, steps, step=2)
        def _(i):
            for buf in range(2):
                off = base + (i + buf) * chunk

                @pl.when(i >= 2)
                def _():
                    write_out(buf, off - 2 * chunk).wait()

                pltpu.sync_copy(idx_hbm.at[pl.ds(off, chunk)], idx_v.at[buf])
                pltpu.async_copy(table_hbm.at[idx_v.at[buf]], rows_v.at[buf], gather_sem).wait()
                write_out(buf, off).start()

        for buf in range(2):
            write_out(buf, base + (steps - 2 + buf) * chunk).wait()

    return gather(table, idx)


def _gmm_kernel(blk_w_ref, used_ref, xs_ref, wg_ref, wu_ref, wd_ref, out_ref):
    half = PACKED

    @pl.when(pl.program_id(0) < used_ref[0])
    def _():
        lo, hi = (t.astype(BF16) for t in _unpack_words(xs_ref[...]))
        hg = _dot(lo, wg_ref[0, :half, :]) + _dot(hi, wg_ref[0, half:, :])
        hu = _dot(lo, wu_ref[0, :half, :]) + _dot(hi, wu_ref[0, half:, :])
        h = (hg * _sigmoid(hg) * hu).astype(BF16)
        out_ref[...] = _pack_words(_dot(h, wd_ref[0]))


def _gmm(xs, blk_w, used, wg, wu, wd):
    n_slots, width = xs.shape
    bm = EXPERT_BLOCK
    d = wg.shape[1]
    live = lambda i, blk_w, used: (jnp.minimum(i, used[0] - 1), 0)
    slab = lambda i, blk_w, used: (blk_w[i], 0, 0)
    return pl.pallas_call(
        _gmm_kernel,
        grid_spec=pltpu.PrefetchScalarGridSpec(
            num_scalar_prefetch=2,
            grid=(n_slots // bm,),
            in_specs=[pl.BlockSpec((bm, width), live),
                      pl.BlockSpec((1, d, EXPERT_FF), slab),
                      pl.BlockSpec((1, d, EXPERT_FF), slab),
                      pl.BlockSpec((1, EXPERT_FF, d), slab)],
            out_specs=pl.BlockSpec((bm, width), live)),
        out_shape=jax.ShapeDtypeStruct((n_slots, width), F32),
        compiler_params=_params(("arbitrary",), 32),
        name="gmm",
    )(blk_w, used, xs, wg, wu, wd)


def _ffn_out_kernel(x_ref, u_ref, yg_ref, gw_ref, g2_ref, wsgu_ref, wsd_ref, lng_ref, lnb_ref, *rest):
    out_ref = rest[-1]
    hgu = _dot(u_ref[0], wsgu_ref[...])
    hg = hgu[:, :SHARED_FF]
    h = (hg * _sigmoid(hg) * hgu[:, SHARED_FF:]).astype(BF16)
    shared = _dot(h, wsd_ref[...])
    gw = gw_ref[0]
    acc_lo = shared[:, :PACKED]
    acc_hi = shared[:, PACKED:]
    for k in range(TOP_K):
        lo, hi = _unpack_words(yg_ref[0, k])
        w = gw[:, k:k + 1]
        acc_lo = acc_lo + w * lo
        acc_hi = acc_hi + w * hi
    y = jnp.concatenate([acc_lo, acc_hi], axis=1)
    z = DN_ALPHA * x_ref[0] + (1.0 + g2_ref[0]) * y
    out_ref[0] = _layer_norm(z, lng_ref[...], lnb_ref[...])


def _ffn_out(x1, u2, yg, gw, mod3, wsgu, wsd, lng, lnb, first_seq, n_seq, shared):
    bsz, seq, d = x1.shape
    tm = ROW_TILE
    const = lambda shape: pl.BlockSpec(shape, lambda b, j: (0,) * len(shape))
    tok = pl.BlockSpec((1, tm, d), lambda b, j: (b, j, 0))
    in_specs = [tok, tok,
                pl.BlockSpec((1, TOP_K, tm, PACKED), lambda b, j: (b, 0, j, 0)),
                pl.BlockSpec((1, tm, TOP_K), lambda b, j: (b, j, 0)),
                pl.BlockSpec((1, 1, d), lambda b, j: (b, 0, 5)),
                const(wsgu.shape), const(wsd.shape), const(lng.shape), const(lnb.shape)]
    args = [x1, u2, yg, gw, mod3, wsgu, wsd, lng, lnb]
    aliases = {}
    if shared is not None:
        aliases = {len(args): 0}
        in_specs.append(pl.BlockSpec(memory_space=pl.ANY))
        args.append(shared)
    return pl.pallas_call(
        _ffn_out_kernel,
        grid=(bsz, seq // tm),
        in_specs=in_specs,
        out_specs=pl.BlockSpec((1, tm, d), lambda b, j: (b + first_seq, j, 0)),
        out_shape=jax.ShapeDtypeStruct((n_seq, seq, d), F32),
        input_output_aliases=aliases,
        compiler_params=_params(("arbitrary", "arbitrary"), 48),
        name="ffn_out",
    )(*args)


def _moe_layout(cnt, n_blocks):
    bm = EXPERT_BLOCK
    total = jnp.sum(cnt, axis=0)
    padded = (total + bm - 1) // bm * bm
    ends = jnp.cumsum(padded)
    base = (ends - padded)[None, :] + jnp.cumsum(cnt, axis=0) - cnt
    first_row = jnp.arange(n_blocks, dtype=ends.dtype) * bm
    blk_e = jnp.minimum(jnp.sum(ends[None, :] <= first_row[:, None], axis=1), N_EXPERTS - 1)
    used = (ends[-1:] // bm)
    return base.astype(jnp.int32), blk_e.astype(jnp.int32), used.astype(jnp.int32)


def _rope_tables(seq):
    half = HEAD_DIM // 2
    inv_freq = ROPE_THETA ** (-jnp.arange(half, dtype=F32) * 2.0 / HEAD_DIM)
    ang = jnp.arange(seq, dtype=F32)[:, None] * inv_freq[None, :]
    cos, sin = jnp.cos(ang), jnp.sin(ang)
    zero = jnp.zeros_like(sin)
    reps = LANES // HEAD_DIM
    cos_t = jnp.tile(jnp.concatenate([cos, cos], axis=1), (1, reps))
    sina_t = jnp.tile(jnp.concatenate([-sin, zero], axis=1), (1, reps))
    sinb_t = jnp.tile(jnp.concatenate([zero, sin], axis=1), (1, reps))
    return cos_t, sina_t, sinb_t


def _fox_bias_tables():
    n_split = 3
    aw = N_HEADS * LANES
    heads = jnp.arange(N_HEADS)
    selq = jnp.zeros((LANES, aw), F32)
    selk = jnp.zeros((LANES, aw), F32)
    oneq = jnp.zeros((1, aw), F32)
    onek = jnp.zeros((1, aw), F32)
    for j in range(n_split):
        selk = selk.at[j * N_HEADS + heads, heads * LANES + FOX_BIAS_LANE + j].set(-1.0)
        selq = selq.at[j * N_HEADS + heads, heads * LANES + FOX_BIAS_LANE + n_split + j].set(1.0)
        oneq = oneq.at[0, heads * LANES + FOX_BIAS_LANE + j].set(1.0)
        onek = onek.at[0, heads * LANES + FOX_BIAS_LANE + n_split + j].set(1.0)
    return selq.astype(BF16), selk.astype(BF16), oneq, onek


def kernel(x, c, w_ada, b_ada, w_in, b_forget, b_gate, w_br_fox, w_br_dil, w_o, ln_g, ln_b, w_router, router_bias,
           w_exp_gate, w_exp_up, w_exp_down, w_sh_gate, w_sh_up, w_sh_down):
    bsz, seq, d = x.shape
    depth = w_ada.shape[0]
    hw = HEADS_WIDTH
    n_tok = bsz * seq

    mod = _ada(c, w_ada, b_ada)

    o_f, o_d, o_g = 3 * hw, 3 * hw + N_HEADS, 6 * hw + N_HEADS
    wfox = w_in[:, :, :o_f].astype(BF16)
    wfg = jnp.pad(w_in[:, :, o_f:o_d], ((0, 0), (0, 0), (0, LANES - N_HEADS))).astype(BF16)
    wdil = w_in[:, :, o_d:o_g].astype(BF16)
    wgate = w_in[:, :, o_g:].astype(BF16)
    bfg = jnp.pad(b_forget, ((0, 0), (0, LANES - N_HEADS))).reshape(depth, 1, LANES)
    bgate = b_gate.reshape(depth, 1, 2 * d)
    wr_t = jnp.swapaxes(w_router, 1, 2)
    wr_hi = wr_t.astype(BF16)
    wr_lo = (wr_t - wr_hi.astype(F32)).astype(BF16)
    rbias = router_bias.reshape(depth, N_EXPERTS, 1)
    wg = w_exp_gate.astype(BF16).reshape(depth * N_EXPERTS, d, EXPERT_FF)
    wu = w_exp_up.astype(BF16).reshape(depth * N_EXPERTS, d, EXPERT_FF)
    wd = w_exp_down.astype(BF16).reshape(depth * N_EXPERTS, EXPERT_FF, d)
    wsgu = jnp.concatenate([w_sh_gate, w_sh_up], axis=-1).astype(BF16)
    wsd = w_sh_down.astype(BF16)
    cos_t, sina_t, sinb_t = _rope_tables(seq)
    selq, selk, oneq, onek = _fox_bias_tables()

    layer_params = (mod, wfox, wdil, wgate, wfg, bfg, bgate, w_br_fox.astype(BF16), w_br_dil.astype(BF16),
                    w_o.astype(BF16), ln_g, ln_b, wr_hi, wr_lo, rbias, wsgu, wsd,
                    jnp.arange(depth, dtype=jnp.int32) * N_EXPERTS)

    n_streams = 2 if n_tok % (2 * SC_WORKERS * DISPATCH_CHUNK) == 0 and bsz % 2 == 0 else 1
    sb = bsz // n_streams
    s_tok = sb * seq
    n_blocks = s_tok * TOP_K // EXPERT_BLOCK + N_EXPERTS

    def stream_layer(x_all, first_seq, mod3, prm, shared):
        (_, wfox_l, wdil_l, wgate_l, wfg_l, bfg_l, bgate_l, wbf_l, wbd_l, wo_l, lng_l, lnb_l,
         wrh_l, wrl_l, rb_l, wsgu_l, wsd_l, first_slab) = prm
        qa, ka, vt, *rest = _inproj(x_all, first_seq, mod3, wfox_l, wdil_l, wgate_l, wfg_l, bfg_l, bgate_l,
                                    cos_t, sina_t, sinb_t, selq, selk, oneq, onek)
        gf, gd = rest[9:]
        yf = _fox(qa, ka, vt)
        dil = [_dil(*rest[3 * i:3 * i + 3]) for i in range(len(DIL_PATTERNS))]
        x1 = _mix(x_all, first_seq, yf, [o for o, _ in dil], [l for _, l in dil], gf, gd, mod3, wbf_l, wbd_l, wo_l,
                  lng_l[0:1], lnb_l[0:1])
        u2, u2p, eidx, rank, gw, cnt = _router(x1, mod3, wrh_l, wrl_l, rb_l)
        base, blk_e, used = _moe_layout(cnt.reshape(sb, N_EXPERTS), n_blocks)
        slot = _slots(base, eidx, rank)
        xs = _sc_dispatch(u2p.reshape(s_tok, PACKED), slot.reshape(sb * TOP_K, seq), n_blocks * EXPERT_BLOCK)
        ys = _gmm(xs, blk_e + first_slab, used, wg, wu, wd)
        yg = _sc_gather(ys, slot.reshape(s_tok * TOP_K)).reshape(sb, TOP_K, seq, PACKED)
        return _ffn_out(x1, u2, yg, gw, mod3, wsgu_l, wsd_l, lng_l[1:2], lnb_l[1:2], first_seq, bsz, shared)

    def layer(x_all, prm):
        mod3 = prm[0].reshape(bsz, 1, 6 * d)
        shared = None
        for i in range(n_streams):
            shared = stream_layer(x_all, i * sb, mod3[i * sb:(i + 1) * sb], prm, shared)
        return shared, None

    out, _ = lax.scan(layer, x, layer_params)
    return out
```

```python
import functools

import jax
import jax.numpy as jnp
from jax import lax
from jax.experimental import pallas as pl
from jax.experimental.pallas import tpu as pltpu
from jax.experimental.pallas import tpu_sc as plsc

D_MODEL = 1024
DEPTH = 4
HEAD_DIM = 64
N_HEADS = 8
HEADS_WIDTH = N_HEADS * HEAD_DIM
DIL_PATTERNS = ((128, 1), (512, 4), (2048, 16))
ATTN_BLOCK = 128
ROPE_THETA = 10000.0
N_EXPERTS = 64
TOP_K = 8
N_GROUPS = 8
GROUP_SIZE = N_EXPERTS // N_GROUPS
TOPK_GROUPS = 4
EXPERT_FF = 256
SHARED_FF = 256
ROUTED_SCALE = 2.5
DN_ALPHA = (2 * DEPTH) ** 0.25
LN_EPS = 1e-5
NEG_INF = -1e30
SCORE_SCALE = HEAD_DIM ** -0.5
LOG2_E = 1.4426950408889634

LANES = 128
PAIR = 2 * HEAD_DIM
N_PAIRS = HEADS_WIDTH // PAIR

ROW_TILE = 512
FOX_Q = 256
FOX_K = 256
FOX_BIAS_LANE = HEAD_DIM
DIL_STEP_BLOCKS = 4
ROUTE_CHUNK = 512
EXPERT_BLOCK = 1024
PACKED = D_MODEL // 2

SC_CORES = 2
SC_SUBCORES = 16
SC_WORKERS = SC_CORES * SC_SUBCORES
DISPATCH_CHUNK = 128
GATHER_CHUNK = 64

BF16 = jnp.bfloat16
F32 = jnp.float32


def _dot(a, b):
    return jnp.dot(a, b, preferred_element_type=F32)


def _dot_nt(a, b):
    return lax.dot_general(a, b, (((1,), (1,)), ((), ())), preferred_element_type=F32)


def _split2(x):
    hi = x.astype(BF16)
    lo = (x - hi.astype(F32)).astype(BF16)
    return hi, lo


def _split3(x):
    hi = x.astype(BF16)
    r = x - hi.astype(F32)
    mid = r.astype(BF16)
    lo = (r - mid.astype(F32)).astype(BF16)
    return hi, mid, lo


def _sigmoid(x):
    return 1.0 / (1.0 + jnp.exp(-x))


def _layer_norm(z, g, b):
    mu = jnp.mean(z, axis=-1, keepdims=True)
    zc = z - mu
    var = jnp.mean(zc * zc, axis=-1, keepdims=True)
    return zc * lax.rsqrt(var + LN_EPS) * g + b


def _pack_words(v):
    half = v.shape[1] // 2
    bits = pltpu.bitcast(v.astype(BF16).astype(F32), jnp.uint32)
    word = (bits[:, half:] & jnp.uint32(0xFFFF0000)) | (bits[:, :half] >> 16)
    return pltpu.bitcast(word, F32)


def _unpack_words(w):
    bits = pltpu.bitcast(w, jnp.uint32)
    return pltpu.bitcast(bits << 16, F32), pltpu.bitcast(bits & jnp.uint32(0xFFFF0000), F32)


def _params(sem, vmem_mb):
    return pltpu.CompilerParams(dimension_semantics=sem, vmem_limit_bytes=vmem_mb * 1024 * 1024)


def _ada_kernel(c_ref, w_ref, b_ref, o_ref):
    c = c_ref[...]
    cs = c * _sigmoid(c)
    c_hi, c_lo = _split2(cs)
    w_hi, w_lo = _split2(w_ref[0])
    o_ref[0] = _dot(c_hi, w_hi) + _dot(c_hi, w_lo) + _dot(c_lo, w_hi) + b_ref[0]


def _ada(c, w_ada, b_ada):
    depth, d, n = w_ada.shape
    bsz = c.shape[0]
    tn = 1536
    return pl.pallas_call(
        _ada_kernel,
        grid=(depth, n // tn),
        in_specs=[
            pl.BlockSpec((bsz, d), lambda l, j: (0, 0)),
            pl.BlockSpec((1, d, tn), lambda l, j: (l, 0, j)),
            pl.BlockSpec((1, 1, tn), lambda l, j: (l, 0, j)),
        ],
        out_specs=pl.BlockSpec((1, bsz, tn), lambda l, j: (l, 0, j)),
        out_shape=jax.ShapeDtypeStruct((depth, bsz, n), F32),
        compiler_params=_params(("arbitrary", "arbitrary"), 40),
        name="ada",
    )(c, w_ada, b_ada.reshape(depth, 1, n))


def _inproj_kernel(x_ref, sc_ref, sh_ref, wfox_ref, wdil_ref, wgate_ref, wfg_ref, bfg_ref, bgate_ref,
                   cos_ref, sina_ref, sinb_ref, selq_ref, selk_ref, oneq_ref, onek_ref,
                   qa_ref, ka_ref, vt_ref, *rest):
    dil_refs = rest[:9]
    gf_ref, gd_ref, carry_ref, perm_ref = rest[9:]
    j = pl.program_id(1)
    tm = x_ref.shape[1]
    u = (x_ref[0] * (1.0 + sc_ref[0]) + sh_ref[0]).astype(BF16)

    ff = _dot(u, wfg_ref[...]) + bfg_ref[...]
    lf = jnp.minimum(ff, 0.0) - jnp.log(1.0 + jnp.exp(-jnp.abs(ff)))
    r = lax.broadcasted_iota(jnp.int32, (tm, tm), 0)
    c = lax.broadcasted_iota(jnp.int32, (tm, tm), 1)
    tri = jnp.where(r >= c, 1.0, 0.0).astype(BF16)
    hi, mid, lo = _split3(lf)
    csum = _dot(tri, hi) + _dot(tri, mid) + _dot(tri, lo)

    @pl.when(j == 0)
    def _():
        carry_ref[...] = jnp.zeros_like(carry_ref)

    cum = csum + carry_ref[...]
    carry_ref[...] = cum[tm - 1:tm, :]

    lane = lax.broadcasted_iota(jnp.int32, (1, LANES), 1)
    c_hi, c_mid, c_lo = (t.astype(F32) for t in _split3(cum * LOG2_E))
    parts = jnp.where(lane < N_HEADS, c_hi,
                      jnp.where(lane < 2 * N_HEADS, pltpu.roll(c_mid, N_HEADS, 1),
                                jnp.where(lane < 3 * N_HEADS, pltpu.roll(c_lo, 2 * N_HEADS, 1), 0.0))).astype(BF16)
    bias_q = _dot(parts, selq_ref[...]) + oneq_ref[...]
    bias_k = _dot(parts, selk_ref[...]) + onek_ref[...]
    pf = _dot(u, wfox_ref[...])
    data = lane < HEAD_DIM
    for dst, off, bias, scale in ((qa_ref, 0, bias_q, SCORE_SCALE * LOG2_E), (ka_ref, HEADS_WIDTH, bias_k, 1.0)):
        for p in range(N_PAIRS):
            pair = pf[:, off + p * PAIR: off + (p + 1) * PAIR] * scale
            for hh, src in ((0, pair), (1, pltpu.roll(pair, HEAD_DIM, 1))):
                t = (2 * p + hh) * LANES
                dst[0, :, t:t + LANES] = jnp.where(data, src, bias[:, t:t + LANES]).astype(BF16)
    vt = pf[:, 2 * HEADS_WIDTH:].T
    for cb in range(tm // FOX_K):
        vt_ref[0, cb] = vt[:, cb * FOX_K:(cb + 1) * FOX_K].astype(BF16)

    pd = _dot(u, wdil_ref[...])
    rows = pl.ds(pl.multiple_of(j * tm, tm), tm)
    cos = cos_ref[rows, :]
    sina = sina_ref[rows, :]
    sinb = sinb_ref[rows, :]
    for i, off in ((0, 0), (1, HEADS_WIDTH)):
        for g in range(N_PAIRS):
            xg = pd[:, off + g * LANES: off + (g + 1) * LANES]
            ahead = pltpu.roll(xg, LANES - HEAD_DIM // 2, 1)
            behind = pltpu.roll(xg, HEAD_DIM // 2, 1)
            rot = xg * cos + ahead * sina + behind * sinb
            perm_ref[i, g] = rot * SCORE_SCALE if i == 0 else rot
    for g in range(N_PAIRS):
        perm_ref[2, g] = pd[:, 2 * HEADS_WIDTH + g * LANES:2 * HEADS_WIDTH + (g + 1) * LANES]
    for i in range(3):
        for (_, dilation), dst in zip(DIL_PATTERNS, dil_refs[i::3]):
            for res in range(dilation):
                for g in range(N_PAIRS):
                    dst[0, res, :, g * LANES:(g + 1) * LANES] = perm_ref[
                        i, g, pl.ds(res, tm // dilation, stride=dilation), :].astype(BF16)

    pg = _sigmoid(_dot(u, wgate_ref[...]) + bgate_ref[...])
    gf_ref[0] = pg[:, :D_MODEL].astype(BF16)
    gd_ref[0] = pg[:, D_MODEL:].astype(BF16)


def _inproj(x, first_seq, mod3, wfox, wdil, wgate, wfg, bfg, bgate, cos_t, sina_t, sinb_t, selq, selk, oneq, onek):
    _, seq, d = x.shape
    bsz = mod3.shape[0]
    tm = ROW_TILE
    const = lambda shape: pl.BlockSpec(shape, lambda b, j: (0,) * len(shape))
    tok = lambda w: pl.BlockSpec((1, tm, w), lambda b, j: (b, j, 0))
    hw = HEADS_WIDTH
    aw = N_HEADS * LANES
    dil_shapes, dil_specs = [], []
    for _, dil in DIL_PATTERNS:
        dil_shapes += [jax.ShapeDtypeStruct((bsz, dil, seq // dil, hw), BF16)] * 3
        dil_specs += [pl.BlockSpec((1, dil, tm // dil, hw), lambda b, j: (b, 0, j, 0))] * 3
    outs = ([jax.ShapeDtypeStruct((bsz, seq, aw), BF16)] * 2
            + [jax.ShapeDtypeStruct((bsz, seq // FOX_K, hw, FOX_K), BF16)]
            + dil_shapes + [jax.ShapeDtypeStruct((bsz, seq, d), BF16)] * 2)
    return pl.pallas_call(
        _inproj_kernel,
        grid=(bsz, seq // tm),
        in_specs=[
            pl.BlockSpec((1, tm, d), lambda b, j: (b + first_seq, j, 0)),
            pl.BlockSpec((1, 1, d), lambda b, j: (b, 0, 1)),
            pl.BlockSpec((1, 1, d), lambda b, j: (b, 0, 0)),
            const(wfox.shape), const(wdil.shape), const(wgate.shape), const(wfg.shape),
            const(bfg.shape), const(bgate.shape),
            const(cos_t.shape), const(sina_t.shape), const(sinb_t.shape),
            const(selq.shape), const(selk.shape), const(oneq.shape), const(onek.shape),
        ],
        out_specs=[tok(aw)] * 2 + [pl.BlockSpec((1, tm // FOX_K, hw, FOX_K), lambda b, j: (b, j, 0, 0))]
                  + dil_specs + [tok(d)] * 2,
        out_shape=outs,
        scratch_shapes=[pltpu.VMEM((1, LANES), F32), pltpu.VMEM((3, N_PAIRS, tm, LANES), F32)],
        compiler_params=_params(("arbitrary", "arbitrary"), 56),
        name="inproj",
    )(x, mod3, mod3, wfox, wdil, wgate, wfg, bfg, bgate, cos_t, sina_t, sinb_t, selq, selk, oneq, onek)


def _fox_kernel(qa_ref, ka_ref, vt_ref, o_ref, m_ref, l_ref, acc_ref):
    seq = qa_ref.shape[1]
    tq, tk = FOX_Q, FOX_K
    k_pos = lax.broadcasted_iota(jnp.int32, (tk, tq), 0)
    q_pos = lax.broadcasted_iota(jnp.int32, (tk, tq), 1)
    ones = jnp.ones((16, tk), BF16)

    def q_body(qi, _):
        q0 = pl.multiple_of(qi * tq, tq)
        m_ref[...] = jnp.full(m_ref.shape, NEG_INF, F32)
        l_ref[...] = jnp.zeros(l_ref.shape, F32)
        acc_ref[...] = jnp.zeros(acc_ref.shape, F32)

        def kv_step(kb, k0, visible):
            scores = []
            for h in range(N_HEADS):
                ls = slice(h * LANES, (h + 1) * LANES)
                scores.append(_dot_nt(ka_ref[0, pl.ds(k0, tk), ls], qa_ref[0, pl.ds(q0, tq), ls]))
            m_all = m_ref[...]
            l_all = l_ref[...]
            probs, decay, m_rows, l_rows = [], [], [], []
            for h in range(N_HEADS):
                s = scores[h] if visible is None else jnp.where(visible, scores[h], NEG_INF)
                m_old = m_all[h:h + 1, :]
                m_new = jnp.maximum(m_old, jnp.max(s, axis=0, keepdims=True))
                m_rows.append(m_new)
                decay.append(jnp.exp2(m_old - m_new))
                probs.append(jnp.exp2((s - m_new).astype(BF16)))
            m_ref[...] = jnp.concatenate(m_rows, axis=0)
            for h in range(N_HEADS):
                vh = vt_ref[0, kb, h * HEAD_DIM:(h + 1) * HEAD_DIM, :]
                acc_ref[h] = decay[h] * acc_ref[h] + _dot(vh, probs[h])
                l_rows.append(decay[h] * l_all[h:h + 1, :] + _dot(ones, probs[h])[0:1])
            l_ref[...] = jnp.concatenate(l_rows, axis=0)

        def full_block(jb, carry):
            kv_step(jb, pl.multiple_of(jb * tk, tk), None)
            return carry

        lax.fori_loop(0, qi * (tq // tk), full_block, 0)
        for dblk in range(tq // tk):
            kb = qi * (tq // tk) + dblk
            kv_step(kb, pl.multiple_of(kb * tk, tk), k_pos + dblk * tk <= q_pos)

        for p in range(N_PAIRS):
            pair = jnp.concatenate([acc_ref[2 * p + hh] / l_ref[2 * p + hh:2 * p + hh + 1, :] for hh in range(2)],
                                   axis=0)
            o_ref[0, pl.ds(q0, tq), p * PAIR:(p + 1) * PAIR] = pair.T.astype(BF16)
        return 0

    lax.fori_loop(0, seq // tq, q_body, 0)


def _fox(qa, ka, vt):
    bsz, seq, aw = qa.shape
    hw = HEADS_WIDTH
    full = pl.BlockSpec((1, seq, aw), lambda b: (b, 0, 0))
    return pl.pallas_call(
        _fox_kernel,
        grid=(bsz,),
        in_specs=[full, full, pl.BlockSpec((1, seq // FOX_K, hw, FOX_K), lambda b: (b, 0, 0, 0))],
        out_specs=pl.BlockSpec((1, seq, hw), lambda b: (b, 0, 0)),
        out_shape=jax.ShapeDtypeStruct((bsz, seq, hw), BF16),
        scratch_shapes=[pltpu.VMEM((N_HEADS, FOX_Q), F32), pltpu.VMEM((N_HEADS, FOX_Q), F32),
                        pltpu.VMEM((N_HEADS, HEAD_DIM, FOX_Q), F32)],
        compiler_params=_params(("arbitrary",), 48),
        name="fox",
    )(qa, ka, vt)


def _dil_kernel(q_ref, kc_ref, kp_ref, vc_ref, vp_ref, o_ref, lse_ref):
    n = pl.program_id(2)
    blk = ATTN_BLOCK
    n_res = q_ref.shape[0]
    n_sub = q_ref.shape[1] // blk
    lane = lax.broadcasted_iota(jnp.int32, (1, PAIR), 1)
    first = lane < HEAD_DIM
    i = lax.broadcasted_iota(jnp.int32, (blk, 2 * blk), 0)
    c = lax.broadcasted_iota(jnp.int32, (blk, 2 * blk), 1)
    in_window = (c >= i) & (c <= i + blk)
    has_prev = (c >= blk) | (n > 0)

    for res in range(n_res):
        for sub in range(n_sub):
            rows = slice(sub * blk, (sub + 1) * blk)
            before = slice((sub - 1) * blk, sub * blk)
            valid = in_window & has_prev if sub == 0 else in_window
            scores, values = [], []
            for p in range(N_PAIRS):
                ls = slice(p * PAIR, (p + 1) * PAIR)
                qb = q_ref[res, rows, ls]
                zero = jnp.zeros_like(qb)
                k_prev = kp_ref[res, :, ls] if sub == 0 else kc_ref[res, before, ls]
                v_prev = vp_ref[res, :, ls] if sub == 0 else vc_ref[res, before, ls]
                kb = jnp.concatenate([k_prev, kc_ref[res, rows, ls]], axis=0)
                values.append(jnp.concatenate([v_prev, vc_ref[res, rows, ls]], axis=0))
                scores.append(_dot_nt(jnp.where(first, qb, zero), kb))
                scores.append(_dot_nt(jnp.where(first, zero, qb), kb))
            probs, denom, lses = [], [], []
            for s in scores:
                s = jnp.where(valid, s, NEG_INF)
                m = jnp.max(s, axis=-1, keepdims=True)
                e = jnp.exp(s - m)
                l = jnp.sum(e, axis=-1, keepdims=True)
                probs.append(e.astype(BF16))
                denom.append(l)
                lses.append(m + jnp.log(l))
            for p in range(N_PAIRS):
                ls = slice(p * PAIR, (p + 1) * PAIR)
                o0 = _dot(probs[2 * p], values[p]) / denom[2 * p]
                o1 = _dot(probs[2 * p + 1], values[p]) / denom[2 * p + 1]
                o_ref[res, rows, ls] = jnp.where(first, o0, o1).astype(BF16)
                lse_ref[res, rows, ls] = jnp.where(first, lses[2 * p], lses[2 * p + 1])


def _dil(qd, kd, vd):
    bsz, dilation, length, hw = qd.shape
    nb = length // ATTN_BLOCK
    n_sub = min(nb, DIL_STEP_BLOCKS)
    n_res = min(dilation, DIL_STEP_BLOCKS // n_sub)
    cur = pl.BlockSpec((None, n_res, n_sub * ATTN_BLOCK, hw), lambda b, r, n: (b, r, n, 0))
    prev = pl.BlockSpec((None, n_res, ATTN_BLOCK, hw), lambda b, r, n: (b, r, jnp.maximum(n * n_sub - 1, 0), 0))
    return pl.pallas_call(
        _dil_kernel,
        grid=(bsz, dilation // n_res, nb // n_sub),
        in_specs=[cur, cur, prev, cur, prev],
        out_specs=[cur, cur],
        out_shape=[jax.ShapeDtypeStruct(qd.shape, BF16), jax.ShapeDtypeStruct(qd.shape, F32)],
        compiler_params=_params(("arbitrary", "arbitrary", "arbitrary"), 32),
        name=f"dil{dilation}",
    )(qd, kd, kd, vd, vd)


def _mix_kernel(x_ref, yf_ref, o1_ref, o2_ref, o3_ref, l1_ref, l2_ref, l3_ref, gf_ref, gd_ref, g1_ref,
                wbf_ref, wbd_ref, wo_ref, lng_ref, lnb_ref, out_ref, nat_ref):
    tm = x_ref.shape[1]

    def natural(i, src_ref):
        dilation = src_ref.shape[1]
        for res in range(dilation):
            for g in range(N_PAIRS):
                nat_ref[i, g, pl.ds(res, tm // dilation, stride=dilation), :] = src_ref[
                    0, res, :, g * LANES:(g + 1) * LANES].astype(F32)
        return jnp.concatenate([nat_ref[i, g] for g in range(N_PAIRS)], axis=1)

    os_ = [natural(i, ref) for i, ref in enumerate((o1_ref, o2_ref, o3_ref))]
    l1, l2, l3 = [natural(3 + i, ref) for i, ref in enumerate((l1_ref, l2_ref, l3_ref))]
    mx = jnp.maximum(jnp.maximum(l1, l2), l3)
    e1, e2, e3 = jnp.exp(l1 - mx), jnp.exp(l2 - mx), jnp.exp(l3 - mx)
    yd = (e1 * os_[0] + e2 * os_[1] + e3 * os_[2]) / (e1 + e2 + e3)
    merged = (gf_ref[0].astype(F32) * _dot(yf_ref[0], wbf_ref[...])
              + gd_ref[0].astype(F32) * _dot(yd.astype(BF16), wbd_ref[...]))
    y = _dot(merged.astype(BF16), wo_ref[...])
    z = DN_ALPHA * x_ref[0] + (1.0 + g1_ref[0]) * y
    out_ref[0] = _layer_norm(z, lng_ref[...], lnb_ref[...])


def _mix(x, first_seq, yf, os_, lses, gf, gd, mod3, wbf, wbd, wo, lng, lnb):
    _, seq, d = x.shape
    bsz = mod3.shape[0]
    tm = ROW_TILE
    const = lambda shape: pl.BlockSpec(shape, lambda b, j: (0,) * len(shape))
    tok = lambda w: pl.BlockSpec((1, tm, w), lambda b, j: (b, j, 0))
    hw = HEADS_WIDTH
    res_major = [pl.BlockSpec((1, dil, tm // dil, hw), lambda b, j: (b, 0, j, 0)) for _, dil in DIL_PATTERNS]
    return pl.pallas_call(
        _mix_kernel,
        grid=(bsz, seq // tm),
        in_specs=[pl.BlockSpec((1, tm, d), lambda b, j: (b + first_seq, j, 0)), tok(hw)] + res_major * 2 + [
                  tok(d), tok(d),
                  pl.BlockSpec((1, 1, d), lambda b, j: (b, 0, 2)),
                  const(wbf.shape), const(wbd.shape), const(wo.shape), const(lng.shape), const(lnb.shape)],
        out_specs=tok(d),
        out_shape=jax.ShapeDtypeStruct((bsz, seq, d), F32),
        scratch_shapes=[pltpu.VMEM((6, N_PAIRS, tm, LANES), F32)],
        compiler_params=_params(("arbitrary", "arbitrary"), 48),
        name="mix",
    )(x, yf, *os_, *lses, gf, gd, mod3, wbf, wbd, wo, lng, lnb)


def _router_kernel(x_ref, sc_ref, sh_ref, wrh_ref, wrl_ref, bias_ref, wsgu_ref, wsd_ref,
                   shared_ref, up_ref, eidx_ref, rank_ref, gw_ref, cnt_ref):
    seq = x_ref.shape[1]
    tc = ROUTE_CHUNK
    e_iota = lax.broadcasted_iota(jnp.int32, (N_EXPERTS, tc), 0)
    g_iota = lax.broadcasted_iota(jnp.int32, (N_GROUPS, tc), 0)
    s_iota = lax.broadcasted_iota(jnp.int32, (GROUP_SIZE, tc), 0)
    r = lax.broadcasted_iota(jnp.int32, (tc, tc), 0)
    c = lax.broadcasted_iota(jnp.int32, (tc, tc), 1)
    upper = jnp.where(r <= c, 1.0, 0.0).astype(BF16)
    ones = jnp.ones((16, tc), BF16)
    carry = jnp.zeros((N_EXPERTS, 1), F32)
    cnt = jnp.zeros((16, N_EXPERTS), F32)

    for ch in range(seq // tc):
        rows = slice(ch * tc, (ch + 1) * tc)
        u = x_ref[0, rows, :] * (1.0 + sc_ref[0]) + sh_ref[0]
        hgu = _dot(u.astype(BF16), wsgu_ref[...])
        hg = hgu[:, :SHARED_FF]
        shared_ref[0, rows, :] = _dot((hg * _sigmoid(hg) * hgu[:, SHARED_FF:]).astype(BF16),
                                      wsd_ref[...]).astype(BF16)
        up_ref[0, rows, :] = _pack_words(u)
        u_hi, u_lo = _split2(u)
        logits = _dot_nt(wrh_ref[...], u_hi) + _dot_nt(wrh_ref[...], u_lo) + _dot_nt(wrl_ref[...], u_hi)
        scores = _sigmoid(logits)
        sel = scores + bias_ref[...]

        gs_rows = []
        for g in range(N_GROUPS):
            blk = sel[g * GROUP_SIZE:(g + 1) * GROUP_SIZE]
            m1 = jnp.max(blk, axis=0, keepdims=True)
            i1 = jnp.min(jnp.where(blk == m1, s_iota, GROUP_SIZE), axis=0, keepdims=True)
            m2 = jnp.max(jnp.where(s_iota == i1, -jnp.inf, blk), axis=0, keepdims=True)
            gs_rows.append(m1 + m2)
        gs = jnp.concatenate(gs_rows, axis=0)
        beaten = jnp.zeros((N_GROUPS, tc), F32)
        for g in range(N_GROUPS):
            other = gs[g:g + 1]
            wins = (other > gs) | ((other == gs) & (g_iota > g))
            beaten = beaten + jnp.where(wins, 1.0, 0.0)
        keep = beaten < TOPK_GROUPS
        cand = jnp.concatenate(
            [jnp.where(keep[g:g + 1], sel[g * GROUP_SIZE:(g + 1) * GROUP_SIZE], NEG_INF) for g in range(N_GROUPS)],
            axis=0)
        chosen = jnp.zeros((N_EXPERTS, tc), F32)
        picks = []
        for _ in range(TOP_K):
            m = jnp.max(cand, axis=0, keepdims=True)
            idx = jnp.min(jnp.where(cand == m, e_iota, N_EXPERTS), axis=0, keepdims=True)
            hit = e_iota == idx
            chosen = jnp.where(hit, 1.0, chosen)
            cand = jnp.where(hit, -jnp.inf, cand)
            picks.append(idx)
        g_raw = jnp.where(chosen > 0.0, scores, 0.0)
        gate = g_raw / jnp.sum(g_raw, axis=0, keepdims=True) * ROUTED_SCALE

        chosen_b = chosen.astype(BF16)
        incl = _dot(chosen_b, upper) + carry
        carry = incl[:, tc - 1:tc]
        cnt = cnt + _dot_nt(ones, chosen_b)

        rank_rows, gate_rows = [], []
        for idx in picks:
            hit = e_iota == idx
            rank_rows.append(jnp.sum(jnp.where(hit, incl - 1.0, 0.0), axis=0, keepdims=True))
            gate_rows.append(jnp.sum(jnp.where(hit, gate, 0.0), axis=0, keepdims=True))
        eidx_ref[0, :, rows] = jnp.concatenate(picks, axis=0)
        rank_ref[0, :, rows] = jnp.concatenate(rank_rows, axis=0).astype(jnp.int32)
        gate_t = jnp.concatenate(gate_rows + [jnp.zeros((LANES - TOP_K, tc), F32)], axis=0).T
        gw_ref[0, rows, :] = gate_t[:, :TOP_K]

    cnt_ref[0] = cnt[0:1].astype(jnp.int32)


def _router(x1, mod3, wr_hi, wr_lo, bias, wsgu, wsd):
    bsz, seq, d = x1.shape
    full = pl.BlockSpec((1, seq, d), lambda b: (b, 0, 0))
    ks = pl.BlockSpec((1, TOP_K, seq), lambda b: (b, 0, 0))
    return pl.pallas_call(
        _router_kernel,
        grid=(bsz,),
        in_specs=[full,
                  pl.BlockSpec((1, 1, d), lambda b: (b, 0, 4)),
                  pl.BlockSpec((1, 1, d), lambda b: (b, 0, 3)),
                  pl.BlockSpec(wr_hi.shape, lambda b: (0, 0)),
                  pl.BlockSpec(wr_lo.shape, lambda b: (0, 0)),
                  pl.BlockSpec(bias.shape, lambda b: (0, 0)),
                  pl.BlockSpec(wsgu.shape, lambda b: (0, 0)),
                  pl.BlockSpec(wsd.shape, lambda b: (0, 0))],
        out_specs=[full, pl.BlockSpec((1, seq, PACKED), lambda b: (b, 0, 0)), ks, ks,
                   pl.BlockSpec((1, seq, TOP_K), lambda b: (b, 0, 0)),
                   pl.BlockSpec((1, 1, N_EXPERTS), lambda b: (b, 0, 0))],
        out_shape=[jax.ShapeDtypeStruct((bsz, seq, d), BF16),
                   jax.ShapeDtypeStruct((bsz, seq, PACKED), F32),
                   jax.ShapeDtypeStruct((bsz, TOP_K, seq), jnp.int32),
                   jax.ShapeDtypeStruct((bsz, TOP_K, seq), jnp.int32),
                   jax.ShapeDtypeStruct((bsz, seq, TOP_K), F32),
                   jax.ShapeDtypeStruct((bsz, 1, N_EXPERTS), jnp.int32)],
        compiler_params=_params(("arbitrary",), 48),
        name="router",
    )(x1, mod3, mod3, wr_hi, wr_lo, bias, wsgu, wsd)


def _slot_kernel(base_ref, eidx_ref, rank_ref, slot_ref):
    b = pl.program_id(0)
    eidx = eidx_ref[0]
    start = jnp.zeros(eidx.shape, jnp.int32)
    for e in range(N_EXPERTS):
        start = jnp.where(eidx == e, base_ref[b * N_EXPERTS + e], start)
    slot_ref[0] = start + rank_ref[0]


def _slots(base, eidx, rank):
    bsz, _, seq = eidx.shape
    ks = pl.BlockSpec((1, TOP_K, seq), lambda b, c: (b, 0, 0))
    return pl.pallas_call(
        _slot_kernel,
        grid_spec=pltpu.PrefetchScalarGridSpec(num_scalar_prefetch=1, grid=(bsz,), in_specs=[ks, ks], out_specs=ks),
        out_shape=jax.ShapeDtypeStruct((bsz, TOP_K, seq), jnp.int32),
        compiler_params=_params(("arbitrary",), 16),
        name="slots",
    )(base.reshape(bsz * N_EXPERTS), eidx, rank)


def _sc_mesh():
    return plsc.VectorSubcoreMesh(core_axis_name="core", subcore_axis_name="subcore")


def _sc_worker():
    return lax.axis_index("subcore") * SC_CORES + lax.axis_index("core")


def _sc_dispatch(rows, slot, n_slots):
    n_tok, width = rows.shape
    seq = slot.shape[1]
    chunk = DISPATCH_CHUNK
    per_worker = n_tok // SC_WORKERS
    assert per_worker % chunk == 0 and seq % chunk == 0

    @functools.partial(
        pl.kernel, mesh=_sc_mesh(),
        out_type=jax.ShapeDtypeStruct((n_slots, width), rows.dtype),
        scratch_types=[pltpu.VMEM((TOP_K, chunk), jnp.int32), pltpu.VMEM((chunk, width), rows.dtype),
                       pltpu.SemaphoreType.DMA],
    )
    def dispatch(rows_hbm, slot_hbm, out_hbm, idx_v, rows_v, sem):
        base = _sc_worker() * per_worker

        @pl.loop(0, per_worker // chunk)
        def _(i):
            off = base + i * chunk
            b = off // seq
            n0 = off - b * seq
            pltpu.sync_copy(slot_hbm.at[pl.ds(b * TOP_K, TOP_K), pl.ds(n0, chunk)], idx_v)
            pltpu.sync_copy(rows_hbm.at[pl.ds(off, chunk)], rows_v)
            copies = [pltpu.async_copy(rows_v, out_hbm.at[idx_v.at[k]], sem) for k in range(TOP_K)]
            for cp in copies:
                cp.wait()

    return dispatch(rows, slot)


def _sc_gather(table, idx):
    n_out = idx.shape[0]
    width = table.shape[1]
    chunk = GATHER_CHUNK
    per_worker = n_out // SC_WORKERS
    steps = per_worker // chunk
    assert per_worker % chunk == 0 and steps % 2 == 0

    @functools.partial(
        pl.kernel, mesh=_sc_mesh(),
        out_type=jax.ShapeDtypeStruct((n_out, width), table.dtype),
        scratch_types=[pltpu.VMEM((2, chunk), jnp.int32), pltpu.VMEM((2, chunk, width), table.dtype),
                       pltpu.SemaphoreType.DMA, pltpu.SemaphoreType.DMA((2,))],
    )
    def gather(table_hbm, idx_hbm, out_hbm, idx_v, rows_v, gather_sem, out_sems):
        base = _sc_worker() * per_worker

        def write_out(buf, off):
            return pltpu.make_async_copy(rows_v.at[buf], out_hbm.at[pl.ds(off, chunk)], out_sems.at[buf])

        @pl.loop(0, steps, step=2)
        def _(i):
            for buf in range(2):
                off = base + (i + buf) * chunk

                @pl.when(i >= 2)
                def _():
                    write_out(buf, off - 2 * chunk).wait()

                pltpu.sync_copy(idx_hbm.at[pl.ds(off, chunk)], idx_v.at[buf])
                pltpu.async_copy(table_hbm.at[idx_v.at[buf]], rows_v.at[buf], gather_sem).wait()
                write_out(buf, off).start()

        for buf in range(2):
            write_out(buf, base + (steps - 2 + buf) * chunk).wait()

    return gather(table, idx)


def _gmm_kernel(blk_w_ref, used_ref, xs_ref, wg_ref, wu_ref, wd_ref, out_ref):
    half = PACKED

    @pl.when(pl.program_id(0) < used_ref[0])
    def _():
        lo, hi = (t.astype(BF16) for t in _unpack_words(xs_ref[...]))
        hg = _dot(lo, wg_ref[0, :half, :]) + _dot(hi, wg_ref[0, half:, :])
        hu = _dot(lo, wu_ref[0, :half, :]) + _dot(hi, wu_ref[0, half:, :])
        h = (hg * _sigmoid(hg) * hu).astype(BF16)
        out_ref[...] = _pack_words(_dot(h, wd_ref[0]))


def _gmm(xs, blk_w, used, wg, wu, wd):
    n_slots, width = xs.shape
    bm = EXPERT_BLOCK
    d = wg.shape[1]
    live = lambda i, blk_w, used: (jnp.minimum(i, used[0] - 1), 0)
    slab = lambda i, blk_w, used: (blk_w[i], 0, 0)
    return pl.pallas_call(
        _gmm_kernel,
        grid_spec=pltpu.PrefetchScalarGridSpec(
            num_scalar_prefetch=2,
            grid=(n_slots // bm,),
            in_specs=[pl.BlockSpec((bm, width), live),
                      pl.BlockSpec((1, d, EXPERT_FF), slab),
                      pl.BlockSpec((1, d, EXPERT_FF), slab),
                      pl.BlockSpec((1, EXPERT_FF, d), slab)],
            out_specs=pl.BlockSpec((bm, width), live)),
        out_shape=jax.ShapeDtypeStruct((n_slots, width), F32),
        compiler_params=_params(("arbitrary",), 32),
        name="gmm",
    )(blk_w, used, xs, wg, wu, wd)


def _ffn_out_kernel(x_ref, sh_ref, yg_ref, gw_ref, g2_ref, lng_ref, lnb_ref, *rest):
    out_ref = rest[-1]
    shared = sh_ref[0].astype(F32)
    gw = gw_ref[0]
    acc_lo = shared[:, :PACKED]
    acc_hi = shared[:, PACKED:]
    for k in range(TOP_K):
        lo, hi = _unpack_words(yg_ref[0, k])
        w = gw[:, k:k + 1]
        acc_lo = acc_lo + w * lo
        acc_hi = acc_hi + w * hi
    y = jnp.concatenate([acc_lo, acc_hi], axis=1)
    z = DN_ALPHA * x_ref[0] + (1.0 + g2_ref[0]) * y
    out_ref[0] = _layer_norm(z, lng_ref[...], lnb_ref[...])


def _ffn_out(x1, sh, yg, gw, mod3, lng, lnb, first_seq, n_seq, shared):
    bsz, seq, d = x1.shape
    tm = ROW_TILE
    const = lambda shape: pl.BlockSpec(shape, lambda b, j: (0,) * len(shape))
    tok = pl.BlockSpec((1, tm, d), lambda b, j: (b, j, 0))
    in_specs = [tok, tok,
                pl.BlockSpec((1, TOP_K, tm, PACKED), lambda b, j: (b, 0, j, 0)),
                pl.BlockSpec((1, tm, TOP_K), lambda b, j: (b, j, 0)),
                pl.BlockSpec((1, 1, d), lambda b, j: (b, 0, 5)),
                const(lng.shape), const(lnb.shape)]
    args = [x1, sh, yg, gw, mod3, lng, lnb]
    aliases = {}
    if shared is not None:
        aliases = {len(args): 0}
        in_specs.append(pl.BlockSpec(memory_space=pl.ANY))
        args.append(shared)
    return pl.pallas_call(
        _ffn_out_kernel,
        grid=(bsz, seq // tm),
        in_specs=in_specs,
        out_specs=pl.BlockSpec((1, tm, d), lambda b, j: (b + first_seq, j, 0)),
        out_shape=jax.ShapeDtypeStruct((n_seq, seq, d), F32),
        input_output_aliases=aliases,
        compiler_params=_params(("arbitrary", "arbitrary"), 48),
        name="ffn_out",
    )(*args)


def _moe_layout(cnt, n_blocks):
    bm = EXPERT_BLOCK
    total = jnp.sum(cnt, axis=0)
    padded = (total + bm - 1) // bm * bm
    ends = jnp.cumsum(padded)
    base = (ends - padded)[None, :] + jnp.cumsum(cnt, axis=0) - cnt
    first_row = jnp.arange(n_blocks, dtype=ends.dtype) * bm
    blk_e = jnp.minimum(jnp.sum(ends[None, :] <= first_row[:, None], axis=1), N_EXPERTS - 1)
    used = (ends[-1:] // bm)
    return base.astype(jnp.int32), blk_e.astype(jnp.int32), used.astype(jnp.int32)


def _rope_tables(seq):
    half = HEAD_DIM // 2
    inv_freq = ROPE_THETA ** (-jnp.arange(half, dtype=F32) * 2.0 / HEAD_DIM)
    ang = jnp.arange(seq, dtype=F32)[:, None] * inv_freq[None, :]
    cos, sin = jnp.cos(ang), jnp.sin(ang)
    zero = jnp.zeros_like(sin)
    reps = LANES // HEAD_DIM
    cos_t = jnp.tile(jnp.concatenate([cos, cos], axis=1), (1, reps))
    sina_t = jnp.tile(jnp.concatenate([-sin, zero], axis=1), (1, reps))
    sinb_t = jnp.tile(jnp.concatenate([zero, sin], axis=1), (1, reps))
    return cos_t, sina_t, sinb_t


def _fox_bias_tables():
    n_split = 3
    aw = N_HEADS * LANES
    heads = jnp.arange(N_HEADS)
    selq = jnp.zeros((LANES, aw), F32)
    selk = jnp.zeros((LANES, aw), F32)
    oneq = jnp.zeros((1, aw), F32)
    onek = jnp.zeros((1, aw), F32)
    for j in range(n_split):
        selk = selk.at[j * N_HEADS + heads, heads * LANES + FOX_BIAS_LANE + j].set(-1.0)
        selq = selq.at[j * N_HEADS + heads, heads * LANES + FOX_BIAS_LANE + n_split + j].set(1.0)
        oneq = oneq.at[0, heads * LANES + FOX_BIAS_LANE + j].set(1.0)
        onek = onek.at[0, heads * LANES + FOX_BIAS_LANE + n_split + j].set(1.0)
    return selq.astype(BF16), selk.astype(BF16), oneq, onek


def kernel(x, c, w_ada, b_ada, w_in, b_forget, b_gate, w_br_fox, w_br_dil, w_o, ln_g, ln_b, w_router, router_bias,
           w_exp_gate, w_exp_up, w_exp_down, w_sh_gate, w_sh_up, w_sh_down):
    bsz, seq, d = x.shape
    depth = w_ada.shape[0]
    hw = HEADS_WIDTH
    n_tok = bsz * seq

    mod = _ada(c, w_ada, b_ada)

    o_f, o_d, o_g = 3 * hw, 3 * hw + N_HEADS, 6 * hw + N_HEADS
    wfox = w_in[:, :, :o_f].astype(BF16)
    wfg = jnp.pad(w_in[:, :, o_f:o_d], ((0, 0), (0, 0), (0, LANES - N_HEADS))).astype(BF16)
    wdil = w_in[:, :, o_d:o_g].astype(BF16)
    wgate = w_in[:, :, o_g:].astype(BF16)
    bfg = jnp.pad(b_forget, ((0, 0), (0, LANES - N_HEADS))).reshape(depth, 1, LANES)
    bgate = b_gate.reshape(depth, 1, 2 * d)
    wr_t = jnp.swapaxes(w_router, 1, 2)
    wr_hi = wr_t.astype(BF16)
    wr_lo = (wr_t - wr_hi.astype(F32)).astype(BF16)
    rbias = router_bias.reshape(depth, N_EXPERTS, 1)
    wg = w_exp_gate.astype(BF16).reshape(depth * N_EXPERTS, d, EXPERT_FF)
    wu = w_exp_up.astype(BF16).reshape(depth * N_EXPERTS, d, EXPERT_FF)
    wd = w_exp_down.astype(BF16).reshape(depth * N_EXPERTS, EXPERT_FF, d)
    wsgu = jnp.concatenate([w_sh_gate, w_sh_up], axis=-1).astype(BF16)
    wsd = w_sh_down.astype(BF16)
    cos_t, sina_t, sinb_t = _rope_tables(seq)
    selq, selk, oneq, onek = _fox_bias_tables()

    layer_params = (mod, wfox, wdil, wgate, wfg, bfg, bgate, w_br_fox.astype(BF16), w_br_dil.astype(BF16),
                    w_o.astype(BF16), ln_g, ln_b, wr_hi, wr_lo, rbias, wsgu, wsd,
                    jnp.arange(depth, dtype=jnp.int32) * N_EXPERTS)

    n_streams = 2 if n_tok % (2 * SC_WORKERS * DISPATCH_CHUNK) == 0 and bsz % 2 == 0 else 1
    sb = bsz // n_streams
    s_tok = sb * seq
    n_blocks = s_tok * TOP_K // EXPERT_BLOCK + N_EXPERTS

    def stream_layer(x_all, first_seq, mod3, prm, shared):
        (_, wfox_l, wdil_l, wgate_l, wfg_l, bfg_l, bgate_l, wbf_l, wbd_l, wo_l, lng_l, lnb_l,
         wrh_l, wrl_l, rb_l, wsgu_l, wsd_l, first_slab) = prm
        qa, ka, vt, *rest = _inproj(x_all, first_seq, mod3, wfox_l, wdil_l, wgate_l, wfg_l, bfg_l, bgate_l,
                                    cos_t, sina_t, sinb_t, selq, selk, oneq, onek)
        gf, gd = rest[9:]
        yf = _fox(qa, ka, vt)
        dil = [_dil(*rest[3 * i:3 * i + 3]) for i in range(len(DIL_PATTERNS))]
        x1 = _mix(x_all, first_seq, yf, [o for o, _ in dil], [l for _, l in dil], gf, gd, mod3, wbf_l, wbd_l, wo_l,
                  lng_l[0:1], lnb_l[0:1])
        sh, u2p, eidx, rank, gw, cnt = _router(x1, mod3, wrh_l, wrl_l, rb_l, wsgu_l, wsd_l)
        base, blk_e, used = _moe_layout(cnt.reshape(sb, N_EXPERTS), n_blocks)
        slot = _slots(base, eidx, rank)
        xs = _sc_dispatch(u2p.reshape(s_tok, PACKED), slot.reshape(sb * TOP_K, seq), n_blocks * EXPERT_BLOCK)
        ys = _gmm(xs, blk_e + first_slab, used, wg, wu, wd)
        yg = _sc_gather(ys, slot.reshape(s_tok * TOP_K)).reshape(sb, TOP_K, seq, PACKED)
        return _ffn_out(x1, sh, yg, gw, mod3, lng_l[1:2], lnb_l[1:2], first_seq, bsz, shared)

    def layer(x_all, prm):
        mod3 = prm[0].reshape(bsz, 1, 6 * d)
        shared = None
        for i in range(n_streams):
            shared = stream_layer(x_all, i * sb, mod3[i * sb:(i + 1) * sb], prm, shared)
        return shared, None

    out, _ = lax.scan(layer, x, layer_params)
    return out
```
